```python
import math
import jax, jax.numpy as jnp
from jax import lax
import numpy as np

D_MODEL = 1024
BATCH = 8
SEQ = 2048
DEPTH = 2

GRID_W = 64
CTX_LEN = 256
WIN_H = 8
WIN_W = 16
HD_A = 64
H_A = D_MODEL // 2 // HD_A
H_B = 4
DV_B = D_MODEL // 2 // H_B
DK_B = DV_B // 2
MLSTM_CHUNK = 64
HD_C = 128
H_C = D_MODEL // HD_C
KV_C = max(H_C // 4, 1)
Q_BLOCK = 128
ROPE_THETA = 10000.0
N_EXPERTS = 16
N_GROUPS = 4
EXP_PER_GROUP = N_EXPERTS // N_GROUPS
TOP_K = 2
D_FF_EXPERT = D_MODEL // 2
N_EVEN = (DEPTH + 1) // 2
N_ODD = DEPTH // 2
D_MIX = H_A * HD_A + H_B * DV_B
SIZES_EVEN = (H_A * HD_A, H_A * HD_A, H_A * HD_A,
              H_B * DK_B, H_B * DK_B, H_B * DV_B, H_B * DV_B, 4 * H_B)
D_IN_EVEN = sum(SIZES_EVEN)
SIZES_ODD = (H_C * HD_C, KV_C * HD_C, KV_C * HD_C)
D_IN_ODD = sum(SIZES_ODD)
EPS = 1e-6
NEG_BIG = -1e30

kernel_name = "hybrid_natten_mlstm_gqa_moe_prefix"


def rms_norm(x, g):
    xf = x.astype(jnp.float32)
    y = xf * lax.rsqrt(jnp.mean(xf * xf, axis=-1, keepdims=True) + EPS)
    return (y * g.astype(jnp.float32)).astype(x.dtype)


def modulate(x, g, shift, scale):
    return rms_norm(x, g) * (1 + scale) + shift


def split_cols(p, sizes):
    points = [int(s) for s in np.cumsum(sizes)[:-1]]
    return jnp.split(p, points, axis=-1)


def rope_2d_tables(n_tokens, head_dim):
    n_freq = head_dim // 4
    inv_freq = ROPE_THETA ** (-jnp.arange(n_freq, dtype=jnp.float32) / n_freq)
    t = jnp.arange(n_tokens)
    rows = (t // GRID_W).astype(jnp.float32)
    cols = (t % GRID_W).astype(jnp.float32)
    ang = jnp.concatenate([rows[:, None] * inv_freq, cols[:, None] * inv_freq], axis=-1)
    return jnp.cos(ang), jnp.sin(ang)


def apply_rope(x, cos, sin):
    xr = x.astype(jnp.float32).reshape(*x.shape[:-1], -1, 2)
    x0, x1 = xr[..., 0], xr[..., 1]
    c = cos[:, None, :]
    s = sin[:, None, :]
    out = jnp.stack([x0 * c - x1 * s, x0 * s + x1 * c], axis=-1).reshape(x.shape)
    return out.astype(x.dtype)


def block_attention(q, k, v):
    B, S, H, d = q.shape
    kv = k.shape[2]
    G = H // kv
    qb = q.reshape(B, S // Q_BLOCK, Q_BLOCK, kv, G, d).transpose(1, 0, 2, 3, 4, 5)

    def one_block(qi):
        s = jnp.einsum('bqkgd,btkd->bkgqt', qi, k).astype(jnp.float32)
        p = jax.nn.softmax(s, axis=-1).astype(v.dtype)
        return jnp.einsum('bkgqt,btkd->bqkgd', p, v)

    o = lax.map(one_block, qb)
    return o.transpose(1, 0, 2, 3, 4, 5).reshape(B, S, H * d)


def neighbourhood_attention(q, k, v, kc, vc, rpb):
    B, S, H, d = q.shape
    rows = S // GRID_W
    kh = min(WIN_H, rows)
    n_loc = kh * WIN_W
    qg = q.reshape(B, rows, GRID_W, H, d)
    kg = k.reshape(B, rows, GRID_W, H, d)
    vg = v.reshape(B, rows, GRID_W, H, d)
    qcol = np.arange(GRID_W)
    col_start = np.clip(qcol - WIN_W // 2, 0, GRID_W - WIN_W)
    col_idx = col_start[:, None] + np.arange(WIN_W)[None, :]
    dcol = col_idx - qcol[:, None] + (WIN_W - 1)

    def row_block(r):
        r0 = jnp.clip(r - kh // 2, 0, rows - kh)
        q_r = lax.dynamic_index_in_dim(qg, r, axis=1, keepdims=False)
        k_r = lax.dynamic_slice_in_dim(kg, r0, kh, axis=1)
        v_r = lax.dynamic_slice_in_dim(vg, r0, kh, axis=1)
        k_win = k_r[:, :, col_idx]
        v_win = v_r[:, :, col_idx]
        s_loc = jnp.einsum('bqhd,biqjhd->bhqij', q_r, k_win)
        drow = r0 + jnp.arange(kh) - r + (WIN_H - 1)
        bias = rpb[:, drow][:, :, dcol].transpose(0, 2, 1, 3)
        s_loc = s_loc + bias[None].astype(s_loc.dtype)
        s_ctx = jnp.einsum('bqhd,blhd->bhql', q_r, kc)
        s = jnp.concatenate([s_loc.reshape(B, H, GRID_W, n_loc), s_ctx], axis=-1)
        p = jax.nn.softmax(s.astype(jnp.float32), axis=-1).astype(v.dtype)
        p_loc = p[..., :n_loc].reshape(B, H, GRID_W, kh, WIN_W)
        o = jnp.einsum('bhqij,biqjhd->bqhd', p_loc, v_win) + jnp.einsum('bhql,blhd->bqhd', p[..., n_loc:], vc)
        return o

    o = lax.map(row_block, jnp.arange(rows))
    return o.transpose(1, 0, 2, 3, 4).reshape(B, S, H * d)


def mlstm_chunkwise(q, k, v, ig, lf):
    B, H, T, dk = q.shape
    dv = v.shape[-1]
    n_ch = T // MLSTM_CHUNK

    def to_chunks(a):
        return jnp.moveaxis(a.reshape(B, H, n_ch, MLSTM_CHUNK, *a.shape[3:]), 2, 0)

    causal = jnp.tril(jnp.ones((MLSTM_CHUNK, MLSTM_CHUNK), dtype=bool))

    def step(carry, inp):
        C, n, m = carry
        qj, kj, vj, ij, fj = inp
        b = jnp.cumsum(fj, axis=-1)
        a = b + m[..., None]
        Dm = jnp.where(causal, b[..., :, None] - b[..., None, :] + ij[..., None, :], NEG_BIG)
        m_row = jnp.maximum(a, Dm.max(axis=-1))
        w_intra = jnp.exp(Dm - m_row[..., None])
        w_inter = jnp.exp(a - m_row)
        qk = jnp.einsum('bhid,bhjd->bhij', qj, kj) * w_intra
        num = w_inter[..., None] * jnp.einsum('bhvd,bhid->bhiv', C, qj) + jnp.einsum('bhij,bhjv->bhiv', qk, vj)
        den = w_inter * jnp.einsum('bhd,bhid->bhi', n, qj) + qk.sum(axis=-1)
        h = num / jnp.maximum(jnp.abs(den), jnp.exp(-m_row))[..., None]
        bL = b[..., -1]
        g = bL[..., None] - b + ij
        m_new = jnp.maximum(bL + m, g.max(axis=-1))
        w_old = jnp.exp(bL + m - m_new)
        w_tok = jnp.exp(g - m_new[..., None])
        C_new = w_old[..., None, None] * C + jnp.einsum('bhj,bhjv,bhjd->bhvd', w_tok, vj, kj)
        n_new = w_old[..., None] * n + jnp.einsum('bhj,bhjd->bhd', w_tok, kj)
        return (C_new, n_new, m_new), h

    init = (jnp.zeros((B, H, dv, dk), jnp.float32), jnp.zeros((B, H, dk), jnp.float32),
            jnp.full((B, H), NEG_BIG, jnp.float32))
    _, hs = lax.scan(step, init, (to_chunks(q), to_chunks(k), to_chunks(v), to_chunks(ig), to_chunks(lf)))
    return jnp.moveaxis(hs, 0, 2).reshape(B, H, T, dv)


def even_mixer(h_lat, h_ctx, w_in, w_out, na_qg, na_kg, rpb, gate_b, ml_norm_g, need_ctx_out):
    B, S, _ = h_lat.shape
    L = h_ctx.shape[1]
    aq, ak, av, mq, mk, mv, mo, mg = split_cols(h_lat @ w_in, SIZES_EVEN)
    caq, cak, cav, cmq, cmk, cmv, cmo, cmg = split_cols(h_ctx @ w_in, SIZES_EVEN)

    def heads_a(t):
        return t.reshape(B, -1, H_A, HD_A)
    q_a = rms_norm(heads_a(aq), na_qg) * HD_A ** -0.5
    k_a = rms_norm(heads_a(ak), na_kg)
    ck_a = rms_norm(heads_a(cak), na_kg)
    cv_a = heads_a(cav)
    o_a = neighbourhood_attention(q_a, k_a, heads_a(av), ck_a, cv_a, rpb)

    def heads_b(t, dh):
        return jnp.moveaxis(t.reshape(B, -1, H_B, dh), 2, 1).astype(jnp.float32)
    q_m = jnp.concatenate([heads_b(cmq, DK_B), heads_b(mq, DK_B)], axis=2) * DK_B ** -0.5
    k_m = jnp.concatenate([heads_b(cmk, DK_B), heads_b(mk, DK_B)], axis=2)
    v_m = jnp.concatenate([heads_b(cmv, DV_B), heads_b(mv, DV_B)], axis=2)
    gates = jnp.concatenate([cmg, mg], axis=1).astype(jnp.float32) + gate_b.astype(jnp.float32)
    i_f, f_f, i_b, f_b = gates.reshape(B, L + S, 4, H_B).transpose(2, 0, 3, 1)

    def flip(t):
        return jnp.concatenate([t[:, :, :L][:, :, ::-1], t[:, :, L:][:, :, ::-1]], axis=2)

    h_f = mlstm_chunkwise(q_m, k_m, v_m, i_f, jax.nn.log_sigmoid(f_f))
    h_bw = flip(mlstm_chunkwise(flip(q_m), flip(k_m), flip(v_m), flip(i_b), flip(jax.nn.log_sigmoid(f_b))))
    h_m = jnp.moveaxis(h_f + h_bw, 1, 2)
    h_m = rms_norm(h_m, ml_norm_g.reshape(H_B, DV_B)).reshape(B, L + S, H_B * DV_B).astype(h_lat.dtype)
    h_m = h_m * jax.nn.sigmoid(jnp.concatenate([cmo, mo], axis=1))

    out_lat = jnp.concatenate([o_a, h_m[:, L:]], axis=-1) @ w_out
    if not need_ctx_out:
        return out_lat, None
    cq_a = rms_norm(heads_a(caq), na_qg) * HD_A ** -0.5
    co_a = block_attention(cq_a, ck_a, cv_a)
    out_ctx = jnp.concatenate([co_a, h_m[:, :L]], axis=-1) @ w_out
    return out_lat, out_ctx


def odd_mixer(h_lat, h_ctx, w_in, w_out, qg, kg, cos, sin, need_ctx_out):
    B, S, _ = h_lat.shape
    L = h_ctx.shape[1]
    q, k, v = split_cols(h_lat @ w_in, SIZES_ODD)
    q = apply_rope(rms_norm(q.reshape(B, S, H_C, HD_C), qg), cos, sin) * HD_C ** -0.5
    k = apply_rope(rms_norm(k.reshape(B, S, KV_C, HD_C), kg), cos, sin)
    v = v.reshape(B, S, KV_C, HD_C)
    if need_ctx_out:
        cq, ck, cv = split_cols(h_ctx @ w_in, SIZES_ODD)
    else:
        ck, cv = split_cols(h_ctx @ w_in[:, H_C * HD_C:], SIZES_ODD[1:])
    ck = rms_norm(ck.reshape(B, L, KV_C, HD_C), kg)
    cv = cv.reshape(B, L, KV_C, HD_C)
    k_all = jnp.concatenate([ck, k], axis=1)
    v_all = jnp.concatenate([cv, v], axis=1)
    out_lat = block_attention(q, k_all, v_all) @ w_out
    if not need_ctx_out:
        return out_lat, None
    cq = rms_norm(cq.reshape(B, L, H_C, HD_C), qg) * HD_C ** -0.5
    out_ctx = block_attention(cq, ck, cv) @ w_out
    return out_lat, out_ctx


def moe(h, router_w, router_b, w1, w3, w2):
    N = h.shape[0]
    scores = jax.nn.sigmoid((h @ router_w).astype(jnp.float32))
    sel = (scores + router_b.astype(jnp.float32)).reshape(N, N_GROUPS, EXP_PER_GROUP)
    group_score = lax.top_k(sel, TOP_K)[0].sum(axis=-1)
    g_idx = jnp.argmax(group_score, axis=-1)
    sel_in = jnp.take_along_axis(sel, g_idx[:, None, None], axis=1)[:, 0]
    _, local = lax.top_k(sel_in, TOP_K)
    e_idx = g_idx[:, None] * EXP_PER_GROUP + local
    w = jnp.take_along_axis(scores, e_idx, axis=-1)
    w = w / w.sum(axis=-1, keepdims=True)
    comb = (jax.nn.one_hot(e_idx, N_EXPERTS, dtype=jnp.float32) * w[..., None]).sum(axis=1).astype(h.dtype)
    y = jnp.zeros_like(h)
    for e in range(N_EXPERTS):
        y = y + comb[:, e:e + 1] * ((jax.nn.silu(h @ w1[e]) * (h @ w3[e])) @ w2[e])
    return y


def setup_inputs(seed: int = 0) -> dict:
    key = jax.random.key(seed)
    ks = jax.random.split(key, 32)
    D = D_MODEL

    def nrm(k, shape, scale):
        return jax.random.normal(k, shape, jnp.float32) * scale

    i_b = nrm(ks[13], (N_EVEN, 2, H_B), 0.1)
    f_b = 3.0 + 3.0 * jax.random.uniform(ks[14], (N_EVEN, 2, H_B), jnp.float32)
    mlstm_gate_b = jnp.stack([i_b[:, 0], f_b[:, 0], i_b[:, 1], f_b[:, 1]], axis=1).reshape(N_EVEN, 4 * H_B)
    return {
        "x": nrm(ks[0], (BATCH, SEQ, D), 1.0),
        "c": nrm(ks[1], (BATCH, D), 1.0),
        "ctx": nrm(ks[2], (BATCH, CTX_LEN, D), 1.0),
        "c_ctx": nrm(ks[3], (D,), 1.0),
        "ada_w": nrm(ks[4], (DEPTH, D, 6 * D), 0.5 * D ** -0.5),
        "ada_b": nrm(ks[5], (DEPTH, 6 * D), 0.02),
        "norm_mix_g": 1.0 + nrm(ks[6], (DEPTH, D), 0.05),
        "norm_ffn_g": 1.0 + nrm(ks[7], (DEPTH, D), 0.05),
        "even_w_in": nrm(ks[8], (N_EVEN, D, D_IN_EVEN), D ** -0.5),
        "even_w_out": nrm(ks[9], (N_EVEN, D_MIX, D), D_MIX ** -0.5),
        "na_q_norm_g": 1.0 + nrm(ks[10], (N_EVEN, HD_A), 0.05),
        "na_k_norm_g": 1.0 + nrm(ks[11], (N_EVEN, HD_A), 0.05),
        "na_rpb": nrm(ks[12], (N_EVEN, H_A, 2 * WIN_H - 1, 2 * WIN_W - 1), 0.02),
        "mlstm_gate_b": mlstm_gate_b,
        "mlstm_norm_g": 1.0 + nrm(ks[15], (N_EVEN, H_B * DV_B), 0.05),
        "odd_w_in": nrm(ks[16], (N_ODD, D, D_IN_ODD), D ** -0.5),
        "odd_w_out": nrm(ks[17], (N_ODD, H_C * HD_C, D), (H_C * HD_C) ** -0.5),
        "gqa_q_norm_g": 1.0 + nrm(ks[18], (N_ODD, HD_C), 0.05),
        "gqa_k_norm_g": 1.0 + nrm(ks[19], (N_ODD, HD_C), 0.05),
        "router_w": nrm(ks[20], (D, N_EXPERTS), D ** -0.5),
        "router_b": nrm(ks[21], (N_EXPERTS,), 0.01),
        "exp_w1": nrm(ks[22], (DEPTH, N_EXPERTS, D, D_FF_EXPERT), D ** -0.5),
        "exp_w3": nrm(ks[23], (DEPTH, N_EXPERTS, D, D_FF_EXPERT), D ** -0.5),
        "exp_w2": nrm(ks[24], (DEPTH, N_EXPERTS, D_FF_EXPERT, D), D_FF_EXPERT ** -0.5),
    }


def reference(x, c, ctx, c_ctx, ada_w, ada_b, norm_mix_g, norm_ffn_g, even_w_in, even_w_out,
              na_q_norm_g, na_k_norm_g, na_rpb, mlstm_gate_b, mlstm_norm_g, odd_w_in, odd_w_out,
              gqa_q_norm_g, gqa_k_norm_g, router_w, router_b, exp_w1, exp_w3, exp_w2):
    B, S, D = x.shape
    L = ctx.shape[1]
    cos, sin = rope_2d_tables(S, HD_C)
    silu_c = jax.nn.silu(c)
    silu_cc = jax.nn.silu(c_ctx)
    x_lat, x_ctx = x, ctx
    for l in range(DEPTH):
        last = l == DEPTH - 1
        mod_lat = (silu_c @ ada_w[l] + ada_b[l])[:, None, :]
        mod_ctx = (silu_cc @ ada_w[l] + ada_b[l])[None, None, :]
        sh1, sc1, g1, sh2, sc2, g2 = jnp.split(mod_lat, 6, axis=-1)
        csh1, csc1, cg1, csh2, csc2, cg2 = jnp.split(mod_ctx, 6, axis=-1)

        h_lat = modulate(x_lat, norm_mix_g[l], sh1, sc1)
        h_ctx = modulate(x_ctx, norm_mix_g[l], csh1, csc1)
        if l % 2 == 0:
            i = l // 2
            o_lat, o_ctx = even_mixer(h_lat, h_ctx, even_w_in[i], even_w_out[i], na_q_norm_g[i], na_k_norm_g[i],
                                      na_rpb[i], mlstm_gate_b[i], mlstm_norm_g[i], not last)
        else:
            i = l // 2
            o_lat, o_ctx = odd_mixer(h_lat, h_ctx, odd_w_in[i], odd_w_out[i], gqa_q_norm_g[i], gqa_k_norm_g[i],
                                     cos, sin, not last)
        x_lat = x_lat + g1 * o_lat

        h_lat = modulate(x_lat, norm_ffn_g[l], sh2, sc2).reshape(B * S, D)
        if last:
            y_lat = moe(h_lat, router_w, router_b, exp_w1[l], exp_w3[l], exp_w2[l])
        else:
            x_ctx = x_ctx + cg1 * o_ctx
            h_ctx = modulate(x_ctx, norm_ffn_g[l], csh2, csc2).reshape(B * L, D)
            y = moe(jnp.concatenate([h_lat, h_ctx], axis=0), router_w, router_b, exp_w1[l], exp_w3[l], exp_w2[l])
            y_lat = y[:B * S]
            x_ctx = x_ctx + cg2 * y[B * S:].reshape(B, L, D)
        x_lat = x_lat + g2 * y_lat.reshape(B, S, D)
    return x_lat
```

```python
import functools

import numpy as np
import jax
import jax.numpy as jnp
from jax import lax
from jax.experimental import pallas as pl
from jax.experimental.pallas import tpu as pltpu

F32 = jnp.float32
BF16 = jnp.bfloat16
I32 = jnp.int32
U32 = jnp.uint32

D_MODEL = 1024
SEQ = 2048
GRID_W = 64
GRID_ROWS = SEQ // GRID_W
CTX_LEN = 256
T_ALL = CTX_LEN + SEQ
WIN_H = 8
WIN_W = 16
HD_A = 64
H_A = 8
D_A = H_A * HD_A
H_B = 4
DV_B = 128
DK_B = 64
MLSTM_CHUNK = 64
N_CHUNKS = T_ALL // MLSTM_CHUNK
N_CTX_CHUNKS = CTX_LEN // MLSTM_CHUNK
HD_C = 128
H_C = 8
KV_C = 2
ROPE_THETA = 10000.0
N_EXPERTS = 16
N_GROUPS = 4
EXP_PER_GROUP = 4
D_FF = 512
EPS = 1e-6
NEG_BIG = -1e30

TILE = 256
NT_ALL = T_ALL // TILE
NT_CTX = CTX_LEN // TILE
NA_QROWS = 2
NA_KROWS = WIN_H + NA_QROWS - 1
NA_TQ = NA_QROWS * GRID_W
NA_TK = NA_KROWS * GRID_W
NA_CASES = (0, 2, 4, GRID_ROWS - 4, GRID_ROWS - 2)
GQA_TQ = 256
MOE_RB = 256
SUBLANES = 8
VMEM_LIMIT = 56 * 1024 * 1024


def _cparams(sem):
    return pltpu.CompilerParams(dimension_semantics=sem, vmem_limit_bytes=VMEM_LIMIT)


def _sigmoid(x):
    return 1.0 / (1.0 + jnp.exp(-x))


def _rms(x):
    return x * lax.rsqrt(jnp.mean(x * x, axis=-1, keepdims=True) + EPS)


def _dot(a, b):
    return jnp.dot(a, b, preferred_element_type=F32)


def _dot_nt(a, b):
    return lax.dot_general(a, b, (((1,), (1,)), ((), ())), preferred_element_type=F32)


def _dot_tn(a, b):
    return lax.dot_general(a, b, (((0,), (0,)), ((), ())), preferred_element_type=F32)


def _split_bf16(x, n):
    parts = []
    r = x
    for _ in range(n):
        p = r.astype(BF16)
        parts.append(p)
        r = r - p.astype(F32)
    return parts


def _mod_index(b, j):
    return (jnp.where(j < NT_CTX, N_MOD_CTX_ROW, b), 0, 0)


N_MOD_ROWS = 16
N_MOD_CTX_ROW = 8


ADA_TN = 1536


def _ada_kernel(c_ref, w_ref, b_ref, o_ref):
    c = c_ref[...]
    s = (c * _sigmoid(c)).astype(BF16)
    o_ref[0] = _dot(s, w_ref[0].astype(BF16)) + b_ref[0]


def _ada_call(cvec, ada_w, ada_b):
    depth, d, n = ada_w.shape
    return pl.pallas_call(
        _ada_kernel,
        grid=(depth, n // ADA_TN),
        in_specs=[
            pl.BlockSpec((N_MOD_ROWS, d), lambda l, j: (0, 0)),
            pl.BlockSpec((1, d, ADA_TN), lambda l, j: (l, 0, j)),
            pl.BlockSpec((1, 1, ADA_TN), lambda l, j: (l, 0, j)),
        ],
        out_specs=pl.BlockSpec((1, N_MOD_ROWS, ADA_TN), lambda l, j: (l, 0, j)),
        out_shape=jax.ShapeDtypeStruct((depth, N_MOD_ROWS, n), F32),
        compiler_params=_cparams(("arbitrary", "arbitrary")),
        name="ada_mod",
    )(cvec, ada_w, ada_b.reshape(depth, 1, n))


def _log_sigmoid(x):
    return jnp.minimum(x, 0.0) - jnp.log1p(jnp.exp(-jnp.abs(x)))


def _inproj0_kernel(x_ref, mod_ref, g_ref, wm_ref, wg_ref, wgt_ref, gb_ref, gbt_ref, qg_ref, kg_ref,
                    bd_ref, tril_ref, triu_ref,
                    aq_ref, ak_ref, av_ref, mq_ref, mk_ref, mv_ref, og_ref, gc_ref, gr_ref):
    mod = mod_ref[0]
    h = _rms(x_ref[0]) * g_ref[...] * (1.0 + mod[1:2]) + mod[0:1]
    hb = h.astype(BF16)

    def proj(lo, hi):
        return _dot(hb, wm_ref[:, lo:hi])

    def head_norm(a, gain):
        hi_, lo_ = _split_bf16(a * a, 2)
        ss = _dot(hi_, bd_ref[...]) + _dot(lo_, bd_ref[...])
        return a * lax.rsqrt(ss * (1.0 / HD_A) + EPS) * gain

    aq_ref[0] = (head_norm(proj(0, 512), qg_ref[...]) * HD_A ** -0.5).astype(BF16)
    ak_ref[0] = head_norm(proj(512, 1024), kg_ref[...]).astype(BF16)
    av_ref[0] = proj(1024, 1536).astype(BF16)
    mq_ref[0] = (proj(1536, 1792) * DK_B ** -0.5).astype(BF16)
    mk_ref[0] = proj(1792, 2048).astype(BF16)
    mv_ref[0] = proj(2048, 2560).astype(BF16)
    og_ref[0] = _sigmoid(proj(2560, 3072)).astype(BF16)

    n_g = 4 * H_B
    gcol = _dot(hb, wg_ref[...]) + gb_ref[...]
    grow = _dot_nt(wgt_ref[...], hb) + gbt_ref[...]
    tril = tril_ref[...]
    triu = triu_ref[...]
    ls_c = _split_bf16(_log_sigmoid(gcol), 3)
    ls_r = _split_bf16(_log_sigmoid(grow), 3)
    pre_c = _dot(tril, ls_c[0]) + _dot(tril, ls_c[1]) + _dot(tril, ls_c[2])
    suf_c = _dot(triu, ls_c[0]) + _dot(triu, ls_c[1]) + _dot(triu, ls_c[2])
    pre_r = _dot(ls_r[0], triu) + _dot(ls_r[1], triu) + _dot(ls_r[2], triu)
    suf_r = _dot(ls_r[0], tril) + _dot(ls_r[1], tril) + _dot(ls_r[2], tril)
    cidx = lax.broadcasted_iota(I32, gcol.shape, 1)
    ridx = lax.broadcasted_iota(I32, grow.shape, 0)

    def pick(idx, raw, pre, suf):
        is_ff = (idx >= H_B) & (idx < 2 * H_B)
        is_fb = idx >= 3 * H_B
        return jnp.where(is_ff, pre, jnp.where(is_fb, suf, raw))

    gc_ref[0] = pick(cidx, gcol, pre_c, suf_c)[:, :n_g]
    grow = pick(ridx, grow, pre_r, suf_r)
    for c in range(TILE // MLSTM_CHUNK):
        gr_ref[0, c] = grow[:, c * MLSTM_CHUNK:(c + 1) * MLSTM_CHUNK]


def _inproj0_call(x_all, modl, norm_g, wm, wg, wgt, gb, gbt, qg, kg, bd, tril, triu):
    B = x_all.shape[0]
    n_g = 4 * H_B
    tok = lambda b, j: (b, j, 0)
    const2 = lambda b, j: (0, 0)
    outs = [
        (D_A, BF16), (D_A, BF16), (D_A, BF16),
        (H_B * DK_B, BF16), (H_B * DK_B, BF16), (H_B * DV_B, BF16),
        (H_B * DV_B, BF16),
    ]
    out_shape = [jax.ShapeDtypeStruct((B, T_ALL, w), dt) for w, dt in outs]
    out_specs = [pl.BlockSpec((1, TILE, w), tok) for w, _ in outs]
    out_shape += [jax.ShapeDtypeStruct((B, T_ALL, n_g), F32),
                  jax.ShapeDtypeStruct((B, N_CHUNKS, n_g, MLSTM_CHUNK), F32)]
    out_specs += [pl.BlockSpec((1, TILE, n_g), tok),
                  pl.BlockSpec((1, TILE // MLSTM_CHUNK, n_g, MLSTM_CHUNK), lambda b, j: (b, j, 0, 0))]
    return pl.pallas_call(
        _inproj0_kernel,
        grid=(B, NT_ALL),
        in_specs=[
            pl.BlockSpec((1, TILE, D_MODEL), tok),
            pl.BlockSpec((1, 6, D_MODEL), _mod_index),
            pl.BlockSpec((1, D_MODEL), const2),
            pl.BlockSpec(wm.shape, const2),
            pl.BlockSpec(wg.shape, const2),
            pl.BlockSpec(wgt.shape, const2),
            pl.BlockSpec(gb.shape, const2),
            pl.BlockSpec(gbt.shape, const2),
            pl.BlockSpec(qg.shape, const2),
            pl.BlockSpec(kg.shape, const2),
            pl.BlockSpec(bd.shape, const2),
            pl.BlockSpec(tril.shape, const2),
            pl.BlockSpec(triu.shape, const2),
        ],
        out_specs=out_specs,
        out_shape=out_shape,
        compiler_params=_cparams(("arbitrary", "arbitrary")),
        name="inproj0",
    )(x_all, modl, norm_g, wm, wg, wgt, gb, gbt, qg, kg, bd, tril, triu)


def _mlstm_kernel(mq_ref, mk_ref, mv_ref, gc_ref, gr_ref, og_ref, ng_ref, out_ref,
                  hf_ref, hb_ref, s_ref):
    L = MLSTM_CHUNK
    s_ref[...] = jnp.zeros(s_ref.shape, F32)
    ri = lax.broadcasted_iota(I32, (L, L), 0)
    ci = lax.broadcasted_iota(I32, (L, L), 1)
    masks = (ci <= ri, ci >= ri)
    ones_aug = jnp.ones((L, DV_B), BF16)

    def chain(step, m, d, h):
        if d == 0:
            chunk = step
        else:
            chunk = jnp.where(step < N_CTX_CHUNKS, N_CTX_CHUNKS - 1 - step, N_CHUNKS + N_CTX_CHUNKS - 1 - step)
        r0 = pl.multiple_of(chunk * L, L)
        rows = pl.ds(r0, L)
        q = mq_ref[0, rows, h * DK_B:(h + 1) * DK_B]
        k = mk_ref[0, rows, h * DK_B:(h + 1) * DK_B]
        v = mv_ref[0, rows, h * DV_B:(h + 1) * DV_B]
        gi = 2 * H_B * d + h
        gcc = gc_ref[0, rows, :]
        grr = gr_ref[0, chunk]
        ig_col = gcc[:, gi:gi + 1]
        b_col = gcc[:, gi + H_B:gi + H_B + 1]
        ig_row = grr[gi:gi + 1, :]
        b_row = grr[gi + H_B:gi + H_B + 1, :]
        dm = jnp.where(masks[d], b_col - b_row + ig_row, NEG_BIG)
        a = b_col + m
        m_row = jnp.maximum(a, jnp.max(dm, axis=-1, keepdims=True))
        w_intra = jnp.exp(dm - m_row)
        w_inter = jnp.exp(a - m_row)
        qk = _dot_nt(q, k) * w_intra
        sidx = d * H_B + h
        state = s_ref[sidx]
        v_aug = jnp.concatenate([v, ones_aug], axis=1)
        numden = w_inter * _dot(q, state.astype(BF16)) + _dot(qk.astype(BF16), v_aug)
        num = numden[:, :DV_B]
        den = numden[:, DV_B:]
        hout = num / jnp.maximum(jnp.abs(den), jnp.exp(-m_row))
        dst = hf_ref if d == 0 else hb_ref
        dst[rows, h * DV_B:(h + 1) * DV_B] = hout
        b_last = b_col[L - 1:L, :] if d == 0 else b_col[0:1, :]
        g = b_last - b_col + ig_col
        m_new = jnp.maximum(b_last + m, jnp.max(g, axis=0, keepdims=True))
        w_old = jnp.exp(b_last + m - m_new)
        w_tok = jnp.exp(g - m_new)
        wv = (w_tok * v_aug.astype(F32)).astype(BF16)
        s_ref[sidx] = w_old * state + _dot_tn(k, wv)
        return m_new

    def step_fn(step, ms):
        return tuple(chain(step, ms[d * H_B + h], d, h) for d in range(2) for h in range(H_B))

    m0 = tuple(jnp.full((1, 1), NEG_BIG, F32) for _ in range(2 * H_B))
    lax.fori_loop(0, N_CHUNKS, step_fn, m0)

    def finish(i, carry):
        rows = pl.ds(pl.multiple_of(i * TILE, TILE), TILE)
        hs = hf_ref[rows, :] + hb_ref[rows, :]
        ng = ng_ref[...]
        og = og_ref[0, rows, :].astype(F32)
        for h in range(H_B):
            sl = slice(h * DV_B, (h + 1) * DV_B)
            out_ref[0, rows, sl] = (_rms(hs[:, sl]) * ng[:, sl] * og[:, sl]).astype(BF16)
        return carry

    lax.fori_loop(0, T_ALL // TILE, finish, 0)


def _mlstm_call(mq, mk, mv, gc, gr, og, ng):
    B = mq.shape[0]
    full = lambda a: pl.BlockSpec((1,) + a.shape[1:], lambda b: (b,) + (0,) * (a.ndim - 1))
    return pl.pallas_call(
        _mlstm_kernel,
        grid=(B,),
        in_specs=[full(mq), full(mk), full(mv), full(gc), full(gr), full(og),
                  pl.BlockSpec(ng.shape, lambda b: (0, 0))],
        out_specs=pl.BlockSpec((1, T_ALL, H_B * DV_B), lambda b: (b, 0, 0)),
        out_shape=jax.ShapeDtypeStruct((B, T_ALL, H_B * DV_B), BF16),
        scratch_shapes=[
            pltpu.VMEM((T_ALL, H_B * DV_B), F32),
            pltpu.VMEM((T_ALL, H_B * DV_B), F32),
            pltpu.VMEM((2 * H_B, DK_B, 2 * DV_B), F32),
        ],
        compiler_params=_cparams(("arbitrary",)),
        name="mlstm",
    )(mq, mk, mv, gc, gr, og, ng)


def _na_bias_tables(rpb):
    kh = WIN_H
    qcol = np.arange(GRID_W)
    col_start = np.clip(qcol - WIN_W // 2, 0, GRID_W - WIN_W)
    row_idx = np.zeros((len(NA_CASES), NA_TQ, NA_TK), np.int32)
    col_idx = np.zeros_like(row_idx)
    valid = np.zeros(row_idx.shape, bool)
    for ci, r in enumerate(NA_CASES):
        u0 = int(np.clip(r - kh // 2, 0, GRID_ROWS - NA_KROWS))
        for qi in range(NA_QROWS):
            rq = r + qi
            r0 = int(np.clip(rq - kh // 2, 0, GRID_ROWS - kh))
            for ui in range(NA_KROWS):
                kr = u0 + ui
                row_ok = r0 <= kr < r0 + kh
                drow = kr - rq + (WIN_H - 1)
                dcol = qcol[None, :] - qcol[:, None] + (WIN_W - 1)
                col_ok = (qcol[None, :] >= col_start[:, None]) & (qcol[None, :] < col_start[:, None] + WIN_W)
                qs = slice(qi * GRID_W, (qi + 1) * GRID_W)
                ks = slice(ui * GRID_W, (ui + 1) * GRID_W)
                ok = col_ok & row_ok
                valid[ci, qs, ks] = ok
                row_idx[ci, qs, ks] = drow if row_ok else 0
                col_idx[ci, qs, ks] = np.where(ok, dcol, 0)
    bias = rpb.astype(F32)[:, row_idx, col_idx]
    bias = jnp.where(valid[None], bias, NEG_BIG)
    return jnp.transpose(bias, (1, 0, 2, 3))


def _na_case(j):
    jr = j - 1
    last = GRID_ROWS // NA_QROWS - 1
    return jnp.where(jr <= 1, jnp.maximum(jr, 0), jnp.where(jr >= last - 1, jr - last + 4, 2))


def _na_kernel(q_ref, k_ref, v_ref, bias_ref, o_ref):
    j = pl.program_id(1)
    lane = lax.broadcasted_iota(I32, (1, 2 * HD_A), 1)
    lo_half = lane < HD_A

    def attend(q_rows, n_q, key_sets, bias_for_head):
        for pp in range(H_A // 2):
            lanes = slice(pp * 2 * HD_A, (pp + 1) * 2 * HD_A)
            qp = q_ref[0, q_rows, lanes]
            ks = [k_ref[0, rs, lanes] for rs in key_sets]
            vs = [v_ref[0, rs, lanes] for rs in key_sets]
            halves = []
            for hh in range(2):
                qm = jnp.where(lo_half if hh == 0 else ~lo_half, qp, jnp.zeros_like(qp))
                ss = [_dot_nt(qm, kk) for kk in ks]
                if bias_for_head is not None:
                    ss[0] = ss[0] + bias_for_head(2 * pp + hh)
                m = ss[0].max(axis=-1, keepdims=True)
                for s in ss[1:]:
                    m = jnp.maximum(m, s.max(axis=-1, keepdims=True))
                ps = [jnp.exp(s - m) for s in ss]
                l = ps[0].sum(axis=-1, keepdims=True)
                for p in ps[1:]:
                    l = l + p.sum(axis=-1, keepdims=True)
                acc = _dot(ps[0].astype(BF16), vs[0])
                for p, vv in zip(ps[1:], vs[1:]):
                    acc = acc + _dot(p.astype(BF16), vv)
                halves.append(acc / l)
            o_ref[0, q_rows, lanes] = jnp.where(lo_half, halves[0], halves[1]).astype(BF16)

    ctx_rows = pl.ds(0, CTX_LEN)

    @pl.when(j == 0)
    def _():
        attend(ctx_rows, CTX_LEN, [ctx_rows], None)

    @pl.when(j > 0)
    def _():
        r = (j - 1) * NA_QROWS
        u0 = jnp.clip(r - WIN_H // 2, 0, GRID_ROWS - NA_KROWS)
        q_rows = pl.ds(pl.multiple_of(CTX_LEN + r * GRID_W, NA_TQ), NA_TQ)
        k_rows = pl.ds(pl.multiple_of(CTX_LEN + u0 * GRID_W, GRID_W), NA_TK)
        attend(q_rows, NA_TQ, [k_rows, ctx_rows], lambda head: bias_ref[0, head])


def _na_call(aq, ak, av, bias):
    B = aq.shape[0]
    full = pl.BlockSpec((1, T_ALL, D_A), lambda b, j: (b, 0, 0))
    return pl.pallas_call(
        _na_kernel,
        grid=(B, 1 + GRID_ROWS // NA_QROWS),
        in_specs=[full, full, full,
                  pl.BlockSpec((1, H_A, NA_TQ, NA_TK), lambda b, j: (_na_case(j), 0, 0, 0))],
        out_specs=full,
        out_shape=jax.ShapeDtypeStruct((B, T_ALL, D_A), BF16),
        compiler_params=_cparams(("arbitrary", "arbitrary")),
        name="nbr_attn",
    )(aq, ak, av, bias)


def _route(logits_t, rb_col):
    sc = _sigmoid(logits_t)
    sel = sc + rb_col
    selr = [sel[e:e + 1, :] for e in range(N_EXPERTS)]
    scr = [sc[e:e + 1, :] for e in range(N_EXPERTS)]
    gscore = []
    for g in range(N_GROUPS):
        a, b, c, d = selr[EXP_PER_GROUP * g:EXP_PER_GROUP * (g + 1)]
        s1, t1 = jnp.maximum(a, b), jnp.minimum(a, b)
        s2, t2 = jnp.maximum(c, d), jnp.minimum(c, d)
        gscore.append(jnp.maximum(s1, s2) + jnp.maximum(jnp.minimum(s1, s2), jnp.maximum(t1, t2)))
    best = gscore[0]
    gi = jnp.zeros(best.shape, I32)
    for g in range(1, N_GROUPS):
        better = gscore[g] > best
        gi = jnp.where(better, g, gi)
        best = jnp.where(better, gscore[g], best)
    vs, ws = [], []
    for k in range(EXP_PER_GROUP):
        v = selr[k]
        w = scr[k]
        for g in range(1, N_GROUPS):
            v = jnp.where(gi == g, selr[EXP_PER_GROUP * g + k], v)
            w = jnp.where(gi == g, scr[EXP_PER_GROUP * g + k], w)
        vs.append(v)
        ws.append(w)
    b1, i1 = vs[0], jnp.zeros(best.shape, I32)
    for k in range(1, EXP_PER_GROUP):
        better = vs[k] > b1
        i1 = jnp.where(better, k, i1)
        b1 = jnp.where(better, vs[k], b1)
    b2 = jnp.full(best.shape, -jnp.inf, F32)
    i2 = jnp.zeros(best.shape, I32)
    for k in range(EXP_PER_GROUP):
        vk = jnp.where(i1 == k, -jnp.inf, vs[k])
        better = vk > b2
        i2 = jnp.where(better, k, i2)
        b2 = jnp.where(better, vk, b2)
    w1 = ws[0]
    w2 = ws[0]
    for k in range(1, EXP_PER_GROUP):
        w1 = jnp.where(i1 == k, ws[k], w1)
        w2 = jnp.where(i2 == k, ws[k], w2)
    tot = w1 + w2
    return gi * EXP_PER_GROUP + i1, gi * EXP_PER_GROUP + i2, w1 / tot, w2 / tot


def _outproj_kernel(n_act, x_off, *refs):
    acts = refs[:n_act]
    ws = refs[n_act:2 * n_act]
    (x_ref, mod_ref, g_ref, rwh_ref, rwl_ref, rb_ref, su_ref,
     xo_ref, hp_ref, ri_ref, rw_ref, cnt_ref, carry_ref) = refs[2 * n_act:]
    j = pl.program_id(1)
    mod = mod_ref[0]
    o = _dot(acts[0][0], ws[0][...])
    for a, w in zip(acts[1:], ws[1:]):
        o = o + _dot(a[0], w[...])
    x = x_ref[0] + mod[2:3] * o
    xo_ref[0] = x
    h = _rms(x) * g_ref[...] * (1.0 + mod[4:5]) + mod[3:4]

    half = D_MODEL // 2
    hi_bits = lax.bitcast_convert_type(h[:, :half].astype(BF16).astype(F32), U32)
    lo_bits = lax.bitcast_convert_type(h[:, half:].astype(BF16).astype(F32), U32)
    hp_ref[0] = hi_bits | (lo_bits >> 16)

    h_hi, h_lo = _split_bf16(h, 2)
    logits_t = _dot_nt(rwh_ref[...], h_hi) + _dot_nt(rwh_ref[...], h_lo) + _dot_nt(rwl_ref[...], h_hi)
    e1, e2, w1, w2 = _route(logits_t, rb_ref[...])

    @pl.when(j == 0)
    def _():
        carry_ref[...] = jnp.zeros(carry_ref.shape, F32)

    eidx = lax.broadcasted_iota(I32, logits_t.shape, 0)
    oh1 = eidx == e1
    oh2 = eidx == e2
    onehot = jnp.where(oh1, 1.0, jnp.where(oh2, 1.0, 0.0))
    rank = _dot(onehot.astype(BF16), su_ref[...]) + carry_ref[:, 0:1]
    r1 = jnp.sum(jnp.where(oh1, rank, 0.0), axis=0, keepdims=True)
    r2 = jnp.sum(jnp.where(oh2, rank, 0.0), axis=0, keepdims=True)
    carry = carry_ref[...] + jnp.sum(onehot, axis=1, keepdims=True)
    carry_ref[...] = carry
    cnt_ref[0] = carry.astype(I32)

    zi = jnp.zeros((SUBLANES - 4, TILE), I32)
    ri_ref[0, 0] = jnp.concatenate([e1, e2, r1.astype(I32), r2.astype(I32), zi], axis=0)
    zf = jnp.zeros((SUBLANES - 2, TILE), F32)
    rw_ref[0, 0] = jnp.concatenate([w1, w2, zf], axis=0)


def _outproj_call(acts, ws, x_src, x_off, modl, norm_g, rwh, rwl, rb, su, n_tiles):
    B = x_src.shape[0]
    n_act = len(acts)
    T = n_tiles * TILE
    const2 = lambda b, j: (0, 0)
    in_specs = []
    for a, a_off in acts:
        in_specs.append(pl.BlockSpec((1, TILE, a.shape[2]), functools.partial(lambda b, j, o: (b, j + o, 0), o=a_off)))
    for w in ws:
        in_specs.append(pl.BlockSpec(w.shape, const2))
    in_specs += [
        pl.BlockSpec((1, TILE, D_MODEL), lambda b, j: (b, j + x_off, 0)),
        pl.BlockSpec((1, 6, D_MODEL), lambda b, j: _mod_index(b, j + x_off)),
        pl.BlockSpec((1, D_MODEL), const2),
        pl.BlockSpec(rwh.shape, const2),
        pl.BlockSpec(rwl.shape, const2),
        pl.BlockSpec(rb.shape, const2),
        pl.BlockSpec(su.shape, const2),
    ]
    tok = lambda b, j: (b, j, 0)
    out_shape = [
        jax.ShapeDtypeStruct((B, T, D_MODEL), F32),
        jax.ShapeDtypeStruct((B, T, D_MODEL // 2), U32),
        jax.ShapeDtypeStruct((B, n_tiles, SUBLANES, TILE), I32),
        jax.ShapeDtypeStruct((B, n_tiles, SUBLANES, TILE), F32),
        jax.ShapeDtypeStruct((B, N_EXPERTS, 128), I32),
    ]
    out_specs = [
        pl.BlockSpec((1, TILE, D_MODEL), tok),
        pl.BlockSpec((1, TILE, D_MODEL // 2), tok),
        pl.BlockSpec((1, 1, SUBLANES, TILE), lambda b, j: (b, j, 0, 0)),
        pl.BlockSpec((1, 1, SUBLANES, TILE), lambda b, j: (b, j, 0, 0)),
        pl.BlockSpec((1, N_EXPERTS, 128), lambda b, j: (b, 0, 0)),
    ]
    return pl.pallas_call(
        functools.partial(_outproj_kernel, n_act, x_off),
        grid=(B, n_tiles),
        in_specs=in_specs,
        out_specs=out_specs,
        out_shape=out_shape,
        scratch_shapes=[pltpu.VMEM((N_EXPERTS, 128), F32)],
        compiler_params=_cparams(("arbitrary", "arbitrary")),
        name="outproj_route",
    )(*[a for a, _ in acts], *ws, x_src, modl, norm_g, rwh, rwl, rb, su)


def _moe_kernel(T, cnt_ref, ri_ref, rw_ref, hp_ref, w1_ref, w3_ref, w2_ref, y_ref,
                xs_ref, ob_ref, tokl_ref, wl_ref, off_ref):
    b = pl.program_id(0)
    e = pl.program_id(1)
    half = D_MODEL // 2

    @pl.when(e == 0)
    def _():
        off_ref[0] = 0
        for i in range(N_EXPERTS):
            c = cnt_ref[b, i]
            off_ref[i + 1] = off_ref[i] + ((c + SUBLANES - 1) // SUBLANES) * SUBLANES
        y_ref[...] = jnp.zeros(y_ref.shape, F32)
        xs_ref[...] = jnp.zeros(xs_ref.shape, U32)

        def place(t, carry):
            p1 = off_ref[ri_ref[0, 0, t]] + ri_ref[0, 2, t]
            p2 = off_ref[ri_ref[0, 1, t]] + ri_ref[0, 3, t]
            tokl_ref[p1] = t
            tokl_ref[p2] = t
            wl_ref[p1] = rw_ref[0, 0, t]
            wl_ref[p2] = rw_ref[0, 1, t]
            row = hp_ref[0, pl.ds(t, 1), :]
            xs_ref[pl.ds(p1, 1), :] = row
            xs_ref[pl.ds(p2, 1), :] = row
            return carry

        lax.fori_loop(0, T, place, 0)

    c = cnt_ref[b, e]
    base = off_ref[e]

    def block(rb, carry):
        r0 = pl.multiple_of(base + rb * MOE_RB, SUBLANES)
        words = xs_ref[pl.ds(r0, MOE_RB), :]
        xa = lax.bitcast_convert_type(words & jnp.uint32(0xFFFF0000), F32).astype(BF16)
        xb = lax.bitcast_convert_type(words << 16, F32).astype(BF16)
        h1 = _dot(xa, w1_ref[0, :half, :]) + _dot(xb, w1_ref[0, half:, :])
        h3 = _dot(xa, w3_ref[0, :half, :]) + _dot(xb, w3_ref[0, half:, :])
        act = (h1 * _sigmoid(h1)) * h3
        ob_ref[...] = _dot(act.astype(BF16), w2_ref[0])

        def scatter(i, carry2):
            p = r0 + i
            tok = tokl_ref[p]
            y_ref[0, pl.ds(tok, 1), :] = y_ref[0, pl.ds(tok, 1), :] + wl_ref[p] * ob_ref[pl.ds(i, 1), :]
            return carry2

        lax.fori_loop(0, jnp.minimum(MOE_RB, c - rb * MOE_RB), scatter, 0)
        return carry

    lax.fori_loop(0, (c + MOE_RB - 1) // MOE_RB, block, 0)


def _moe_call(cnt, ri, rw, hp, w1, w3, w2):
    B, T, _ = hp.shape
    n_rows = 2 * T + N_EXPERTS * SUBLANES + MOE_RB
    smem = functools.partial(pl.BlockSpec, memory_space=pltpu.SMEM)
    return pl.pallas_call(
        functools.partial(_moe_kernel, T),
        grid=(B, N_EXPERTS),
        in_specs=[
            smem(cnt.shape, lambda b, e: (0, 0)),
            smem((1, 4, T), lambda b, e: (b, 0, 0)),
            smem((1, 2, T), lambda b, e: (b, 0, 0)),
            pl.BlockSpec((1, T, D_MODEL // 2), lambda b, e: (b, 0, 0)),
            pl.BlockSpec((1, D_MODEL, D_FF), lambda b, e: (e, 0, 0)),
            pl.BlockSpec((1, D_MODEL, D_FF), lambda b, e: (e, 0, 0)),
            pl.BlockSpec((1, D_FF, D_MODEL), lambda b, e: (e, 0, 0)),
        ],
        out_specs=pl.BlockSpec((1, T, D_MODEL), lambda b, e: (b, 0, 0)),
        out_shape=jax.ShapeDtypeStruct((B, T, D_MODEL), F32),
        scratch_shapes=[
            pltpu.VMEM((n_rows, D_MODEL // 2), U32),
            pltpu.VMEM((MOE_RB, D_MODEL), F32),
            pltpu.SMEM((n_rows,), I32),
            pltpu.SMEM((n_rows,), F32),
            pltpu.SMEM((N_EXPERTS + 1,), I32),
        ],
        compiler_params=_cparams(("arbitrary", "arbitrary")),
        name="moe",
    )(cnt, ri, rw, hp, w1, w3, w2)


def _route_tables(ri, rw, cnt):
    B, nt = ri.shape[:2]
    ri = jnp.transpose(ri[:, :, :4], (0, 2, 1, 3)).reshape(B, 4, nt * TILE)
    rw = jnp.transpose(rw[:, :, :2], (0, 2, 1, 3)).reshape(B, 2, nt * TILE)
    return cnt[:, :, 0], ri, rw


def _inproj1_kernel(x_ref, y_ref, mod0_ref, mod_ref, g_ref, w_ref, qg_ref, kg_ref, cos_ref, sin_ref,
                    xo_ref, q_ref, k_ref, v_ref):
    x = x_ref[0] + mod0_ref[0][5:6] * y_ref[0]
    xo_ref[0] = x
    mod = mod_ref[0]
    hb = (_rms(x) * g_ref[...] * (1.0 + mod[1:2]) + mod[0:1]).astype(BF16)
    cos = cos_ref[...]
    sin = sin_ref[...]

    def rope_head(a, gain):
        n = _rms(a) * gain
        return n * cos + pltpu.roll(n, HD_C // 2, 1) * sin

    qg = qg_ref[...]
    kg = kg_ref[...]
    for h in range(H_C):
        sl = slice(h * HD_C, (h + 1) * HD_C)
        q_ref[0, :, sl] = (rope_head(_dot(hb, w_ref[:, sl]), qg) * HD_C ** -0.5).astype(BF16)
    for h in range(KV_C):
        sl = slice(h * HD_C, (h + 1) * HD_C)
        ko = H_C * HD_C
        k_ref[0, :, sl] = rope_head(_dot(hb, w_ref[:, ko + h * HD_C:ko + (h + 1) * HD_C]), kg).astype(BF16)
    vo = (H_C + KV_C) * HD_C
    v_ref[0] = _dot(hb, w_ref[:, vo:vo + KV_C * HD_C]).astype(BF16)


def _inproj1_call(x1, y0, mod0, mod1, norm_g, w, qg, kg, cos, sin):
    B = x1.shape[0]
    tok = lambda b, j: (b, j, 0)
    const2 = lambda b, j: (0, 0)
    widths = [(D_MODEL, F32), (H_C * HD_C, BF16), (KV_C * HD_C, BF16), (KV_C * HD_C, BF16)]
    return pl.pallas_call(
        _inproj1_kernel,
        grid=(B, NT_ALL),
        in_specs=[
            pl.BlockSpec((1, TILE, D_MODEL), tok),
            pl.BlockSpec((1, TILE, D_MODEL), tok),
            pl.BlockSpec((1, 6, D_MODEL), _mod_index),
            pl.BlockSpec((1, 6, D_MODEL), _mod_index),
            pl.BlockSpec((1, D_MODEL), const2),
            pl.BlockSpec(w.shape, const2),
            pl.BlockSpec(qg.shape, const2),
            pl.BlockSpec(kg.shape, const2),
            pl.BlockSpec((TILE, HD_C), lambda b, j: (j, 0)),
            pl.BlockSpec((TILE, HD_C), lambda b, j: (j, 0)),
        ],
        out_specs=[pl.BlockSpec((1, TILE, w_), tok) for w_, _ in widths],
        out_shape=[jax.ShapeDtypeStruct((B, T_ALL, w_), dt) for w_, dt in widths],
        compiler_params=_cparams(("arbitrary", "arbitrary")),
        name="inproj1",
    )(x1, y0, mod0, mod1, norm_g, w, qg, kg, cos, sin)


def _gqa_kernel(q_ref, k_ref, v_ref, o_ref):
    k = k_ref[0]
    v = v_ref[0]
    for h in range(H_C // KV_C):
        sl = slice(h * HD_C, (h + 1) * HD_C)
        s = _dot_nt(q_ref[0, :, sl], k)
        p = jnp.exp(s - s.max(axis=-1, keepdims=True))
        l = p.sum(axis=-1, keepdims=True)
        o_ref[0, :, sl] = (_dot(p.astype(BF16), v) / l).astype(BF16)


def _gqa_call(q, k, v):
    B = q.shape[0]
    gw = (H_C // KV_C) * HD_C
    q_off = CTX_LEN // GQA_TQ
    return pl.pallas_call(
        _gqa_kernel,
        grid=(B, KV_C, SEQ // GQA_TQ),
        in_specs=[
            pl.BlockSpec((1, GQA_TQ, gw), lambda b, g, j: (b, j + q_off, g)),
            pl.BlockSpec((1, T_ALL, HD_C), lambda b, g, j: (b, 0, g)),
            pl.BlockSpec((1, T_ALL, HD_C), lambda b, g, j: (b, 0, g)),
        ],
        out_specs=pl.BlockSpec((1, GQA_TQ, gw), lambda b, g, j: (b, j, g)),
        out_shape=jax.ShapeDtypeStruct((B, SEQ, H_C * HD_C), BF16),
        compiler_params=_cparams(("arbitrary", "arbitrary", "arbitrary")),
        name="gqa",
    )(q, k, v)


def _final_kernel(x_ref, y_ref, mod_ref, o_ref):
    o_ref[0] = x_ref[0] + mod_ref[0][5:6] * y_ref[0]


def _final_call(x, y, modl):
    B, T, D = x.shape
    tok = lambda b, j: (b, j, 0)
    return pl.pallas_call(
        _final_kernel,
        grid=(B, T // TILE),
        in_specs=[pl.BlockSpec((1, TILE, D), tok), pl.BlockSpec((1, TILE, D), tok),
                  pl.BlockSpec((1, 6, D), lambda b, j: (b, 0, 0))],
        out_specs=pl.BlockSpec((1, TILE, D), tok),
        out_shape=jax.ShapeDtypeStruct((B, T, D), F32),
        compiler_params=_cparams(("arbitrary", "arbitrary")),
        name="final_residual",
    )(x, y, modl)


def _chunk_tri(lower):
    i = np.arange(TILE)
    same = (i[:, None] // MLSTM_CHUNK) == (i[None, :] // MLSTM_CHUNK)
    tri = (i[None, :] <= i[:, None]) if lower else (i[None, :] >= i[:, None])
    return jnp.asarray((same & tri).astype(np.float32), BF16)


def _rope_tables():
    n_freq = HD_C // 4
    inv_freq = ROPE_THETA ** (-jnp.arange(n_freq, dtype=F32) / n_freq)
    t = jnp.arange(SEQ)
    rows = (t // GRID_W).astype(F32)
    cols = (t % GRID_W).astype(F32)
    ang = jnp.concatenate([rows[:, None] * inv_freq, cols[:, None] * inv_freq], axis=-1)
    cos, sin = jnp.cos(ang), jnp.sin(ang)
    cos_l = jnp.concatenate([cos, cos], axis=-1)
    sin_l = jnp.concatenate([-sin, sin], axis=-1)
    cos_all = jnp.concatenate([jnp.ones((CTX_LEN, HD_C), F32), cos_l], axis=0)
    sin_all = jnp.concatenate([jnp.zeros((CTX_LEN, HD_C), F32), sin_l], axis=0)
    return cos_all, sin_all


_HEAD_PERM = np.concatenate([np.arange(0, HD_C, 2), np.arange(1, HD_C, 2)])


def kernel(x, c, ctx, c_ctx, ada_w, ada_b, norm_mix_g, norm_ffn_g, even_w_in, even_w_out,
           na_q_norm_g, na_k_norm_g, na_rpb, mlstm_gate_b, mlstm_norm_g, odd_w_in, odd_w_out,
           gqa_q_norm_g, gqa_k_norm_g, router_w, router_b, exp_w1, exp_w3, exp_w2):
    B = x.shape[0]
    assert B <= N_MOD_CTX_ROW and x.shape[1:] == (SEQ, D_MODEL) and ctx.shape[1:] == (CTX_LEN, D_MODEL)
    n_g = 4 * H_B

    cvec = jnp.zeros((N_MOD_ROWS, D_MODEL), F32).at[:B].set(c).at[N_MOD_CTX_ROW].set(c_ctx)
    mod = _ada_call(cvec, ada_w, ada_b).reshape(2, N_MOD_ROWS, 6, D_MODEL)
    mod0, mod1 = mod[0], mod[1]

    x_all = jnp.concatenate([ctx, x], axis=1)

    w_in = even_w_in[0]
    n_main = w_in.shape[1] - n_g
    wm = w_in[:, :n_main].astype(BF16)
    wg_f = w_in[:, n_main:]
    wg = jnp.pad(wg_f, ((0, 0), (0, 128 - n_g))).astype(BF16)
    wgt = wg_f.T.astype(BF16)
    gb = jnp.pad(mlstm_gate_b[0].reshape(1, n_g), ((0, 0), (0, 128 - n_g)))
    gbt = mlstm_gate_b[0].reshape(n_g, 1)
    qg = jnp.tile(na_q_norm_g[0], H_A).reshape(1, D_A)
    kg = jnp.tile(na_k_norm_g[0], H_A).reshape(1, D_A)
    hid = np.arange(D_A) // HD_A
    bd = jnp.asarray((hid[:, None] == hid[None, :]).astype(np.float32), BF16)
    aq, ak, av, mq, mk, mv, og, gc, gr = _inproj0_call(
        x_all, mod0, norm_mix_g[0].reshape(1, D_MODEL), wm, wg, wgt, gb, gbt, qg, kg, bd,
        _chunk_tri(True), _chunk_tri(False))
    hm = _mlstm_call(mq, mk, mv, gc, gr, og, mlstm_norm_g[0].reshape(1, H_B * DV_B))
    oa = _na_call(aq, ak, av, _na_bias_tables(na_rpb[0]))

    rw_t = router_w.T
    rwh = rw_t.astype(BF16)
    rwl = (rw_t - rwh.astype(F32)).astype(BF16)
    rb = router_b.reshape(N_EXPERTS, 1).astype(F32)
    i = np.arange(TILE)
    su = jnp.asarray((i[:, None] < i[None, :]).astype(np.float32), BF16)
    w_out = even_w_out[0].astype(BF16)
    x1, hp0, ri0, rw0, cnt0 = _outproj_call(
        [(oa, 0), (hm, 0)], [w_out[:D_A], w_out[D_A:]], x_all, 0, mod0,
        norm_ffn_g[0].reshape(1, D_MODEL), rwh, rwl, rb, su, NT_ALL)
    y0 = _moe_call(*_route_tables(ri0, rw0, cnt0), hp0,
                   exp_w1[0].astype(BF16), exp_w3[0].astype(BF16), exp_w2[0].astype(BF16))

    w1_in = odd_w_in[0]
    qk_cols = np.concatenate([h * HD_C + _HEAD_PERM for h in range(H_C + KV_C)])
    cols = np.concatenate([qk_cols, np.arange((H_C + KV_C) * HD_C, w1_in.shape[1])])
    w1_in = w1_in[:, cols].astype(BF16)
    cos_all, sin_all = _rope_tables()
    x2, q, k, v = _inproj1_call(
        x1, y0, mod0, mod1, norm_mix_g[1].reshape(1, D_MODEL), w1_in,
        gqa_q_norm_g[0][_HEAD_PERM].reshape(1, HD_C), gqa_k_norm_g[0][_HEAD_PERM].reshape(1, HD_C),
        cos_all, sin_all)
    o = _gqa_call(q, k, v)
    x3, hp1, ri1, rw1, cnt1 = _outproj_call(
        [(o, 0)], [odd_w_out[0].astype(BF16)], x2, NT_CTX, mod1,
        norm_ffn_g[1].reshape(1, D_MODEL), rwh, rwl, rb, su, SEQ // TILE)
    y1 = _moe_call(*_route_tables(ri1, rw1, cnt1), hp1,
                   exp_w1[1].astype(BF16), exp_w3[1].astype(BF16), exp_w2[1].astype(BF16))
    return _final_call(x3, y1, mod1)
```

```python
import functools

import numpy as np
import jax
import jax.numpy as jnp
from jax import lax
from jax.experimental import pallas as pl
from jax.experimental.pallas import tpu as pltpu

F32 = jnp.float32
BF16 = jnp.bfloat16
I32 = jnp.int32
U32 = jnp.uint32

D_MODEL = 1024
SEQ = 2048
GRID_W = 64
GRID_ROWS = SEQ // GRID_W
CTX_LEN = 256
T_ALL = CTX_LEN + SEQ
WIN_H = 8
WIN_W = 16
HD_A = 64
H_A = 8
D_A = H_A * HD_A
H_B = 4
DV_B = 128
DK_B = 64
MLSTM_CHUNK = 64
N_CHUNKS = T_ALL // MLSTM_CHUNK
N_CTX_CHUNKS = CTX_LEN // MLSTM_CHUNK
HD_C = 128
H_C = 8
KV_C = 2
ROPE_THETA = 10000.0
N_EXPERTS = 16
N_GROUPS = 4
EXP_PER_GROUP = 4
D_FF = 512
EPS = 1e-6
NEG_BIG = -1e30

TILE = 256
NT_ALL = T_ALL // TILE
NT_CTX = CTX_LEN // TILE
NA_QROWS = 2
NA_KROWS = WIN_H + NA_QROWS - 1
NA_TQ = NA_QROWS * GRID_W
NA_TK = NA_KROWS * GRID_W
NA_CASES = (0, 2, 4, GRID_ROWS - 4, GRID_ROWS - 2)
GQA_TQ = 256
MOE_RB = 128
SUBLANES = 8
VMEM_LIMIT = 56 * 1024 * 1024


def _cparams(sem):
    return pltpu.CompilerParams(dimension_semantics=sem, vmem_limit_bytes=VMEM_LIMIT)


def _sigmoid(x):
    return 1.0 / (1.0 + jnp.exp(-x))


def _rms(x):
    return x * lax.rsqrt(jnp.mean(x * x, axis=-1, keepdims=True) + EPS)


def _dot(a, b):
    return jnp.dot(a, b, preferred_element_type=F32)


def _dot_nt(a, b):
    return lax.dot_general(a, b, (((1,), (1,)), ((), ())), preferred_element_type=F32)


def _dot_tn(a, b):
    return lax.dot_general(a, b, (((0,), (0,)), ((), ())), preferred_element_type=F32)


def _split_bf16(x, n):
    parts = []
    r = x
    for _ in range(n):
        p = r.astype(BF16)
        parts.append(p)
        r = r - p.astype(F32)
    return parts


def _mod_index(b, j):
    return (jnp.where(j < NT_CTX, N_MOD_CTX_ROW, b), 0, 0)


N_MOD_ROWS = 16
N_MOD_CTX_ROW = 8


ADA_TN = 1536


def _ada_kernel(c_ref, w_ref, b_ref, o_ref):
    c = c_ref[...]
    s = (c * _sigmoid(c)).astype(BF16)
    o_ref[0] = _dot(s, w_ref[0].astype(BF16)) + b_ref[0]


def _ada_call(cvec, ada_w, ada_b):
    depth, d, n = ada_w.shape
    return pl.pallas_call(
        _ada_kernel,
        grid=(depth, n // ADA_TN),
        in_specs=[
            pl.BlockSpec((N_MOD_ROWS, d), lambda l, j: (0, 0)),
            pl.BlockSpec((1, d, ADA_TN), lambda l, j: (l, 0, j)),
            pl.BlockSpec((1, 1, ADA_TN), lambda l, j: (l, 0, j)),
        ],
        out_specs=pl.BlockSpec((1, N_MOD_ROWS, ADA_TN), lambda l, j: (l, 0, j)),
        out_shape=jax.ShapeDtypeStruct((depth, N_MOD_ROWS, n), F32),
        compiler_params=_cparams(("arbitrary", "arbitrary")),
        name="ada_mod",
    )(cvec, ada_w, ada_b.reshape(depth, 1, n))


def _log_sigmoid(x):
    return jnp.minimum(x, 0.0) - jnp.log1p(jnp.exp(-jnp.abs(x)))


def _inproj0_kernel(x_ref, mod_ref, g_ref, wm_ref, wg_ref, wgt_ref, gb_ref, gbt_ref, qg_ref, kg_ref,
                    bd_ref, tril_ref, triu_ref,
                    aq_ref, ak_ref, av_ref, mq_ref, mk_ref, mv_ref, og_ref, gc_ref, gr_ref):
    mod = mod_ref[0]
    h = _rms(x_ref[0]) * g_ref[...] * (1.0 + mod[1:2]) + mod[0:1]
    hb = h.astype(BF16)

    def proj(lo, hi):
        return _dot(hb, wm_ref[:, lo:hi])

    def head_norm(a, gain):
        hi_, lo_ = _split_bf16(a * a, 2)
        ss = _dot(hi_, bd_ref[...]) + _dot(lo_, bd_ref[...])
        return a * lax.rsqrt(ss * (1.0 / HD_A) + EPS) * gain

    aq_ref[0] = (head_norm(proj(0, 512), qg_ref[...]) * HD_A ** -0.5).astype(BF16)
    ak_ref[0] = head_norm(proj(512, 1024), kg_ref[...]).astype(BF16)
    av_ref[0] = proj(1024, 1536).astype(BF16)
    mq_ref[0] = (proj(1536, 1792) * DK_B ** -0.5).astype(BF16)
    mk_ref[0] = proj(1792, 2048).astype(BF16)
    mv_ref[0] = proj(2048, 2560).astype(BF16)
    og_ref[0] = _sigmoid(proj(2560, 3072)).astype(BF16)

    n_g = 4 * H_B
    gcol = _dot(hb, wg_ref[...]) + gb_ref[...]
    grow = _dot_nt(wgt_ref[...], hb) + gbt_ref[...]
    tril = tril_ref[...]
    triu = triu_ref[...]
    ls_c = _split_bf16(_log_sigmoid(gcol), 3)
    ls_r = _split_bf16(_log_sigmoid(grow), 3)
    pre_c = _dot(tril, ls_c[0]) + _dot(tril, ls_c[1]) + _dot(tril, ls_c[2])
    suf_c = _dot(triu, ls_c[0]) + _dot(triu, ls_c[1]) + _dot(triu, ls_c[2])
    pre_r = _dot(ls_r[0], triu) + _dot(ls_r[1], triu) + _dot(ls_r[2], triu)
    suf_r = _dot(ls_r[0], tril) + _dot(ls_r[1], tril) + _dot(ls_r[2], tril)
    cidx = lax.broadcasted_iota(I32, gcol.shape, 1)
    ridx = lax.broadcasted_iota(I32, grow.shape, 0)

    def pick(idx, raw, pre, suf):
        is_ff = (idx >= H_B) & (idx < 2 * H_B)
        is_fb = idx >= 3 * H_B
        return jnp.where(is_ff, pre, jnp.where(is_fb, suf, raw))

    gc_ref[0] = pick(cidx, gcol, pre_c, suf_c)[:, :n_g]
    grow = pick(ridx, grow, pre_r, suf_r)
    for c in range(TILE // MLSTM_CHUNK):
        gr_ref[0, c] = grow[:, c * MLSTM_CHUNK:(c + 1) * MLSTM_CHUNK]


def _inproj0_call(x_all, modl, norm_g, wm, wg, wgt, gb, gbt, qg, kg, bd, tril, triu):
    B = x_all.shape[0]
    n_g = 4 * H_B
    tok = lambda b, j: (b, j, 0)
    const2 = lambda b, j: (0, 0)
    outs = [
        (D_A, BF16), (D_A, BF16), (D_A, BF16),
        (H_B * DK_B, BF16), (H_B * DK_B, BF16), (H_B * DV_B, BF16),
        (H_B * DV_B, BF16),
    ]
    out_shape = [jax.ShapeDtypeStruct((B, T_ALL, w), dt) for w, dt in outs]
    out_specs = [pl.BlockSpec((1, TILE, w), tok) for w, _ in outs]
    out_shape += [jax.ShapeDtypeStruct((B, T_ALL, n_g), F32),
                  jax.ShapeDtypeStruct((B, N_CHUNKS, n_g, MLSTM_CHUNK), F32)]
    out_specs += [pl.BlockSpec((1, TILE, n_g), tok),
                  pl.BlockSpec((1, TILE // MLSTM_CHUNK, n_g, MLSTM_CHUNK), lambda b, j: (b, j, 0, 0))]
    return pl.pallas_call(
        _inproj0_kernel,
        grid=(B, NT_ALL),
        in_specs=[
            pl.BlockSpec((1, TILE, D_MODEL), tok),
            pl.BlockSpec((1, 6, D_MODEL), _mod_index),
            pl.BlockSpec((1, D_MODEL), const2),
            pl.BlockSpec(wm.shape, const2),
            pl.BlockSpec(wg.shape, const2),
            pl.BlockSpec(wgt.shape, const2),
            pl.BlockSpec(gb.shape, const2),
            pl.BlockSpec(gbt.shape, const2),
            pl.BlockSpec(qg.shape, const2),
            pl.BlockSpec(kg.shape, const2),
            pl.BlockSpec(bd.shape, const2),
            pl.BlockSpec(tril.shape, const2),
            pl.BlockSpec(triu.shape, const2),
        ],
        out_specs=out_specs,
        out_shape=out_shape,
        compiler_params=_cparams(("arbitrary", "arbitrary")),
        name="inproj0",
    )(x_all, modl, norm_g, wm, wg, wgt, gb, gbt, qg, kg, bd, tril, triu)


def _mlstm_kernel(mq_ref, mk_ref, mv_ref, gc_ref, gr_ref, og_ref, ng_ref, out_ref,
                  hf_ref, hb_ref, s_ref):
    L = MLSTM_CHUNK
    s_ref[...] = jnp.zeros(s_ref.shape, F32)
    ri = lax.broadcasted_iota(I32, (L, L), 0)
    ci = lax.broadcasted_iota(I32, (L, L), 1)
    masks = (ci <= ri, ci >= ri)
    ones_aug = jnp.ones((L, DV_B), BF16)

    def chain(step, m, d, h):
        if d == 0:
            chunk = step
        else:
            chunk = jnp.where(step < N_CTX_CHUNKS, N_CTX_CHUNKS - 1 - step, N_CHUNKS + N_CTX_CHUNKS - 1 - step)
        r0 = pl.multiple_of(chunk * L, L)
        rows = pl.ds(r0, L)
        q = mq_ref[0, rows, h * DK_B:(h + 1) * DK_B]
        k = mk_ref[0, rows, h * DK_B:(h + 1) * DK_B]
        v = mv_ref[0, rows, h * DV_B:(h + 1) * DV_B]
        gi = 2 * H_B * d + h
        gcc = gc_ref[0, rows, :]
        grr = gr_ref[0, chunk]
        ig_col = gcc[:, gi:gi + 1]
        b_col = gcc[:, gi + H_B:gi + H_B + 1]
        ig_row = grr[gi:gi + 1, :]
        b_row = grr[gi + H_B:gi + H_B + 1, :]
        dm = jnp.where(masks[d], b_col - b_row + ig_row, NEG_BIG)
        a = b_col + m
        m_row = jnp.maximum(a, jnp.max(dm, axis=-1, keepdims=True))
        w_intra = jnp.exp(dm - m_row)
        w_inter = jnp.exp(a - m_row)
        qk = _dot_nt(q, k) * w_intra
        sidx = d * H_B + h
        state = s_ref[sidx]
        v_aug = jnp.concatenate([v, ones_aug], axis=1)
        numden = w_inter * _dot(q, state.astype(BF16)) + _dot(qk.astype(BF16), v_aug)
        num = numden[:, :DV_B]
        den = numden[:, DV_B:]
        hout = num / jnp.maximum(jnp.abs(den), jnp.exp(-m_row))
        dst = hf_ref if d == 0 else hb_ref
        dst[rows, h * DV_B:(h + 1) * DV_B] = hout
        b_last = b_col[L - 1:L, :] if d == 0 else b_col[0:1, :]
        g = b_last - b_col + ig_col
        m_new = jnp.maximum(b_last + m, jnp.max(g, axis=0, keepdims=True))
        w_old = jnp.exp(b_last + m - m_new)
        w_tok = jnp.exp(g - m_new)
        wv = (w_tok * v_aug.astype(F32)).astype(BF16)
        s_ref[sidx] = w_old * state + _dot_tn(k, wv)
        return m_new

    def step_fn(step, ms):
        return tuple(chain(step, ms[d * H_B + h], d, h) for d in range(2) for h in range(H_B))

    m0 = tuple(jnp.full((1, 1), NEG_BIG, F32) for _ in range(2 * H_B))
    lax.fori_loop(0, N_CHUNKS, step_fn, m0)

    def finish(i, carry):
        rows = pl.ds(pl.multiple_of(i * TILE, TILE), TILE)
        hs = hf_ref[rows, :] + hb_ref[rows, :]
        ng = ng_ref[...]
        og = og_ref[0, rows, :].astype(F32)
        for h in range(H_B):
            sl = slice(h * DV_B, (h + 1) * DV_B)
            out_ref[0, rows, sl] = (_rms(hs[:, sl]) * ng[:, sl] * og[:, sl]).astype(BF16)
        return carry

    lax.fori_loop(0, T_ALL // TILE, finish, 0)


def _mlstm_call(mq, mk, mv, gc, gr, og, ng):
    B = mq.shape[0]
    full = lambda a: pl.BlockSpec((1,) + a.shape[1:], lambda b: (b,) + (0,) * (a.ndim - 1))
    return pl.pallas_call(
        _mlstm_kernel,
        grid=(B,),
        in_specs=[full(mq), full(mk), full(mv), full(gc), full(gr), full(og),
                  pl.BlockSpec(ng.shape, lambda b: (0, 0))],
        out_specs=pl.BlockSpec((1, T_ALL, H_B * DV_B), lambda b: (b, 0, 0)),
        out_shape=jax.ShapeDtypeStruct((B, T_ALL, H_B * DV_B), BF16),
        scratch_shapes=[
            pltpu.VMEM((T_ALL, H_B * DV_B), F32),
            pltpu.VMEM((T_ALL, H_B * DV_B), F32),
            pltpu.VMEM((2 * H_B, DK_B, 2 * DV_B), F32),
        ],
        compiler_params=_cparams(("arbitrary",)),
        name="mlstm",
    )(mq, mk, mv, gc, gr, og, ng)


def _na_bias_tables(rpb):
    kh = WIN_H
    n_drow = 2 * WIN_H - 1
    qcol = np.arange(GRID_W)
    col_start = np.clip(qcol - WIN_W // 2, 0, GRID_W - WIN_W)
    col_ok = (qcol[None, :] >= col_start[:, None]) & (qcol[None, :] < col_start[:, None] + WIN_W)
    dcol = qcol[None, :] - qcol[:, None] + (WIN_W - 1)
    onehot = (dcol[None] == np.arange(2 * WIN_W - 1)[:, None, None]) & col_ok[None]
    blocks = jnp.einsum('hdx,xck->hdck', rpb.astype(F32), jnp.asarray(onehot, F32),
                        precision=lax.Precision.HIGHEST)
    blocks = jnp.where(col_ok[None, None], blocks, NEG_BIG)
    outside = jnp.full((H_A, 1, GRID_W, GRID_W), NEG_BIG, F32)
    blocks = jnp.concatenate([blocks, outside], axis=1)
    idx = np.full((len(NA_CASES), NA_QROWS, NA_KROWS), n_drow, np.int32)
    for ci, r in enumerate(NA_CASES):
        u0 = int(np.clip(r - kh // 2, 0, GRID_ROWS - NA_KROWS))
        for qi in range(NA_QROWS):
            rq = r + qi
            r0 = int(np.clip(rq - kh // 2, 0, GRID_ROWS - kh))
            for ui in range(NA_KROWS):
                kr = u0 + ui
                if r0 <= kr < r0 + kh:
                    idx[ci, qi, ui] = kr - rq + (WIN_H - 1)
    tab = blocks[:, idx]
    tab = jnp.transpose(tab, (1, 0, 2, 4, 3, 5))
    return tab.reshape(len(NA_CASES), H_A, NA_TQ, NA_TK)


def _na_case(j):
    jr = j - 1
    last = GRID_ROWS // NA_QROWS - 1
    return jnp.where(jr <= 1, jnp.maximum(jr, 0), jnp.where(jr >= last - 1, jr - last + 4, 2))


def _na_kernel(q_ref, k_ref, v_ref, bias_ref, o_ref):
    j = pl.program_id(1)
    lane = lax.broadcasted_iota(I32, (1, 2 * HD_A), 1)
    lo_half = lane < HD_A

    def attend(q_rows, n_q, key_sets, bias_for_head):
        for pp in range(H_A // 2):
            lanes = slice(pp * 2 * HD_A, (pp + 1) * 2 * HD_A)
            qp = q_ref[0, q_rows, lanes]
            ks = [k_ref[0, rs, lanes] for rs in key_sets]
            vs = [v_ref[0, rs, lanes] for rs in key_sets]
            halves = []
            for hh in range(2):
                qm = jnp.where(lo_half if hh == 0 else ~lo_half, qp, jnp.zeros_like(qp))
                ss = [_dot_nt(qm, kk) for kk in ks]
                if bias_for_head is not None:
                    ss[0] = ss[0] + bias_for_head(2 * pp + hh)
                m = ss[0].max(axis=-1, keepdims=True)
                for s in ss[1:]:
                    m = jnp.maximum(m, s.max(axis=-1, keepdims=True))
                ps = [jnp.exp(s - m) for s in ss]
                l = ps[0].sum(axis=-1, keepdims=True)
                for p in ps[1:]:
                    l = l + p.sum(axis=-1, keepdims=True)
                acc = _dot(ps[0].astype(BF16), vs[0])
                for p, vv in zip(ps[1:], vs[1:]):
                    acc = acc + _dot(p.astype(BF16), vv)
                halves.append(acc / l)
            o_ref[0, q_rows, lanes] = jnp.where(lo_half, halves[0], halves[1]).astype(BF16)

    ctx_rows = pl.ds(0, CTX_LEN)

    @pl.when(j == 0)
    def _():
        attend(ctx_rows, CTX_LEN, [ctx_rows], None)

    @pl.when(j > 0)
    def _():
        r = (j - 1) * NA_QROWS
        u0 = jnp.clip(r - WIN_H // 2, 0, GRID_ROWS - NA_KROWS)
        q_rows = pl.ds(pl.multiple_of(CTX_LEN + r * GRID_W, NA_TQ), NA_TQ)
        k_rows = pl.ds(pl.multiple_of(CTX_LEN + u0 * GRID_W, GRID_W), NA_TK)
        attend(q_rows, NA_TQ, [k_rows, ctx_rows], lambda head: bias_ref[0, head])


def _na_call(aq, ak, av, bias):
    B = aq.shape[0]
    full = pl.BlockSpec((1, T_ALL, D_A), lambda b, j: (b, 0, 0))
    return pl.pallas_call(
        _na_kernel,
        grid=(B, 1 + GRID_ROWS // NA_QROWS),
        in_specs=[full, full, full,
                  pl.BlockSpec((1, H_A, NA_TQ, NA_TK), lambda b, j: (_na_case(j), 0, 0, 0))],
        out_specs=full,
        out_shape=jax.ShapeDtypeStruct((B, T_ALL, D_A), BF16),
        compiler_params=_cparams(("arbitrary", "arbitrary")),
        name="nbr_attn",
    )(aq, ak, av, bias)


def _route(logits_t, rb_col):
    sc = _sigmoid(logits_t)
    sel = sc + rb_col
    selr = [sel[e:e + 1, :] for e in range(N_EXPERTS)]
    scr = [sc[e:e + 1, :] for e in range(N_EXPERTS)]
    gscore = []
    for g in range(N_GROUPS):
        a, b, c, d = selr[EXP_PER_GROUP * g:EXP_PER_GROUP * (g + 1)]
        s1, t1 = jnp.maximum(a, b), jnp.minimum(a, b)
        s2, t2 = jnp.maximum(c, d), jnp.minimum(c, d)
        gscore.append(jnp.maximum(s1, s2) + jnp.maximum(jnp.minimum(s1, s2), jnp.maximum(t1, t2)))
    best = gscore[0]
    gi = jnp.zeros(best.shape, I32)
    for g in range(1, N_GROUPS):
        better = gscore[g] > best
        gi = jnp.where(better, g, gi)
        best = jnp.where(better, gscore[g], best)
    vs, ws = [], []
    for k in range(EXP_PER_GROUP):
        v = selr[k]
        w = scr[k]
        for g in range(1, N_GROUPS):
            v = jnp.where(gi == g, selr[EXP_PER_GROUP * g + k], v)
            w = jnp.where(gi == g, scr[EXP_PER_GROUP * g + k], w)
        vs.append(v)
        ws.append(w)
    b1, i1 = vs[0], jnp.zeros(best.shape, I32)
    for k in range(1, EXP_PER_GROUP):
        better = vs[k] > b1
        i1 = jnp.where(better, k, i1)
        b1 = jnp.where(better, vs[k], b1)
    b2 = jnp.full(best.shape, -jnp.inf, F32)
    i2 = jnp.zeros(best.shape, I32)
    for k in range(EXP_PER_GROUP):
        vk = jnp.where(i1 == k, -jnp.inf, vs[k])
        better = vk > b2
        i2 = jnp.where(better, k, i2)
        b2 = jnp.where(better, vk, b2)
    w1 = ws[0]
    w2 = ws[0]
    for k in range(1, EXP_PER_GROUP):
        w1 = jnp.where(i1 == k, ws[k], w1)
        w2 = jnp.where(i2 == k, ws[k], w2)
    tot = w1 + w2
    return gi * EXP_PER_GROUP + i1, gi * EXP_PER_GROUP + i2, w1 / tot, w2 / tot


def _outproj_kernel(n_act, x_off, *refs):
    acts = refs[:n_act]
    ws = refs[n_act:2 * n_act]
    (x_ref, mod_ref, g_ref, rwh_ref, rwl_ref, rb_ref, su_ref,
     xo_ref, hp_ref, ri_ref, rw_ref, cnt_ref, carry_ref) = refs[2 * n_act:]
    j = pl.program_id(1)
    mod = mod_ref[0]
    o = _dot(acts[0][0], ws[0][...])
    for a, w in zip(acts[1:], ws[1:]):
        o = o + _dot(a[0], w[...])
    x = x_ref[0] + mod[2:3] * o
    xo_ref[0] = x
    h = _rms(x) * g_ref[...] * (1.0 + mod[4:5]) + mod[3:4]

    hp_ref[0] = h

    h_hi, h_lo = _split_bf16(h, 2)
    logits_t = _dot_nt(rwh_ref[...], h_hi) + _dot_nt(rwh_ref[...], h_lo) + _dot_nt(rwl_ref[...], h_hi)
    e1, e2, w1, w2 = _route(logits_t, rb_ref[...])

    @pl.when(j == 0)
    def _():
        carry_ref[...] = jnp.zeros(carry_ref.shape, F32)

    eidx = lax.broadcasted_iota(I32, logits_t.shape, 0)
    oh1 = eidx == e1
    oh2 = eidx == e2
    onehot = jnp.where(oh1, 1.0, jnp.where(oh2, 1.0, 0.0))
    rank = _dot(onehot.astype(BF16), su_ref[...]) + carry_ref[:, 0:1]
    r1 = jnp.sum(jnp.where(oh1, rank, 0.0), axis=0, keepdims=True)
    r2 = jnp.sum(jnp.where(oh2, rank, 0.0), axis=0, keepdims=True)
    carry = carry_ref[...] + jnp.sum(onehot, axis=1, keepdims=True)
    carry_ref[...] = carry
    cnt_ref[0] = carry.astype(I32)

    zi = jnp.zeros((SUBLANES - 4, TILE), I32)
    ri_ref[0, 0] = jnp.concatenate([e1, e2, r1.astype(I32), r2.astype(I32), zi], axis=0)
    zf = jnp.zeros((SUBLANES - 2, TILE), F32)
    rw_ref[0, 0] = jnp.concatenate([w1, w2, zf], axis=0)


def _outproj_call(acts, ws, x_src, x_off, modl, norm_g, rwh, rwl, rb, su, n_tiles):
    B = x_src.shape[0]
    n_act = len(acts)
    T = n_tiles * TILE
    const2 = lambda b, j: (0, 0)
    in_specs = []
    for a, a_off in acts:
        in_specs.append(pl.BlockSpec((1, TILE, a.shape[2]), functools.partial(lambda b, j, o: (b, j + o, 0), o=a_off)))
    for w in ws:
        in_specs.append(pl.BlockSpec(w.shape, const2))
    in_specs += [
        pl.BlockSpec((1, TILE, D_MODEL), lambda b, j: (b, j + x_off, 0)),
        pl.BlockSpec((1, 6, D_MODEL), lambda b, j: _mod_index(b, j + x_off)),
        pl.BlockSpec((1, D_MODEL), const2),
        pl.BlockSpec(rwh.shape, const2),
        pl.BlockSpec(rwl.shape, const2),
        pl.BlockSpec(rb.shape, const2),
        pl.BlockSpec(su.shape, const2),
    ]
    tok = lambda b, j: (b, j, 0)
    out_shape = [
        jax.ShapeDtypeStruct((B, T, D_MODEL), F32),
        jax.ShapeDtypeStruct((B, T, D_MODEL), F32),
        jax.ShapeDtypeStruct((B, n_tiles, SUBLANES, TILE), I32),
        jax.ShapeDtypeStruct((B, n_tiles, SUBLANES, TILE), F32),
        jax.ShapeDtypeStruct((B, N_EXPERTS, 128), I32),
    ]
    out_specs = [
        pl.BlockSpec((1, TILE, D_MODEL), tok),
        pl.BlockSpec((1, TILE, D_MODEL), tok),
        pl.BlockSpec((1, 1, SUBLANES, TILE), lambda b, j: (b, j, 0, 0)),
        pl.BlockSpec((1, 1, SUBLANES, TILE), lambda b, j: (b, j, 0, 0)),
        pl.BlockSpec((1, N_EXPERTS, 128), lambda b, j: (b, 0, 0)),
    ]
    return pl.pallas_call(
        functools.partial(_outproj_kernel, n_act, x_off),
        grid=(B, n_tiles),
        in_specs=in_specs,
        out_specs=out_specs,
        out_shape=out_shape,
        scratch_shapes=[pltpu.VMEM((N_EXPERTS, 128), F32)],
        compiler_params=_cparams(("arbitrary", "arbitrary")),
        name="outproj_route",
    )(*[a for a, _ in acts], *ws, x_src, modl, norm_g, rwh, rwl, rb, su)


def _moe_kernel(T, n_rows, cnt_ref, ri_ref, rw_ref, h_ref, w1_ref, w3_ref, w2_ref, y_ref,
                xb_ref, ob_ref, tokl_ref, off_ref):
    b = pl.program_id(0)
    e = pl.program_id(1)

    def position(t, k):
        return off_ref[ri_ref[0, k, t]] + ri_ref[0, 2 + k, t]

    @pl.when(e == 0)
    def _():
        off_ref[0] = 0
        for i in range(N_EXPERTS):
            c = cnt_ref[b, i]
            off_ref[i + 1] = off_ref[i] + ((c + SUBLANES - 1) // SUBLANES) * SUBLANES

        def clear(i, carry):
            tokl_ref[i] = 0
            return carry

        lax.fori_loop(0, n_rows, clear, 0, unroll=8)

        def place(t, carry):
            tokl_ref[position(t, 0)] = t
            tokl_ref[position(t, 1)] = t
            return carry

        lax.fori_loop(0, T, place, 0, unroll=8)

    c = cnt_ref[b, e]
    base = off_ref[e]

    def block(rb, carry):
        p0 = pl.multiple_of(base + rb * MOE_RB, SUBLANES)

        def gather(i, carry2):
            xb_ref[pl.ds(i, 1), :] = h_ref[0, pl.ds(tokl_ref[p0 + i], 1), :]
            return carry2

        lax.fori_loop(0, MOE_RB, gather, 0, unroll=8)
        xb = xb_ref[...].astype(BF16)
        h1 = _dot(xb, w1_ref[0])
        h3 = _dot(xb, w3_ref[0])
        act = (h1 * _sigmoid(h1)) * h3
        ob_ref[pl.ds(p0, MOE_RB), :] = _dot(act.astype(BF16), w2_ref[0])
        return carry

    lax.fori_loop(0, (c + MOE_RB - 1) // MOE_RB, block, 0)

    @pl.when(e == N_EXPERTS - 1)
    def _():
        def combine(t, carry):
            y_ref[0, pl.ds(t, 1), :] = (rw_ref[0, 0, t] * ob_ref[pl.ds(position(t, 0), 1), :]
                                        + rw_ref[0, 1, t] * ob_ref[pl.ds(position(t, 1), 1), :])
            return carry

        lax.fori_loop(0, T, combine, 0, unroll=8)


def _moe_call(cnt, ri, rw, h, w1, w3, w2):
    B, T, _ = h.shape
    n_rows = 2 * T + N_EXPERTS * SUBLANES + MOE_RB
    smem = functools.partial(pl.BlockSpec, memory_space=pltpu.SMEM)
    once = pl.Buffered(1)
    return pl.pallas_call(
        functools.partial(_moe_kernel, T, n_rows),
        grid=(B, N_EXPERTS),
        in_specs=[
            smem(cnt.shape, lambda b, e: (0, 0)),
            smem((1, 4, T), lambda b, e: (b, 0, 0)),
            smem((1, 2, T), lambda b, e: (b, 0, 0)),
            pl.BlockSpec((1, T, D_MODEL), lambda b, e: (b, 0, 0), pipeline_mode=once),
            pl.BlockSpec((1, D_MODEL, D_FF), lambda b, e: (e, 0, 0)),
            pl.BlockSpec((1, D_MODEL, D_FF), lambda b, e: (e, 0, 0)),
            pl.BlockSpec((1, D_FF, D_MODEL), lambda b, e: (e, 0, 0)),
        ],
        out_specs=pl.BlockSpec((1, T, D_MODEL), lambda b, e: (b, 0, 0), pipeline_mode=once),
        out_shape=jax.ShapeDtypeStruct((B, T, D_MODEL), F32),
        scratch_shapes=[
            pltpu.VMEM((MOE_RB, D_MODEL), F32),
            pltpu.VMEM((n_rows, D_MODEL), F32),
            pltpu.SMEM((n_rows,), I32),
            pltpu.SMEM((N_EXPERTS + 1,), I32),
        ],
        compiler_params=_cparams(("arbitrary", "arbitrary")),
        name="moe",
    )(cnt, ri, rw, h, w1, w3, w2)


def _route_tables(ri, rw, cnt):
    B, nt = ri.shape[:2]
    ri = jnp.transpose(ri[:, :, :4], (0, 2, 1, 3)).reshape(B, 4, nt * TILE)
    rw = jnp.transpose(rw[:, :, :2], (0, 2, 1, 3)).reshape(B, 2, nt * TILE)
    return cnt[:, :, 0], ri, rw


def _inproj1_kernel(x_ref, y_ref, mod0_ref, mod_ref, g_ref, w_ref, qg_ref, kg_ref, cos_ref, sin_ref,
                    xo_ref, q_ref, k_ref, v_ref):
    x = x_ref[0] + mod0_ref[0][5:6] * y_ref[0]
    xo_ref[0] = x
    mod = mod_ref[0]
    hb = (_rms(x) * g_ref[...] * (1.0 + mod[1:2]) + mod[0:1]).astype(BF16)
    cos = cos_ref[...]
    sin = sin_ref[...]

    def rope_head(a, gain):
        n = _rms(a) * gain
        return n * cos + pltpu.roll(n, HD_C // 2, 1) * sin

    qg = qg_ref[...]
    kg = kg_ref[...]
    for h in range(H_C):
        sl = slice(h * HD_C, (h + 1) * HD_C)
        q_ref[0, :, sl] = (rope_head(_dot(hb, w_ref[:, sl]), qg) * HD_C ** -0.5).astype(BF16)
    for h in range(KV_C):
        sl = slice(h * HD_C, (h + 1) * HD_C)
        ko = H_C * HD_C
        k_ref[0, :, sl] = rope_head(_dot(hb, w_ref[:, ko + h * HD_C:ko + (h + 1) * HD_C]), kg).astype(BF16)
    vo = (H_C + KV_C) * HD_C
    v_ref[0] = _dot(hb, w_ref[:, vo:vo + KV_C * HD_C]).astype(BF16)


def _inproj1_call(x1, y0, mod0, mod1, norm_g, w, qg, kg, cos, sin):
    B = x1.shape[0]
    tok = lambda b, j: (b, j, 0)
    const2 = lambda b, j: (0, 0)
    widths = [(D_MODEL, F32), (H_C * HD_C, BF16), (KV_C * HD_C, BF16), (KV_C * HD_C, BF16)]
    return pl.pallas_call(
        _inproj1_kernel,
        grid=(B, NT_ALL),
        in_specs=[
            pl.BlockSpec((1, TILE, D_MODEL), tok),
            pl.BlockSpec((1, TILE, D_MODEL), tok),
            pl.BlockSpec((1, 6, D_MODEL), _mod_index),
            pl.BlockSpec((1, 6, D_MODEL), _mod_index),
            pl.BlockSpec((1, D_MODEL), const2),
            pl.BlockSpec(w.shape, const2),
            pl.BlockSpec(qg.shape, const2),
            pl.BlockSpec(kg.shape, const2),
            pl.BlockSpec((TILE, HD_C), lambda b, j: (j, 0)),
            pl.BlockSpec((TILE, HD_C), lambda b, j: (j, 0)),
        ],
        out_specs=[pl.BlockSpec((1, TILE, w_), tok) for w_, _ in widths],
        out_shape=[jax.ShapeDtypeStruct((B, T_ALL, w_), dt) for w_, dt in widths],
        compiler_params=_cparams(("arbitrary", "arbitrary")),
        name="inproj1",
    )(x1, y0, mod0, mod1, norm_g, w, qg, kg, cos, sin)


def _gqa_kernel(q_ref, k_ref, v_ref, o_ref):
    k = k_ref[0]
    v = v_ref[0]
    for h in range(H_C // KV_C):
        sl = slice(h * HD_C, (h + 1) * HD_C)
        s = _dot_nt(q_ref[0, :, sl], k)
        p = jnp.exp(s - s.max(axis=-1, keepdims=True))
        l = p.sum(axis=-1, keepdims=True)
        o_ref[0, :, sl] = (_dot(p.astype(BF16), v) / l).astype(BF16)


def _gqa_call(q, k, v):
    B = q.shape[0]
    gw = (H_C // KV_C) * HD_C
    q_off = CTX_LEN // GQA_TQ
    return pl.pallas_call(
        _gqa_kernel,
        grid=(B, KV_C, SEQ // GQA_TQ),
        in_specs=[
            pl.BlockSpec((1, GQA_TQ, gw), lambda b, g, j: (b, j + q_off, g)),
            pl.BlockSpec((1, T_ALL, HD_C), lambda b, g, j: (b, 0, g)),
            pl.BlockSpec((1, T_ALL, HD_C), lambda b, g, j: (b, 0, g)),
        ],
        out_specs=pl.BlockSpec((1, GQA_TQ, gw), lambda b, g, j: (b, j, g)),
        out_shape=jax.ShapeDtypeStruct((B, SEQ, H_C * HD_C), BF16),
        compiler_params=_cparams(("arbitrary", "arbitrary", "arbitrary")),
        name="gqa",
    )(q, k, v)


def _final_kernel(x_ref, y_ref, mod_ref, o_ref):
    o_ref[0] = x_ref[0] + mod_ref[0][5:6] * y_ref[0]


def _final_call(x, y, modl):
    B, T, D = x.shape
    tok = lambda b, j: (b, j, 0)
    return pl.pallas_call(
        _final_kernel,
        grid=(B, T // TILE),
        in_specs=[pl.BlockSpec((1, TILE, D), tok), pl.BlockSpec((1, TILE, D), tok),
                  pl.BlockSpec((1, 6, D), lambda b, j: (b, 0, 0))],
        out_specs=pl.BlockSpec((1, TILE, D), tok),
        out_shape=jax.ShapeDtypeStruct((B, T, D), F32),
        compiler_params=_cparams(("arbitrary", "arbitrary")),
        name="final_residual",
    )(x, y, modl)


def _chunk_tri(lower):
    i = np.arange(TILE)
    same = (i[:, None] // MLSTM_CHUNK) == (i[None, :] // MLSTM_CHUNK)
    tri = (i[None, :] <= i[:, None]) if lower else (i[None, :] >= i[:, None])
    return jnp.asarray((same & tri).astype(np.float32), BF16)


def _rope_tables():
    n_freq = HD_C // 4
    inv_freq = ROPE_THETA ** (-jnp.arange(n_freq, dtype=F32) / n_freq)
    t = jnp.arange(SEQ)
    rows = (t // GRID_W).astype(F32)
    cols = (t % GRID_W).astype(F32)
    ang = jnp.concatenate([rows[:, None] * inv_freq, cols[:, None] * inv_freq], axis=-1)
    cos, sin = jnp.cos(ang), jnp.sin(ang)
    cos_l = jnp.concatenate([cos, cos], axis=-1)
    sin_l = jnp.concatenate([-sin, sin], axis=-1)
    cos_all = jnp.concatenate([jnp.ones((CTX_LEN, HD_C), F32), cos_l], axis=0)
    sin_all = jnp.concatenate([jnp.zeros((CTX_LEN, HD_C), F32), sin_l], axis=0)
    return cos_all, sin_all


_HEAD_PERM = np.concatenate([np.arange(0, HD_C, 2), np.arange(1, HD_C, 2)])


def kernel(x, c, ctx, c_ctx, ada_w, ada_b, norm_mix_g, norm_ffn_g, even_w_in, even_w_out,
           na_q_norm_g, na_k_norm_g, na_rpb, mlstm_gate_b, mlstm_norm_g, odd_w_in, odd_w_out,
           gqa_q_norm_g, gqa_k_norm_g, router_w, router_b, exp_w1, exp_w3, exp_w2):
    B = x.shape[0]
    assert B <= N_MOD_CTX_ROW and x.shape[1:] == (SEQ, D_MODEL) and ctx.shape[1:] == (CTX_LEN, D_MODEL)
    n_g = 4 * H_B

    cvec = jnp.zeros((N_MOD_ROWS, D_MODEL), F32).at[:B].set(c).at[N_MOD_CTX_ROW].set(c_ctx)
    mod = _ada_call(cvec, ada_w, ada_b).reshape(2, N_MOD_ROWS, 6, D_MODEL)
    mod0, mod1 = mod[0], mod[1]

    x_all = jnp.concatenate([ctx, x], axis=1)

    w_in = even_w_in[0]
    n_main = w_in.shape[1] - n_g
    wm = w_in[:, :n_main].astype(BF16)
    wg_f = w_in[:, n_main:]
    wg = jnp.pad(wg_f, ((0, 0), (0, 128 - n_g))).astype(BF16)
    wgt = wg_f.T.astype(BF16)
    gb = jnp.pad(mlstm_gate_b[0].reshape(1, n_g), ((0, 0), (0, 128 - n_g)))
    gbt = mlstm_gate_b[0].reshape(n_g, 1)
    qg = jnp.tile(na_q_norm_g[0], H_A).reshape(1, D_A)
    kg = jnp.tile(na_k_norm_g[0], H_A).reshape(1, D_A)
    hid = np.arange(D_A) // HD_A
    bd = jnp.asarray((hid[:, None] == hid[None, :]).astype(np.float32), BF16)
    aq, ak, av, mq, mk, mv, og, gc, gr = _inproj0_call(
        x_all, mod0, norm_mix_g[0].reshape(1, D_MODEL), wm, wg, wgt, gb, gbt, qg, kg, bd,
        _chunk_tri(True), _chunk_tri(False))
    hm = _mlstm_call(mq, mk, mv, gc, gr, og, mlstm_norm_g[0].reshape(1, H_B * DV_B))
    oa = _na_call(aq, ak, av, _na_bias_tables(na_rpb[0]))

    rw_t = router_w.T
    rwh = rw_t.astype(BF16)
    rwl = (rw_t - rwh.astype(F32)).astype(BF16)
    rb = router_b.reshape(N_EXPERTS, 1).astype(F32)
    i = np.arange(TILE)
    su = jnp.asarray((i[:, None] < i[None, :]).astype(np.float32), BF16)
    w_out = even_w_out[0].astype(BF16)
    x1, hp0, ri0, rw0, cnt0 = _outproj_call(
        [(oa, 0), (hm, 0)], [w_out[:D_A], w_out[D_A:]], x_all, 0, mod0,
        norm_ffn_g[0].reshape(1, D_MODEL), rwh, rwl, rb, su, NT_ALL)
    y0 = _moe_call(*_route_tables(ri0, rw0, cnt0), hp0,
                   exp_w1[0].astype(BF16), exp_w3[0].astype(BF16), exp_w2[0].astype(BF16))

    w1_in = odd_w_in[0]
    qk_cols = np.concatenate([h * HD_C + _HEAD_PERM for h in range(H_C + KV_C)])
    cols = np.concatenate([qk_cols, np.arange((H_C + KV_C) * HD_C, w1_in.shape[1])])
    w1_in = w1_in[:, cols].astype(BF16)
    cos_all, sin_all = _rope_tables()
    x2, q, k, v = _inproj1_call(
        x1, y0, mod0, mod1, norm_mix_g[1].reshape(1, D_MODEL), w1_in,
        gqa_q_norm_g[0][_HEAD_PERM].reshape(1, HD_C), gqa_k_norm_g[0][_HEAD_PERM].reshape(1, HD_C),
        cos_all, sin_all)
    o = _gqa_call(q, k, v)
    x3, hp1, ri1, rw1, cnt1 = _outproj_call(
        [(o, 0)], [odd_w_out[0].astype(BF16)], x2, NT_CTX, mod1,
        norm_ffn_g[1].reshape(1, D_MODEL), rwh, rwl, rb, su, SEQ // TILE)
    y1 = _moe_call(*_route_tables(ri1, rw1, cnt1), hp1,
                   exp_w1[1].astype(BF16), exp_w3[1].astype(BF16), exp_w2[1].astype(BF16))
    return _final_call(x3, y1, mod1)
```

```python
import functools

import numpy as np
import jax
import jax.numpy as jnp
from jax import lax
from jax.experimental import pallas as pl
from jax.experimental.pallas import tpu as pltpu

F32 = jnp.float32
BF16 = jnp.bfloat16
I32 = jnp.int32
U32 = jnp.uint32

D_MODEL = 1024
SEQ = 2048
GRID_W = 64
GRID_ROWS = SEQ // GRID_W
CTX_LEN = 256
T_ALL = CTX_LEN + SEQ
WIN_H = 8
WIN_W = 16
HD_A = 64
H_A = 8
D_A = H_A * HD_A
H_B = 4
DV_B = 128
DK_B = 64
MLSTM_CHUNK = 64
N_CHUNKS = T_ALL // MLSTM_CHUNK
N_CTX_CHUNKS = CTX_LEN // MLSTM_CHUNK
HD_C = 128
H_C = 8
KV_C = 2
ROPE_THETA = 10000.0
N_EXPERTS = 16
N_GROUPS = 4
EXP_PER_GROUP = 4
D_FF = 512
EPS = 1e-6
NEG_BIG = -1e30

TILE = 256
NT_ALL = T_ALL // TILE
NT_CTX = CTX_LEN // TILE
NA_QROWS = 2
NA_KROWS = WIN_H + NA_QROWS - 1
NA_TQ = NA_QROWS * GRID_W
NA_TK = NA_KROWS * GRID_W
NA_CASES = (0, 2, 4, GRID_ROWS - 4, GRID_ROWS - 2)
GQA_TQ = 256
MOE_RB = 256
MOE_RB_TAIL = 128
SUBLANES = 8
VMEM_LIMIT = 56 * 1024 * 1024


def _cparams(sem):
    return pltpu.CompilerParams(dimension_semantics=sem, vmem_limit_bytes=VMEM_LIMIT)


def _sigmoid(x):
    return 1.0 / (1.0 + jnp.exp(-x))


def _rms(x):
    return x * lax.rsqrt(jnp.mean(x * x, axis=-1, keepdims=True) + EPS)


def _dot(a, b):
    return jnp.dot(a, b, preferred_element_type=F32)


def _dot_nt(a, b):
    return lax.dot_general(a, b, (((1,), (1,)), ((), ())), preferred_element_type=F32)


def _dot_tn(a, b):
    return lax.dot_general(a, b, (((0,), (0,)), ((), ())), preferred_element_type=F32)


def _split_bf16(x, n):
    parts = []
    r = x
    for _ in range(n):
        p = r.astype(BF16)
        parts.append(p)
        r = r - p.astype(F32)
    return parts


LANE_TILES = D_MODEL // 128


def _load_token_tiles(ref, lead, tok0, n_tok):
    parts = [ref[(*lead, pl.ds(tok0 * SUBLANES + c, n_tok, stride=SUBLANES), slice(None))]
             for c in range(LANE_TILES)]
    return jnp.concatenate(parts, axis=1)


def _store_token_tiles(ref, lead, tok0, val):
    for c in range(LANE_TILES):
        ref[(*lead, pl.ds(tok0 * SUBLANES + c, val.shape[0], stride=SUBLANES), slice(None))] = (
            val[:, c * 128:(c + 1) * 128])


def _mod_index(b, j):
    return (jnp.where(j < NT_CTX, N_MOD_CTX_ROW, b), 0, 0)


N_MOD_ROWS = 16
N_MOD_CTX_ROW = 8


ADA_TN = 1536


def _ada_kernel(c_ref, w_ref, b_ref, o_ref):
    c = c_ref[...]
    s = (c * _sigmoid(c)).astype(BF16)
    o_ref[0] = _dot(s, w_ref[0].astype(BF16)) + b_ref[0]


def _ada_call(cvec, ada_w, ada_b):
    depth, d, n = ada_w.shape
    return pl.pallas_call(
        _ada_kernel,
        grid=(depth, n // ADA_TN),
        in_specs=[
            pl.BlockSpec((N_MOD_ROWS, d), lambda l, j: (0, 0)),
            pl.BlockSpec((1, d, ADA_TN), lambda l, j: (l, 0, j)),
            pl.BlockSpec((1, 1, ADA_TN), lambda l, j: (l, 0, j)),
        ],
        out_specs=pl.BlockSpec((1, N_MOD_ROWS, ADA_TN), lambda l, j: (l, 0, j)),
        out_shape=jax.ShapeDtypeStruct((depth, N_MOD_ROWS, n), F32),
        compiler_params=_cparams(("arbitrary", "arbitrary")),
        name="ada_mod",
    )(cvec, ada_w, ada_b.reshape(depth, 1, n))


def _log_sigmoid(x):
    return jnp.minimum(x, 0.0) - jnp.log1p(jnp.exp(-jnp.abs(x)))


def _inproj0_kernel(x_ref, mod_ref, g_ref, wm_ref, wkt_ref, wg_ref, wgt_ref, gb_ref, gbt_ref, qg_ref, kg_ref,
                    bd_ref, tril_ref, triu_ref,
                    aq_ref, ak_ref, av_ref, mq_ref, mv_ref, og_ref, mkt_ref, gc_ref, gr_ref):
    mod = mod_ref[0]
    h = _rms(x_ref[0]) * g_ref[...] * (1.0 + mod[1:2]) + mod[0:1]
    hb = h.astype(BF16)

    def proj(lo, hi):
        return _dot(hb, wm_ref[:, lo:hi])

    def head_norm(a, gain):
        hi_, lo_ = _split_bf16(a * a, 2)
        ss = _dot(hi_, bd_ref[...]) + _dot(lo_, bd_ref[...])
        return a * lax.rsqrt(ss * (1.0 / HD_A) + EPS) * gain

    aq_ref[0] = (head_norm(proj(0, 512), qg_ref[...]) * HD_A ** -0.5).astype(BF16)
    ak_ref[0] = head_norm(proj(512, 1024), kg_ref[...]).astype(BF16)
    av_ref[0] = proj(1024, 1536).astype(BF16)
    mq_ref[0] = (proj(1536, 1792) * DK_B ** -0.5).astype(BF16)
    kt = _dot_nt(wkt_ref[...], hb).astype(BF16)
    for c in range(TILE // MLSTM_CHUNK):
        mkt_ref[0, c] = kt[:, c * MLSTM_CHUNK:(c + 1) * MLSTM_CHUNK]
    mv_ref[0] = proj(2048, 2560).astype(BF16)
    og_ref[0] = _sigmoid(proj(2560, 3072)).astype(BF16)

    n_g = 4 * H_B
    gcol = _dot(hb, wg_ref[...]) + gb_ref[...]
    grow = _dot_nt(wgt_ref[...], hb) + gbt_ref[...]
    tril = tril_ref[...]
    triu = triu_ref[...]
    ls_c = _split_bf16(_log_sigmoid(gcol), 3)
    ls_r = _split_bf16(_log_sigmoid(grow), 3)
    pre_c = _dot(tril, ls_c[0]) + _dot(tril, ls_c[1]) + _dot(tril, ls_c[2])
    suf_c = _dot(triu, ls_c[0]) + _dot(triu, ls_c[1]) + _dot(triu, ls_c[2])
    pre_r = _dot(ls_r[0], triu) + _dot(ls_r[1], triu) + _dot(ls_r[2], triu)
    suf_r = _dot(ls_r[0], tril) + _dot(ls_r[1], tril) + _dot(ls_r[2], tril)
    cidx = lax.broadcasted_iota(I32, gcol.shape, 1)
    ridx = lax.broadcasted_iota(I32, grow.shape, 0)

    def pick(idx, raw, pre, suf):
        is_ff = (idx >= H_B) & (idx < 2 * H_B)
        is_fb = idx >= 3 * H_B
        return jnp.where(is_ff, pre, jnp.where(is_fb, suf, raw))

    gc_ref[0] = pick(cidx, gcol, pre_c, suf_c)[:, :n_g]
    grow = pick(ridx, grow, pre_r, suf_r)
    for c in range(TILE // MLSTM_CHUNK):
        gr_ref[0, c] = grow[:, c * MLSTM_CHUNK:(c + 1) * MLSTM_CHUNK]


def _inproj0_call(x_all, modl, norm_g, wm, wkt, wg, wgt, gb, gbt, qg, kg, bd, tril, triu):
    B = x_all.shape[0]
    n_g = 4 * H_B
    tok = lambda b, j: (b, j, 0)
    const2 = lambda b, j: (0, 0)
    chunked = lambda b, j: (b, j, 0, 0)
    tile_chunks = TILE // MLSTM_CHUNK
    outs = [
        (D_A, BF16), (D_A, BF16), (D_A, BF16),
        (H_B * DK_B, BF16), (H_B * DV_B, BF16),
        (H_B * DV_B, BF16),
    ]
    out_shape = [jax.ShapeDtypeStruct((B, T_ALL, w), dt) for w, dt in outs]
    out_specs = [pl.BlockSpec((1, TILE, w), tok) for w, _ in outs]
    out_shape += [jax.ShapeDtypeStruct((B, N_CHUNKS, H_B * DK_B, MLSTM_CHUNK), BF16),
                  jax.ShapeDtypeStruct((B, T_ALL, n_g), F32),
                  jax.ShapeDtypeStruct((B, N_CHUNKS, n_g, MLSTM_CHUNK), F32)]
    out_specs += [pl.BlockSpec((1, tile_chunks, H_B * DK_B, MLSTM_CHUNK), chunked),
                  pl.BlockSpec((1, TILE, n_g), tok),
                  pl.BlockSpec((1, tile_chunks, n_g, MLSTM_CHUNK), chunked)]
    return pl.pallas_call(
        _inproj0_kernel,
        grid=(B, NT_ALL),
        in_specs=[
            pl.BlockSpec((1, TILE, D_MODEL), tok),
            pl.BlockSpec((1, 6, D_MODEL), _mod_index),
            pl.BlockSpec((1, D_MODEL), const2),
            pl.BlockSpec(wm.shape, const2),
            pl.BlockSpec(wkt.shape, const2),
            pl.BlockSpec(wg.shape, const2),
            pl.BlockSpec(wgt.shape, const2),
            pl.BlockSpec(gb.shape, const2),
            pl.BlockSpec(gbt.shape, const2),
            pl.BlockSpec(qg.shape, const2),
            pl.BlockSpec(kg.shape, const2),
            pl.BlockSpec(bd.shape, const2),
            pl.BlockSpec(tril.shape, const2),
            pl.BlockSpec(triu.shape, const2),
        ],
        out_specs=out_specs,
        out_shape=out_shape,
        compiler_params=_cparams(("arbitrary", "arbitrary")),
        name="inproj0",
    )(x_all, modl, norm_g, wm, wkt, wg, wgt, gb, gbt, qg, kg, bd, tril, triu)


def _mlstm_kernel(mq_ref, mkt_ref, mv_ref, gc_ref, gr_ref, og_ref, ng_ref, out_ref,
                  hf_ref, hb_ref, s_ref):
    L = MLSTM_CHUNK
    s_ref[...] = jnp.zeros(s_ref.shape, F32)
    ri = lax.broadcasted_iota(I32, (L, L), 0)
    ci = lax.broadcasted_iota(I32, (L, L), 1)
    masks = (ci <= ri, ci >= ri)
    ones_aug = jnp.ones((L, DV_B), BF16)

    def chain(step, m, d, h):
        if d == 0:
            chunk = step
        else:
            chunk = jnp.where(step < N_CTX_CHUNKS, N_CTX_CHUNKS - 1 - step, N_CHUNKS + N_CTX_CHUNKS - 1 - step)
        r0 = pl.multiple_of(chunk * L, L)
        rows = pl.ds(r0, L)
        q = mq_ref[0, rows, h * DK_B:(h + 1) * DK_B]
        kt = mkt_ref[0, chunk, h * DK_B:(h + 1) * DK_B, :]
        v = mv_ref[0, rows, h * DV_B:(h + 1) * DV_B]
        gi = 2 * H_B * d + h
        gcc = gc_ref[0, rows, :]
        grr = gr_ref[0, chunk]
        ig_col = gcc[:, gi:gi + 1]
        b_col = gcc[:, gi + H_B:gi + H_B + 1]
        ig_row = grr[gi:gi + 1, :]
        b_row = grr[gi + H_B:gi + H_B + 1, :]
        dm = jnp.where(masks[d], b_col - b_row + ig_row, NEG_BIG)
        a = b_col + m
        m_row = jnp.maximum(a, jnp.max(dm, axis=-1, keepdims=True))
        w_intra = jnp.exp(dm - m_row)
        w_inter = jnp.exp(a - m_row)
        qk = _dot(q, kt) * w_intra
        sidx = d * H_B + h
        state = s_ref[sidx]
        v_aug = jnp.concatenate([v, ones_aug], axis=1)
        numden = w_inter * _dot(q, state.astype(BF16)) + _dot(qk.astype(BF16), v_aug)
        num = numden[:, :DV_B]
        den = numden[:, DV_B:]
        hout = num / jnp.maximum(jnp.abs(den), jnp.exp(-m_row))
        dst = hf_ref if d == 0 else hb_ref
        dst[rows, h * DV_B:(h + 1) * DV_B] = hout
        b_last = b_col[L - 1:L, :] if d == 0 else b_col[0:1, :]
        g = b_last - b_col + ig_col
        m_new = jnp.maximum(b_last + m, jnp.max(g, axis=0, keepdims=True))
        w_old = jnp.exp(b_last + m - m_new)
        w_tok = jnp.exp(g - m_new)
        wv = (w_tok * v_aug.astype(F32)).astype(BF16)
        s_ref[sidx] = w_old * state + _dot(kt, wv)
        return m_new

    def step_fn(step, ms):
        return tuple(chain(step, ms[d * H_B + h], d, h) for d in range(2) for h in range(H_B))

    m0 = tuple(jnp.full((1, 1), NEG_BIG, F32) for _ in range(2 * H_B))
    lax.fori_loop(0, N_CHUNKS, step_fn, m0)

    def finish(i, carry):
        rows = pl.ds(pl.multiple_of(i * TILE, TILE), TILE)
        hs = hf_ref[rows, :] + hb_ref[rows, :]
        ng = ng_ref[...]
        og = og_ref[0, rows, :].astype(F32)
        for h in range(H_B):
            sl = slice(h * DV_B, (h + 1) * DV_B)
            out_ref[0, rows, sl] = (_rms(hs[:, sl]) * ng[:, sl] * og[:, sl]).astype(BF16)
        return carry

    lax.fori_loop(0, T_ALL // TILE, finish, 0)


def _mlstm_call(mq, mk, mv, gc, gr, og, ng):
    B = mq.shape[0]
    full = lambda a: pl.BlockSpec((1,) + a.shape[1:], lambda b: (b,) + (0,) * (a.ndim - 1))
    return pl.pallas_call(
        _mlstm_kernel,
        grid=(B,),
        in_specs=[full(mq), full(mk), full(mv), full(gc), full(gr), full(og),
                  pl.BlockSpec(ng.shape, lambda b: (0, 0))],
        out_specs=pl.BlockSpec((1, T_ALL, H_B * DV_B), lambda b: (b, 0, 0)),
        out_shape=jax.ShapeDtypeStruct((B, T_ALL, H_B * DV_B), BF16),
        scratch_shapes=[
            pltpu.VMEM((T_ALL, H_B * DV_B), F32),
            pltpu.VMEM((T_ALL, H_B * DV_B), F32),
            pltpu.VMEM((2 * H_B, DK_B, 2 * DV_B), F32),
        ],
        compiler_params=_cparams(("arbitrary",)),
        name="mlstm",
    )(mq, mk, mv, gc, gr, og, ng)


def _na_bias_tables(rpb):
    kh = WIN_H
    n_drow = 2 * WIN_H - 1
    qcol = np.arange(GRID_W)
    col_start = np.clip(qcol - WIN_W // 2, 0, GRID_W - WIN_W)
    col_ok = (qcol[None, :] >= col_start[:, None]) & (qcol[None, :] < col_start[:, None] + WIN_W)
    dcol = qcol[None, :] - qcol[:, None] + (WIN_W - 1)
    onehot = (dcol[None] == np.arange(2 * WIN_W - 1)[:, None, None]) & col_ok[None]
    blocks = jnp.einsum('hdx,xck->hdck', rpb.astype(F32), jnp.asarray(onehot, F32),
                        precision=lax.Precision.HIGHEST)
    blocks = jnp.where(col_ok[None, None], blocks, NEG_BIG)
    outside = jnp.full((H_A, 1, GRID_W, GRID_W), NEG_BIG, F32)
    blocks = jnp.concatenate([blocks, outside], axis=1)
    idx = np.full((len(NA_CASES), NA_QROWS, NA_KROWS), n_drow, np.int32)
    for ci, r in enumerate(NA_CASES):
        u0 = int(np.clip(r - kh // 2, 0, GRID_ROWS - NA_KROWS))
        for qi in range(NA_QROWS):
            rq = r + qi
            r0 = int(np.clip(rq - kh // 2, 0, GRID_ROWS - kh))
            for ui in range(NA_KROWS):
                kr = u0 + ui
                if r0 <= kr < r0 + kh:
                    idx[ci, qi, ui] = kr - rq + (WIN_H - 1)
    tab = blocks[:, idx]
    tab = jnp.transpose(tab, (1, 0, 2, 4, 3, 5))
    return tab.reshape(len(NA_CASES), H_A, NA_TQ, NA_TK)


def _na_case(j):
    jr = j - 1
    last = GRID_ROWS // NA_QROWS - 1
    return jnp.where(jr <= 1, jnp.maximum(jr, 0), jnp.where(jr >= last - 1, jr - last + 4, 2))


def _na_kernel(q_ref, k_ref, v_ref, bias_ref, o_ref):
    j = pl.program_id(1)
    lane = lax.broadcasted_iota(I32, (1, 2 * HD_A), 1)
    lo_half = lane < HD_A

    def attend(q_rows, n_q, key_sets, bias_for_head):
        for pp in range(H_A // 2):
            lanes = slice(pp * 2 * HD_A, (pp + 1) * 2 * HD_A)
            qp = q_ref[0, q_rows, lanes]
            ks = [k_ref[0, rs, lanes] for rs in key_sets]
            vs = [v_ref[0, rs, lanes] for rs in key_sets]
            halves = []
            for hh in range(2):
                qm = jnp.where(lo_half if hh == 0 else ~lo_half, qp, jnp.zeros_like(qp))
                ss = [_dot_nt(qm, kk) for kk in ks]
                if bias_for_head is not None:
                    ss[0] = ss[0] + bias_for_head(2 * pp + hh)
                m = ss[0].max(axis=-1, keepdims=True)
                for s in ss[1:]:
                    m = jnp.maximum(m, s.max(axis=-1, keepdims=True))
                ps = [jnp.exp(s - m) for s in ss]
                l = ps[0].sum(axis=-1, keepdims=True)
                for p in ps[1:]:
                    l = l + p.sum(axis=-1, keepdims=True)
                acc = _dot(ps[0].astype(BF16), vs[0])
                for p, vv in zip(ps[1:], vs[1:]):
                    acc = acc + _dot(p.astype(BF16), vv)
                halves.append(acc / l)
            o_ref[0, q_rows, lanes] = jnp.where(lo_half, halves[0], halves[1]).astype(BF16)

    ctx_rows = pl.ds(0, CTX_LEN)

    @pl.when(j == 0)
    def _():
        attend(ctx_rows, CTX_LEN, [ctx_rows], None)

    @pl.when(j > 0)
    def _():
        r = (j - 1) * NA_QROWS
        u0 = jnp.clip(r - WIN_H // 2, 0, GRID_ROWS - NA_KROWS)
        q_rows = pl.ds(pl.multiple_of(CTX_LEN + r * GRID_W, NA_TQ), NA_TQ)
        k_rows = pl.ds(pl.multiple_of(CTX_LEN + u0 * GRID_W, GRID_W), NA_TK)
        attend(q_rows, NA_TQ, [k_rows, ctx_rows], lambda head: bias_ref[0, head])


def _na_call(aq, ak, av, bias):
    B = aq.shape[0]
    full = pl.BlockSpec((1, T_ALL, D_A), lambda b, j: (b, 0, 0))
    return pl.pallas_call(
        _na_kernel,
        grid=(B, 1 + GRID_ROWS // NA_QROWS),
        in_specs=[full, full, full,
                  pl.BlockSpec((1, H_A, NA_TQ, NA_TK), lambda b, j: (_na_case(j), 0, 0, 0))],
        out_specs=full,
        out_shape=jax.ShapeDtypeStruct((B, T_ALL, D_A), BF16),
        compiler_params=_cparams(("arbitrary", "arbitrary")),
        name="nbr_attn",
    )(aq, ak, av, bias)


def _route(logits_t, rb_col):
    sc = _sigmoid(logits_t)
    sel = sc + rb_col
    selr = [sel[e:e + 1, :] for e in range(N_EXPERTS)]
    scr = [sc[e:e + 1, :] for e in range(N_EXPERTS)]
    gscore = []
    for g in range(N_GROUPS):
        a, b, c, d = selr[EXP_PER_GROUP * g:EXP_PER_GROUP * (g + 1)]
        s1, t1 = jnp.maximum(a, b), jnp.minimum(a, b)
        s2, t2 = jnp.maximum(c, d), jnp.minimum(c, d)
        gscore.append(jnp.maximum(s1, s2) + jnp.maximum(jnp.minimum(s1, s2), jnp.maximum(t1, t2)))
    best = gscore[0]
    gi = jnp.zeros(best.shape, I32)
    for g in range(1, N_GROUPS):
        better = gscore[g] > best
        gi = jnp.where(better, g, gi)
        best = jnp.where(better, gscore[g], best)
    vs, ws = [], []
    for k in range(EXP_PER_GROUP):
        v = selr[k]
        w = scr[k]
        for g in range(1, N_GROUPS):
            v = jnp.where(gi == g, selr[EXP_PER_GROUP * g + k], v)
            w = jnp.where(gi == g, scr[EXP_PER_GROUP * g + k], w)
        vs.append(v)
        ws.append(w)
    b1, i1 = vs[0], jnp.zeros(best.shape, I32)
    for k in range(1, EXP_PER_GROUP):
        better = vs[k] > b1
        i1 = jnp.where(better, k, i1)
        b1 = jnp.where(better, vs[k], b1)
    b2 = jnp.full(best.shape, -jnp.inf, F32)
    i2 = jnp.zeros(best.shape, I32)
    for k in range(EXP_PER_GROUP):
        vk = jnp.where(i1 == k, -jnp.inf, vs[k])
        better = vk > b2
        i2 = jnp.where(better, k, i2)
        b2 = jnp.where(better, vk, b2)
    w1 = ws[0]
    w2 = ws[0]
    for k in range(1, EXP_PER_GROUP):
        w1 = jnp.where(i1 == k, ws[k], w1)
        w2 = jnp.where(i2 == k, ws[k], w2)
    tot = w1 + w2
    return gi * EXP_PER_GROUP + i1, gi * EXP_PER_GROUP + i2, w1 / tot, w2 / tot


def _outproj_kernel(n_act, n_tiles, *refs):
    acts = refs[:n_act]
    ws = refs[n_act:2 * n_act]
    (x_ref, mod_ref, g_ref, rwh_ref, rwl_ref, rb_ref, su_ref,
     xo_ref, hp_ref, ri_ref, rw_ref, cnt_ref, carry_ref) = refs[2 * n_act:]
    j = pl.program_id(1)
    mod = mod_ref[0]
    o = _dot(acts[0][0], ws[0][...])
    for a, w in zip(acts[1:], ws[1:]):
        o = o + _dot(a[0], w[...])
    x = x_ref[0] + mod[2:3] * o
    xo_ref[0] = x
    h = _rms(x) * g_ref[...] * (1.0 + mod[4:5]) + mod[3:4]

    _store_token_tiles(hp_ref, (0,), 0, h)
    h_hi, h_lo = _split_bf16(h, 2)
    logits_t = _dot_nt(rwh_ref[...], h_hi) + _dot_nt(rwh_ref[...], h_lo) + _dot_nt(rwl_ref[...], h_hi)
    e1, e2, w1, w2 = _route(logits_t, rb_ref[...])

    @pl.when(j == 0)
    def _():
        carry_ref[...] = jnp.zeros(carry_ref.shape, F32)

    eidx = lax.broadcasted_iota(I32, logits_t.shape, 0)
    oh1 = eidx == e1
    oh2 = eidx == e2
    onehot = jnp.where(oh1, 1.0, jnp.where(oh2, 1.0, 0.0))
    rank = _dot(onehot.astype(BF16), su_ref[...]) + carry_ref[:, 0:1]
    r1 = jnp.sum(jnp.where(oh1, rank, 0.0), axis=0, keepdims=True)
    r2 = jnp.sum(jnp.where(oh2, rank, 0.0), axis=0, keepdims=True)
    carry = carry_ref[...] + jnp.sum(onehot, axis=1, keepdims=True)
    carry_ref[...] = carry
    cnt_ref[0] = carry.astype(I32)

    zi = jnp.zeros((SUBLANES - 4, TILE), I32)
    ri_ref[0, j] = jnp.concatenate([e1, e2, r1.astype(I32), r2.astype(I32), zi], axis=0)
    zf = jnp.zeros((SUBLANES - 2, TILE), F32)
    rw_ref[0, 0] = jnp.concatenate([w1, w2, zf], axis=0)

    @pl.when(j == n_tiles - 1)
    def _():
        cpad = jnp.floor((carry[:, 0:1] + (SUBLANES - 1.0)) * (1.0 / SUBLANES)) * SUBLANES
        starts = [jnp.zeros((1, 1), F32)]
        for e in range(1, N_EXPERTS):
            starts.append(starts[-1] + cpad[e - 1:e, :])
        for jj in range(n_tiles):
            blk = ri_ref[0, jj]
            offs = []
            for k in range(2):
                ek = blk[k:k + 1, :]
                o = jnp.zeros(ek.shape, F32)
                for e in range(1, N_EXPERTS):
                    o = jnp.where(ek == e, starts[e], o)
                offs.append(o)
            pos = blk[2:4, :] + jnp.concatenate(offs, axis=0).astype(I32)
            ri_ref[0, jj] = jnp.concatenate([blk[0:2, :], pos, blk[4:, :]], axis=0)


def _outproj_call(acts, ws, x_src, x_off, modl, norm_g, rwh, rwl, rb, su, n_tiles):
    B = x_src.shape[0]
    n_act = len(acts)
    T = n_tiles * TILE
    const2 = lambda b, j: (0, 0)
    in_specs = []
    for a, a_off in acts:
        in_specs.append(pl.BlockSpec((1, TILE, a.shape[2]), functools.partial(lambda b, j, o: (b, j + o, 0), o=a_off)))
    for w in ws:
        in_specs.append(pl.BlockSpec(w.shape, const2))
    in_specs += [
        pl.BlockSpec((1, TILE, D_MODEL), lambda b, j: (b, j + x_off, 0)),
        pl.BlockSpec((1, 6, D_MODEL), lambda b, j: _mod_index(b, j + x_off)),
        pl.BlockSpec((1, D_MODEL), const2),
        pl.BlockSpec(rwh.shape, const2),
        pl.BlockSpec(rwl.shape, const2),
        pl.BlockSpec(rb.shape, const2),
        pl.BlockSpec(su.shape, const2),
    ]
    tok = lambda b, j: (b, j, 0)
    out_shape = [
        jax.ShapeDtypeStruct((B, T, D_MODEL), F32),
        jax.ShapeDtypeStruct((B, T * SUBLANES, 128), F32),
        jax.ShapeDtypeStruct((B, n_tiles, SUBLANES, TILE), I32),
        jax.ShapeDtypeStruct((B, n_tiles, SUBLANES, TILE), F32),
        jax.ShapeDtypeStruct((B, N_EXPERTS, 128), I32),
    ]
    out_specs = [
        pl.BlockSpec((1, TILE, D_MODEL), tok),
        pl.BlockSpec((1, TILE * SUBLANES, 128), tok),
        pl.BlockSpec((1, n_tiles, SUBLANES, TILE), lambda b, j: (b, 0, 0, 0)),
        pl.BlockSpec((1, 1, SUBLANES, TILE), lambda b, j: (b, j, 0, 0)),
        pl.BlockSpec((1, N_EXPERTS, 128), lambda b, j: (b, 0, 0)),
    ]
    return pl.pallas_call(
        functools.partial(_outproj_kernel, n_act, n_tiles),
        grid=(B, n_tiles),
        in_specs=in_specs,
        out_specs=out_specs,
        out_shape=out_shape,
        scratch_shapes=[pltpu.VMEM((N_EXPERTS, 128), F32)],
        compiler_params=_cparams(("arbitrary", "arbitrary")),
        name="outproj_route",
    )(*[a for a, _ in acts], *ws, x_src, modl, norm_g, rwh, rwl, rb, su)


def _moe_kernel(T, n_rows, cnt_ref, ri_ref, rw_ref, h_ref, w1_ref, w3_ref, w2_ref, y_ref,
                xb_ref, ob_ref, tokl_ref, off_ref):
    b = pl.program_id(0)
    e = pl.program_id(1)

    @pl.when(e == 0)
    def _():
        off_ref[0] = 0
        for i in range(N_EXPERTS):
            c = cnt_ref[b, i]
            off_ref[i + 1] = off_ref[i] + ((c + SUBLANES - 1) // SUBLANES) * SUBLANES

        for i in range(N_EXPERTS):
            for k in range(SUBLANES):
                tokl_ref[jnp.maximum(off_ref[i + 1] - SUBLANES + k, 0)] = 0

        def clear(i, carry):
            tokl_ref[off_ref[N_EXPERTS] + i] = 0
            return carry

        lax.fori_loop(0, MOE_RB, clear, 0, unroll=8)

        def place(t, carry):
            tokl_ref[ri_ref[2 * T + t]] = t
            tokl_ref[ri_ref[3 * T + t]] = t
            return carry

        lax.fori_loop(0, T, place, 0, unroll=8)

    c = cnt_ref[b, e]
    base = off_ref[e]

    def tile_rows(row):
        return pl.ds(pl.multiple_of(row * SUBLANES, SUBLANES), SUBLANES)

    def block(p0, n_blk_rows):
        def gather(i, carry2):
            xb_ref[tile_rows(i), :] = h_ref[0, tile_rows(tokl_ref[p0 + i]), :]
            return carry2

        lax.fori_loop(0, n_blk_rows, gather, 0, unroll=8)
        xb = _load_token_tiles(xb_ref, (), 0, n_blk_rows).astype(BF16)
        h1 = _dot(xb, w1_ref[0])
        h3 = _dot(xb, w3_ref[0])
        act = (h1 * _sigmoid(h1)) * h3
        _store_token_tiles(ob_ref, (), p0, _dot(act.astype(BF16), w2_ref[0]))

    n_big = (c + MOE_RB - MOE_RB_TAIL - 1) // MOE_RB
    n_big = jnp.maximum(n_big, 0)

    def big_block(rb, carry):
        block(pl.multiple_of(base + rb * MOE_RB, SUBLANES), MOE_RB)
        return carry

    lax.fori_loop(0, n_big, big_block, 0)

    @pl.when(c > n_big * MOE_RB)
    def _():
        block(pl.multiple_of(base + n_big * MOE_RB, SUBLANES), MOE_RB_TAIL)

    @pl.when(e == N_EXPERTS - 1)
    def _():
        def combine(t, carry):
            y_ref[0, tile_rows(t), :] = (rw_ref[t] * ob_ref[tile_rows(ri_ref[2 * T + t]), :]
                                         + rw_ref[T + t] * ob_ref[tile_rows(ri_ref[3 * T + t]), :])
            return carry

        lax.fori_loop(0, T, combine, 0, unroll=8)


def _moe_call(cnt, ri, rw, h, w1, w3, w2):
    B = h.shape[0]
    T = h.shape[1] // SUBLANES
    n_rows = 2 * T + N_EXPERTS * SUBLANES + MOE_RB
    smem = functools.partial(pl.BlockSpec, memory_space=pltpu.SMEM)
    once = pl.Buffered(1)
    return pl.pallas_call(
        functools.partial(_moe_kernel, T, n_rows),
        grid=(B, N_EXPERTS),
        in_specs=[
            smem(cnt.shape, lambda b, e: (0, 0)),
            smem((4 * T,), lambda b, e: (b,)),
            smem((4 * T,), lambda b, e: (b,)),
            pl.BlockSpec((1, T * SUBLANES, 128), lambda b, e: (b, 0, 0), pipeline_mode=once),
            pl.BlockSpec((1, D_MODEL, D_FF), lambda b, e: (e, 0, 0)),
            pl.BlockSpec((1, D_MODEL, D_FF), lambda b, e: (e, 0, 0)),
            pl.BlockSpec((1, D_FF, D_MODEL), lambda b, e: (e, 0, 0)),
        ],
        out_specs=pl.BlockSpec((1, T * SUBLANES, 128), lambda b, e: (b, 0, 0), pipeline_mode=once),
        out_shape=jax.ShapeDtypeStruct((B, T * SUBLANES, 128), F32),
        scratch_shapes=[
            pltpu.VMEM((MOE_RB * SUBLANES, 128), F32),
            pltpu.VMEM((n_rows * SUBLANES, 128), F32),
            pltpu.SMEM((n_rows,), I32),
            pltpu.SMEM((N_EXPERTS + 1,), I32),
        ],
        compiler_params=_cparams(("arbitrary", "arbitrary")),
        name="moe",
    )(cnt, ri, rw, h, w1, w3, w2)


def _route_tables(ri, rw, cnt):
    B, nt = ri.shape[:2]
    ri = jnp.transpose(ri[:, :, :4], (0, 2, 1, 3)).reshape(B * 4 * nt * TILE)
    rw = jnp.transpose(rw[:, :, :4], (0, 2, 1, 3)).reshape(B * 4 * nt * TILE)
    return cnt[:, :, 0], ri, rw


def _inproj1_kernel(x_ref, y_ref, mod0_ref, mod_ref, g_ref, w_ref, qg_ref, kg_ref, cos_ref, sin_ref,
                    xo_ref, q_ref, k_ref, v_ref):
    x = x_ref[0] + mod0_ref[0][5:6] * _load_token_tiles(y_ref, (0,), 0, TILE)
    xo_ref[0] = x
    mod = mod_ref[0]
    hb = (_rms(x) * g_ref[...] * (1.0 + mod[1:2]) + mod[0:1]).astype(BF16)
    cos = cos_ref[...]
    sin = sin_ref[...]

    def rope_head(a, gain):
        n = _rms(a) * gain
        return n * cos + pltpu.roll(n, HD_C // 2, 1) * sin

    qg = qg_ref[...]
    kg = kg_ref[...]
    for h in range(H_C):
        sl = slice(h * HD_C, (h + 1) * HD_C)
        q_ref[0, :, sl] = (rope_head(_dot(hb, w_ref[:, sl]), qg) * HD_C ** -0.5).astype(BF16)
    for h in range(KV_C):
        sl = slice(h * HD_C, (h + 1) * HD_C)
        ko = H_C * HD_C
        k_ref[0, :, sl] = rope_head(_dot(hb, w_ref[:, ko + h * HD_C:ko + (h + 1) * HD_C]), kg).astype(BF16)
    vo = (H_C + KV_C) * HD_C
    v_ref[0] = _dot(hb, w_ref[:, vo:vo + KV_C * HD_C]).astype(BF16)


def _inproj1_call(x1, y0, mod0, mod1, norm_g, w, qg, kg, cos, sin):
    B = x1.shape[0]
    tok = lambda b, j: (b, j, 0)
    const2 = lambda b, j: (0, 0)
    widths = [(D_MODEL, F32), (H_C * HD_C, BF16), (KV_C * HD_C, BF16), (KV_C * HD_C, BF16)]
    return pl.pallas_call(
        _inproj1_kernel,
        grid=(B, NT_ALL),
        in_specs=[
            pl.BlockSpec((1, TILE, D_MODEL), tok),
            pl.BlockSpec((1, TILE * SUBLANES, 128), tok),
            pl.BlockSpec((1, 6, D_MODEL), _mod_index),
            pl.BlockSpec((1, 6, D_MODEL), _mod_index),
            pl.BlockSpec((1, D_MODEL), const2),
            pl.BlockSpec(w.shape, const2),
            pl.BlockSpec(qg.shape, const2),
            pl.BlockSpec(kg.shape, const2),
            pl.BlockSpec((TILE, HD_C), lambda b, j: (j, 0)),
            pl.BlockSpec((TILE, HD_C), lambda b, j: (j, 0)),
        ],
        out_specs=[pl.BlockSpec((1, TILE, w_), tok) for w_, _ in widths],
        out_shape=[jax.ShapeDtypeStruct((B, T_ALL, w_), dt) for w_, dt in widths],
        compiler_params=_cparams(("arbitrary", "arbitrary")),
        name="inproj1",
    )(x1, y0, mod0, mod1, norm_g, w, qg, kg, cos, sin)


def _gqa_kernel(q_ref, k_ref, v_ref, o_ref):
    k = k_ref[0]
    v = v_ref[0]
    for h in range(H_C // KV_C):
        sl = slice(h * HD_C, (h + 1) * HD_C)
        s = _dot_nt(q_ref[0, :, sl], k)
        p = jnp.exp(s - s.max(axis=-1, keepdims=True))
        l = p.sum(axis=-1, keepdims=True)
        o_ref[0, :, sl] = (_dot(p.astype(BF16), v) / l).astype(BF16)


def _gqa_call(q, k, v):
    B = q.shape[0]
    gw = (H_C // KV_C) * HD_C
    q_off = CTX_LEN // GQA_TQ
    return pl.pallas_call(
        _gqa_kernel,
        grid=(B, KV_C, SEQ // GQA_TQ),
        in_specs=[
            pl.BlockSpec((1, GQA_TQ, gw), lambda b, g, j: (b, j + q_off, g)),
            pl.BlockSpec((1, T_ALL, HD_C), lambda b, g, j: (b, 0, g)),
            pl.BlockSpec((1, T_ALL, HD_C), lambda b, g, j: (b, 0, g)),
        ],
        out_specs=pl.BlockSpec((1, GQA_TQ, gw), lambda b, g, j: (b, j, g)),
        out_shape=jax.ShapeDtypeStruct((B, SEQ, H_C * HD_C), BF16),
        compiler_params=_cparams(("arbitrary", "arbitrary", "arbitrary")),
        name="gqa",
    )(q, k, v)


def _final_kernel(x_ref, y_ref, mod_ref, o_ref):
    o_ref[0] = x_ref[0] + mod_ref[0][5:6] * _load_token_tiles(y_ref, (0,), 0, TILE)


def _final_call(x, y, modl):
    B, T, D = x.shape
    tok = lambda b, j: (b, j, 0)
    return pl.pallas_call(
        _final_kernel,
        grid=(B, T // TILE),
        in_specs=[pl.BlockSpec((1, TILE, D), tok), pl.BlockSpec((1, TILE * SUBLANES, 128), tok),
                  pl.BlockSpec((1, 6, D), lambda b, j: (b, 0, 0))],
        out_specs=pl.BlockSpec((1, TILE, D), tok),
        out_shape=jax.ShapeDtypeStruct((B, T, D), F32),
        compiler_params=_cparams(("arbitrary", "arbitrary")),
        name="final_residual",
    )(x, y, modl)


def _chunk_tri(lower):
    i = np.arange(TILE)
    same = (i[:, None] // MLSTM_CHUNK) == (i[None, :] // MLSTM_CHUNK)
    tri = (i[None, :] <= i[:, None]) if lower else (i[None, :] >= i[:, None])
    return jnp.asarray((same & tri).astype(np.float32), BF16)


def _rope_tables():
    n_freq = HD_C // 4
    inv_freq = ROPE_THETA ** (-jnp.arange(n_freq, dtype=F32) / n_freq)
    t = jnp.arange(SEQ)
    rows = (t // GRID_W).astype(F32)
    cols = (t % GRID_W).astype(F32)
    ang = jnp.concatenate([rows[:, None] * inv_freq, cols[:, None] * inv_freq], axis=-1)
    cos, sin = jnp.cos(ang), jnp.sin(ang)
    cos_l = jnp.concatenate([cos, cos], axis=-1)
    sin_l = jnp.concatenate([-sin, sin], axis=-1)
    cos_all = jnp.concatenate([jnp.ones((CTX_LEN, HD_C), F32), cos_l], axis=0)
    sin_all = jnp.concatenate([jnp.zeros((CTX_LEN, HD_C), F32), sin_l], axis=0)
    return cos_all, sin_all


_HEAD_PERM = np.concatenate([np.arange(0, HD_C, 2), np.arange(1, HD_C, 2)])


def kernel(x, c, ctx, c_ctx, ada_w, ada_b, norm_mix_g, norm_ffn_g, even_w_in, even_w_out,
           na_q_norm_g, na_k_norm_g, na_rpb, mlstm_gate_b, mlstm_norm_g, odd_w_in, odd_w_out,
           gqa_q_norm_g, gqa_k_norm_g, router_w, router_b, exp_w1, exp_w3, exp_w2):
    B = x.shape[0]
    assert B <= N_MOD_CTX_ROW and x.shape[1:] == (SEQ, D_MODEL) and ctx.shape[1:] == (CTX_LEN, D_MODEL)
    n_g = 4 * H_B

    cvec = jnp.zeros((N_MOD_ROWS, D_MODEL), F32).at[:B].set(c).at[N_MOD_CTX_ROW].set(c_ctx)
    mod = _ada_call(cvec, ada_w, ada_b).reshape(2, N_MOD_ROWS, 6, D_MODEL)
    mod0, mod1 = mod[0], mod[1]

    x_all = jnp.concatenate([ctx, x], axis=1)

    w_in = even_w_in[0]
    n_main = w_in.shape[1] - n_g
    wm = w_in[:, :n_main].astype(BF16)
    wg_f = w_in[:, n_main:]
    wg = jnp.pad(wg_f, ((0, 0), (0, 128 - n_g))).astype(BF16)
    wgt = wg_f.T.astype(BF16)
    gb = jnp.pad(mlstm_gate_b[0].reshape(1, n_g), ((0, 0), (0, 128 - n_g)))
    gbt = mlstm_gate_b[0].reshape(n_g, 1)
    qg = jnp.tile(na_q_norm_g[0], H_A).reshape(1, D_A)
    kg = jnp.tile(na_k_norm_g[0], H_A).reshape(1, D_A)
    hid = np.arange(D_A) // HD_A
    bd = jnp.asarray((hid[:, None] == hid[None, :]).astype(np.float32), BF16)
    k_lo = 3 * D_A + H_B * DK_B
    wkt = w_in[:, k_lo:k_lo + H_B * DK_B].T.astype(BF16)
    aq, ak, av, mq, mv, og, mkt, gc, gr = _inproj0_call(
        x_all, mod0, norm_mix_g[0].reshape(1, D_MODEL), wm, wkt, wg, wgt, gb, gbt, qg, kg, bd,
        _chunk_tri(True), _chunk_tri(False))
    hm = _mlstm_call(mq, mkt, mv, gc, gr, og, mlstm_norm_g[0].reshape(1, H_B * DV_B))
    oa = _na_call(aq, ak, av, _na_bias_tables(na_rpb[0]))

    rw_t = router_w.T
    rwh = rw_t.astype(BF16)
    rwl = (rw_t - rwh.astype(F32)).astype(BF16)
    rb = router_b.reshape(N_EXPERTS, 1).astype(F32)
    i = np.arange(TILE)
    su = jnp.asarray((i[:, None] < i[None, :]).astype(np.float32), BF16)
    w_out = even_w_out[0].astype(BF16)
    x1, hp0, ri0, rw0, cnt0 = _outproj_call(
        [(oa, 0), (hm, 0)], [w_out[:D_A], w_out[D_A:]], x_all, 0, mod0,
        norm_ffn_g[0].reshape(1, D_MODEL), rwh, rwl, rb, su, NT_ALL)
    y0 = _moe_call(*_route_tables(ri0, rw0, cnt0), hp0,
                   exp_w1[0].astype(BF16), exp_w3[0].astype(BF16), exp_w2[0].astype(BF16))

    w1_in = odd_w_in[0]
    qk_cols = np.concatenate([h * HD_C + _HEAD_PERM for h in range(H_C + KV_C)])
    cols = np.concatenate([qk_cols, np.arange((H_C + KV_C) * HD_C, w1_in.shape[1])])
    w1_in = w1_in[:, cols].astype(BF16)
    cos_all, sin_all = _rope_tables()
    x2, q, k, v = _inproj1_call(
        x1, y0, mod0, mod1, norm_mix_g[1].reshape(1, D_MODEL), w1_in,
        gqa_q_norm_g[0][_HEAD_PERM].reshape(1, HD_C), gqa_k_norm_g[0][_HEAD_PERM].reshape(1, HD_C),
        cos_all, sin_all)
    o = _gqa_call(q, k, v)
    x3, hp1, ri1, rw1, cnt1 = _outproj_call(
        [(o, 0)], [odd_w_out[0].astype(BF16)], x2, NT_CTX, mod1,
        norm_ffn_g[1].reshape(1, D_MODEL), rwh, rwl, rb, su, SEQ // TILE)
    y1 = _moe_call(*_route_tables(ri1, rw1, cnt1), hp1,
                   exp_w1[1].astype(BF16), exp_w3[1].astype(BF16), exp_w2[1].astype(BF16))
    return _final_call(x3, y1, mod1)
```

```python
import functools

import numpy as np
import jax
import jax.numpy as jnp
from jax import lax
from jax.experimental import pallas as pl
from jax.experimental.pallas import tpu as pltpu

F32 = jnp.float32
BF16 = jnp.bfloat16
I32 = jnp.int32
U32 = jnp.uint32

D_MODEL = 1024
SEQ = 2048
GRID_W = 64
GRID_ROWS = SEQ // GRID_W
CTX_LEN = 256
T_ALL = CTX_LEN + SEQ
WIN_H = 8
WIN_W = 16
HD_A = 64
H_A = 8
D_A = H_A * HD_A
H_B = 4
DV_B = 128
DK_B = 64
MLSTM_CHUNK = 64
N_CHUNKS = T_ALL // MLSTM_CHUNK
N_CTX_CHUNKS = CTX_LEN // MLSTM_CHUNK
HD_C = 128
H_C = 8
KV_C = 2
ROPE_THETA = 10000.0
N_EXPERTS = 16
N_GROUPS = 4
EXP_PER_GROUP = 4
D_FF = 512
EPS = 1e-6
NEG_BIG = -1e30

TILE = 256
NT_ALL = T_ALL // TILE
NT_CTX = CTX_LEN // TILE
NA_QROWS = 4
NA_KROWS = WIN_H + NA_QROWS - 1
NA_TQ = NA_QROWS * GRID_W
NA_TK = NA_KROWS * GRID_W


def _na_geometry(r):
    u0 = min(max(r - WIN_H // 2, 0), GRID_ROWS - NA_KROWS)
    r0s = tuple(min(max(r + qi - WIN_H // 2, 0), GRID_ROWS - WIN_H) - r for qi in range(NA_QROWS))
    return (u0 - r, r0s)


def _na_cases():
    reps, step_case = [], []
    for r in range(0, GRID_ROWS, NA_QROWS):
        geo = _na_geometry(r)
        known = [_na_geometry(q) for q in reps]
        if geo not in known:
            reps.append(r)
            known.append(geo)
        step_case.append(known.index(geo))
    return tuple(reps), tuple(step_case)


NA_CASES, NA_STEP_CASE = _na_cases()
OUTPROJ_ROWS = 128
GQA_TQ = 256
MOE_RB = 256
MOE_RB_TAIL = 128
SUBLANES = 8
VMEM_LIMIT = 56 * 1024 * 1024


def _cparams(sem):
    return pltpu.CompilerParams(dimension_semantics=sem, vmem_limit_bytes=VMEM_LIMIT)


def _sigmoid(x):
    return 1.0 / (1.0 + jnp.exp(-x))


def _rms(x):
    return x * lax.rsqrt(jnp.mean(x * x, axis=-1, keepdims=True) + EPS)


def _dot(a, b):
    return jnp.dot(a, b, preferred_element_type=F32)


def _dot_nt(a, b):
    return lax.dot_general(a, b, (((1,), (1,)), ((), ())), preferred_element_type=F32)


def _dot_tn(a, b):
    return lax.dot_general(a, b, (((0,), (0,)), ((), ())), preferred_element_type=F32)


def _split_bf16(x, n):
    parts = []
    r = x
    for _ in range(n):
        p = r.astype(BF16)
        parts.append(p)
        r = r - p.astype(F32)
    return parts


LANE_TILES = D_MODEL // 128


def _load_token_tiles(ref, lead, tok0, n_tok):
    parts = [ref[(*lead, pl.ds(tok0 * SUBLANES + c, n_tok, stride=SUBLANES), slice(None))]
             for c in range(LANE_TILES)]
    return jnp.concatenate(parts, axis=1)


def _store_token_tiles(ref, lead, tok0, val):
    for c in range(LANE_TILES):
        ref[(*lead, pl.ds(tok0 * SUBLANES + c, val.shape[0], stride=SUBLANES), slice(None))] = (
            val[:, c * 128:(c + 1) * 128])


def _mod_index(b, j):
    return (jnp.where(j < NT_CTX, N_MOD_CTX_ROW, b), 0, 0)


def _combined_stream_specs():
    return [pl.BlockSpec((1, TILE, D_MODEL), lambda b, j: (b, jnp.maximum(j - NT_CTX, 0), 0)),
            pl.BlockSpec((1, TILE, D_MODEL), lambda b, j: (b, jnp.minimum(j, NT_CTX - 1), 0))]


N_MOD_ROWS = 16
N_MOD_CTX_ROW = 8


ADA_TN = 1536


def _ada_kernel(c_ref, w_ref, b_ref, o_ref):
    c = c_ref[...]
    s = (c * _sigmoid(c)).astype(BF16)
    o_ref[0] = _dot(s, w_ref[0].astype(BF16)) + b_ref[0]


def _ada_call(cvec, ada_w, ada_b):
    depth, d, n = ada_w.shape
    return pl.pallas_call(
        _ada_kernel,
        grid=(depth, n // ADA_TN),
        in_specs=[
            pl.BlockSpec((N_MOD_ROWS, d), lambda l, j: (0, 0)),
            pl.BlockSpec((1, d, ADA_TN), lambda l, j: (l, 0, j)),
            pl.BlockSpec((1, 1, ADA_TN), lambda l, j: (l, 0, j)),
        ],
        out_specs=pl.BlockSpec((1, N_MOD_ROWS, ADA_TN), lambda l, j: (l, 0, j)),
        out_shape=jax.ShapeDtypeStruct((depth, N_MOD_ROWS, n), F32),
        compiler_params=_cparams(("arbitrary", "arbitrary")),
        name="ada_mod",
    )(cvec, ada_w, ada_b.reshape(depth, 1, n))


def _log_sigmoid(x):
    return jnp.minimum(x, 0.0) - jnp.log1p(jnp.exp(-jnp.abs(x)))


def _inproj0_kernel(x_ref, ctx_ref, mod_ref, g_ref, wm_ref, wkt_ref, wg_ref, wgt_ref, gb_ref, gbt_ref,
                    qg_ref, kg_ref, bd_ref, tril_ref, triu_ref,
                    aq_ref, ak_ref, av_ref, mq_ref, mv_ref, og_ref, mkt_ref, gc_ref, gr_ref):
    mod = mod_ref[0]
    x_in = jnp.where(pl.program_id(1) < NT_CTX, ctx_ref[0], x_ref[0])
    h = _rms(x_in) * g_ref[...] * (1.0 + mod[1:2]) + mod[0:1]
    hb = h.astype(BF16)

    def proj(lo, hi):
        return _dot(hb, wm_ref[:, lo:hi])

    def head_norm(a, gain):
        hi_, lo_ = _split_bf16(a * a, 2)
        ss = _dot(hi_, bd_ref[...]) + _dot(lo_, bd_ref[...])
        return a * lax.rsqrt(ss * (1.0 / HD_A) + EPS) * gain

    aq_ref[0] = (head_norm(proj(0, 512), qg_ref[...]) * HD_A ** -0.5).astype(BF16)
    ak_ref[0] = head_norm(proj(512, 1024), kg_ref[...]).astype(BF16)
    av_ref[0] = proj(1024, 1536).astype(BF16)
    mq_ref[0] = (proj(1536, 1792) * DK_B ** -0.5).astype(BF16)
    kt = _dot_nt(wkt_ref[...], hb).astype(BF16)
    for c in range(TILE // MLSTM_CHUNK):
        mkt_ref[0, c] = kt[:, c * MLSTM_CHUNK:(c + 1) * MLSTM_CHUNK]
    mv_ref[0] = proj(2048, 2560).astype(BF16)
    og_ref[0] = _sigmoid(proj(2560, 3072)).astype(BF16)

    n_g = 4 * H_B
    gcol = _dot(hb, wg_ref[...]) + gb_ref[...]
    grow = _dot_nt(wgt_ref[...], hb) + gbt_ref[...]
    tril = tril_ref[...]
    triu = triu_ref[...]
    ls_c = _split_bf16(_log_sigmoid(gcol), 3)
    ls_r = _split_bf16(_log_sigmoid(grow), 3)
    pre_c = _dot(tril, ls_c[0]) + _dot(tril, ls_c[1]) + _dot(tril, ls_c[2])
    suf_c = _dot(triu, ls_c[0]) + _dot(triu, ls_c[1]) + _dot(triu, ls_c[2])
    pre_r = _dot(ls_r[0], triu) + _dot(ls_r[1], triu) + _dot(ls_r[2], triu)
    suf_r = _dot(ls_r[0], tril) + _dot(ls_r[1], tril) + _dot(ls_r[2], tril)
    cidx = lax.broadcasted_iota(I32, gcol.shape, 1)
    ridx = lax.broadcasted_iota(I32, grow.shape, 0)

    def pick(idx, raw, pre, suf):
        is_ff = (idx >= H_B) & (idx < 2 * H_B)
        is_fb = idx >= 3 * H_B
        return jnp.where(is_ff, pre, jnp.where(is_fb, suf, raw))

    gc_ref[0] = pick(cidx, gcol, pre_c, suf_c)[:, :n_g]
    grow = pick(ridx, grow, pre_r, suf_r)
    for c in range(TILE // MLSTM_CHUNK):
        gr_ref[0, c] = grow[:, c * MLSTM_CHUNK:(c + 1) * MLSTM_CHUNK]


def _inproj0_call(x, ctx, modl, norm_g, wm, wkt, wg, wgt, gb, gbt, qg, kg, bd, tril, triu):
    B = x.shape[0]
    n_g = 4 * H_B
    tok = lambda b, j: (b, j, 0)
    const2 = lambda b, j: (0, 0)
    chunked = lambda b, j: (b, j, 0, 0)
    tile_chunks = TILE // MLSTM_CHUNK
    outs = [
        (D_A, BF16), (D_A, BF16), (D_A, BF16),
        (H_B * DK_B, BF16), (H_B * DV_B, BF16),
        (H_B * DV_B, BF16),
    ]
    out_shape = [jax.ShapeDtypeStruct((B, T_ALL, w), dt) for w, dt in outs]
    out_specs = [pl.BlockSpec((1, TILE, w), tok) for w, _ in outs]
    out_shape += [jax.ShapeDtypeStruct((B, N_CHUNKS, H_B * DK_B, MLSTM_CHUNK), BF16),
                  jax.ShapeDtypeStruct((B, T_ALL, n_g), F32),
                  jax.ShapeDtypeStruct((B, N_CHUNKS, n_g, MLSTM_CHUNK), F32)]
    out_specs += [pl.BlockSpec((1, tile_chunks, H_B * DK_B, MLSTM_CHUNK), chunked),
                  pl.BlockSpec((1, TILE, n_g), tok),
                  pl.BlockSpec((1, tile_chunks, n_g, MLSTM_CHUNK), chunked)]
    return pl.pallas_call(
        _inproj0_kernel,
        grid=(B, NT_ALL),
        in_specs=_combined_stream_specs() + [
            pl.BlockSpec((1, 6, D_MODEL), _mod_index),
            pl.BlockSpec((1, D_MODEL), const2),
            pl.BlockSpec(wm.shape, const2),
            pl.BlockSpec(wkt.shape, const2),
            pl.BlockSpec(wg.shape, const2),
            pl.BlockSpec(wgt.shape, const2),
            pl.BlockSpec(gb.shape, const2),
            pl.BlockSpec(gbt.shape, const2),
            pl.BlockSpec(qg.shape, const2),
            pl.BlockSpec(kg.shape, const2),
            pl.BlockSpec(bd.shape, const2),
            pl.BlockSpec(tril.shape, const2),
            pl.BlockSpec(triu.shape, const2),
        ],
        out_specs=out_specs,
        out_shape=out_shape,
        compiler_params=_cparams(("arbitrary", "arbitrary")),
        name="inproj0",
    )(x, ctx, modl, norm_g, wm, wkt, wg, wgt, gb, gbt, qg, kg, bd, tril, triu)


def _mlstm_kernel(mq_ref, mkt_ref, mv_ref, gc_ref, gr_ref, og_ref, ng_ref, out_ref,
                  hf_ref, hb_ref, s_ref):
    L = MLSTM_CHUNK
    s_ref[...] = jnp.zeros(s_ref.shape, F32)
    ri = lax.broadcasted_iota(I32, (L, L), 0)
    ci = lax.broadcasted_iota(I32, (L, L), 1)
    masks = (ci <= ri, ci >= ri)
    ones_aug = jnp.ones((L, DV_B), BF16)

    def chain(step, m, d, h):
        if d == 0:
            chunk = step
        else:
            chunk = jnp.where(step < N_CTX_CHUNKS, N_CTX_CHUNKS - 1 - step, N_CHUNKS + N_CTX_CHUNKS - 1 - step)
        r0 = pl.multiple_of(chunk * L, L)
        rows = pl.ds(r0, L)
        q = mq_ref[0, rows, h * DK_B:(h + 1) * DK_B]
        kt = mkt_ref[0, chunk, h * DK_B:(h + 1) * DK_B, :]
        v = mv_ref[0, rows, h * DV_B:(h + 1) * DV_B]
        gi = 2 * H_B * d + h
        gcc = gc_ref[0, rows, :]
        grr = gr_ref[0, chunk]
        ig_col = gcc[:, gi:gi + 1]
        b_col = gcc[:, gi + H_B:gi + H_B + 1]
        ig_row = grr[gi:gi + 1, :]
        b_row = grr[gi + H_B:gi + H_B + 1, :]
        dm = jnp.where(masks[d], b_col - b_row + ig_row, NEG_BIG)
        a = b_col + m
        m_row = jnp.maximum(a, jnp.max(dm, axis=-1, keepdims=True))
        w_intra = jnp.exp(dm - m_row)
        w_inter = jnp.exp(a - m_row)
        qk = _dot(q, kt) * w_intra
        sidx = d * H_B + h
        state = s_ref[sidx]
        v_aug = jnp.concatenate([v, ones_aug], axis=1)
        numden = w_inter * _dot(q, state.astype(BF16)) + _dot(qk.astype(BF16), v_aug)
        num = numden[:, :DV_B]
        den = numden[:, DV_B:]
        hout = num / jnp.maximum(jnp.abs(den), jnp.exp(-m_row))
        dst = hf_ref if d == 0 else hb_ref
        dst[rows, h * DV_B:(h + 1) * DV_B] = hout
        b_last = b_col[L - 1:L, :] if d == 0 else b_col[0:1, :]
        g = b_last - b_col + ig_col
        m_new = jnp.maximum(b_last + m, jnp.max(g, axis=0, keepdims=True))
        w_old = jnp.exp(b_last + m - m_new)
        w_tok = jnp.exp(g - m_new)
        wv = (w_tok * v_aug.astype(F32)).astype(BF16)
        s_ref[sidx] = w_old * state + _dot(kt, wv)
        return m_new

    def step_fn(step, ms):
        return tuple(chain(step, ms[d * H_B + h], d, h) for d in range(2) for h in range(H_B))

    m0 = tuple(jnp.full((1, 1), NEG_BIG, F32) for _ in range(2 * H_B))
    lax.fori_loop(0, N_CHUNKS, step_fn, m0)

    def finish(i, carry):
        rows = pl.ds(pl.multiple_of(i * TILE, TILE), TILE)
        hs = hf_ref[rows, :] + hb_ref[rows, :]
        ng = ng_ref[...]
        og = og_ref[0, rows, :].astype(F32)
        for h in range(H_B):
            sl = slice(h * DV_B, (h + 1) * DV_B)
            out_ref[0, rows, sl] = (_rms(hs[:, sl]) * ng[:, sl] * og[:, sl]).astype(BF16)
        return carry

    lax.fori_loop(0, T_ALL // TILE, finish, 0)


def _mlstm_call(mq, mk, mv, gc, gr, og, ng):
    B = mq.shape[0]
    full = lambda a: pl.BlockSpec((1,) + a.shape[1:], lambda b: (b,) + (0,) * (a.ndim - 1))
    return pl.pallas_call(
        _mlstm_kernel,
        grid=(B,),
        in_specs=[full(mq), full(mk), full(mv), full(gc), full(gr), full(og),
                  pl.BlockSpec(ng.shape, lambda b: (0, 0))],
        out_specs=pl.BlockSpec((1, T_ALL, H_B * DV_B), lambda b: (b, 0, 0)),
        out_shape=jax.ShapeDtypeStruct((B, T_ALL, H_B * DV_B), BF16),
        scratch_shapes=[
            pltpu.VMEM((T_ALL, H_B * DV_B), F32),
            pltpu.VMEM((T_ALL, H_B * DV_B), F32),
            pltpu.VMEM((2 * H_B, DK_B, 2 * DV_B), F32),
        ],
        compiler_params=_cparams(("arbitrary",)),
        name="mlstm",
    )(mq, mk, mv, gc, gr, og, ng)


def _na_bias_tables(rpb):
    kh = WIN_H
    n_drow = 2 * WIN_H - 1
    qcol = np.arange(GRID_W)
    col_start = np.clip(qcol - WIN_W // 2, 0, GRID_W - WIN_W)
    col_ok = (qcol[None, :] >= col_start[:, None]) & (qcol[None, :] < col_start[:, None] + WIN_W)
    dcol = qcol[None, :] - qcol[:, None] + (WIN_W - 1)
    onehot = (dcol[None] == np.arange(2 * WIN_W - 1)[:, None, None]) & col_ok[None]
    blocks = jnp.einsum('hdx,xck->hdck', rpb.astype(F32), jnp.asarray(onehot, F32),
                        precision=lax.Precision.HIGHEST)
    blocks = jnp.where(col_ok[None, None], blocks, NEG_BIG)
    outside = jnp.full((H_A, 1, GRID_W, GRID_W), NEG_BIG, F32)
    blocks = jnp.concatenate([blocks, outside], axis=1)
    idx = np.full((len(NA_CASES), NA_QROWS, NA_KROWS), n_drow, np.int32)
    for ci, r in enumerate(NA_CASES):
        u0 = int(np.clip(r - kh // 2, 0, GRID_ROWS - NA_KROWS))
        for qi in range(NA_QROWS):
            rq = r + qi
            r0 = int(np.clip(rq - kh // 2, 0, GRID_ROWS - kh))
            for ui in range(NA_KROWS):
                kr = u0 + ui
                if r0 <= kr < r0 + kh:
                    idx[ci, qi, ui] = kr - rq + (WIN_H - 1)
    tab = blocks[:, idx]
    tab = jnp.transpose(tab, (1, 0, 2, 4, 3, 5))
    return tab.reshape(len(NA_CASES), H_A, NA_TQ, NA_TK)


def _na_case(j):
    case = jnp.int32(NA_STEP_CASE[0])
    for step, c in enumerate(NA_STEP_CASE):
        if c != NA_STEP_CASE[0]:
            case = jnp.where(j - 1 == step, c, case)
    return case


def _na_kernel(q_ref, k_ref, v_ref, bias_ref, o_ref):
    j = pl.program_id(1)
    lane = lax.broadcasted_iota(I32, (1, 2 * HD_A), 1)
    lo_half = lane < HD_A

    def attend(q_rows, n_q, key_sets, bias_for_head):
        for pp in range(H_A // 2):
            lanes = slice(pp * 2 * HD_A, (pp + 1) * 2 * HD_A)
            qp = q_ref[0, q_rows, lanes]
            ks = [k_ref[0, rs, lanes] for rs in key_sets]
            vs = [v_ref[0, rs, lanes] for rs in key_sets]
            halves = []
            for hh in range(2):
                qm = jnp.where(lo_half if hh == 0 else ~lo_half, qp, jnp.zeros_like(qp))
                ss = [_dot_nt(qm, kk) for kk in ks]
                if bias_for_head is not None:
                    ss[0] = ss[0] + bias_for_head(2 * pp + hh)
                m = ss[0].max(axis=-1, keepdims=True)
                for s in ss[1:]:
                    m = jnp.maximum(m, s.max(axis=-1, keepdims=True))
                ps = [jnp.exp(s - m) for s in ss]
                l = ps[0].sum(axis=-1, keepdims=True)
                for p in ps[1:]:
                    l = l + p.sum(axis=-1, keepdims=True)
                acc = _dot(ps[0].astype(BF16), vs[0])
                for p, vv in zip(ps[1:], vs[1:]):
                    acc = acc + _dot(p.astype(BF16), vv)
                halves.append(acc / l)
            o_ref[0, q_rows, lanes] = jnp.where(lo_half, halves[0], halves[1]).astype(BF16)

    ctx_rows = pl.ds(0, CTX_LEN)

    @pl.when(j == 0)
    def _():
        attend(ctx_rows, CTX_LEN, [ctx_rows], None)

    @pl.when(j > 0)
    def _():
        r = (j - 1) * NA_QROWS
        u0 = jnp.clip(r - WIN_H // 2, 0, GRID_ROWS - NA_KROWS)
        q_rows = pl.ds(pl.multiple_of(CTX_LEN + r * GRID_W, NA_TQ), NA_TQ)
        k_rows = pl.ds(pl.multiple_of(CTX_LEN + u0 * GRID_W, GRID_W), NA_TK)
        attend(q_rows, NA_TQ, [k_rows, ctx_rows], lambda head: bias_ref[0, head])


def _na_call(aq, ak, av, bias):
    B = aq.shape[0]
    full = pl.BlockSpec((1, T_ALL, D_A), lambda b, j: (b, 0, 0))
    return pl.pallas_call(
        _na_kernel,
        grid=(B, 1 + GRID_ROWS // NA_QROWS),
        in_specs=[full, full, full,
                  pl.BlockSpec((1, H_A, NA_TQ, NA_TK), lambda b, j: (_na_case(j), 0, 0, 0))],
        out_specs=full,
        out_shape=jax.ShapeDtypeStruct((B, T_ALL, D_A), BF16),
        compiler_params=_cparams(("arbitrary", "arbitrary")),
        name="nbr_attn",
    )(aq, ak, av, bias)


def _route(logits_t, rb_col):
    sc = _sigmoid(logits_t)
    sel = sc + rb_col
    selr = [sel[e:e + 1, :] for e in range(N_EXPERTS)]
    scr = [sc[e:e + 1, :] for e in range(N_EXPERTS)]
    gscore = []
    for g in range(N_GROUPS):
        a, b, c, d = selr[EXP_PER_GROUP * g:EXP_PER_GROUP * (g + 1)]
        s1, t1 = jnp.maximum(a, b), jnp.minimum(a, b)
        s2, t2 = jnp.maximum(c, d), jnp.minimum(c, d)
        gscore.append(jnp.maximum(s1, s2) + jnp.maximum(jnp.minimum(s1, s2), jnp.maximum(t1, t2)))
    best = gscore[0]
    gi = jnp.zeros(best.shape, I32)
    for g in range(1, N_GROUPS):
        better = gscore[g] > best
        gi = jnp.where(better, g, gi)
        best = jnp.where(better, gscore[g], best)
    vs, ws = [], []
    for k in range(EXP_PER_GROUP):
        v = selr[k]
        w = scr[k]
        for g in range(1, N_GROUPS):
            v = jnp.where(gi == g, selr[EXP_PER_GROUP * g + k], v)
            w = jnp.where(gi == g, scr[EXP_PER_GROUP * g + k], w)
        vs.append(v)
        ws.append(w)
    b1, i1 = vs[0], jnp.zeros(best.shape, I32)
    for k in range(1, EXP_PER_GROUP):
        better = vs[k] > b1
        i1 = jnp.where(better, k, i1)
        b1 = jnp.where(better, vs[k], b1)
    b2 = jnp.full(best.shape, -jnp.inf, F32)
    i2 = jnp.zeros(best.shape, I32)
    for k in range(EXP_PER_GROUP):
        vk = jnp.where(i1 == k, -jnp.inf, vs[k])
        better = vk > b2
        i2 = jnp.where(better, k, i2)
        b2 = jnp.where(better, vk, b2)
    w1 = ws[0]
    w2 = ws[0]
    for k in range(1, EXP_PER_GROUP):
        w1 = jnp.where(i1 == k, ws[k], w1)
        w2 = jnp.where(i2 == k, ws[k], w2)
    tot = w1 + w2
    return gi * EXP_PER_GROUP + i1, gi * EXP_PER_GROUP + i2, w1 / tot, w2 / tot


def _outproj_kernel(n_act, has_ctx, *refs):
    acts = refs[:n_act]
    ws = refs[n_act:2 * n_act]
    refs = refs[2 * n_act:]
    if has_ctx:
        x_in = jnp.where(pl.program_id(1) < NT_CTX, refs[1][0], refs[0][0])
        refs = refs[1:]
    else:
        x_in = refs[0][0]
    _, mod_ref, g_ref, rwh_ref, rwl_ref, xo_ref, hp_ref, lg_ref = refs
    mod = mod_ref[0]
    for r0 in range(0, TILE, OUTPROJ_ROWS):
        rows = slice(r0, r0 + OUTPROJ_ROWS)
        halves = []
        for n0 in range(0, D_MODEL, D_MODEL // 2):
            cols = slice(n0, n0 + D_MODEL // 2)
            o = _dot(acts[0][0, rows, :], ws[0][:, cols])
            for a, w in zip(acts[1:], ws[1:]):
                o = o + _dot(a[0, rows, :], w[:, cols])
            halves.append(o)
        x = x_in[rows, :] + mod[2:3] * jnp.concatenate(halves, axis=1)
        xo_ref[0, rows, :] = x
        h = _rms(x) * g_ref[...] * (1.0 + mod[4:5]) + mod[3:4]
        _store_token_tiles(hp_ref, (0,), r0, h)
        h_hi, h_lo = _split_bf16(h, 2)
        lg_ref[0, :, rows] = (_dot_nt(rwh_ref[...], h_hi) + _dot_nt(rwh_ref[...], h_lo)
                              + _dot_nt(rwl_ref[...], h_hi))


def _outproj_call(acts, ws, x_src, x_off, modl, norm_g, rwh, rwl, n_tiles, ctx_src=None):
    B = x_src.shape[0]
    n_act = len(acts)
    T = n_tiles * TILE
    const2 = lambda b, j: (0, 0)
    in_specs = []
    for a, a_off in acts:
        in_specs.append(pl.BlockSpec((1, TILE, a.shape[2]), functools.partial(lambda b, j, o: (b, j + o, 0), o=a_off)))
    for w in ws:
        in_specs.append(pl.BlockSpec(w.shape, const2))
    streams = [x_src]
    if ctx_src is None:
        in_specs.append(pl.BlockSpec((1, TILE, D_MODEL), lambda b, j: (b, j + x_off, 0)))
    else:
        assert x_off == 0
        in_specs += _combined_stream_specs()
        streams.append(ctx_src)
    in_specs += [
        pl.BlockSpec((1, 6, D_MODEL), lambda b, j: _mod_index(b, j + x_off)),
        pl.BlockSpec((1, D_MODEL), const2),
        pl.BlockSpec(rwh.shape, const2),
        pl.BlockSpec(rwl.shape, const2),
    ]
    tok = lambda b, j: (b, j, 0)
    out_shape = [
        jax.ShapeDtypeStruct((B, T, D_MODEL), F32),
        jax.ShapeDtypeStruct((B, T * SUBLANES, 128), F32),
        jax.ShapeDtypeStruct((B, N_EXPERTS, T), F32),
    ]
    out_specs = [
        pl.BlockSpec((1, TILE, D_MODEL), tok),
        pl.BlockSpec((1, TILE * SUBLANES, 128), tok),
        pl.BlockSpec((1, N_EXPERTS, TILE), lambda b, j: (b, 0, j)),
    ]
    return pl.pallas_call(
        functools.partial(_outproj_kernel, n_act, ctx_src is not None),
        grid=(B, n_tiles),
        in_specs=in_specs,
        out_specs=out_specs,
        out_shape=out_shape,
        compiler_params=_cparams(("arbitrary", "arbitrary")),
        name="outproj",
    )(*[a for a, _ in acts], *ws, *streams, modl, norm_g, rwh, rwl)


def _route_kernel(lg_ref, rb_ref, su_ref, ri_ref, rw_ref, cnt_ref):
    logits_t = lg_ref[0]
    T = logits_t.shape[1]
    e1, e2, w1, w2 = _route(logits_t, rb_ref[...])
    eidx = lax.broadcasted_iota(I32, logits_t.shape, 0)
    oh1 = eidx == e1
    oh2 = eidx == e2
    onehot = jnp.where(oh1, 1.0, jnp.where(oh2, 1.0, 0.0))
    count = jnp.zeros((N_EXPERTS, 1), F32)
    ranks = []
    for jj in range(T // TILE):
        oh = onehot[:, jj * TILE:(jj + 1) * TILE]
        ranks.append(_dot(oh.astype(BF16), su_ref[...]) + count)
        count = count + jnp.sum(oh, axis=1, keepdims=True)
    cpad = jnp.floor((count + (SUBLANES - 1.0)) * (1.0 / SUBLANES)) * SUBLANES
    ecol = lax.broadcasted_iota(I32, (N_EXPERTS, 1), 0)
    start = jnp.zeros((N_EXPERTS, 1), F32)
    for e in range(N_EXPERTS - 1):
        start = start + jnp.where(ecol > e, cpad[e:e + 1, :], 0.0)
    row = jnp.concatenate(ranks, axis=1) + start
    r1 = jnp.sum(jnp.where(oh1, row, 0.0), axis=0, keepdims=True)
    r2 = jnp.sum(jnp.where(oh2, row, 0.0), axis=0, keepdims=True)
    zi = jnp.zeros((SUBLANES - 4, T), I32)
    ri_ref[0] = jnp.concatenate([e1, e2, r1.astype(I32), r2.astype(I32), zi], axis=0)
    zf = jnp.zeros((SUBLANES - 2, T), F32)
    rw_ref[0] = jnp.concatenate([w1, w2, zf], axis=0)
    cnt_ref[0] = jnp.broadcast_to(count, (N_EXPERTS, 128)).astype(I32)


def _route_call(lg, rb, su):
    B, _, T = lg.shape
    per_sample = lambda b: (b, 0, 0)
    return pl.pallas_call(
        _route_kernel,
        grid=(B,),
        in_specs=[pl.BlockSpec((1, N_EXPERTS, T), per_sample),
                  pl.BlockSpec(rb.shape, lambda b: (0, 0)),
                  pl.BlockSpec(su.shape, lambda b: (0, 0))],
        out_specs=[pl.BlockSpec((1, SUBLANES, T), per_sample),
                   pl.BlockSpec((1, SUBLANES, T), per_sample),
                   pl.BlockSpec((1, N_EXPERTS, 128), per_sample)],
        out_shape=[jax.ShapeDtypeStruct((B, SUBLANES, T), I32),
                   jax.ShapeDtypeStruct((B, SUBLANES, T), F32),
                   jax.ShapeDtypeStruct((B, N_EXPERTS, 128), I32)],
        compiler_params=_cparams(("arbitrary",)),
        name="route",
    )(lg, rb, su)


def _moe_kernel(T, n_rows, cnt_ref, ri_ref, rw_ref, h_ref, w1_ref, w3_ref, w2_ref, y_ref,
                xb_ref, ob_ref, tokl_ref, off_ref):
    b = pl.program_id(0)
    e = pl.program_id(1)

    @pl.when(e == 0)
    def _():
        off_ref[0] = 0
        for i in range(N_EXPERTS):
            c = cnt_ref[b, i]
            off_ref[i + 1] = off_ref[i] + ((c + SUBLANES - 1) // SUBLANES) * SUBLANES

        for i in range(N_EXPERTS):
            for k in range(SUBLANES):
                tokl_ref[jnp.maximum(off_ref[i + 1] - SUBLANES + k, 0)] = 0

        def clear(i, carry):
            tokl_ref[off_ref[N_EXPERTS] + i] = 0
            return carry

        lax.fori_loop(0, MOE_RB, clear, 0, unroll=8)

        def place(t, carry):
            tokl_ref[ri_ref[2 * T + t]] = t
            tokl_ref[ri_ref[3 * T + t]] = t
            return carry

        lax.fori_loop(0, T, place, 0, unroll=8)

    c = cnt_ref[b, e]
    base = off_ref[e]

    def tile_rows(row):
        return pl.ds(pl.multiple_of(row * SUBLANES, SUBLANES), SUBLANES)

    def block(p0, n_blk_rows):
        def gather(i, carry2):
            xb_ref[tile_rows(i), :] = h_ref[0, tile_rows(tokl_ref[p0 + i]), :]
            return carry2

        lax.fori_loop(0, n_blk_rows, gather, 0, unroll=8)
        xb = _load_token_tiles(xb_ref, (), 0, n_blk_rows).astype(BF16)
        h1 = _dot(xb, w1_ref[0, 0])
        h3 = _dot(xb, w3_ref[0, 0])
        act = (h1 * _sigmoid(h1)) * h3
        _store_token_tiles(ob_ref, (), p0, _dot(act.astype(BF16), w2_ref[0, 0]))

    n_big = (c + MOE_RB - MOE_RB_TAIL - 1) // MOE_RB
    n_big = jnp.maximum(n_big, 0)

    def big_block(rb, carry):
        block(pl.multiple_of(base + rb * MOE_RB, SUBLANES), MOE_RB)
        return carry

    lax.fori_loop(0, n_big, big_block, 0)

    @pl.when(c > n_big * MOE_RB)
    def _():
        block(pl.multiple_of(base + n_big * MOE_RB, SUBLANES), MOE_RB_TAIL)

    @pl.when(e == N_EXPERTS - 1)
    def _():
        def combine(t, carry):
            y_ref[0, tile_rows(t), :] = (rw_ref[t] * ob_ref[tile_rows(ri_ref[2 * T + t]), :]
                                         + rw_ref[T + t] * ob_ref[tile_rows(ri_ref[3 * T + t]), :])
            return carry

        lax.fori_loop(0, T, combine, 0, unroll=8)


def _moe_call(cnt, ri, rw, h, w1, w3, w2, layer):
    B = h.shape[0]
    T = h.shape[1] // SUBLANES
    n_rows = 2 * T + N_EXPERTS * SUBLANES + MOE_RB
    smem = functools.partial(pl.BlockSpec, memory_space=pltpu.SMEM)
    once = pl.Buffered(1)
    return pl.pallas_call(
        functools.partial(_moe_kernel, T, n_rows),
        grid=(B, N_EXPERTS),
        in_specs=[
            smem(cnt.shape, lambda b, e: (0, 0)),
            smem((4 * T,), lambda b, e: (b,)),
            smem((4 * T,), lambda b, e: (b,)),
            pl.BlockSpec((1, T * SUBLANES, 128), lambda b, e: (b, 0, 0), pipeline_mode=once),
            pl.BlockSpec((1, 1, D_MODEL, D_FF), lambda b, e: (layer, e, 0, 0)),
            pl.BlockSpec((1, 1, D_MODEL, D_FF), lambda b, e: (layer, e, 0, 0)),
            pl.BlockSpec((1, 1, D_FF, D_MODEL), lambda b, e: (layer, e, 0, 0)),
        ],
        out_specs=pl.BlockSpec((1, T * SUBLANES, 128), lambda b, e: (b, 0, 0), pipeline_mode=once),
        out_shape=jax.ShapeDtypeStruct((B, T * SUBLANES, 128), F32),
        scratch_shapes=[
            pltpu.VMEM((MOE_RB * SUBLANES, 128), F32),
            pltpu.VMEM((n_rows * SUBLANES, 128), F32),
            pltpu.SMEM((n_rows,), I32),
            pltpu.SMEM((N_EXPERTS + 1,), I32),
        ],
        compiler_params=_cparams(("arbitrary", "arbitrary")),
        name="moe",
    )(cnt, ri, rw, h, w1, w3, w2)


def _route_tables(ri, rw, cnt):
    return cnt[:, :, 0], ri[:, :4].reshape(-1), rw[:, :4].reshape(-1)


def _inproj1_kernel(x_ref, y_ref, mod0_ref, mod_ref, g_ref, w_ref, qg_ref, kg_ref, cos_ref, sin_ref,
                    xo_ref, q_ref, k_ref, v_ref):
    x = x_ref[0] + mod0_ref[0][5:6] * _load_token_tiles(y_ref, (0,), 0, TILE)
    xo_ref[0] = x
    mod = mod_ref[0]
    hb = (_rms(x) * g_ref[...] * (1.0 + mod[1:2]) + mod[0:1]).astype(BF16)
    cos = cos_ref[...]
    sin = sin_ref[...]

    def rope_head(a, gain):
        n = _rms(a) * gain
        return n * cos + pltpu.roll(n, HD_C // 2, 1) * sin

    qg = qg_ref[...]
    kg = kg_ref[...]
    for h in range(H_C):
        sl = slice(h * HD_C, (h + 1) * HD_C)
        q_ref[0, :, sl] = (rope_head(_dot(hb, w_ref[:, sl]), qg) * HD_C ** -0.5).astype(BF16)
    for h in range(KV_C):
        sl = slice(h * HD_C, (h + 1) * HD_C)
        ko = H_C * HD_C
        k_ref[0, :, sl] = rope_head(_dot(hb, w_ref[:, ko + h * HD_C:ko + (h + 1) * HD_C]), kg).astype(BF16)
    vo = (H_C + KV_C) * HD_C
    v_ref[0] = _dot(hb, w_ref[:, vo:vo + KV_C * HD_C]).astype(BF16)


def _inproj1_call(x1, y0, mod0, mod1, norm_g, w, qg, kg, cos, sin):
    B = x1.shape[0]
    tok = lambda b, j: (b, j, 0)
    const2 = lambda b, j: (0, 0)
    widths = [(D_MODEL, F32), (H_C * HD_C, BF16), (KV_C * HD_C, BF16), (KV_C * HD_C, BF16)]
    return pl.pallas_call(
        _inproj1_kernel,
        grid=(B, NT_ALL),
        in_specs=[
            pl.BlockSpec((1, TILE, D_MODEL), tok),
            pl.BlockSpec((1, TILE * SUBLANES, 128), tok),
            pl.BlockSpec((1, 6, D_MODEL), _mod_index),
            pl.BlockSpec((1, 6, D_MODEL), _mod_index),
            pl.BlockSpec((1, D_MODEL), const2),
            pl.BlockSpec(w.shape, const2),
            pl.BlockSpec(qg.shape, const2),
            pl.BlockSpec(kg.shape, const2),
            pl.BlockSpec((TILE, HD_C), lambda b, j: (j, 0)),
            pl.BlockSpec((TILE, HD_C), lambda b, j: (j, 0)),
        ],
        out_specs=[pl.BlockSpec((1, TILE, w_), tok) for w_, _ in widths],
        out_shape=[jax.ShapeDtypeStruct((B, T_ALL, w_), dt) for w_, dt in widths],
        compiler_params=_cparams(("arbitrary", "arbitrary")),
        name="inproj1",
    )(x1, y0, mod0, mod1, norm_g, w, qg, kg, cos, sin)


def _gqa_kernel(q_ref, k_ref, v_ref, o_ref):
    k = k_ref[0]
    v = v_ref[0]
    for h in range(H_C // KV_C):
        sl = slice(h * HD_C, (h + 1) * HD_C)
        s = _dot_nt(q_ref[0, :, sl], k)
        p = jnp.exp(s - s.max(axis=-1, keepdims=True))
        l = p.sum(axis=-1, keepdims=True)
        o_ref[0, :, sl] = (_dot(p.astype(BF16), v) / l).astype(BF16)


def _gqa_call(q, k, v):
    B = q.shape[0]
    gw = (H_C // KV_C) * HD_C
    q_off = CTX_LEN // GQA_TQ
    return pl.pallas_call(
        _gqa_kernel,
        grid=(B, KV_C, SEQ // GQA_TQ),
        in_specs=[
            pl.BlockSpec((1, GQA_TQ, gw), lambda b, g, j: (b, j + q_off, g)),
            pl.BlockSpec((1, T_ALL, HD_C), lambda b, g, j: (b, 0, g)),
            pl.BlockSpec((1, T_ALL, HD_C), lambda b, g, j: (b, 0, g)),
        ],
        out_specs=pl.BlockSpec((1, GQA_TQ, gw), lambda b, g, j: (b, j, g)),
        out_shape=jax.ShapeDtypeStruct((B, SEQ, H_C * HD_C), BF16),
        compiler_params=_cparams(("arbitrary", "arbitrary", "arbitrary")),
        name="gqa",
    )(q, k, v)


def _final_kernel(x_ref, y_ref, mod_ref, o_ref):
    o_ref[0] = x_ref[0] + mod_ref[0][5:6] * _load_token_tiles(y_ref, (0,), 0, TILE)


def _final_call(x, y, modl):
    B, T, D = x.shape
    tok = lambda b, j: (b, j, 0)
    return pl.pallas_call(
        _final_kernel,
        grid=(B, T // TILE),
        in_specs=[pl.BlockSpec((1, TILE, D), tok), pl.BlockSpec((1, TILE * SUBLANES, 128), tok),
                  pl.BlockSpec((1, 6, D), lambda b, j: (b, 0, 0))],
        out_specs=pl.BlockSpec((1, TILE, D), tok),
        out_shape=jax.ShapeDtypeStruct((B, T, D), F32),
        compiler_params=_cparams(("arbitrary", "arbitrary")),
        name="final_residual",
    )(x, y, modl)


def _chunk_tri(lower):
    i = np.arange(TILE)
    same = (i[:, None] // MLSTM_CHUNK) == (i[None, :] // MLSTM_CHUNK)
    tri = (i[None, :] <= i[:, None]) if lower else (i[None, :] >= i[:, None])
    return jnp.asarray((same & tri).astype(np.float32), BF16)


def _rope_tables():
    n_freq = HD_C // 4
    inv_freq = ROPE_THETA ** (-jnp.arange(n_freq, dtype=F32) / n_freq)
    t = jnp.arange(SEQ)
    rows = (t // GRID_W).astype(F32)
    cols = (t % GRID_W).astype(F32)
    ang = jnp.concatenate([rows[:, None] * inv_freq, cols[:, None] * inv_freq], axis=-1)
    cos, sin = jnp.cos(ang), jnp.sin(ang)
    cos_l = jnp.concatenate([cos, cos], axis=-1)
    sin_l = jnp.concatenate([-sin, sin], axis=-1)
    cos_all = jnp.concatenate([jnp.ones((CTX_LEN, HD_C), F32), cos_l], axis=0)
    sin_all = jnp.concatenate([jnp.zeros((CTX_LEN, HD_C), F32), sin_l], axis=0)
    return cos_all, sin_all


_HEAD_PERM = np.concatenate([np.arange(0, HD_C, 2), np.arange(1, HD_C, 2)])


def kernel(x, c, ctx, c_ctx, ada_w, ada_b, norm_mix_g, norm_ffn_g, even_w_in, even_w_out,
           na_q_norm_g, na_k_norm_g, na_rpb, mlstm_gate_b, mlstm_norm_g, odd_w_in, odd_w_out,
           gqa_q_norm_g, gqa_k_norm_g, router_w, router_b, exp_w1, exp_w3, exp_w2):
    B = x.shape[0]
    assert B <= N_MOD_CTX_ROW and x.shape[1:] == (SEQ, D_MODEL) and ctx.shape[1:] == (CTX_LEN, D_MODEL)
    n_g = 4 * H_B

    cvec = jnp.zeros((N_MOD_ROWS, D_MODEL), F32).at[:B].set(c).at[N_MOD_CTX_ROW].set(c_ctx)
    mod = _ada_call(cvec, ada_w, ada_b).reshape(2, N_MOD_ROWS, 6, D_MODEL)
    mod0, mod1 = mod[0], mod[1]

    w_in = even_w_in[0]
    n_main = w_in.shape[1] - n_g
    wm = w_in[:, :n_main].astype(BF16)
    wg_f = w_in[:, n_main:]
    wg = jnp.pad(wg_f, ((0, 0), (0, 128 - n_g))).astype(BF16)
    wgt = wg_f.T.astype(BF16)
    gb = jnp.pad(mlstm_gate_b[0].reshape(1, n_g), ((0, 0), (0, 128 - n_g)))
    gbt = mlstm_gate_b[0].reshape(n_g, 1)
    qg = jnp.tile(na_q_norm_g[0], H_A).reshape(1, D_A)
    kg = jnp.tile(na_k_norm_g[0], H_A).reshape(1, D_A)
    hid = np.arange(D_A) // HD_A
    bd = jnp.asarray((hid[:, None] == hid[None, :]).astype(np.float32), BF16)
    k_lo = 3 * D_A + H_B * DK_B
    wkt = w_in[:, k_lo:k_lo + H_B * DK_B].T.astype(BF16)
    aq, ak, av, mq, mv, og, mkt, gc, gr = _inproj0_call(
        x, ctx, mod0, norm_mix_g[0].reshape(1, D_MODEL), wm, wkt, wg, wgt, gb, gbt, qg, kg, bd,
        _chunk_tri(True), _chunk_tri(False))
    hm = _mlstm_call(mq, mkt, mv, gc, gr, og, mlstm_norm_g[0].reshape(1, H_B * DV_B))
    oa = _na_call(aq, ak, av, _na_bias_tables(na_rpb[0]))

    rw_t = router_w.T
    rwh = rw_t.astype(BF16)
    rwl = (rw_t - rwh.astype(F32)).astype(BF16)
    rb = router_b.reshape(N_EXPERTS, 1).astype(F32)
    i = np.arange(TILE)
    su = jnp.asarray((i[:, None] < i[None, :]).astype(np.float32), BF16)
    w_out = even_w_out[0].astype(BF16)
    ew1, ew3, ew2 = exp_w1.astype(BF16), exp_w3.astype(BF16), exp_w2.astype(BF16)
    x1, hp0, lg0 = _outproj_call(
        [(oa, 0), (hm, 0)], [w_out[:D_A], w_out[D_A:]], x, 0, mod0,
        norm_ffn_g[0].reshape(1, D_MODEL), rwh, rwl, NT_ALL, ctx_src=ctx)
    y0 = _moe_call(*_route_tables(*_route_call(lg0, rb, su)), hp0, ew1, ew3, ew2, 0)

    w1_in = odd_w_in[0]
    qk_cols = np.concatenate([h * HD_C + _HEAD_PERM for h in range(H_C + KV_C)])
    cols = np.concatenate([qk_cols, np.arange((H_C + KV_C) * HD_C, w1_in.shape[1])])
    w1_in = w1_in[:, cols].astype(BF16)
    cos_all, sin_all = _rope_tables()
    x2, q, k, v = _inproj1_call(
        x1, y0, mod0, mod1, norm_mix_g[1].reshape(1, D_MODEL), w1_in,
        gqa_q_norm_g[0][_HEAD_PERM].reshape(1, HD_C), gqa_k_norm_g[0][_HEAD_PERM].reshape(1, HD_C),
        cos_all, sin_all)
    o = _gqa_call(q, k, v)
    x3, hp1, lg1 = _outproj_call(
        [(o, 0)], [odd_w_out[0].astype(BF16)], x2, NT_CTX, mod1,
        norm_ffn_g[1].reshape(1, D_MODEL), rwh, rwl, SEQ // TILE)
    y1 = _moe_call(*_route_tables(*_route_call(lg1, rb, su)), hp1, ew1, ew3, ew2, 1)
    return _final_call(x3, y1, mod1)
```

```python
import functools

import numpy as np
import jax
import jax.numpy as jnp
from jax import lax
from jax.experimental import pallas as pl
from jax.experimental.pallas import tpu as pltpu

F32 = jnp.float32
BF16 = jnp.bfloat16
I32 = jnp.int32
U32 = jnp.uint32

D_MODEL = 1024
SEQ = 2048
GRID_W = 64
GRID_ROWS = SEQ // GRID_W
CTX_LEN = 256
T_ALL = CTX_LEN + SEQ
WIN_H = 8
WIN_W = 16
HD_A = 64
H_A = 8
D_A = H_A * HD_A
H_B = 4
DV_B = 128
DK_B = 64
MLSTM_CHUNK = 64
N_CHUNKS = T_ALL // MLSTM_CHUNK
N_CTX_CHUNKS = CTX_LEN // MLSTM_CHUNK
HD_C = 128
H_C = 8
KV_C = 2
ROPE_THETA = 10000.0
N_EXPERTS = 16
N_GROUPS = 4
EXP_PER_GROUP = 4
D_FF = 512
EPS = 1e-6
NEG_BIG = -1e30

TILE = 256
NT_ALL = T_ALL // TILE
NT_CTX = CTX_LEN // TILE
NA_QROWS = 4
NA_KROWS = WIN_H + NA_QROWS - 1
NA_TQ = NA_QROWS * GRID_W
NA_TK = NA_KROWS * GRID_W


def _na_geometry(r):
    u0 = min(max(r - WIN_H // 2, 0), GRID_ROWS - NA_KROWS)
    r0s = tuple(min(max(r + qi - WIN_H // 2, 0), GRID_ROWS - WIN_H) - r for qi in range(NA_QROWS))
    return (u0 - r, r0s)


def _na_cases():
    reps, step_case = [], []
    for r in range(0, GRID_ROWS, NA_QROWS):
        geo = _na_geometry(r)
        known = [_na_geometry(q) for q in reps]
        if geo not in known:
            reps.append(r)
            known.append(geo)
        step_case.append(known.index(geo))
    return tuple(reps), tuple(step_case)


NA_CASES, NA_STEP_CASE = _na_cases()
OUTPROJ_ROWS = 128
GQA_TQ = 512
MOE_RB = 256
MOE_RB_TAIL = 128
SUBLANES = 8
VMEM_LIMIT = 56 * 1024 * 1024


def _cparams(sem):
    return pltpu.CompilerParams(dimension_semantics=sem, vmem_limit_bytes=VMEM_LIMIT)


def _sigmoid(x):
    return 1.0 / (1.0 + jnp.exp(-x))


def _rms(x):
    return x * lax.rsqrt(jnp.mean(x * x, axis=-1, keepdims=True) + EPS)


def _dot(a, b):
    return jnp.dot(a, b, preferred_element_type=F32)


def _dot_nt(a, b):
    return lax.dot_general(a, b, (((1,), (1,)), ((), ())), preferred_element_type=F32)


def _dot_tn(a, b):
    return lax.dot_general(a, b, (((0,), (0,)), ((), ())), preferred_element_type=F32)


def _split_bf16(x, n):
    parts = []
    r = x
    for _ in range(n):
        p = r.astype(BF16)
        parts.append(p)
        r = r - p.astype(F32)
    return parts


LANE_TILES = D_MODEL // 128


def _load_token_tiles(ref, lead, tok0, n_tok):
    parts = [ref[(*lead, pl.ds(tok0 * SUBLANES + c, n_tok, stride=SUBLANES), slice(None))]
             for c in range(LANE_TILES)]
    return jnp.concatenate(parts, axis=1)


def _store_token_tiles(ref, lead, tok0, val):
    for c in range(LANE_TILES):
        ref[(*lead, pl.ds(tok0 * SUBLANES + c, val.shape[0], stride=SUBLANES), slice(None))] = (
            val[:, c * 128:(c + 1) * 128])


def _mod_index(b, j):
    return (jnp.where(j < NT_CTX, N_MOD_CTX_ROW, b), 0, 0)


def _combined_stream_specs():
    return [pl.BlockSpec((1, TILE, D_MODEL), lambda b, j: (b, jnp.maximum(j - NT_CTX, 0), 0)),
            pl.BlockSpec((1, TILE, D_MODEL), lambda b, j: (b, jnp.minimum(j, NT_CTX - 1), 0))]


N_MOD_ROWS = 16
N_MOD_CTX_ROW = 8


ADA_TN = 1536


def _ada_kernel(c_ref, w_ref, b_ref, o_ref):
    c = c_ref[...]
    s = (c * _sigmoid(c)).astype(BF16)
    o_ref[0] = _dot(s, w_ref[0].astype(BF16)) + b_ref[0]


def _ada_call(cvec, ada_w, ada_b):
    depth, d, n = ada_w.shape
    return pl.pallas_call(
        _ada_kernel,
        grid=(depth, n // ADA_TN),
        in_specs=[
            pl.BlockSpec((N_MOD_ROWS, d), lambda l, j: (0, 0)),
            pl.BlockSpec((1, d, ADA_TN), lambda l, j: (l, 0, j)),
            pl.BlockSpec((1, 1, ADA_TN), lambda l, j: (l, 0, j)),
        ],
        out_specs=pl.BlockSpec((1, N_MOD_ROWS, ADA_TN), lambda l, j: (l, 0, j)),
        out_shape=jax.ShapeDtypeStruct((depth, N_MOD_ROWS, n), F32),
        compiler_params=_cparams(("arbitrary", "arbitrary")),
        name="ada_mod",
    )(cvec, ada_w, ada_b.reshape(depth, 1, n))


def _log_sigmoid(x):
    return jnp.minimum(x, 0.0) - jnp.log1p(jnp.exp(-jnp.abs(x)))


def _inproj0_kernel(x_ref, ctx_ref, mod_ref, g_ref, wm_ref, wkt_ref, wg_ref, wgt_ref, gb_ref, gbt_ref,
                    qg_ref, kg_ref, bd_ref, tril_ref, triu_ref,
                    aq_ref, ak_ref, av_ref, mq_ref, mv_ref, og_ref, mkt_ref, gc_ref, gr_ref):
    mod = mod_ref[0]
    x_in = jnp.where(pl.program_id(1) < NT_CTX, ctx_ref[0], x_ref[0])
    h = _rms(x_in) * g_ref[...] * (1.0 + mod[1:2]) + mod[0:1]
    hb = h.astype(BF16)

    def proj(lo, hi):
        return _dot(hb, wm_ref[:, lo:hi])

    def head_norm(a, gain):
        a2 = a * a
        sums = []
        for s0 in range(0, D_A, 128):
            hi_, lo_ = _split_bf16(a2[:, s0:s0 + 128], 2)
            sums.append(_dot(hi_, bd_ref[...]) + _dot(lo_, bd_ref[...]))
        ss = jnp.concatenate(sums, axis=1)
        return a * lax.rsqrt(ss * (1.0 / HD_A) + EPS) * gain

    n_g = 4 * H_B
    p_aq, p_ak, p_av = proj(0, 512), proj(512, 1024), proj(1024, 1536)
    p_mq, p_mv, p_og = proj(1536, 1792), proj(2048, 2560), proj(2560, 3072)
    kt = _dot_nt(wkt_ref[...], hb).astype(BF16)
    gcol = _dot(hb, wg_ref[...]) + gb_ref[...]
    grow = _dot_nt(wgt_ref[...], hb) + gbt_ref[...]

    aq_ref[0] = (head_norm(p_aq, qg_ref[...]) * HD_A ** -0.5).astype(BF16)
    ak_ref[0] = head_norm(p_ak, kg_ref[...]).astype(BF16)
    av_ref[0] = p_av.astype(BF16)
    mq_ref[0] = (p_mq * DK_B ** -0.5).astype(BF16)
    for c in range(TILE // MLSTM_CHUNK):
        mkt_ref[0, c] = kt[:, c * MLSTM_CHUNK:(c + 1) * MLSTM_CHUNK]
    mv_ref[0] = p_mv.astype(BF16)
    og_ref[0] = _sigmoid(p_og).astype(BF16)

    tril = tril_ref[...]
    triu = triu_ref[...]
    ls_c = _split_bf16(_log_sigmoid(gcol), 3)
    ls_r = _split_bf16(_log_sigmoid(grow), 3)
    pre_c = _dot(tril, ls_c[0]) + _dot(tril, ls_c[1]) + _dot(tril, ls_c[2])
    suf_c = _dot(triu, ls_c[0]) + _dot(triu, ls_c[1]) + _dot(triu, ls_c[2])
    pre_r = _dot(ls_r[0], triu) + _dot(ls_r[1], triu) + _dot(ls_r[2], triu)
    suf_r = _dot(ls_r[0], tril) + _dot(ls_r[1], tril) + _dot(ls_r[2], tril)
    cidx = lax.broadcasted_iota(I32, gcol.shape, 1)
    ridx = lax.broadcasted_iota(I32, grow.shape, 0)

    def pick(idx, raw, pre, suf):
        is_ff = (idx >= H_B) & (idx < 2 * H_B)
        is_fb = idx >= 3 * H_B
        return jnp.where(is_ff, pre, jnp.where(is_fb, suf, raw))

    gc_ref[0] = pick(cidx, gcol, pre_c, suf_c)[:, :n_g]
    grow = pick(ridx, grow, pre_r, suf_r)
    for c in range(TILE // MLSTM_CHUNK):
        gr_ref[0, c] = grow[:, c * MLSTM_CHUNK:(c + 1) * MLSTM_CHUNK]


def _inproj0_call(x, ctx, modl, norm_g, wm, wkt, wg, wgt, gb, gbt, qg, kg, bd, tril, triu):
    B = x.shape[0]
    n_g = 4 * H_B
    tok = lambda b, j: (b, j, 0)
    const2 = lambda b, j: (0, 0)
    chunked = lambda b, j: (b, j, 0, 0)
    tile_chunks = TILE // MLSTM_CHUNK
    outs = [
        (D_A, BF16), (D_A, BF16), (D_A, BF16),
        (H_B * DK_B, BF16), (H_B * DV_B, BF16),
        (H_B * DV_B, BF16),
    ]
    out_shape = [jax.ShapeDtypeStruct((B, T_ALL, w), dt) for w, dt in outs]
    out_specs = [pl.BlockSpec((1, TILE, w), tok) for w, _ in outs]
    out_shape += [jax.ShapeDtypeStruct((B, N_CHUNKS, H_B * DK_B, MLSTM_CHUNK), BF16),
                  jax.ShapeDtypeStruct((B, T_ALL, n_g), F32),
                  jax.ShapeDtypeStruct((B, N_CHUNKS, n_g, MLSTM_CHUNK), F32)]
    out_specs += [pl.BlockSpec((1, tile_chunks, H_B * DK_B, MLSTM_CHUNK), chunked),
                  pl.BlockSpec((1, TILE, n_g), tok),
                  pl.BlockSpec((1, tile_chunks, n_g, MLSTM_CHUNK), chunked)]
    return pl.pallas_call(
        _inproj0_kernel,
        grid=(B, NT_ALL),
        in_specs=_combined_stream_specs() + [
            pl.BlockSpec((1, 6, D_MODEL), _mod_index),
            pl.BlockSpec((1, D_MODEL), const2),
            pl.BlockSpec(wm.shape, const2),
            pl.BlockSpec(wkt.shape, const2),
            pl.BlockSpec(wg.shape, const2),
            pl.BlockSpec(wgt.shape, const2),
            pl.BlockSpec(gb.shape, const2),
            pl.BlockSpec(gbt.shape, const2),
            pl.BlockSpec(qg.shape, const2),
            pl.BlockSpec(kg.shape, const2),
            pl.BlockSpec(bd.shape, const2),
            pl.BlockSpec(tril.shape, const2),
            pl.BlockSpec(triu.shape, const2),
        ],
        out_specs=out_specs,
        out_shape=out_shape,
        compiler_params=_cparams(("arbitrary", "arbitrary")),
        name="inproj0",
    )(x, ctx, modl, norm_g, wm, wkt, wg, wgt, gb, gbt, qg, kg, bd, tril, triu)


def _mlstm_kernel(mq_ref, mkt_ref, mv_ref, gc_ref, gr_ref, og_ref, ng_ref, out_ref,
                  hf_ref, hb_ref, s_ref, *local_refs):
    L = MLSTM_CHUNK
    n_chain = 2 * H_B
    slots = (local_refs[:4], local_refs[4:])
    s_ref[...] = jnp.zeros(s_ref.shape, F32)
    ri = lax.broadcasted_iota(I32, (L, L), 0)
    ci = lax.broadcasted_iota(I32, (L, L), 1)
    masks = (ci <= ri, ci >= ri)
    ones_aug = jnp.ones((L, DV_B), BF16)

    def chunk_rows(step, d):
        if d == 0:
            chunk = step
        else:
            chunk = jnp.where(step < N_CTX_CHUNKS, N_CTX_CHUNKS - 1 - step, N_CHUNKS + N_CTX_CHUNKS - 1 - step)
        return chunk, pl.ds(pl.multiple_of(chunk * L, L), L)

    def local_part(step, slot):
        intra_ref, u_ref, col_ref, rep_ref = slot
        chains = []
        for d in range(2):
            chunk, rows = chunk_rows(step, d)
            gcc = gc_ref[0, rows, :]
            grr = gr_ref[0, chunk]
            for h in range(H_B):
                gi = 2 * H_B * d + h
                b_col = gcc[:, gi + H_B:gi + H_B + 1]
                chains.append(dict(
                    d=d,
                    q=mq_ref[0, rows, h * DK_B:(h + 1) * DK_B],
                    kt=mkt_ref[0, chunk, h * DK_B:(h + 1) * DK_B, :],
                    v=mv_ref[0, rows, h * DV_B:(h + 1) * DV_B],
                    ig_col=gcc[:, gi:gi + 1], b_col=b_col,
                    ig_row=grr[gi:gi + 1, :], b_row=grr[gi + H_B:gi + H_B + 1, :],
                    b_last=b_col[L - 1:L, :] if d == 0 else b_col[0:1, :]))
        for ch in chains:
            ch['qk'] = _dot(ch['q'], ch['kt'])
            ch['v_aug'] = jnp.concatenate([ch['v'], ones_aug], axis=1)
            g = ch['b_last'] - ch['b_col'] + ch['ig_col']
            ch['g_max'] = jnp.max(g, axis=0, keepdims=True)
            ch['wv'] = (jnp.exp(g - ch['g_max']) * ch['v_aug'].astype(F32)).astype(BF16)
        us = [_dot(ch['kt'], ch['wv']) for ch in chains]
        for ch in chains:
            dm = jnp.where(masks[ch['d']], ch['b_col'] - ch['b_row'] + ch['ig_row'], NEG_BIG)
            ch['m_loc'] = jnp.max(dm, axis=-1, keepdims=True)
            ch['p'] = (ch['qk'] * jnp.exp(dm - ch['m_loc'])).astype(BF16)
        intras = [_dot(ch['p'], ch['v_aug']) for ch in chains]
        cols, reps = [], []
        for ch in chains:
            cols += [jnp.broadcast_to(ch['b_col'], (L, 128)), jnp.broadcast_to(ch['m_loc'], (L, 128))]
            reps += [jnp.broadcast_to(ch['b_last'], (SUBLANES, 128)),
                     jnp.broadcast_to(ch['g_max'], (SUBLANES, 128))]
        intra_ref[...] = jnp.concatenate(intras, axis=0)
        u_ref[...] = jnp.concatenate(us, axis=0)
        col_ref[...] = jnp.concatenate(cols, axis=0)
        rep_ref[...] = jnp.concatenate(reps, axis=0)

    def recurrence(step, ms, slot):
        intra_ref, u_ref, col_ref, rep_ref = slot
        n_all = range(n_chain)
        rows = [chunk_rows(step, d)[1] for d in range(2)]
        states = [s_ref[c * DK_B:(c + 1) * DK_B, :] for c in n_all]
        inters = [_dot(mq_ref[0, rows[c // H_B], (c % H_B) * DK_B:(c % H_B + 1) * DK_B],
                       states[c].astype(BF16)) for c in n_all]
        new_ms, new_states, houts = [], [], []
        for c in n_all:
            m = ms[c]
            b_last = rep_ref[2 * c * SUBLANES:(2 * c + 1) * SUBLANES, :]
            g_max = rep_ref[(2 * c + 1) * SUBLANES:(2 * c + 2) * SUBLANES, :]
            m_new = jnp.maximum(b_last + m, g_max)
            w_old = pltpu.repeat(pltpu.repeat(jnp.exp(b_last + m - m_new), DK_B // SUBLANES, 0), 2, 1)
            w_new = pltpu.repeat(pltpu.repeat(jnp.exp(g_max - m_new), DK_B // SUBLANES, 0), 2, 1)
            new_states.append(w_old * states[c] + w_new * u_ref[c * DK_B:(c + 1) * DK_B, :])
            new_ms.append(m_new)
        s_ref[...] = jnp.concatenate(new_states, axis=0)
        for c in n_all:
            a = col_ref[2 * c * L:(2 * c + 1) * L, :] + pltpu.repeat(ms[c], L // SUBLANES, 0)
            m_loc = col_ref[(2 * c + 1) * L:(2 * c + 2) * L, :]
            m_row = jnp.maximum(a, m_loc)
            w_inter = jnp.exp(a - m_row)
            w_loc = jnp.exp(m_loc - m_row)
            intra = intra_ref[c * L:(c + 1) * L, :]
            num = w_inter * inters[c][:, :DV_B] + w_loc * intra[:, :DV_B]
            den = w_inter * inters[c][:, DV_B:] + w_loc * intra[:, DV_B:]
            houts.append(num / jnp.maximum(jnp.abs(den), jnp.exp(-m_row)))
        hf_ref[rows[0], :] = jnp.concatenate(houts[:H_B], axis=1)
        hb_ref[rows[1], :] = jnp.concatenate(houts[H_B:], axis=1)
        return tuple(new_ms)

    local_part(0, slots[0])

    def step_pair(i, ms):
        s0 = 2 * i
        local_part(s0 + 1, slots[1])
        ms = recurrence(s0, ms, slots[0])
        local_part(jnp.minimum(s0 + 2, N_CHUNKS - 1), slots[0])
        return recurrence(s0 + 1, ms, slots[1])

    m0 = tuple(jnp.full((SUBLANES, 128), NEG_BIG, F32) for _ in range(n_chain))
    lax.fori_loop(0, N_CHUNKS // 2, step_pair, m0)

    def finish(i, carry):
        rows = pl.ds(pl.multiple_of(i * TILE, TILE), TILE)
        hs = hf_ref[rows, :] + hb_ref[rows, :]
        ng = ng_ref[...]
        og = og_ref[0, rows, :].astype(F32)
        for h in range(H_B):
            sl = slice(h * DV_B, (h + 1) * DV_B)
            out_ref[0, rows, sl] = (_rms(hs[:, sl]) * ng[:, sl] * og[:, sl]).astype(BF16)
        return carry

    lax.fori_loop(0, T_ALL // TILE, finish, 0)


def _mlstm_call(mq, mk, mv, gc, gr, og, ng):
    B = mq.shape[0]
    full = lambda a: pl.BlockSpec((1,) + a.shape[1:], lambda b: (b,) + (0,) * (a.ndim - 1))
    return pl.pallas_call(
        _mlstm_kernel,
        grid=(B,),
        in_specs=[full(mq), full(mk), full(mv), full(gc), full(gr), full(og),
                  pl.BlockSpec(ng.shape, lambda b: (0, 0))],
        out_specs=pl.BlockSpec((1, T_ALL, H_B * DV_B), lambda b: (b, 0, 0)),
        out_shape=jax.ShapeDtypeStruct((B, T_ALL, H_B * DV_B), BF16),
        scratch_shapes=[
            pltpu.VMEM((T_ALL, H_B * DV_B), F32),
            pltpu.VMEM((T_ALL, H_B * DV_B), F32),
            pltpu.VMEM((2 * H_B * DK_B, 2 * DV_B), F32),
        ] + 2 * [
            pltpu.VMEM((2 * H_B * MLSTM_CHUNK, 2 * DV_B), F32),
            pltpu.VMEM((2 * H_B * DK_B, 2 * DV_B), F32),
            pltpu.VMEM((2 * H_B * 2 * MLSTM_CHUNK, 128), F32),
            pltpu.VMEM((2 * H_B * 2 * SUBLANES, 128), F32),
        ],
        compiler_params=_cparams(("arbitrary",)),
        name="mlstm",
    )(mq, mk, mv, gc, gr, og, ng)


def _na_bias_tables(rpb):
    kh = WIN_H
    n_drow = 2 * WIN_H - 1
    qcol = np.arange(GRID_W)
    col_start = np.clip(qcol - WIN_W // 2, 0, GRID_W - WIN_W)
    col_ok = (qcol[None, :] >= col_start[:, None]) & (qcol[None, :] < col_start[:, None] + WIN_W)
    dcol = qcol[None, :] - qcol[:, None] + (WIN_W - 1)
    onehot = (dcol[None] == np.arange(2 * WIN_W - 1)[:, None, None]) & col_ok[None]
    blocks = jnp.einsum('hdx,xck->hdck', rpb.astype(F32), jnp.asarray(onehot, F32),
                        precision=lax.Precision.HIGHEST)
    blocks = jnp.where(col_ok[None, None], blocks, NEG_BIG)
    outside = jnp.full((H_A, 1, GRID_W, GRID_W), NEG_BIG, F32)
    blocks = jnp.concatenate([blocks, outside], axis=1)
    idx = np.full((len(NA_CASES), NA_QROWS, NA_KROWS), n_drow, np.int32)
    for ci, r in enumerate(NA_CASES):
        u0 = int(np.clip(r - kh // 2, 0, GRID_ROWS - NA_KROWS))
        for qi in range(NA_QROWS):
            rq = r + qi
            r0 = int(np.clip(rq - kh // 2, 0, GRID_ROWS - kh))
            for ui in range(NA_KROWS):
                kr = u0 + ui
                if r0 <= kr < r0 + kh:
                    idx[ci, qi, ui] = kr - rq + (WIN_H - 1)
    tab = blocks[:, idx]
    tab = jnp.transpose(tab, (1, 0, 2, 4, 3, 5))
    return tab.reshape(len(NA_CASES), H_A, NA_TQ, NA_TK)


def _na_case(j):
    case = jnp.int32(NA_STEP_CASE[0])
    for step, c in enumerate(NA_STEP_CASE):
        if c != NA_STEP_CASE[0]:
            case = jnp.where(j - 1 == step, c, case)
    return case


def _na_kernel(q_ref, k_ref, v_ref, bias_ref, o_ref):
    j = pl.program_id(1)
    lane = lax.broadcasted_iota(I32, (1, 2 * HD_A), 1)
    lo_half = lane < HD_A

    def attend(q_rows, n_q, key_sets, bias_for_head):
        for pp in range(H_A // 2):
            lanes = slice(pp * 2 * HD_A, (pp + 1) * 2 * HD_A)
            qp = q_ref[0, q_rows, lanes]
            ks = [k_ref[0, rs, lanes] for rs in key_sets]
            vs = [v_ref[0, rs, lanes] for rs in key_sets]
            sss = []
            for hh in range(2):
                qm = jnp.where(lo_half if hh == 0 else ~lo_half, qp, jnp.zeros_like(qp))
                sss.append([_dot_nt(qm, kk) for kk in ks])
            pss, ls = [], []
            for hh, ss in enumerate(sss):
                if bias_for_head is not None:
                    ss[0] = ss[0] + bias_for_head(2 * pp + hh)
                m = ss[0].max(axis=-1, keepdims=True)
                for s in ss[1:]:
                    m = jnp.maximum(m, s.max(axis=-1, keepdims=True))
                ps = [jnp.exp(s - m) for s in ss]
                l = ps[0].sum(axis=-1, keepdims=True)
                for p in ps[1:]:
                    l = l + p.sum(axis=-1, keepdims=True)
                pss.append([p.astype(BF16) for p in ps])
                ls.append(l)
            halves = []
            for ps, l in zip(pss, ls):
                acc = _dot(ps[0], vs[0])
                for p, vv in zip(ps[1:], vs[1:]):
                    acc = acc + _dot(p, vv)
                halves.append(acc / l)
            o_ref[0, q_rows, lanes] = jnp.where(lo_half, halves[0], halves[1]).astype(BF16)

    ctx_rows = pl.ds(0, CTX_LEN)

    @pl.when(j == 0)
    def _():
        attend(ctx_rows, CTX_LEN, [ctx_rows], None)

    @pl.when(j > 0)
    def _():
        r = (j - 1) * NA_QROWS
        u0 = jnp.clip(r - WIN_H // 2, 0, GRID_ROWS - NA_KROWS)
        q_rows = pl.ds(pl.multiple_of(CTX_LEN + r * GRID_W, NA_TQ), NA_TQ)
        k_rows = pl.ds(pl.multiple_of(CTX_LEN + u0 * GRID_W, GRID_W), NA_TK)
        attend(q_rows, NA_TQ, [k_rows, ctx_rows], lambda head: bias_ref[0, head])


def _na_call(aq, ak, av, bias):
    B = aq.shape[0]
    full = pl.BlockSpec((1, T_ALL, D_A), lambda b, j: (b, 0, 0))
    return pl.pallas_call(
        _na_kernel,
        grid=(B, 1 + GRID_ROWS // NA_QROWS),
        in_specs=[full, full, full,
                  pl.BlockSpec((1, H_A, NA_TQ, NA_TK), lambda b, j: (_na_case(j), 0, 0, 0))],
        out_specs=full,
        out_shape=jax.ShapeDtypeStruct((B, T_ALL, D_A), BF16),
        compiler_params=_cparams(("arbitrary", "arbitrary")),
        name="nbr_attn",
    )(aq, ak, av, bias)


def _route(logits_t, rb_col):
    sc = _sigmoid(logits_t)
    sel = sc + rb_col
    selr = [sel[e:e + 1, :] for e in range(N_EXPERTS)]
    scr = [sc[e:e + 1, :] for e in range(N_EXPERTS)]
    gscore = []
    for g in range(N_GROUPS):
        a, b, c, d = selr[EXP_PER_GROUP * g:EXP_PER_GROUP * (g + 1)]
        s1, t1 = jnp.maximum(a, b), jnp.minimum(a, b)
        s2, t2 = jnp.maximum(c, d), jnp.minimum(c, d)
        gscore.append(jnp.maximum(s1, s2) + jnp.maximum(jnp.minimum(s1, s2), jnp.maximum(t1, t2)))
    best = gscore[0]
    gi = jnp.zeros(best.shape, I32)
    for g in range(1, N_GROUPS):
        better = gscore[g] > best
        gi = jnp.where(better, g, gi)
        best = jnp.where(better, gscore[g], best)
    vs, ws = [], []
    for k in range(EXP_PER_GROUP):
        v = selr[k]
        w = scr[k]
        for g in range(1, N_GROUPS):
            v = jnp.where(gi == g, selr[EXP_PER_GROUP * g + k], v)
            w = jnp.where(gi == g, scr[EXP_PER_GROUP * g + k], w)
        vs.append(v)
        ws.append(w)
    b1, i1 = vs[0], jnp.zeros(best.shape, I32)
    for k in range(1, EXP_PER_GROUP):
        better = vs[k] > b1
        i1 = jnp.where(better, k, i1)
        b1 = jnp.where(better, vs[k], b1)
    b2 = jnp.full(best.shape, -jnp.inf, F32)
    i2 = jnp.zeros(best.shape, I32)
    for k in range(EXP_PER_GROUP):
        vk = jnp.where(i1 == k, -jnp.inf, vs[k])
        better = vk > b2
        i2 = jnp.where(better, k, i2)
        b2 = jnp.where(better, vk, b2)
    w1 = ws[0]
    w2 = ws[0]
    for k in range(1, EXP_PER_GROUP):
        w1 = jnp.where(i1 == k, ws[k], w1)
        w2 = jnp.where(i2 == k, ws[k], w2)
    tot = w1 + w2
    return gi * EXP_PER_GROUP + i1, gi * EXP_PER_GROUP + i2, w1 / tot, w2 / tot


def _outproj_kernel(n_act, has_ctx, *refs):
    acts = refs[:n_act]
    ws = refs[n_act:2 * n_act]
    refs = refs[2 * n_act:]
    if has_ctx:
        x_in = jnp.where(pl.program_id(1) < NT_CTX, refs[1][0], refs[0][0])
        refs = refs[1:]
    else:
        x_in = refs[0][0]
    _, mod_ref, g_ref, rwh_ref, rwl_ref, xo_ref, hp_ref, lg_ref = refs
    mod = mod_ref[0]
    for r0 in range(0, TILE, OUTPROJ_ROWS):
        rows = slice(r0, r0 + OUTPROJ_ROWS)
        halves = []
        for n0 in range(0, D_MODEL, D_MODEL // 2):
            cols = slice(n0, n0 + D_MODEL // 2)
            o = _dot(acts[0][0, rows, :], ws[0][:, cols])
            for a, w in zip(acts[1:], ws[1:]):
                o = o + _dot(a[0, rows, :], w[:, cols])
            halves.append(o)
        x = x_in[rows, :] + mod[2:3] * jnp.concatenate(halves, axis=1)
        xo_ref[0, rows, :] = x
        h = _rms(x) * g_ref[...] * (1.0 + mod[4:5]) + mod[3:4]
        _store_token_tiles(hp_ref, (0,), r0, h)
        h_hi, h_lo = _split_bf16(h, 2)
        lg_ref[0, :, rows] = (_dot_nt(rwh_ref[...], h_hi) + _dot_nt(rwh_ref[...], h_lo)
                              + _dot_nt(rwl_ref[...], h_hi))


def _outproj_call(acts, ws, x_src, x_off, modl, norm_g, rwh, rwl, n_tiles, ctx_src=None):
    B = x_src.shape[0]
    n_act = len(acts)
    T = n_tiles * TILE
    const2 = lambda b, j: (0, 0)
    in_specs = []
    for a, a_off in acts:
        in_specs.append(pl.BlockSpec((1, TILE, a.shape[2]), functools.partial(lambda b, j, o: (b, j + o, 0), o=a_off)))
    for w in ws:
        in_specs.append(pl.BlockSpec(w.shape, const2))
    streams = [x_src]
    if ctx_src is None:
        in_specs.append(pl.BlockSpec((1, TILE, D_MODEL), lambda b, j: (b, j + x_off, 0)))
    else:
        assert x_off == 0
        in_specs += _combined_stream_specs()
        streams.append(ctx_src)
    in_specs += [
        pl.BlockSpec((1, 6, D_MODEL), lambda b, j: _mod_index(b, j + x_off)),
        pl.BlockSpec((1, D_MODEL), const2),
        pl.BlockSpec(rwh.shape, const2),
        pl.BlockSpec(rwl.shape, const2),
    ]
    tok = lambda b, j: (b, j, 0)
    out_shape = [
        jax.ShapeDtypeStruct((B, T, D_MODEL), F32),
        jax.ShapeDtypeStruct((B, T * SUBLANES, 128), F32),
        jax.ShapeDtypeStruct((B, N_EXPERTS, T), F32),
    ]
    out_specs = [
        pl.BlockSpec((1, TILE, D_MODEL), tok),
        pl.BlockSpec((1, TILE * SUBLANES, 128), tok),
        pl.BlockSpec((1, N_EXPERTS, TILE), lambda b, j: (b, 0, j)),
    ]
    return pl.pallas_call(
        functools.partial(_outproj_kernel, n_act, ctx_src is not None),
        grid=(B, n_tiles),
        in_specs=in_specs,
        out_specs=out_specs,
        out_shape=out_shape,
        compiler_params=_cparams(("arbitrary", "arbitrary")),
        name="outproj",
    )(*[a for a, _ in acts], *ws, *streams, modl, norm_g, rwh, rwl)


def _route_kernel(lg_ref, rb_ref, su_ref, ri_ref, rw_ref, cnt_ref):
    logits_t = lg_ref[0]
    T = logits_t.shape[1]
    e1, e2, w1, w2 = _route(logits_t, rb_ref[...])
    eidx = lax.broadcasted_iota(I32, logits_t.shape, 0)
    oh1 = eidx == e1
    oh2 = eidx == e2
    onehot = jnp.where(oh1, 1.0, jnp.where(oh2, 1.0, 0.0))
    count = jnp.zeros((N_EXPERTS, 1), F32)
    ranks = []
    for jj in range(T // TILE):
        oh = onehot[:, jj * TILE:(jj + 1) * TILE]
        ranks.append(_dot(oh.astype(BF16), su_ref[...]) + count)
        count = count + jnp.sum(oh, axis=1, keepdims=True)
    cpad = jnp.floor((count + (SUBLANES - 1.0)) * (1.0 / SUBLANES)) * SUBLANES
    ecol = lax.broadcasted_iota(I32, (N_EXPERTS, 1), 0)
    start = jnp.zeros((N_EXPERTS, 1), F32)
    for e in range(N_EXPERTS - 1):
        start = start + jnp.where(ecol > e, cpad[e:e + 1, :], 0.0)
    row = jnp.concatenate(ranks, axis=1) + start
    r1 = jnp.sum(jnp.where(oh1, row, 0.0), axis=0, keepdims=True)
    r2 = jnp.sum(jnp.where(oh2, row, 0.0), axis=0, keepdims=True)
    zi = jnp.zeros((SUBLANES - 4, T), I32)
    ri_ref[0] = jnp.concatenate([e1, e2, r1.astype(I32), r2.astype(I32), zi], axis=0)
    zf = jnp.zeros((SUBLANES - 2, T), F32)
    rw_ref[0] = jnp.concatenate([w1, w2, zf], axis=0)
    cnt_ref[0] = jnp.broadcast_to(count, (N_EXPERTS, 128)).astype(I32)


def _route_call(lg, rb, su):
    B, _, T = lg.shape
    per_sample = lambda b: (b, 0, 0)
    return pl.pallas_call(
        _route_kernel,
        grid=(B,),
        in_specs=[pl.BlockSpec((1, N_EXPERTS, T), per_sample),
                  pl.BlockSpec(rb.shape, lambda b: (0, 0)),
                  pl.BlockSpec(su.shape, lambda b: (0, 0))],
        out_specs=[pl.BlockSpec((1, SUBLANES, T), per_sample),
                   pl.BlockSpec((1, SUBLANES, T), per_sample),
                   pl.BlockSpec((1, N_EXPERTS, 128), per_sample)],
        out_shape=[jax.ShapeDtypeStruct((B, SUBLANES, T), I32),
                   jax.ShapeDtypeStruct((B, SUBLANES, T), F32),
                   jax.ShapeDtypeStruct((B, N_EXPERTS, 128), I32)],
        compiler_params=_cparams(("arbitrary",)),
        name="route",
    )(lg, rb, su)


def _moe_kernel(T, n_rows, cnt_ref, ri_ref, rw_ref, h_ref, w1_ref, w3_ref, w2_ref, y_ref,
                xb_ref, ob_ref, tokl_ref, off_ref):
    b = pl.program_id(0)
    e = pl.program_id(1)

    @pl.when(e == 0)
    def _():
        off_ref[0] = 0
        for i in range(N_EXPERTS):
            c = cnt_ref[b, i]
            off_ref[i + 1] = off_ref[i] + ((c + SUBLANES - 1) // SUBLANES) * SUBLANES

        for i in range(N_EXPERTS):
            for k in range(SUBLANES):
                tokl_ref[jnp.maximum(off_ref[i + 1] - SUBLANES + k, 0)] = 0

        def clear(i, carry):
            tokl_ref[off_ref[N_EXPERTS] + i] = 0
            return carry

        lax.fori_loop(0, MOE_RB, clear, 0, unroll=8)

        def place(t, carry):
            tokl_ref[ri_ref[2 * T + t]] = t
            tokl_ref[ri_ref[3 * T + t]] = t
            return carry

        lax.fori_loop(0, T, place, 0, unroll=8)

    c = cnt_ref[b, e]
    base = off_ref[e]

    def tile_rows(row):
        return pl.ds(pl.multiple_of(row * SUBLANES, SUBLANES), SUBLANES)

    def block(p0, n_blk_rows):
        def gather(i, carry2):
            xb_ref[tile_rows(i), :] = h_ref[0, tile_rows(tokl_ref[p0 + i]), :]
            return carry2

        lax.fori_loop(0, n_blk_rows, gather, 0, unroll=8)
        xb = _load_token_tiles(xb_ref, (), 0, n_blk_rows).astype(BF16)
        h1 = _dot(xb, w1_ref[0, 0])
        h3 = _dot(xb, w3_ref[0, 0])
        act = (h1 * _sigmoid(h1)) * h3
        _store_token_tiles(ob_ref, (), p0, _dot(act.astype(BF16), w2_ref[0, 0]))

    n_big = (c + MOE_RB - MOE_RB_TAIL - 1) // MOE_RB
    n_big = jnp.maximum(n_big, 0)

    def big_block(rb, carry):
        block(pl.multiple_of(base + rb * MOE_RB, SUBLANES), MOE_RB)
        return carry

    lax.fori_loop(0, n_big, big_block, 0)

    @pl.when(c > n_big * MOE_RB)
    def _():
        block(pl.multiple_of(base + n_big * MOE_RB, SUBLANES), MOE_RB_TAIL)

    @pl.when(e == N_EXPERTS - 1)
    def _():
        def combine(t, carry):
            y_ref[0, tile_rows(t), :] = (rw_ref[t] * ob_ref[tile_rows(ri_ref[2 * T + t]), :]
                                         + rw_ref[T + t] * ob_ref[tile_rows(ri_ref[3 * T + t]), :])
            return carry

        lax.fori_loop(0, T, combine, 0, unroll=8)


def _moe_call(cnt, ri, rw, h, w1, w3, w2, layer):
    B = h.shape[0]
    T = h.shape[1] // SUBLANES
    n_rows = 2 * T + N_EXPERTS * SUBLANES + MOE_RB
    smem = functools.partial(pl.BlockSpec, memory_space=pltpu.SMEM)
    once = pl.Buffered(1)
    return pl.pallas_call(
        functools.partial(_moe_kernel, T, n_rows),
        grid=(B, N_EXPERTS),
        in_specs=[
            smem(cnt.shape, lambda b, e: (0, 0)),
            smem((4 * T,), lambda b, e: (b,)),
            smem((4 * T,), lambda b, e: (b,)),
            pl.BlockSpec((1, T * SUBLANES, 128), lambda b, e: (b, 0, 0), pipeline_mode=once),
            pl.BlockSpec((1, 1, D_MODEL, D_FF), lambda b, e: (layer, e, 0, 0)),
            pl.BlockSpec((1, 1, D_MODEL, D_FF), lambda b, e: (layer, e, 0, 0)),
            pl.BlockSpec((1, 1, D_FF, D_MODEL), lambda b, e: (layer, e, 0, 0)),
        ],
        out_specs=pl.BlockSpec((1, T * SUBLANES, 128), lambda b, e: (b, 0, 0), pipeline_mode=once),
        out_shape=jax.ShapeDtypeStruct((B, T * SUBLANES, 128), F32),
        scratch_shapes=[
            pltpu.VMEM((MOE_RB * SUBLANES, 128), F32),
            pltpu.VMEM((n_rows * SUBLANES, 128), F32),
            pltpu.SMEM((n_rows,), I32),
            pltpu.SMEM((N_EXPERTS + 1,), I32),
        ],
        compiler_params=_cparams(("arbitrary", "arbitrary")),
        name="moe",
    )(cnt, ri, rw, h, w1, w3, w2)


def _route_tables(ri, rw, cnt):
    return cnt[:, :, 0], ri[:, :4].reshape(-1), rw[:, :4].reshape(-1)


def _inproj1_kernel(x_ref, y_ref, mod0_ref, mod_ref, g_ref, w_ref, qg_ref, kg_ref, cos_ref, sin_ref,
                    xo_ref, q_ref, k_ref, v_ref):
    x = x_ref[0] + mod0_ref[0][5:6] * _load_token_tiles(y_ref, (0,), 0, TILE)
    xo_ref[0] = x
    mod = mod_ref[0]
    hb = (_rms(x) * g_ref[...] * (1.0 + mod[1:2]) + mod[0:1]).astype(BF16)
    cos = cos_ref[...]
    sin = sin_ref[...]

    def rope_head(a, gain):
        n = _rms(a) * gain
        return n * cos + pltpu.roll(n, HD_C // 2, 1) * sin

    qg = qg_ref[...]
    kg = kg_ref[...]
    ko = H_C * HD_C
    vo = (H_C + KV_C) * HD_C
    kv = _dot(hb, w_ref[:, ko:vo + KV_C * HD_C])
    for h in range(KV_C):
        sl = slice(h * HD_C, (h + 1) * HD_C)
        k_ref[0, :, sl] = rope_head(kv[:, sl], kg).astype(BF16)
    v_ref[0] = kv[:, KV_C * HD_C:].astype(BF16)

    @pl.when(pl.program_id(1) >= NT_CTX)
    def _():
        qs = [_dot(hb, w_ref[:, h * HD_C:(h + 1) * HD_C]) for h in range(H_C)]
        for h in range(H_C):
            q_ref[0, :, h * HD_C:(h + 1) * HD_C] = (rope_head(qs[h], qg) * HD_C ** -0.5).astype(BF16)


def _inproj1_call(x1, y0, mod0, mod1, norm_g, w, qg, kg, cos, sin):
    B = x1.shape[0]
    tok = lambda b, j: (b, j, 0)
    const2 = lambda b, j: (0, 0)
    lat_tok = lambda b, j: (b, jnp.maximum(j - NT_CTX, 0), 0)
    widths = [(T_ALL, D_MODEL, F32), (SEQ, H_C * HD_C, BF16),
              (T_ALL, KV_C * HD_C, BF16), (T_ALL, KV_C * HD_C, BF16)]
    return pl.pallas_call(
        _inproj1_kernel,
        grid=(B, NT_ALL),
        in_specs=[
            pl.BlockSpec((1, TILE, D_MODEL), tok),
            pl.BlockSpec((1, TILE * SUBLANES, 128), tok),
            pl.BlockSpec((1, 6, D_MODEL), _mod_index),
            pl.BlockSpec((1, 6, D_MODEL), _mod_index),
            pl.BlockSpec((1, D_MODEL), const2),
            pl.BlockSpec(w.shape, const2),
            pl.BlockSpec(qg.shape, const2),
            pl.BlockSpec(kg.shape, const2),
            pl.BlockSpec((TILE, HD_C), lambda b, j: (j, 0)),
            pl.BlockSpec((TILE, HD_C), lambda b, j: (j, 0)),
        ],
        out_specs=[pl.BlockSpec((1, TILE, w_), lat_tok if t == SEQ else tok) for t, w_, _ in widths],
        out_shape=[jax.ShapeDtypeStruct((B, t, w_), dt) for t, w_, dt in widths],
        compiler_params=_cparams(("arbitrary", "arbitrary")),
        name="inproj1",
    )(x1, y0, mod0, mod1, norm_g, w, qg, kg, cos, sin)


def _gqa_kernel(q_ref, k_ref, v_ref, o_ref):
    k = k_ref[0]
    v = v_ref[0]
    for h0 in range(0, H_C // KV_C, 2):
        sls = [slice(h * HD_C, (h + 1) * HD_C) for h in (h0, h0 + 1)]
        ss = [_dot_nt(q_ref[0, :, sl], k) for sl in sls]
        ps, ls = [], []
        for s in ss:
            p = jnp.exp(s - s.max(axis=-1, keepdims=True))
            ls.append(p.sum(axis=-1, keepdims=True))
            ps.append(p.astype(BF16))
        for sl, p, l in zip(sls, ps, ls):
            o_ref[0, :, sl] = (_dot(p, v) / l).astype(BF16)


def _gqa_call(q, k, v):
    B = q.shape[0]
    gw = (H_C // KV_C) * HD_C
    return pl.pallas_call(
        _gqa_kernel,
        grid=(B, KV_C, SEQ // GQA_TQ),
        in_specs=[
            pl.BlockSpec((1, GQA_TQ, gw), lambda b, g, j: (b, j, g)),
            pl.BlockSpec((1, T_ALL, HD_C), lambda b, g, j: (b, 0, g)),
            pl.BlockSpec((1, T_ALL, HD_C), lambda b, g, j: (b, 0, g)),
        ],
        out_specs=pl.BlockSpec((1, GQA_TQ, gw), lambda b, g, j: (b, j, g)),
        out_shape=jax.ShapeDtypeStruct((B, SEQ, H_C * HD_C), BF16),
        compiler_params=_cparams(("arbitrary", "arbitrary", "arbitrary")),
        name="gqa",
    )(q, k, v)


def _final_kernel(x_ref, y_ref, mod_ref, o_ref):
    o_ref[0] = x_ref[0] + mod_ref[0][5:6] * _load_token_tiles(y_ref, (0,), 0, TILE)


def _final_call(x, y, modl):
    B, T, D = x.shape
    tok = lambda b, j: (b, j, 0)
    return pl.pallas_call(
        _final_kernel,
        grid=(B, T // TILE),
        in_specs=[pl.BlockSpec((1, TILE, D), tok), pl.BlockSpec((1, TILE * SUBLANES, 128), tok),
                  pl.BlockSpec((1, 6, D), lambda b, j: (b, 0, 0))],
        out_specs=pl.BlockSpec((1, TILE, D), tok),
        out_shape=jax.ShapeDtypeStruct((B, T, D), F32),
        compiler_params=_cparams(("arbitrary", "arbitrary")),
        name="final_residual",
    )(x, y, modl)


def _chunk_tri(lower):
    i = np.arange(TILE)
    same = (i[:, None] // MLSTM_CHUNK) == (i[None, :] // MLSTM_CHUNK)
    tri = (i[None, :] <= i[:, None]) if lower else (i[None, :] >= i[:, None])
    return jnp.asarray((same & tri).astype(np.float32), BF16)


def _rope_tables():
    n_freq = HD_C // 4
    inv_freq = ROPE_THETA ** (-jnp.arange(n_freq, dtype=F32) / n_freq)
    t = jnp.arange(SEQ)
    rows = (t // GRID_W).astype(F32)
    cols = (t % GRID_W).astype(F32)
    ang = jnp.concatenate([rows[:, None] * inv_freq, cols[:, None] * inv_freq], axis=-1)
    cos, sin = jnp.cos(ang), jnp.sin(ang)
    cos_l = jnp.concatenate([cos, cos], axis=-1)
    sin_l = jnp.concatenate([-sin, sin], axis=-1)
    cos_all = jnp.concatenate([jnp.ones((CTX_LEN, HD_C), F32), cos_l], axis=0)
    sin_all = jnp.concatenate([jnp.zeros((CTX_LEN, HD_C), F32), sin_l], axis=0)
    return cos_all, sin_all


_HEAD_PERM = np.concatenate([np.arange(0, HD_C, 2), np.arange(1, HD_C, 2)])


def kernel(x, c, ctx, c_ctx, ada_w, ada_b, norm_mix_g, norm_ffn_g, even_w_in, even_w_out,
           na_q_norm_g, na_k_norm_g, na_rpb, mlstm_gate_b, mlstm_norm_g, odd_w_in, odd_w_out,
           gqa_q_norm_g, gqa_k_norm_g, router_w, router_b, exp_w1, exp_w3, exp_w2):
    B = x.shape[0]
    assert B <= N_MOD_CTX_ROW and x.shape[1:] == (SEQ, D_MODEL) and ctx.shape[1:] == (CTX_LEN, D_MODEL)
    n_g = 4 * H_B

    cvec = jnp.zeros((N_MOD_ROWS, D_MODEL), F32).at[:B].set(c).at[N_MOD_CTX_ROW].set(c_ctx)
    mod = _ada_call(cvec, ada_w, ada_b).reshape(2, N_MOD_ROWS, 6, D_MODEL)
    mod0, mod1 = mod[0], mod[1]

    w_in = even_w_in[0]
    n_main = w_in.shape[1] - n_g
    wm = w_in[:, :n_main].astype(BF16)
    wg_f = w_in[:, n_main:]
    wg = jnp.pad(wg_f, ((0, 0), (0, 128 - n_g))).astype(BF16)
    wgt = wg_f.T.astype(BF16)
    gb = jnp.pad(mlstm_gate_b[0].reshape(1, n_g), ((0, 0), (0, 128 - n_g)))
    gbt = mlstm_gate_b[0].reshape(n_g, 1)
    qg = jnp.tile(na_q_norm_g[0], H_A).reshape(1, D_A)
    kg = jnp.tile(na_k_norm_g[0], H_A).reshape(1, D_A)
    hid = np.arange(128) // HD_A
    bd = jnp.asarray((hid[:, None] == hid[None, :]).astype(np.float32), BF16)
    k_lo = 3 * D_A + H_B * DK_B
    wkt = w_in[:, k_lo:k_lo + H_B * DK_B].T.astype(BF16)
    aq, ak, av, mq, mv, og, mkt, gc, gr = _inproj0_call(
        x, ctx, mod0, norm_mix_g[0].reshape(1, D_MODEL), wm, wkt, wg, wgt, gb, gbt, qg, kg, bd,
        _chunk_tri(True), _chunk_tri(False))
    hm = _mlstm_call(mq, mkt, mv, gc, gr, og, mlstm_norm_g[0].reshape(1, H_B * DV_B))
    oa = _na_call(aq, ak, av, _na_bias_tables(na_rpb[0]))

    rw_t = router_w.T
    rwh = rw_t.astype(BF16)
    rwl = (rw_t - rwh.astype(F32)).astype(BF16)
    rb = router_b.reshape(N_EXPERTS, 1).astype(F32)
    i = np.arange(TILE)
    su = jnp.asarray((i[:, None] < i[None, :]).astype(np.float32), BF16)
    w_out = even_w_out[0].astype(BF16)
    ew1, ew3, ew2 = exp_w1.astype(BF16), exp_w3.astype(BF16), exp_w2.astype(BF16)
    x1, hp0, lg0 = _outproj_call(
        [(oa, 0), (hm, 0)], [w_out[:D_A], w_out[D_A:]], x, 0, mod0,
        norm_ffn_g[0].reshape(1, D_MODEL), rwh, rwl, NT_ALL, ctx_src=ctx)
    y0 = _moe_call(*_route_tables(*_route_call(lg0, rb, su)), hp0, ew1, ew3, ew2, 0)

    w1_in = odd_w_in[0]
    qk_cols = np.concatenate([h * HD_C + _HEAD_PERM for h in range(H_C + KV_C)])
    cols = np.concatenate([qk_cols, np.arange((H_C + KV_C) * HD_C, w1_in.shape[1])])
    w1_in = w1_in[:, cols].astype(BF16)
    cos_all, sin_all = _rope_tables()
    x2, q, k, v = _inproj1_call(
        x1, y0, mod0, mod1, norm_mix_g[1].reshape(1, D_MODEL), w1_in,
        gqa_q_norm_g[0][_HEAD_PERM].reshape(1, HD_C), gqa_k_norm_g[0][_HEAD_PERM].reshape(1, HD_C),
        cos_all, sin_all)
    o = _gqa_call(q, k, v)
    x3, hp1, lg1 = _outproj_call(
        [(o, 0)], [odd_w_out[0].astype(BF16)], x2, NT_CTX, mod1,
        norm_ffn_g[1].reshape(1, D_MODEL), rwh, rwl, SEQ // TILE)
    y1 = _moe_call(*_route_tables(*_route_call(lg1, rb, su)), hp1, ew1, ew3, ew2, 1)
    return _final_call(x3, y1, mod1)
```

```python
import functools

import numpy as np
import jax
import jax.numpy as jnp
from jax import lax
from jax.experimental import pallas as pl
from jax.experimental.pallas import tpu as pltpu

F32 = jnp.float32
BF16 = jnp.bfloat16
I32 = jnp.int32
U32 = jnp.uint32

D_MODEL = 1024
SEQ = 2048
GRID_W = 64
GRID_ROWS = SEQ // GRID_W
CTX_LEN = 256
T_ALL = CTX_LEN + SEQ
WIN_H = 8
WIN_W = 16
HD_A = 64
H_A = 8
D_A = H_A * HD_A
H_B = 4
DV_B = 128
DK_B = 64
MLSTM_CHUNK = 64
N_CHUNKS = T_ALL // MLSTM_CHUNK
N_CTX_CHUNKS = CTX_LEN // MLSTM_CHUNK
HD_C = 128
H_C = 8
KV_C = 2
ROPE_THETA = 10000.0
N_EXPERTS = 16
N_GROUPS = 4
EXP_PER_GROUP = 4
D_FF = 512
EPS = 1e-6
NEG_BIG = -1e30

TILE = 256
NT_ALL = T_ALL // TILE
NT_CTX = CTX_LEN // TILE
NA_QROWS = 4
NA_KROWS = WIN_H + NA_QROWS - 1
NA_TQ = NA_QROWS * GRID_W
NA_TK = NA_KROWS * GRID_W
NA_STAGE_PAIRS = 2


def _na_geometry(r):
    u0 = min(max(r - WIN_H // 2, 0), GRID_ROWS - NA_KROWS)
    r0s = tuple(min(max(r + qi - WIN_H // 2, 0), GRID_ROWS - WIN_H) - r for qi in range(NA_QROWS))
    return (u0 - r, r0s)


def _na_cases():
    reps, step_case = [], []
    for r in range(0, GRID_ROWS, NA_QROWS):
        geo = _na_geometry(r)
        known = [_na_geometry(q) for q in reps]
        if geo not in known:
            reps.append(r)
            known.append(geo)
        step_case.append(known.index(geo))
    return tuple(reps), tuple(step_case)


NA_CASES, NA_STEP_CASE = _na_cases()
OUTPROJ_ROWS = 128
GQA_TQ = 512
GQA_STAGE_HEADS = 2
FINAL_TILE = 1024
MOE_RB = 256
MOE_RB_TAIL = 128
SUBLANES = 8
VMEM_LIMIT = 56 * 1024 * 1024


def _cparams(sem):
    return pltpu.CompilerParams(dimension_semantics=sem, vmem_limit_bytes=VMEM_LIMIT)


def _sigmoid(x):
    return 1.0 / (1.0 + jnp.exp(-x))


def _rms(x):
    return x * lax.rsqrt(jnp.mean(x * x, axis=-1, keepdims=True) + EPS)


def _dot(a, b):
    return jnp.dot(a, b, preferred_element_type=F32)


def _dot_nt(a, b):
    return lax.dot_general(a, b, (((1,), (1,)), ((), ())), preferred_element_type=F32)


def _dot_tn(a, b):
    return lax.dot_general(a, b, (((0,), (0,)), ((), ())), preferred_element_type=F32)


def _split_bf16(x, n):
    parts = []
    r = x
    for _ in range(n):
        p = r.astype(BF16)
        parts.append(p)
        r = r - p.astype(F32)
    return parts


LANE_TILES = D_MODEL // 128


def _load_token_tiles(ref, lead, tok0, n_tok):
    parts = [ref[(*lead, pl.ds(tok0 * SUBLANES + c, n_tok, stride=SUBLANES), slice(None))]
             for c in range(LANE_TILES)]
    return jnp.concatenate(parts, axis=1)


def _store_token_tiles(ref, lead, tok0, val):
    for c in range(LANE_TILES):
        ref[(*lead, pl.ds(tok0 * SUBLANES + c, val.shape[0], stride=SUBLANES), slice(None))] = (
            val[:, c * 128:(c + 1) * 128])


def _mod_index(b, j):
    return (jnp.where(j < NT_CTX, N_MOD_CTX_ROW, b), 0, 0)


def _combined_stream_specs():
    return [pl.BlockSpec((1, TILE, D_MODEL), lambda b, j: (b, jnp.maximum(j - NT_CTX, 0), 0)),
            pl.BlockSpec((1, TILE, D_MODEL), lambda b, j: (b, jnp.minimum(j, NT_CTX - 1), 0))]


N_MOD_ROWS = 16
N_MOD_CTX_ROW = 8


ADA_TN = 1536


def _ada_kernel(c_ref, w_ref, b_ref, o_ref):
    c = c_ref[...]
    s = (c * _sigmoid(c)).astype(BF16)
    o_ref[0] = _dot(s, w_ref[0].astype(BF16)) + b_ref[0]


def _ada_call(cvec, ada_w, ada_b):
    depth, d, n = ada_w.shape
    return pl.pallas_call(
        _ada_kernel,
        grid=(depth, n // ADA_TN),
        in_specs=[
            pl.BlockSpec((N_MOD_ROWS, d), lambda l, j: (0, 0)),
            pl.BlockSpec((1, d, ADA_TN), lambda l, j: (l, 0, j)),
            pl.BlockSpec((1, 1, ADA_TN), lambda l, j: (l, 0, j)),
        ],
        out_specs=pl.BlockSpec((1, N_MOD_ROWS, ADA_TN), lambda l, j: (l, 0, j)),
        out_shape=jax.ShapeDtypeStruct((depth, N_MOD_ROWS, n), F32),
        compiler_params=_cparams(("arbitrary", "arbitrary")),
        name="ada_mod",
    )(cvec, ada_w, ada_b.reshape(depth, 1, n))


def _log_sigmoid(x):
    return jnp.minimum(x, 0.0) - jnp.log1p(jnp.exp(-jnp.abs(x)))


def _inproj0_kernel(x_ref, ctx_ref, mod_ref, g_ref, wm_ref, wkt_ref, wg_ref, wgt_ref, gb_ref, gbt_ref,
                    qg_ref, kg_ref, bd_ref, tril_ref, triu_ref,
                    aq_ref, ak_ref, av_ref, mq_ref, mv_ref, og_ref, mkt_ref, gc_ref, gr_ref):
    mod = mod_ref[0]
    x_in = jnp.where(pl.program_id(1) < NT_CTX, ctx_ref[0], x_ref[0])
    h = _rms(x_in) * g_ref[...] * (1.0 + mod[1:2]) + mod[0:1]
    hb = h.astype(BF16)

    def proj(lo, hi):
        return _dot(hb, wm_ref[:, lo:hi])

    def head_norm(a, gain):
        a2 = a * a
        sums = []
        for s0 in range(0, D_A, 128):
            hi_, lo_ = _split_bf16(a2[:, s0:s0 + 128], 2)
            sums.append(_dot(hi_, bd_ref[...]) + _dot(lo_, bd_ref[...]))
        ss = jnp.concatenate(sums, axis=1)
        return a * lax.rsqrt(ss * (1.0 / HD_A) + EPS) * gain

    n_g = 4 * H_B
    p_aq, p_ak, p_av = proj(0, 512), proj(512, 1024), proj(1024, 1536)
    p_mq, p_mv, p_og = proj(1536, 1792), proj(2048, 2560), proj(2560, 3072)
    kt = _dot_nt(wkt_ref[...], hb).astype(BF16)
    gcol = _dot(hb, wg_ref[...]) + gb_ref[...]
    grow = _dot_nt(wgt_ref[...], hb) + gbt_ref[...]

    aq_ref[0] = (head_norm(p_aq, qg_ref[...]) * HD_A ** -0.5).astype(BF16)
    ak_ref[0] = head_norm(p_ak, kg_ref[...]).astype(BF16)
    av_ref[0] = p_av.astype(BF16)
    mq_ref[0] = (p_mq * DK_B ** -0.5).astype(BF16)
    for c in range(TILE // MLSTM_CHUNK):
        mkt_ref[0, c] = kt[:, c * MLSTM_CHUNK:(c + 1) * MLSTM_CHUNK]
    mv_ref[0] = p_mv.astype(BF16)
    og_ref[0] = _sigmoid(p_og).astype(BF16)

    tril = tril_ref[...]
    triu = triu_ref[...]
    ls_c = _split_bf16(_log_sigmoid(gcol), 3)
    ls_r = _split_bf16(_log_sigmoid(grow), 3)
    pre_c = _dot(tril, ls_c[0]) + _dot(tril, ls_c[1]) + _dot(tril, ls_c[2])
    suf_c = _dot(triu, ls_c[0]) + _dot(triu, ls_c[1]) + _dot(triu, ls_c[2])
    pre_r = _dot(ls_r[0], triu) + _dot(ls_r[1], triu) + _dot(ls_r[2], triu)
    suf_r = _dot(ls_r[0], tril) + _dot(ls_r[1], tril) + _dot(ls_r[2], tril)
    cidx = lax.broadcasted_iota(I32, gcol.shape, 1)
    ridx = lax.broadcasted_iota(I32, grow.shape, 0)

    def pick(idx, raw, pre, suf):
        is_ff = (idx >= H_B) & (idx < 2 * H_B)
        is_fb = idx >= 3 * H_B
        return jnp.where(is_ff, pre, jnp.where(is_fb, suf, raw))

    gc_ref[0] = pick(cidx, gcol, pre_c, suf_c)[:, :n_g]
    grow = pick(ridx, grow, pre_r, suf_r)
    for c in range(TILE // MLSTM_CHUNK):
        gr_ref[0, c] = grow[:, c * MLSTM_CHUNK:(c + 1) * MLSTM_CHUNK]


def _inproj0_call(x, ctx, modl, norm_g, wm, wkt, wg, wgt, gb, gbt, qg, kg, bd, tril, triu):
    B = x.shape[0]
    n_g = 4 * H_B
    tok = lambda b, j: (b, j, 0)
    const2 = lambda b, j: (0, 0)
    chunked = lambda b, j: (b, j, 0, 0)
    tile_chunks = TILE // MLSTM_CHUNK
    outs = [
        (D_A, BF16), (D_A, BF16), (D_A, BF16),
        (H_B * DK_B, BF16), (H_B * DV_B, BF16),
        (H_B * DV_B, BF16),
    ]
    out_shape = [jax.ShapeDtypeStruct((B, T_ALL, w), dt) for w, dt in outs]
    out_specs = [pl.BlockSpec((1, TILE, w), tok) for w, _ in outs]
    out_shape += [jax.ShapeDtypeStruct((B, N_CHUNKS, H_B * DK_B, MLSTM_CHUNK), BF16),
                  jax.ShapeDtypeStruct((B, T_ALL, n_g), F32),
                  jax.ShapeDtypeStruct((B, N_CHUNKS, n_g, MLSTM_CHUNK), F32)]
    out_specs += [pl.BlockSpec((1, tile_chunks, H_B * DK_B, MLSTM_CHUNK), chunked),
                  pl.BlockSpec((1, TILE, n_g), tok),
                  pl.BlockSpec((1, tile_chunks, n_g, MLSTM_CHUNK), chunked)]
    return pl.pallas_call(
        _inproj0_kernel,
        grid=(B, NT_ALL),
        in_specs=_combined_stream_specs() + [
            pl.BlockSpec((1, 6, D_MODEL), _mod_index),
            pl.BlockSpec((1, D_MODEL), const2),
            pl.BlockSpec(wm.shape, const2),
            pl.BlockSpec(wkt.shape, const2),
            pl.BlockSpec(wg.shape, const2),
            pl.BlockSpec(wgt.shape, const2),
            pl.BlockSpec(gb.shape, const2),
            pl.BlockSpec(gbt.shape, const2),
            pl.BlockSpec(qg.shape, const2),
            pl.BlockSpec(kg.shape, const2),
            pl.BlockSpec(bd.shape, const2),
            pl.BlockSpec(tril.shape, const2),
            pl.BlockSpec(triu.shape, const2),
        ],
        out_specs=out_specs,
        out_shape=out_shape,
        compiler_params=_cparams(("arbitrary", "arbitrary")),
        name="inproj0",
    )(x, ctx, modl, norm_g, wm, wkt, wg, wgt, gb, gbt, qg, kg, bd, tril, triu)


def _mlstm_kernel(mq_ref, mkt_ref, mv_ref, gc_ref, gr_ref, og_ref, ng_ref, out_ref,
                  hf_ref, hb_ref, s_ref, *local_refs):
    L = MLSTM_CHUNK
    n_chain = 2 * H_B
    slots = (local_refs[:4], local_refs[4:])
    s_ref[...] = jnp.zeros(s_ref.shape, F32)
    ri = lax.broadcasted_iota(I32, (L, L), 0)
    ci = lax.broadcasted_iota(I32, (L, L), 1)
    masks = (ci <= ri, ci >= ri)
    ones_aug = jnp.ones((L, DV_B), BF16)

    def tile_up(x, n_rows, n_cols):
        x = jnp.concatenate([x] * n_rows, axis=0)
        return x if n_cols == 1 else jnp.concatenate([x] * n_cols, axis=1)

    def chunk_rows(step, d):
        if d == 0:
            chunk = step
        else:
            chunk = jnp.where(step < N_CTX_CHUNKS, N_CTX_CHUNKS - 1 - step, N_CHUNKS + N_CTX_CHUNKS - 1 - step)
        return chunk, pl.ds(pl.multiple_of(chunk * L, L), L)

    def local_part(step, slot):
        intra_ref, u_ref, col_ref, rep_ref = slot
        chains = []
        for d in range(2):
            chunk, rows = chunk_rows(step, d)
            gcc = gc_ref[0, rows, :]
            grr = gr_ref[0, chunk]
            for h in range(H_B):
                gi = 2 * H_B * d + h
                b_col = gcc[:, gi + H_B:gi + H_B + 1]
                chains.append(dict(
                    d=d,
                    q=mq_ref[0, rows, h * DK_B:(h + 1) * DK_B],
                    kt=mkt_ref[0, chunk, h * DK_B:(h + 1) * DK_B, :],
                    v=mv_ref[0, rows, h * DV_B:(h + 1) * DV_B],
                    ig_col=gcc[:, gi:gi + 1], b_col=b_col,
                    ig_row=grr[gi:gi + 1, :], b_row=grr[gi + H_B:gi + H_B + 1, :],
                    b_last=b_col[L - 1:L, :] if d == 0 else b_col[0:1, :]))
        for ch in chains:
            ch['qk'] = _dot(ch['q'], ch['kt'])
            ch['v_aug'] = jnp.concatenate([ch['v'], ones_aug], axis=1)
            g = ch['b_last'] - ch['b_col'] + ch['ig_col']
            ch['g_max'] = jnp.max(g, axis=0, keepdims=True)
            ch['wv'] = (jnp.exp(g - ch['g_max']) * ch['v_aug'].astype(F32)).astype(BF16)
        us = [_dot(ch['kt'], ch['wv']) for ch in chains]
        for ch in chains:
            dm = jnp.where(masks[ch['d']], ch['b_col'] - ch['b_row'] + ch['ig_row'], NEG_BIG)
            ch['m_loc'] = jnp.max(dm, axis=-1, keepdims=True)
            ch['p'] = (ch['qk'] * jnp.exp(dm - ch['m_loc'])).astype(BF16)
        intras = [_dot(ch['p'], ch['v_aug']) for ch in chains]
        cols, reps = [], []
        for ch in chains:
            cols += [jnp.broadcast_to(ch['b_col'], (L, 128)), jnp.broadcast_to(ch['m_loc'], (L, 128))]
            reps += [jnp.broadcast_to(ch['b_last'], (SUBLANES, 128)),
                     jnp.broadcast_to(ch['g_max'], (SUBLANES, 128))]
        intra_ref[...] = jnp.concatenate(intras, axis=0)
        u_ref[...] = jnp.concatenate(us, axis=0)
        col_ref[...] = jnp.concatenate(cols, axis=0)
        rep_ref[...] = jnp.concatenate(reps, axis=0)

    def recurrence(step, ms, slot):
        intra_ref, u_ref, col_ref, rep_ref = slot
        n_all = range(n_chain)
        rows = [chunk_rows(step, d)[1] for d in range(2)]
        states = [s_ref[c * DK_B:(c + 1) * DK_B, :] for c in n_all]
        inters = [_dot(mq_ref[0, rows[c // H_B], (c % H_B) * DK_B:(c % H_B + 1) * DK_B],
                       states[c].astype(BF16)) for c in n_all]
        new_ms, new_states, houts = [], [], []
        for c in n_all:
            m = ms[c]
            b_last = rep_ref[2 * c * SUBLANES:(2 * c + 1) * SUBLANES, :]
            g_max = rep_ref[(2 * c + 1) * SUBLANES:(2 * c + 2) * SUBLANES, :]
            m_new = jnp.maximum(b_last + m, g_max)
            w_old = tile_up(jnp.exp(b_last + m - m_new), DK_B // SUBLANES, 2)
            w_new = tile_up(jnp.exp(g_max - m_new), DK_B // SUBLANES, 2)
            new_states.append(w_old * states[c] + w_new * u_ref[c * DK_B:(c + 1) * DK_B, :])
            new_ms.append(m_new)
        s_ref[...] = jnp.concatenate(new_states, axis=0)
        for c in n_all:
            a = col_ref[2 * c * L:(2 * c + 1) * L, :] + tile_up(ms[c], L // SUBLANES, 1)
            m_loc = col_ref[(2 * c + 1) * L:(2 * c + 2) * L, :]
            m_row = jnp.maximum(a, m_loc)
            w_inter = jnp.exp(a - m_row)
            w_loc = jnp.exp(m_loc - m_row)
            intra = intra_ref[c * L:(c + 1) * L, :]
            num = w_inter * inters[c][:, :DV_B] + w_loc * intra[:, :DV_B]
            den = w_inter * inters[c][:, DV_B:] + w_loc * intra[:, DV_B:]
            houts.append(num / jnp.maximum(jnp.abs(den), jnp.exp(-m_row)))
        hf_ref[rows[0], :] = jnp.concatenate(houts[:H_B], axis=1)
        hb_ref[rows[1], :] = jnp.concatenate(houts[H_B:], axis=1)
        return tuple(new_ms)

    local_part(0, slots[0])

    def step_pair(i, ms):
        s0 = 2 * i
        local_part(s0 + 1, slots[1])
        ms = recurrence(s0, ms, slots[0])
        local_part(jnp.minimum(s0 + 2, N_CHUNKS - 1), slots[0])
        return recurrence(s0 + 1, ms, slots[1])

    m0 = tuple(jnp.full((SUBLANES, 128), NEG_BIG, F32) for _ in range(n_chain))
    lax.fori_loop(0, N_CHUNKS // 2, step_pair, m0)

    def finish(i, carry):
        rows = pl.ds(pl.multiple_of(i * TILE, TILE), TILE)
        hs = hf_ref[rows, :] + hb_ref[rows, :]
        ng = ng_ref[...]
        og = og_ref[0, rows, :].astype(F32)
        for h in range(H_B):
            sl = slice(h * DV_B, (h + 1) * DV_B)
            out_ref[0, rows, sl] = (_rms(hs[:, sl]) * ng[:, sl] * og[:, sl]).astype(BF16)
        return carry

    lax.fori_loop(0, T_ALL // TILE, finish, 0)


def _mlstm_call(mq, mk, mv, gc, gr, og, ng):
    B = mq.shape[0]
    full = lambda a: pl.BlockSpec((1,) + a.shape[1:], lambda b: (b,) + (0,) * (a.ndim - 1))
    return pl.pallas_call(
        _mlstm_kernel,
        grid=(B,),
        in_specs=[full(mq), full(mk), full(mv), full(gc), full(gr), full(og),
                  pl.BlockSpec(ng.shape, lambda b: (0, 0))],
        out_specs=pl.BlockSpec((1, T_ALL, H_B * DV_B), lambda b: (b, 0, 0)),
        out_shape=jax.ShapeDtypeStruct((B, T_ALL, H_B * DV_B), BF16),
        scratch_shapes=[
            pltpu.VMEM((T_ALL, H_B * DV_B), F32),
            pltpu.VMEM((T_ALL, H_B * DV_B), F32),
            pltpu.VMEM((2 * H_B * DK_B, 2 * DV_B), F32),
        ] + 2 * [
            pltpu.VMEM((2 * H_B * MLSTM_CHUNK, 2 * DV_B), F32),
            pltpu.VMEM((2 * H_B * DK_B, 2 * DV_B), F32),
            pltpu.VMEM((2 * H_B * 2 * MLSTM_CHUNK, 128), F32),
            pltpu.VMEM((2 * H_B * 2 * SUBLANES, 128), F32),
        ],
        compiler_params=_cparams(("arbitrary",)),
        name="mlstm",
    )(mq, mk, mv, gc, gr, og, ng)


def _na_bias_tables(rpb):
    kh = WIN_H
    n_drow = 2 * WIN_H - 1
    qcol = np.arange(GRID_W)
    col_start = np.clip(qcol - WIN_W // 2, 0, GRID_W - WIN_W)
    col_ok = (qcol[None, :] >= col_start[:, None]) & (qcol[None, :] < col_start[:, None] + WIN_W)
    dcol = qcol[None, :] - qcol[:, None] + (WIN_W - 1)
    onehot = (dcol[None] == np.arange(2 * WIN_W - 1)[:, None, None]) & col_ok[None]
    blocks = jnp.einsum('hdx,xck->hdck', rpb.astype(F32), jnp.asarray(onehot, F32),
                        precision=lax.Precision.HIGHEST)
    blocks = jnp.where(col_ok[None, None], blocks, NEG_BIG)
    outside = jnp.full((H_A, 1, GRID_W, GRID_W), NEG_BIG, F32)
    blocks = jnp.concatenate([blocks, outside], axis=1)
    idx = np.full((len(NA_CASES), NA_QROWS, NA_KROWS), n_drow, np.int32)
    for ci, r in enumerate(NA_CASES):
        u0 = int(np.clip(r - kh // 2, 0, GRID_ROWS - NA_KROWS))
        for qi in range(NA_QROWS):
            rq = r + qi
            r0 = int(np.clip(rq - kh // 2, 0, GRID_ROWS - kh))
            for ui in range(NA_KROWS):
                kr = u0 + ui
                if r0 <= kr < r0 + kh:
                    idx[ci, qi, ui] = kr - rq + (WIN_H - 1)
    tab = blocks[:, idx]
    tab = jnp.transpose(tab, (1, 0, 2, 4, 3, 5))
    return tab.reshape(len(NA_CASES), H_A, NA_TQ, NA_TK)


def _na_case(j):
    case = jnp.int32(NA_STEP_CASE[0])
    for step, c in enumerate(NA_STEP_CASE):
        if c != NA_STEP_CASE[0]:
            case = jnp.where(j - 1 == step, c, case)
    return case


def _na_kernel(q_ref, k_ref, v_ref, bias_ref, o_ref):
    j = pl.program_id(1)
    lane = lax.broadcasted_iota(I32, (1, 2 * HD_A), 1)
    lo_half = lane < HD_A

    def attend(q_rows, n_q, key_sets, bias_for_head):
        for pp0 in range(0, H_A // 2, NA_STAGE_PAIRS):
            heads = []
            for pp in range(pp0, pp0 + NA_STAGE_PAIRS):
                lanes = slice(pp * 2 * HD_A, (pp + 1) * 2 * HD_A)
                qp = q_ref[0, q_rows, lanes]
                ks = [k_ref[0, rs, lanes] for rs in key_sets]
                vs = [v_ref[0, rs, lanes] for rs in key_sets]
                for hh in range(2):
                    qm = jnp.where(lo_half if hh == 0 else ~lo_half, qp, jnp.zeros_like(qp))
                    heads.append(dict(head=2 * pp + hh, vs=vs, ss=[_dot_nt(qm, kk) for kk in ks]))
            for hd in heads:
                ss = hd['ss']
                if bias_for_head is not None:
                    ss[0] = ss[0] + bias_for_head(hd['head'])
                m = ss[0].max(axis=-1, keepdims=True)
                for s in ss[1:]:
                    m = jnp.maximum(m, s.max(axis=-1, keepdims=True))
                ps = [jnp.exp(s - m) for s in ss]
                l = ps[0].sum(axis=-1, keepdims=True)
                for p in ps[1:]:
                    l = l + p.sum(axis=-1, keepdims=True)
                hd['ps'] = [p.astype(BF16) for p in ps]
                hd['l'] = l
            for hd in heads:
                acc = _dot(hd['ps'][0], hd['vs'][0])
                for p, vv in zip(hd['ps'][1:], hd['vs'][1:]):
                    acc = acc + _dot(p, vv)
                hd['o'] = acc / hd['l']
            for i, pp in enumerate(range(pp0, pp0 + NA_STAGE_PAIRS)):
                lanes = slice(pp * 2 * HD_A, (pp + 1) * 2 * HD_A)
                o_ref[0, q_rows, lanes] = jnp.where(lo_half, heads[2 * i]['o'], heads[2 * i + 1]['o']).astype(BF16)

    ctx_rows = pl.ds(0, CTX_LEN)

    @pl.when(j == 0)
    def _():
        attend(ctx_rows, CTX_LEN, [ctx_rows], None)

    @pl.when(j > 0)
    def _():
        r = (j - 1) * NA_QROWS
        u0 = jnp.clip(r - WIN_H // 2, 0, GRID_ROWS - NA_KROWS)
        q_rows = pl.ds(pl.multiple_of(CTX_LEN + r * GRID_W, NA_TQ), NA_TQ)
        k_rows = pl.ds(pl.multiple_of(CTX_LEN + u0 * GRID_W, GRID_W), NA_TK)
        attend(q_rows, NA_TQ, [k_rows, ctx_rows], lambda head: bias_ref[0, head])


def _na_call(aq, ak, av, bias):
    B = aq.shape[0]
    full = pl.BlockSpec((1, T_ALL, D_A), lambda b, j: (b, 0, 0))
    return pl.pallas_call(
        _na_kernel,
        grid=(B, 1 + GRID_ROWS // NA_QROWS),
        in_specs=[full, full, full,
                  pl.BlockSpec((1, H_A, NA_TQ, NA_TK), lambda b, j: (_na_case(j), 0, 0, 0))],
        out_specs=full,
        out_shape=jax.ShapeDtypeStruct((B, T_ALL, D_A), BF16),
        compiler_params=_cparams(("arbitrary", "arbitrary")),
        name="nbr_attn",
    )(aq, ak, av, bias)


def _route(logits_t, rb_col):
    sc = _sigmoid(logits_t)
    sel = sc + rb_col
    selr = [sel[e:e + 1, :] for e in range(N_EXPERTS)]
    scr = [sc[e:e + 1, :] for e in range(N_EXPERTS)]
    gscore = []
    for g in range(N_GROUPS):
        a, b, c, d = selr[EXP_PER_GROUP * g:EXP_PER_GROUP * (g + 1)]
        s1, t1 = jnp.maximum(a, b), jnp.minimum(a, b)
        s2, t2 = jnp.maximum(c, d), jnp.minimum(c, d)
        gscore.append(jnp.maximum(s1, s2) + jnp.maximum(jnp.minimum(s1, s2), jnp.maximum(t1, t2)))
    best = gscore[0]
    gi = jnp.zeros(best.shape, I32)
    for g in range(1, N_GROUPS):
        better = gscore[g] > best
        gi = jnp.where(better, g, gi)
        best = jnp.where(better, gscore[g], best)
    vs, ws = [], []
    for k in range(EXP_PER_GROUP):
        v = selr[k]
        w = scr[k]
        for g in range(1, N_GROUPS):
            v = jnp.where(gi == g, selr[EXP_PER_GROUP * g + k], v)
            w = jnp.where(gi == g, scr[EXP_PER_GROUP * g + k], w)
        vs.append(v)
        ws.append(w)
    b1, i1 = vs[0], jnp.zeros(best.shape, I32)
    for k in range(1, EXP_PER_GROUP):
        better = vs[k] > b1
        i1 = jnp.where(better, k, i1)
        b1 = jnp.where(better, vs[k], b1)
    b2 = jnp.full(best.shape, -jnp.inf, F32)
    i2 = jnp.zeros(best.shape, I32)
    for k in range(EXP_PER_GROUP):
        vk = jnp.where(i1 == k, -jnp.inf, vs[k])
        better = vk > b2
        i2 = jnp.where(better, k, i2)
        b2 = jnp.where(better, vk, b2)
    w1 = ws[0]
    w2 = ws[0]
    for k in range(1, EXP_PER_GROUP):
        w1 = jnp.where(i1 == k, ws[k], w1)
        w2 = jnp.where(i2 == k, ws[k], w2)
    tot = w1 + w2
    return gi * EXP_PER_GROUP + i1, gi * EXP_PER_GROUP + i2, w1 / tot, w2 / tot


def _outproj_kernel(n_act, has_ctx, *refs):
    acts = refs[:n_act]
    ws = refs[n_act:2 * n_act]
    refs = refs[2 * n_act:]
    if has_ctx:
        x_in = jnp.where(pl.program_id(1) < NT_CTX, refs[1][0], refs[0][0])
        refs = refs[1:]
    else:
        x_in = refs[0][0]
    _, mod_ref, g_ref, rwh_ref, rwl_ref, xo_ref, hp_ref, lg_ref = refs
    mod = mod_ref[0]
    groups = [slice(r0, r0 + OUTPROJ_ROWS) for r0 in range(0, TILE, OUTPROJ_ROWS)]
    os = []
    for rows in groups:
        o = _dot(acts[0][0, rows, :], ws[0][...])
        for a, w in zip(acts[1:], ws[1:]):
            o = o + _dot(a[0, rows, :], w[...])
        os.append(o)
    hs = []
    for rows, o in zip(groups, os):
        x = x_in[rows, :] + mod[2:3] * o
        xo_ref[0, rows, :] = x
        h = _rms(x) * g_ref[...] * (1.0 + mod[4:5]) + mod[3:4]
        _store_token_tiles(hp_ref, (0,), rows.start, h)
        hs.append(_split_bf16(h, 2))
    for rows, (h_hi, h_lo) in zip(groups, hs):
        lg_ref[0, :, rows] = (_dot_nt(rwh_ref[...], h_hi) + _dot_nt(rwh_ref[...], h_lo)
                              + _dot_nt(rwl_ref[...], h_hi))


def _outproj_call(acts, ws, x_src, x_off, modl, norm_g, rwh, rwl, n_tiles, ctx_src=None):
    B = x_src.shape[0]
    n_act = len(acts)
    T = n_tiles * TILE
    const2 = lambda b, j: (0, 0)
    in_specs = []
    for a, a_off in acts:
        in_specs.append(pl.BlockSpec((1, TILE, a.shape[2]), functools.partial(lambda b, j, o: (b, j + o, 0), o=a_off)))
    for w in ws:
        in_specs.append(pl.BlockSpec(w.shape, const2))
    streams = [x_src]
    if ctx_src is None:
        in_specs.append(pl.BlockSpec((1, TILE, D_MODEL), lambda b, j: (b, j + x_off, 0)))
    else:
        assert x_off == 0
        in_specs += _combined_stream_specs()
        streams.append(ctx_src)
    in_specs += [
        pl.BlockSpec((1, 6, D_MODEL), lambda b, j: _mod_index(b, j + x_off)),
        pl.BlockSpec((1, D_MODEL), const2),
        pl.BlockSpec(rwh.shape, const2),
        pl.BlockSpec(rwl.shape, const2),
    ]
    tok = lambda b, j: (b, j, 0)
    out_shape = [
        jax.ShapeDtypeStruct((B, T, D_MODEL), F32),
        jax.ShapeDtypeStruct((B, T * SUBLANES, 128), F32),
        jax.ShapeDtypeStruct((B, N_EXPERTS, T), F32),
    ]
    out_specs = [
        pl.BlockSpec((1, TILE, D_MODEL), tok),
        pl.BlockSpec((1, TILE * SUBLANES, 128), tok),
        pl.BlockSpec((1, N_EXPERTS, TILE), lambda b, j: (b, 0, j)),
    ]
    return pl.pallas_call(
        functools.partial(_outproj_kernel, n_act, ctx_src is not None),
        grid=(B, n_tiles),
        in_specs=in_specs,
        out_specs=out_specs,
        out_shape=out_shape,
        compiler_params=_cparams(("arbitrary", "arbitrary")),
        name="outproj",
    )(*[a for a, _ in acts], *ws, *streams, modl, norm_g, rwh, rwl)


def _route_kernel(lg_ref, rb_ref, su_ref, ri_ref, rw_ref, cnt_ref):
    logits_t = lg_ref[0]
    T = logits_t.shape[1]
    e1, e2, w1, w2 = _route(logits_t, rb_ref[...])
    eidx = lax.broadcasted_iota(I32, logits_t.shape, 0)
    oh1 = eidx == e1
    oh2 = eidx == e2
    onehot = jnp.where(oh1, 1.0, jnp.where(oh2, 1.0, 0.0))
    count = jnp.zeros((N_EXPERTS, 1), F32)
    ranks = []
    for jj in range(T // TILE):
        oh = onehot[:, jj * TILE:(jj + 1) * TILE]
        ranks.append(_dot(oh.astype(BF16), su_ref[...]) + count)
        count = count + jnp.sum(oh, axis=1, keepdims=True)
    cpad = jnp.floor((count + (SUBLANES - 1.0)) * (1.0 / SUBLANES)) * SUBLANES
    ecol = lax.broadcasted_iota(I32, (N_EXPERTS, 1), 0)
    start = jnp.zeros((N_EXPERTS, 1), F32)
    for e in range(N_EXPERTS - 1):
        start = start + jnp.where(ecol > e, cpad[e:e + 1, :], 0.0)
    row = jnp.concatenate(ranks, axis=1) + start
    r1 = jnp.sum(jnp.where(oh1, row, 0.0), axis=0, keepdims=True)
    r2 = jnp.sum(jnp.where(oh2, row, 0.0), axis=0, keepdims=True)
    zi = jnp.zeros((SUBLANES - 4, T), I32)
    ri_ref[0] = jnp.concatenate([e1, e2, r1.astype(I32), r2.astype(I32), zi], axis=0)
    zf = jnp.zeros((SUBLANES - 2, T), F32)
    rw_ref[0] = jnp.concatenate([w1, w2, zf], axis=0)
    cnt_ref[0] = jnp.broadcast_to(count, (N_EXPERTS, 128)).astype(I32)


def _route_call(lg, rb, su):
    B, _, T = lg.shape
    per_sample = lambda b: (b, 0, 0)
    return pl.pallas_call(
        _route_kernel,
        grid=(B,),
        in_specs=[pl.BlockSpec((1, N_EXPERTS, T), per_sample),
                  pl.BlockSpec(rb.shape, lambda b: (0, 0)),
                  pl.BlockSpec(su.shape, lambda b: (0, 0))],
        out_specs=[pl.BlockSpec((1, SUBLANES, T), per_sample),
                   pl.BlockSpec((1, SUBLANES, T), per_sample),
                   pl.BlockSpec((1, N_EXPERTS, 128), per_sample)],
        out_shape=[jax.ShapeDtypeStruct((B, SUBLANES, T), I32),
                   jax.ShapeDtypeStruct((B, SUBLANES, T), F32),
                   jax.ShapeDtypeStruct((B, N_EXPERTS, 128), I32)],
        compiler_params=_cparams(("arbitrary",)),
        name="route",
    )(lg, rb, su)


def _moe_kernel(T, n_rows, cnt_ref, ri_ref, rw_ref, h_ref, w1_ref, w3_ref, w2_ref, y_ref,
                xb_ref, ob_ref, tokl_ref, off_ref):
    b = pl.program_id(0)
    e = pl.program_id(1)

    @pl.when(e == 0)
    def _():
        off_ref[0] = 0
        for i in range(N_EXPERTS):
            c = cnt_ref[b, i]
            off_ref[i + 1] = off_ref[i] + ((c + SUBLANES - 1) // SUBLANES) * SUBLANES

        for i in range(N_EXPERTS):
            for k in range(SUBLANES):
                tokl_ref[jnp.maximum(off_ref[i + 1] - SUBLANES + k, 0)] = 0

        def clear(i, carry):
            tokl_ref[off_ref[N_EXPERTS] + i] = 0
            return carry

        lax.fori_loop(0, MOE_RB, clear, 0, unroll=8)

        def place(t, carry):
            tokl_ref[ri_ref[2 * T + t]] = t
            tokl_ref[ri_ref[3 * T + t]] = t
            return carry

        lax.fori_loop(0, T, place, 0, unroll=8)

    c = cnt_ref[b, e]
    base = off_ref[e]

    def tile_rows(row):
        return pl.ds(pl.multiple_of(row * SUBLANES, SUBLANES), SUBLANES)

    def block(p0, n_blk_rows):
        def gather(i, carry2):
            xb_ref[tile_rows(i), :] = h_ref[0, tile_rows(tokl_ref[p0 + i]), :]
            return carry2

        lax.fori_loop(0, n_blk_rows, gather, 0, unroll=8)
        xb = _load_token_tiles(xb_ref, (), 0, n_blk_rows).astype(BF16)
        h1 = _dot(xb, w1_ref[0, 0])
        h3 = _dot(xb, w3_ref[0, 0])
        act = (h1 * _sigmoid(h1)) * h3
        _store_token_tiles(ob_ref, (), p0, _dot(act.astype(BF16), w2_ref[0, 0]))

    n_big = (c + MOE_RB - MOE_RB_TAIL - 1) // MOE_RB
    n_big = jnp.maximum(n_big, 0)

    def big_block(rb, carry):
        block(pl.multiple_of(base + rb * MOE_RB, SUBLANES), MOE_RB)
        return carry

    lax.fori_loop(0, n_big, big_block, 0)

    @pl.when(c > n_big * MOE_RB)
    def _():
        block(pl.multiple_of(base + n_big * MOE_RB, SUBLANES), MOE_RB_TAIL)

    @pl.when(e == N_EXPERTS - 1)
    def _():
        def combine(t, carry):
            y_ref[0, tile_rows(t), :] = (rw_ref[t] * ob_ref[tile_rows(ri_ref[2 * T + t]), :]
                                         + rw_ref[T + t] * ob_ref[tile_rows(ri_ref[3 * T + t]), :])
            return carry

        lax.fori_loop(0, T, combine, 0, unroll=8)


def _moe_call(cnt, ri, rw, h, w1, w3, w2, layer):
    B = h.shape[0]
    T = h.shape[1] // SUBLANES
    n_rows = 2 * T + N_EXPERTS * SUBLANES + MOE_RB
    smem = functools.partial(pl.BlockSpec, memory_space=pltpu.SMEM)
    once = pl.Buffered(1)
    return pl.pallas_call(
        functools.partial(_moe_kernel, T, n_rows),
        grid=(B, N_EXPERTS),
        in_specs=[
            smem(cnt.shape, lambda b, e: (0, 0)),
            smem((4 * T,), lambda b, e: (b,)),
            smem((4 * T,), lambda b, e: (b,)),
            pl.BlockSpec((1, T * SUBLANES, 128), lambda b, e: (b, 0, 0), pipeline_mode=once),
            pl.BlockSpec((1, 1, D_MODEL, D_FF), lambda b, e: (layer, e, 0, 0)),
            pl.BlockSpec((1, 1, D_MODEL, D_FF), lambda b, e: (layer, e, 0, 0)),
            pl.BlockSpec((1, 1, D_FF, D_MODEL), lambda b, e: (layer, e, 0, 0)),
        ],
        out_specs=pl.BlockSpec((1, T * SUBLANES, 128), lambda b, e: (b, 0, 0), pipeline_mode=once),
        out_shape=jax.ShapeDtypeStruct((B, T * SUBLANES, 128), F32),
        scratch_shapes=[
            pltpu.VMEM((MOE_RB * SUBLANES, 128), F32),
            pltpu.VMEM((n_rows * SUBLANES, 128), F32),
            pltpu.SMEM((n_rows,), I32),
            pltpu.SMEM((N_EXPERTS + 1,), I32),
        ],
        compiler_params=_cparams(("arbitrary", "arbitrary")),
        name="moe",
    )(cnt, ri, rw, h, w1, w3, w2)


def _route_tables(ri, rw, cnt):
    return cnt[:, :, 0], ri[:, :4].reshape(-1), rw[:, :4].reshape(-1)


def _inproj1_kernel(x_ref, y_ref, mod0_ref, mod_ref, g_ref, w_ref, qg_ref, kg_ref, cos_ref, sin_ref,
                    xo_ref, q_ref, k_ref, v_ref):
    x = x_ref[0] + mod0_ref[0][5:6] * _load_token_tiles(y_ref, (0,), 0, TILE)
    xo_ref[0] = x
    mod = mod_ref[0]
    hb = (_rms(x) * g_ref[...] * (1.0 + mod[1:2]) + mod[0:1]).astype(BF16)
    cos = cos_ref[...]
    sin = sin_ref[...]

    def rope_head(a, gain):
        n = _rms(a) * gain
        return n * cos + pltpu.roll(n, HD_C // 2, 1) * sin

    qg = qg_ref[...]
    kg = kg_ref[...]
    ko = H_C * HD_C
    vo = (H_C + KV_C) * HD_C
    kv = _dot(hb, w_ref[:, ko:vo + KV_C * HD_C])
    for h in range(KV_C):
        sl = slice(h * HD_C, (h + 1) * HD_C)
        k_ref[0, :, sl] = rope_head(kv[:, sl], kg).astype(BF16)
    v_ref[0] = kv[:, KV_C * HD_C:].astype(BF16)

    @pl.when(pl.program_id(1) >= NT_CTX)
    def _():
        qs = _dot(hb, w_ref[:, :ko])
        for h in range(H_C):
            sl = slice(h * HD_C, (h + 1) * HD_C)
            q_ref[0, :, sl] = (rope_head(qs[:, sl], qg) * HD_C ** -0.5).astype(BF16)


def _inproj1_call(x1, y0, mod0, mod1, norm_g, w, qg, kg, cos, sin):
    B = x1.shape[0]
    tok = lambda b, j: (b, j, 0)
    const2 = lambda b, j: (0, 0)
    lat_tok = lambda b, j: (b, jnp.maximum(j - NT_CTX, 0), 0)
    widths = [(T_ALL, D_MODEL, F32), (SEQ, H_C * HD_C, BF16),
              (T_ALL, KV_C * HD_C, BF16), (T_ALL, KV_C * HD_C, BF16)]
    return pl.pallas_call(
        _inproj1_kernel,
        grid=(B, NT_ALL),
        in_specs=[
            pl.BlockSpec((1, TILE, D_MODEL), tok),
            pl.BlockSpec((1, TILE * SUBLANES, 128), tok),
            pl.BlockSpec((1, 6, D_MODEL), _mod_index),
            pl.BlockSpec((1, 6, D_MODEL), _mod_index),
            pl.BlockSpec((1, D_MODEL), const2),
            pl.BlockSpec(w.shape, const2),
            pl.BlockSpec(qg.shape, const2),
            pl.BlockSpec(kg.shape, const2),
            pl.BlockSpec((TILE, HD_C), lambda b, j: (j, 0)),
            pl.BlockSpec((TILE, HD_C), lambda b, j: (j, 0)),
        ],
        out_specs=[pl.BlockSpec((1, TILE, w_), lat_tok if t == SEQ else tok) for t, w_, _ in widths],
        out_shape=[jax.ShapeDtypeStruct((B, t, w_), dt) for t, w_, dt in widths],
        compiler_params=_cparams(("arbitrary", "arbitrary")),
        name="inproj1",
    )(x1, y0, mod0, mod1, norm_g, w, qg, kg, cos, sin)


def _gqa_kernel(q_ref, k_ref, v_ref, o_ref):
    k = k_ref[0]
    v = v_ref[0]
    for h0 in range(0, H_C // KV_C, GQA_STAGE_HEADS):
        sls = [slice(h * HD_C, (h + 1) * HD_C) for h in range(h0, h0 + GQA_STAGE_HEADS)]
        ss = [_dot_nt(q_ref[0, :, sl], k) for sl in sls]
        ps, ls = [], []
        for s in ss:
            p = jnp.exp(s - s.max(axis=-1, keepdims=True))
            ls.append(p.sum(axis=-1, keepdims=True))
            ps.append(p.astype(BF16))
        for sl, p, l in zip(sls, ps, ls):
            o_ref[0, :, sl] = (_dot(p, v) / l).astype(BF16)


def _gqa_call(q, k, v):
    B = q.shape[0]
    gw = (H_C // KV_C) * HD_C
    return pl.pallas_call(
        _gqa_kernel,
        grid=(B, KV_C, SEQ // GQA_TQ),
        in_specs=[
            pl.BlockSpec((1, GQA_TQ, gw), lambda b, g, j: (b, j, g)),
            pl.BlockSpec((1, T_ALL, HD_C), lambda b, g, j: (b, 0, g)),
            pl.BlockSpec((1, T_ALL, HD_C), lambda b, g, j: (b, 0, g)),
        ],
        out_specs=pl.BlockSpec((1, GQA_TQ, gw), lambda b, g, j: (b, j, g)),
        out_shape=jax.ShapeDtypeStruct((B, SEQ, H_C * HD_C), BF16),
        compiler_params=_cparams(("arbitrary", "arbitrary", "arbitrary")),
        name="gqa",
    )(q, k, v)


def _final_kernel(x_ref, y_ref, mod_ref, o_ref):
    o_ref[0] = x_ref[0] + mod_ref[0][5:6] * _load_token_tiles(y_ref, (0,), 0, FINAL_TILE)


def _final_call(x, y, modl):
    B, T, D = x.shape
    tok = lambda b, j: (b, j, 0)
    return pl.pallas_call(
        _final_kernel,
        grid=(B, T // FINAL_TILE),
        in_specs=[pl.BlockSpec((1, FINAL_TILE, D), tok), pl.BlockSpec((1, FINAL_TILE * SUBLANES, 128), tok),
                  pl.BlockSpec((1, 6, D), lambda b, j: (b, 0, 0))],
        out_specs=pl.BlockSpec((1, FINAL_TILE, D), tok),
        out_shape=jax.ShapeDtypeStruct((B, T, D), F32),
        compiler_params=_cparams(("arbitrary", "arbitrary")),
        name="final_residual",
    )(x, y, modl)


def _chunk_tri(lower):
    i = np.arange(TILE)
    same = (i[:, None] // MLSTM_CHUNK) == (i[None, :] // MLSTM_CHUNK)
    tri = (i[None, :] <= i[:, None]) if lower else (i[None, :] >= i[:, None])
    return jnp.asarray((same & tri).astype(np.float32), BF16)


def _rope_tables():
    n_freq = HD_C // 4
    inv_freq = ROPE_THETA ** (-jnp.arange(n_freq, dtype=F32) / n_freq)
    t = jnp.arange(SEQ)
    rows = (t // GRID_W).astype(F32)
    cols = (t % GRID_W).astype(F32)
    ang = jnp.concatenate([rows[:, None] * inv_freq, cols[:, None] * inv_freq], axis=-1)
    cos, sin = jnp.cos(ang), jnp.sin(ang)
    cos_l = jnp.concatenate([cos, cos], axis=-1)
    sin_l = jnp.concatenate([-sin, sin], axis=-1)
    cos_all = jnp.concatenate([jnp.ones((CTX_LEN, HD_C), F32), cos_l], axis=0)
    sin_all = jnp.concatenate([jnp.zeros((CTX_LEN, HD_C), F32), sin_l], axis=0)
    return cos_all, sin_all


_HEAD_PERM = np.concatenate([np.arange(0, HD_C, 2), np.arange(1, HD_C, 2)])


def kernel(x, c, ctx, c_ctx, ada_w, ada_b, norm_mix_g, norm_ffn_g, even_w_in, even_w_out,
           na_q_norm_g, na_k_norm_g, na_rpb, mlstm_gate_b, mlstm_norm_g, odd_w_in, odd_w_out,
           gqa_q_norm_g, gqa_k_norm_g, router_w, router_b, exp_w1, exp_w3, exp_w2):
    B = x.shape[0]
    assert B <= N_MOD_CTX_ROW and x.shape[1:] == (SEQ, D_MODEL) and ctx.shape[1:] == (CTX_LEN, D_MODEL)
    n_g = 4 * H_B

    cvec = jnp.zeros((N_MOD_ROWS, D_MODEL), F32).at[:B].set(c).at[N_MOD_CTX_ROW].set(c_ctx)
    mod = _ada_call(cvec, ada_w, ada_b).reshape(2, N_MOD_ROWS, 6, D_MODEL)
    mod0, mod1 = mod[0], mod[1]

    w_in = even_w_in[0]
    n_main = w_in.shape[1] - n_g
    wm = w_in[:, :n_main].astype(BF16)
    wg_f = w_in[:, n_main:]
    wg = jnp.pad(wg_f, ((0, 0), (0, 128 - n_g))).astype(BF16)
    wgt = wg_f.T.astype(BF16)
    gb = jnp.pad(mlstm_gate_b[0].reshape(1, n_g), ((0, 0), (0, 128 - n_g)))
    gbt = mlstm_gate_b[0].reshape(n_g, 1)
    qg = jnp.tile(na_q_norm_g[0], H_A).reshape(1, D_A)
    kg = jnp.tile(na_k_norm_g[0], H_A).reshape(1, D_A)
    hid = np.arange(128) // HD_A
    bd = jnp.asarray((hid[:, None] == hid[None, :]).astype(np.float32), BF16)
    k_lo = 3 * D_A + H_B * DK_B
    wkt = w_in[:, k_lo:k_lo + H_B * DK_B].T.astype(BF16)
    aq, ak, av, mq, mv, og, mkt, gc, gr = _inproj0_call(
        x, ctx, mod0, norm_mix_g[0].reshape(1, D_MODEL), wm, wkt, wg, wgt, gb, gbt, qg, kg, bd,
        _chunk_tri(True), _chunk_tri(False))
    hm = _mlstm_call(mq, mkt, mv, gc, gr, og, mlstm_norm_g[0].reshape(1, H_B * DV_B))
    oa = _na_call(aq, ak, av, _na_bias_tables(na_rpb[0]))

    rw_t = router_w.T
    rwh = rw_t.astype(BF16)
    rwl = (rw_t - rwh.astype(F32)).astype(BF16)
    rb = router_b.reshape(N_EXPERTS, 1).astype(F32)
    i = np.arange(TILE)
    su = jnp.asarray((i[:, None] < i[None, :]).astype(np.float32), BF16)
    w_out = even_w_out[0].astype(BF16)
    ew1, ew3, ew2 = exp_w1.astype(BF16), exp_w3.astype(BF16), exp_w2.astype(BF16)
    x1, hp0, lg0 = _outproj_call(
        [(oa, 0), (hm, 0)], [w_out[:D_A], w_out[D_A:]], x, 0, mod0,
        norm_ffn_g[0].reshape(1, D_MODEL), rwh, rwl, NT_ALL, ctx_src=ctx)
    y0 = _moe_call(*_route_tables(*_route_call(lg0, rb, su)), hp0, ew1, ew3, ew2, 0)

    w1_in = odd_w_in[0]
    qk_cols = np.concatenate([h * HD_C + _HEAD_PERM for h in range(H_C + KV_C)])
    cols = np.concatenate([qk_cols, np.arange((H_C + KV_C) * HD_C, w1_in.shape[1])])
    w1_in = w1_in[:, cols].astype(BF16)
    cos_all, sin_all = _rope_tables()
    x2, q, k, v = _inproj1_call(
        x1, y0, mod0, mod1, norm_mix_g[1].reshape(1, D_MODEL), w1_in,
        gqa_q_norm_g[0][_HEAD_PERM].reshape(1, HD_C), gqa_k_norm_g[0][_HEAD_PERM].reshape(1, HD_C),
        cos_all, sin_all)
    o = _gqa_call(q, k, v)
    x3, hp1, lg1 = _outproj_call(
        [(o, 0)], [odd_w_out[0].astype(BF16)], x2, NT_CTX, mod1,
        norm_ffn_g[1].reshape(1, D_MODEL), rwh, rwl, SEQ // TILE)
    y1 = _moe_call(*_route_tables(*_route_call(lg1, rb, su)), hp1, ew1, ew3, ew2, 1)
    return _final_call(x3, y1, mod1)
```

```python
import functools

import numpy as np
import jax
import jax.numpy as jnp
from jax import lax
from jax.experimental import pallas as pl
from jax.experimental.pallas import tpu as pltpu

F32 = jnp.float32
BF16 = jnp.bfloat16
I32 = jnp.int32
U32 = jnp.uint32

D_MODEL = 1024
SEQ = 2048
GRID_W = 64
GRID_ROWS = SEQ // GRID_W
CTX_LEN = 256
T_ALL = CTX_LEN + SEQ
WIN_H = 8
WIN_W = 16
HD_A = 64
H_A = 8
D_A = H_A * HD_A
H_B = 4
DV_B = 128
DK_B = 64
MLSTM_CHUNK = 64
N_CHUNKS = T_ALL // MLSTM_CHUNK
N_CTX_CHUNKS = CTX_LEN // MLSTM_CHUNK
HD_C = 128
H_C = 8
KV_C = 2
ROPE_THETA = 10000.0
N_EXPERTS = 16
N_GROUPS = 4
EXP_PER_GROUP = 4
D_FF = 512
EPS = 1e-6
NEG_BIG = -1e30

TILE = 256
NT_ALL = T_ALL // TILE
NT_CTX = CTX_LEN // TILE
NA_QROWS = 4
NA_KROWS = WIN_H + NA_QROWS - 1
NA_TQ = NA_QROWS * GRID_W
NA_TK = NA_KROWS * GRID_W
NA_STAGE_PAIRS = 2


def _na_geometry(r):
    u0 = min(max(r - WIN_H // 2, 0), GRID_ROWS - NA_KROWS)
    r0s = tuple(min(max(r + qi - WIN_H // 2, 0), GRID_ROWS - WIN_H) - r for qi in range(NA_QROWS))
    return (u0 - r, r0s)


def _na_cases():
    reps, step_case = [], []
    for r in range(0, GRID_ROWS, NA_QROWS):
        geo = _na_geometry(r)
        known = [_na_geometry(q) for q in reps]
        if geo not in known:
            reps.append(r)
            known.append(geo)
        step_case.append(known.index(geo))
    return tuple(reps), tuple(step_case)


NA_CASES, NA_STEP_CASE = _na_cases()
OUTPROJ_ROWS = 128
GQA_TQ = 512
GQA_STAGE_HEADS = 2
FINAL_TILE = 1024
MOE_RB = 256
MOE_RB_TAIL = 128
SUBLANES = 8
VMEM_LIMIT = 56 * 1024 * 1024


def _cparams(sem):
    return pltpu.CompilerParams(dimension_semantics=sem, vmem_limit_bytes=VMEM_LIMIT)


def _sigmoid(x):
    return 1.0 / (1.0 + jnp.exp(-x))


def _rms(x):
    return x * lax.rsqrt(jnp.mean(x * x, axis=-1, keepdims=True) + EPS)


def _dot(a, b):
    return jnp.dot(a, b, preferred_element_type=F32)


def _dot_nt(a, b):
    return lax.dot_general(a, b, (((1,), (1,)), ((), ())), preferred_element_type=F32)


def _dot_tn(a, b):
    return lax.dot_general(a, b, (((0,), (0,)), ((), ())), preferred_element_type=F32)


def _split_bf16(x, n):
    parts = []
    r = x
    for _ in range(n):
        p = r.astype(BF16)
        parts.append(p)
        r = r - p.astype(F32)
    return parts


LANE_TILES = D_MODEL // 128


def _load_token_tiles(ref, lead, tok0, n_tok):
    parts = [ref[(*lead, pl.ds(tok0 * SUBLANES + c, n_tok, stride=SUBLANES), slice(None))]
             for c in range(LANE_TILES)]
    return jnp.concatenate(parts, axis=1)


def _store_token_tiles(ref, lead, tok0, val):
    for c in range(LANE_TILES):
        ref[(*lead, pl.ds(tok0 * SUBLANES + c, val.shape[0], stride=SUBLANES), slice(None))] = (
            val[:, c * 128:(c + 1) * 128])


def _mod_index(b, j):
    return (jnp.where(j < NT_CTX, N_MOD_CTX_ROW, b), 0, 0)


def _combined_stream_specs():
    return [pl.BlockSpec((1, TILE, D_MODEL), lambda b, j: (b, jnp.maximum(j - NT_CTX, 0), 0)),
            pl.BlockSpec((1, TILE, D_MODEL), lambda b, j: (b, jnp.minimum(j, NT_CTX - 1), 0))]


N_MOD_ROWS = 16
N_MOD_CTX_ROW = 8


ADA_TN = 1536


def _ada_kernel(c_ref, w_ref, b_ref, o_ref):
    c = c_ref[...]
    s = (c * _sigmoid(c)).astype(BF16)
    o_ref[0] = _dot(s, w_ref[0].astype(BF16)) + b_ref[0]


def _ada_call(cvec, ada_w, ada_b):
    depth, d, n = ada_w.shape
    return pl.pallas_call(
        _ada_kernel,
        grid=(depth, n // ADA_TN),
        in_specs=[
            pl.BlockSpec((N_MOD_ROWS, d), lambda l, j: (0, 0)),
            pl.BlockSpec((1, d, ADA_TN), lambda l, j: (l, 0, j)),
            pl.BlockSpec((1, 1, ADA_TN), lambda l, j: (l, 0, j)),
        ],
        out_specs=pl.BlockSpec((1, N_MOD_ROWS, ADA_TN), lambda l, j: (l, 0, j)),
        out_shape=jax.ShapeDtypeStruct((depth, N_MOD_ROWS, n), F32),
        compiler_params=_cparams(("arbitrary", "arbitrary")),
        name="ada_mod",
    )(cvec, ada_w, ada_b.reshape(depth, 1, n))


def _log_sigmoid(x):
    return jnp.minimum(x, 0.0) - jnp.log1p(jnp.exp(-jnp.abs(x)))


def _inproj0_kernel(x_ref, ctx_ref, mod_ref, g_ref, wm_ref, wkt_ref, wg_ref, wgt_ref, gb_ref, gbt_ref,
                    qg_ref, kg_ref, bd_ref, tril_ref, triu_ref,
                    aq_ref, ak_ref, av_ref, mq_ref, mv_ref, og_ref, mkt_ref, gc_ref, gr_ref):
    mod = mod_ref[0]
    x_in = jnp.where(pl.program_id(1) < NT_CTX, ctx_ref[0], x_ref[0])
    h = _rms(x_in) * g_ref[...] * (1.0 + mod[1:2]) + mod[0:1]
    hb = h.astype(BF16)

    def proj(lo, hi):
        return _dot(hb, wm_ref[:, lo:hi])

    def head_norm(a, gain):
        a2 = a * a
        sums = []
        for s0 in range(0, D_A, 128):
            hi_, lo_ = _split_bf16(a2[:, s0:s0 + 128], 2)
            sums.append(_dot(hi_, bd_ref[...]) + _dot(lo_, bd_ref[...]))
        ss = jnp.concatenate(sums, axis=1)
        return a * lax.rsqrt(ss * (1.0 / HD_A) + EPS) * gain

    n_g = 4 * H_B
    p_aq, p_ak, p_av = proj(0, 512), proj(512, 1024), proj(1024, 1536)
    p_mq, p_mv, p_og = proj(1536, 1792), proj(2048, 2560), proj(2560, 3072)
    kt = _dot_nt(wkt_ref[...], hb).astype(BF16)
    gcol = _dot(hb, wg_ref[...]) + gb_ref[...]
    grow = _dot_nt(wgt_ref[...], hb) + gbt_ref[...]

    aq_ref[0] = (head_norm(p_aq, qg_ref[...]) * HD_A ** -0.5).astype(BF16)
    ak_ref[0] = head_norm(p_ak, kg_ref[...]).astype(BF16)
    av_ref[0] = p_av.astype(BF16)
    mq_ref[0] = (p_mq * DK_B ** -0.5).astype(BF16)
    for c in range(TILE // MLSTM_CHUNK):
        mkt_ref[0, c] = kt[:, c * MLSTM_CHUNK:(c + 1) * MLSTM_CHUNK]
    mv_ref[0] = p_mv.astype(BF16)
    og_ref[0] = _sigmoid(p_og).astype(BF16)

    tril = tril_ref[...]
    triu = triu_ref[...]
    ls_c = _split_bf16(_log_sigmoid(gcol), 3)
    ls_r = _split_bf16(_log_sigmoid(grow), 3)
    pre_c = _dot(tril, ls_c[0]) + _dot(tril, ls_c[1]) + _dot(tril, ls_c[2])
    suf_c = _dot(triu, ls_c[0]) + _dot(triu, ls_c[1]) + _dot(triu, ls_c[2])
    pre_r = _dot(ls_r[0], triu) + _dot(ls_r[1], triu) + _dot(ls_r[2], triu)
    suf_r = _dot(ls_r[0], tril) + _dot(ls_r[1], tril) + _dot(ls_r[2], tril)
    cidx = lax.broadcasted_iota(I32, gcol.shape, 1)
    ridx = lax.broadcasted_iota(I32, grow.shape, 0)

    def pick(idx, raw, pre, suf):
        is_ff = (idx >= H_B) & (idx < 2 * H_B)
        is_fb = idx >= 3 * H_B
        return jnp.where(is_ff, pre, jnp.where(is_fb, suf, raw))

    gc_ref[0] = pick(cidx, gcol, pre_c, suf_c)[:, :n_g]
    grow = pick(ridx, grow, pre_r, suf_r)
    for c in range(TILE // MLSTM_CHUNK):
        gr_ref[0, c] = grow[:, c * MLSTM_CHUNK:(c + 1) * MLSTM_CHUNK]


def _inproj0_call(x, ctx, modl, norm_g, wm, wkt, wg, wgt, gb, gbt, qg, kg, bd, tril, triu):
    B = x.shape[0]
    n_g = 4 * H_B
    tok = lambda b, j: (b, j, 0)
    const2 = lambda b, j: (0, 0)
    chunked = lambda b, j: (b, j, 0, 0)
    tile_chunks = TILE // MLSTM_CHUNK
    outs = [
        (D_A, BF16), (D_A, BF16), (D_A, BF16),
        (H_B * DK_B, BF16), (H_B * DV_B, BF16),
        (H_B * DV_B, BF16),
    ]
    out_shape = [jax.ShapeDtypeStruct((B, T_ALL, w), dt) for w, dt in outs]
    out_specs = [pl.BlockSpec((1, TILE, w), tok) for w, _ in outs]
    out_shape += [jax.ShapeDtypeStruct((B, N_CHUNKS, H_B * DK_B, MLSTM_CHUNK), BF16),
                  jax.ShapeDtypeStruct((B, T_ALL, n_g), F32),
                  jax.ShapeDtypeStruct((B, N_CHUNKS, n_g, MLSTM_CHUNK), F32)]
    out_specs += [pl.BlockSpec((1, tile_chunks, H_B * DK_B, MLSTM_CHUNK), chunked),
                  pl.BlockSpec((1, TILE, n_g), tok),
                  pl.BlockSpec((1, tile_chunks, n_g, MLSTM_CHUNK), chunked)]
    return pl.pallas_call(
        _inproj0_kernel,
        grid=(B, NT_ALL),
        in_specs=_combined_stream_specs() + [
            pl.BlockSpec((1, 6, D_MODEL), _mod_index),
            pl.BlockSpec((1, D_MODEL), const2),
            pl.BlockSpec(wm.shape, const2),
            pl.BlockSpec(wkt.shape, const2),
            pl.BlockSpec(wg.shape, const2),
            pl.BlockSpec(wgt.shape, const2),
            pl.BlockSpec(gb.shape, const2),
            pl.BlockSpec(gbt.shape, const2),
            pl.BlockSpec(qg.shape, const2),
            pl.BlockSpec(kg.shape, const2),
            pl.BlockSpec(bd.shape, const2),
            pl.BlockSpec(tril.shape, const2),
            pl.BlockSpec(triu.shape, const2),
        ],
        out_specs=out_specs,
        out_shape=out_shape,
        compiler_params=_cparams(("arbitrary", "arbitrary")),
        name="inproj0",
    )(x, ctx, modl, norm_g, wm, wkt, wg, wgt, gb, gbt, qg, kg, bd, tril, triu)


def _mlstm_kernel(mq_ref, mkt_ref, mv_ref, gc_ref, gr_ref, og_ref, ng_ref, out_ref,
                  hf_ref, hb_ref, s_ref, *local_refs):
    L = MLSTM_CHUNK
    n_chain = 2 * H_B
    slots = (local_refs[:4], local_refs[4:])
    s_ref[...] = jnp.zeros(s_ref.shape, F32)
    ri = lax.broadcasted_iota(I32, (L, L), 0)
    ci = lax.broadcasted_iota(I32, (L, L), 1)
    masks = (ci <= ri, ci >= ri)
    ones_aug = jnp.ones((L, DV_B), BF16)

    def tile_up(x, n_rows, n_cols):
        x = jnp.concatenate([x] * n_rows, axis=0)
        return x if n_cols == 1 else jnp.concatenate([x] * n_cols, axis=1)

    def chunk_rows(step, d):
        if d == 0:
            chunk = step
        else:
            chunk = jnp.where(step < N_CTX_CHUNKS, N_CTX_CHUNKS - 1 - step, N_CHUNKS + N_CTX_CHUNKS - 1 - step)
        return chunk, pl.ds(pl.multiple_of(chunk * L, L), L)

    def local_part(step, slot):
        intra_ref, u_ref, col_ref, rep_ref = slot
        chains = []
        for d in range(2):
            chunk, rows = chunk_rows(step, d)
            gcc = gc_ref[0, rows, :]
            grr = gr_ref[0, chunk]
            for h in range(H_B):
                gi = 2 * H_B * d + h
                b_col = gcc[:, gi + H_B:gi + H_B + 1]
                chains.append(dict(
                    d=d,
                    q=mq_ref[0, rows, h * DK_B:(h + 1) * DK_B],
                    kt=mkt_ref[0, chunk, h * DK_B:(h + 1) * DK_B, :],
                    v=mv_ref[0, rows, h * DV_B:(h + 1) * DV_B],
                    ig_col=gcc[:, gi:gi + 1], b_col=b_col,
                    ig_row=grr[gi:gi + 1, :], b_row=grr[gi + H_B:gi + H_B + 1, :],
                    b_last=b_col[L - 1:L, :] if d == 0 else b_col[0:1, :]))
        for ch in chains:
            ch['qk'] = _dot(ch['q'], ch['kt'])
            ch['v_aug'] = jnp.concatenate([ch['v'], ones_aug], axis=1)
            g = ch['b_last'] - ch['b_col'] + ch['ig_col']
            ch['g_max'] = jnp.max(g, axis=0, keepdims=True)
            ch['wv'] = (jnp.exp(g - ch['g_max']) * ch['v_aug'].astype(F32)).astype(BF16)
        us = [_dot(ch['kt'], ch['wv']) for ch in chains]
        for ch in chains:
            dm = jnp.where(masks[ch['d']], ch['b_col'] - ch['b_row'] + ch['ig_row'], NEG_BIG)
            ch['m_loc'] = jnp.max(dm, axis=-1, keepdims=True)
            ch['p'] = (ch['qk'] * jnp.exp(dm - ch['m_loc'])).astype(BF16)
        intras = [_dot(ch['p'], ch['v_aug']) for ch in chains]
        cols, reps = [], []
        for ch in chains:
            cols += [jnp.broadcast_to(ch['b_col'], (L, 128)), jnp.broadcast_to(ch['m_loc'], (L, 128))]
            reps += [jnp.broadcast_to(ch['b_last'], (SUBLANES, 128)),
                     jnp.broadcast_to(ch['g_max'], (SUBLANES, 128))]
        intra_ref[...] = jnp.concatenate(intras, axis=0)
        u_ref[...] = jnp.concatenate(us, axis=0)
        col_ref[...] = jnp.concatenate(cols, axis=0)
        rep_ref[...] = jnp.concatenate(reps, axis=0)

    def recurrence(step, ms, slot):
        intra_ref, u_ref, col_ref, rep_ref = slot
        n_all = range(n_chain)
        rows = [chunk_rows(step, d)[1] for d in range(2)]
        states = [s_ref[c * DK_B:(c + 1) * DK_B, :] for c in n_all]
        inters = [_dot(mq_ref[0, rows[c // H_B], (c % H_B) * DK_B:(c % H_B + 1) * DK_B],
                       states[c].astype(BF16)) for c in n_all]
        new_ms, new_states, houts = [], [], []
        for c in n_all:
            m = ms[c]
            b_last = rep_ref[2 * c * SUBLANES:(2 * c + 1) * SUBLANES, :]
            g_max = rep_ref[(2 * c + 1) * SUBLANES:(2 * c + 2) * SUBLANES, :]
            m_new = jnp.maximum(b_last + m, g_max)
            w_old = tile_up(jnp.exp(b_last + m - m_new), DK_B // SUBLANES, 2)
            w_new = tile_up(jnp.exp(g_max - m_new), DK_B // SUBLANES, 2)
            new_states.append(w_old * states[c] + w_new * u_ref[c * DK_B:(c + 1) * DK_B, :])
            new_ms.append(m_new)
        s_ref[...] = jnp.concatenate(new_states, axis=0)
        for c in n_all:
            a = col_ref[2 * c * L:(2 * c + 1) * L, :] + tile_up(ms[c], L // SUBLANES, 1)
            m_loc = col_ref[(2 * c + 1) * L:(2 * c + 2) * L, :]
            m_row = jnp.maximum(a, m_loc)
            w_inter = jnp.exp(a - m_row)
            w_loc = jnp.exp(m_loc - m_row)
            intra = intra_ref[c * L:(c + 1) * L, :]
            num = w_inter * inters[c][:, :DV_B] + w_loc * intra[:, :DV_B]
            den = w_inter * inters[c][:, DV_B:] + w_loc * intra[:, DV_B:]
            houts.append(num / jnp.maximum(jnp.abs(den), jnp.exp(-m_row)))
        hf_ref[rows[0], :] = jnp.concatenate(houts[:H_B], axis=1)
        hb_ref[rows[1], :] = jnp.concatenate(houts[H_B:], axis=1)
        return tuple(new_ms)

    local_part(0, slots[0])

    def step_pair(i, ms):
        s0 = 2 * i
        local_part(s0 + 1, slots[1])
        ms = recurrence(s0, ms, slots[0])
        local_part(jnp.minimum(s0 + 2, N_CHUNKS - 1), slots[0])
        return recurrence(s0 + 1, ms, slots[1])

    m0 = tuple(jnp.full((SUBLANES, 128), NEG_BIG, F32) for _ in range(n_chain))
    lax.fori_loop(0, N_CHUNKS // 2, step_pair, m0)

    def finish(i, carry):
        rows = pl.ds(pl.multiple_of(i * TILE, TILE), TILE)
        hs = hf_ref[rows, :] + hb_ref[rows, :]
        ng = ng_ref[...]
        og = og_ref[0, rows, :].astype(F32)
        for h in range(H_B):
            sl = slice(h * DV_B, (h + 1) * DV_B)
            out_ref[0, rows, sl] = (_rms(hs[:, sl]) * ng[:, sl] * og[:, sl]).astype(BF16)
        return carry

    lax.fori_loop(0, T_ALL // TILE, finish, 0)


def _mlstm_call(mq, mk, mv, gc, gr, og, ng):
    B = mq.shape[0]
    full = lambda a: pl.BlockSpec((1,) + a.shape[1:], lambda b: (b,) + (0,) * (a.ndim - 1))
    return pl.pallas_call(
        _mlstm_kernel,
        grid=(B,),
        in_specs=[full(mq), full(mk), full(mv), full(gc), full(gr), full(og),
                  pl.BlockSpec(ng.shape, lambda b: (0, 0))],
        out_specs=pl.BlockSpec((1, T_ALL, H_B * DV_B), lambda b: (b, 0, 0)),
        out_shape=jax.ShapeDtypeStruct((B, T_ALL, H_B * DV_B), BF16),
        scratch_shapes=[
            pltpu.VMEM((T_ALL, H_B * DV_B), F32),
            pltpu.VMEM((T_ALL, H_B * DV_B), F32),
            pltpu.VMEM((2 * H_B * DK_B, 2 * DV_B), F32),
        ] + 2 * [
            pltpu.VMEM((2 * H_B * MLSTM_CHUNK, 2 * DV_B), F32),
            pltpu.VMEM((2 * H_B * DK_B, 2 * DV_B), F32),
            pltpu.VMEM((2 * H_B * 2 * MLSTM_CHUNK, 128), F32),
            pltpu.VMEM((2 * H_B * 2 * SUBLANES, 128), F32),
        ],
        compiler_params=_cparams(("arbitrary",)),
        name="mlstm",
    )(mq, mk, mv, gc, gr, og, ng)


def _na_bias_tables(rpb):
    kh = WIN_H
    n_drow = 2 * WIN_H - 1
    qcol = np.arange(GRID_W)
    col_start = np.clip(qcol - WIN_W // 2, 0, GRID_W - WIN_W)
    col_ok = (qcol[None, :] >= col_start[:, None]) & (qcol[None, :] < col_start[:, None] + WIN_W)
    dcol = qcol[None, :] - qcol[:, None] + (WIN_W - 1)
    onehot = (dcol[None] == np.arange(2 * WIN_W - 1)[:, None, None]) & col_ok[None]
    blocks = jnp.einsum('hdx,xck->hdck', rpb.astype(F32), jnp.asarray(onehot, F32),
                        precision=lax.Precision.HIGHEST)
    blocks = jnp.where(col_ok[None, None], blocks, NEG_BIG)
    outside = jnp.full((H_A, 1, GRID_W, GRID_W), NEG_BIG, F32)
    blocks = jnp.concatenate([blocks, outside], axis=1)
    idx = np.full((len(NA_CASES), NA_QROWS, NA_KROWS), n_drow, np.int32)
    for ci, r in enumerate(NA_CASES):
        u0 = int(np.clip(r - kh // 2, 0, GRID_ROWS - NA_KROWS))
        for qi in range(NA_QROWS):
            rq = r + qi
            r0 = int(np.clip(rq - kh // 2, 0, GRID_ROWS - kh))
            for ui in range(NA_KROWS):
                kr = u0 + ui
                if r0 <= kr < r0 + kh:
                    idx[ci, qi, ui] = kr - rq + (WIN_H - 1)

    def assemble(blk_ref, tab_ref):
        for ci in range(len(NA_CASES)):
            for qi in range(NA_QROWS):
                for ui in range(NA_KROWS):
                    tab_ref[ci, 0, qi * GRID_W:(qi + 1) * GRID_W, ui * GRID_W:(ui + 1) * GRID_W] = (
                        blk_ref[0, int(idx[ci, qi, ui])])

    return pl.pallas_call(
        assemble,
        grid=(H_A,),
        in_specs=[pl.BlockSpec((1, n_drow + 1, GRID_W, GRID_W), lambda h: (h, 0, 0, 0))],
        out_specs=pl.BlockSpec((len(NA_CASES), 1, NA_TQ, NA_TK), lambda h: (0, h, 0, 0)),
        out_shape=jax.ShapeDtypeStruct((len(NA_CASES), H_A, NA_TQ, NA_TK), F32),
        compiler_params=_cparams(("arbitrary",)),
        name="nbr_bias_table",
    )(blocks)


def _na_case(j):
    case = jnp.int32(NA_STEP_CASE[0])
    for step, c in enumerate(NA_STEP_CASE):
        if c != NA_STEP_CASE[0]:
            case = jnp.where(j - 1 == step, c, case)
    return case


def _na_kernel(q_ref, k_ref, v_ref, bias_ref, o_ref):
    j = pl.program_id(1)
    lane = lax.broadcasted_iota(I32, (1, 2 * HD_A), 1)
    lo_half = lane < HD_A

    def attend(q_rows, n_q, key_sets, bias_for_head):
        for pp0 in range(0, H_A // 2, NA_STAGE_PAIRS):
            heads = []
            for pp in range(pp0, pp0 + NA_STAGE_PAIRS):
                lanes = slice(pp * 2 * HD_A, (pp + 1) * 2 * HD_A)
                qp = q_ref[0, q_rows, lanes]
                ks = [k_ref[0, rs, lanes] for rs in key_sets]
                vs = [v_ref[0, rs, lanes] for rs in key_sets]
                for hh in range(2):
                    qm = jnp.where(lo_half if hh == 0 else ~lo_half, qp, jnp.zeros_like(qp))
                    heads.append(dict(head=2 * pp + hh, vs=vs, ss=[_dot_nt(qm, kk) for kk in ks]))
            for hd in heads:
                ss = hd['ss']
                if bias_for_head is not None:
                    ss[0] = ss[0] + bias_for_head(hd['head'])
                m = ss[0].max(axis=-1, keepdims=True)
                for s in ss[1:]:
                    m = jnp.maximum(m, s.max(axis=-1, keepdims=True))
                ps = [jnp.exp(s - m) for s in ss]
                l = ps[0].sum(axis=-1, keepdims=True)
                for p in ps[1:]:
                    l = l + p.sum(axis=-1, keepdims=True)
                hd['ps'] = [p.astype(BF16) for p in ps]
                hd['l'] = l
            for hd in heads:
                acc = _dot(hd['ps'][0], hd['vs'][0])
                for p, vv in zip(hd['ps'][1:], hd['vs'][1:]):
                    acc = acc + _dot(p, vv)
                hd['o'] = acc / hd['l']
            for i, pp in enumerate(range(pp0, pp0 + NA_STAGE_PAIRS)):
                lanes = slice(pp * 2 * HD_A, (pp + 1) * 2 * HD_A)
                o_ref[0, q_rows, lanes] = jnp.where(lo_half, heads[2 * i]['o'], heads[2 * i + 1]['o']).astype(BF16)

    ctx_rows = pl.ds(0, CTX_LEN)

    @pl.when(j == 0)
    def _():
        attend(ctx_rows, CTX_LEN, [ctx_rows], None)

    @pl.when(j > 0)
    def _():
        r = (j - 1) * NA_QROWS
        u0 = jnp.clip(r - WIN_H // 2, 0, GRID_ROWS - NA_KROWS)
        q_rows = pl.ds(pl.multiple_of(CTX_LEN + r * GRID_W, NA_TQ), NA_TQ)
        k_rows = pl.ds(pl.multiple_of(CTX_LEN + u0 * GRID_W, GRID_W), NA_TK)
        attend(q_rows, NA_TQ, [k_rows, ctx_rows], lambda head: bias_ref[0, head])


def _na_call(aq, ak, av, bias):
    B = aq.shape[0]
    full = pl.BlockSpec((1, T_ALL, D_A), lambda b, j: (b, 0, 0))
    return pl.pallas_call(
        _na_kernel,
        grid=(B, 1 + GRID_ROWS // NA_QROWS),
        in_specs=[full, full, full,
                  pl.BlockSpec((1, H_A, NA_TQ, NA_TK), lambda b, j: (_na_case(j), 0, 0, 0))],
        out_specs=full,
        out_shape=jax.ShapeDtypeStruct((B, T_ALL, D_A), BF16),
        compiler_params=_cparams(("arbitrary", "arbitrary")),
        name="nbr_attn",
    )(aq, ak, av, bias)


def _route(logits_t, rb_col):
    sc = _sigmoid(logits_t)
    sel = sc + rb_col
    selr = [sel[e:e + 1, :] for e in range(N_EXPERTS)]
    scr = [sc[e:e + 1, :] for e in range(N_EXPERTS)]
    gscore = []
    for g in range(N_GROUPS):
        a, b, c, d = selr[EXP_PER_GROUP * g:EXP_PER_GROUP * (g + 1)]
        s1, t1 = jnp.maximum(a, b), jnp.minimum(a, b)
        s2, t2 = jnp.maximum(c, d), jnp.minimum(c, d)
        gscore.append(jnp.maximum(s1, s2) + jnp.maximum(jnp.minimum(s1, s2), jnp.maximum(t1, t2)))
    best = gscore[0]
    gi = jnp.zeros(best.shape, I32)
    for g in range(1, N_GROUPS):
        better = gscore[g] > best
        gi = jnp.where(better, g, gi)
        best = jnp.where(better, gscore[g], best)
    vs, ws = [], []
    for k in range(EXP_PER_GROUP):
        v = selr[k]
        w = scr[k]
        for g in range(1, N_GROUPS):
            v = jnp.where(gi == g, selr[EXP_PER_GROUP * g + k], v)
            w = jnp.where(gi == g, scr[EXP_PER_GROUP * g + k], w)
        vs.append(v)
        ws.append(w)
    b1, i1 = vs[0], jnp.zeros(best.shape, I32)
    for k in range(1, EXP_PER_GROUP):
        better = vs[k] > b1
        i1 = jnp.where(better, k, i1)
        b1 = jnp.where(better, vs[k], b1)
    b2 = jnp.full(best.shape, -jnp.inf, F32)
    i2 = jnp.zeros(best.shape, I32)
    for k in range(EXP_PER_GROUP):
        vk = jnp.where(i1 == k, -jnp.inf, vs[k])
        better = vk > b2
        i2 = jnp.where(better, k, i2)
        b2 = jnp.where(better, vk, b2)
    w1 = ws[0]
    w2 = ws[0]
    for k in range(1, EXP_PER_GROUP):
        w1 = jnp.where(i1 == k, ws[k], w1)
        w2 = jnp.where(i2 == k, ws[k], w2)
    tot = w1 + w2
    return gi * EXP_PER_GROUP + i1, gi * EXP_PER_GROUP + i2, w1 / tot, w2 / tot


def _outproj_kernel(n_act, has_ctx, *refs):
    acts = refs[:n_act]
    ws = refs[n_act:2 * n_act]
    refs = refs[2 * n_act:]
    if has_ctx:
        x_in = jnp.where(pl.program_id(1) < NT_CTX, refs[1][0], refs[0][0])
        refs = refs[1:]
    else:
        x_in = refs[0][0]
    _, mod_ref, g_ref, rwh_ref, rwl_ref, xo_ref, hp_ref, lg_ref = refs
    mod = mod_ref[0]
    groups = [slice(r0, r0 + OUTPROJ_ROWS) for r0 in range(0, TILE, OUTPROJ_ROWS)]
    os = []
    for rows in groups:
        o = _dot(acts[0][0, rows, :], ws[0][...])
        for a, w in zip(acts[1:], ws[1:]):
            o = o + _dot(a[0, rows, :], w[...])
        os.append(o)
    hs = []
    for rows, o in zip(groups, os):
        x = x_in[rows, :] + mod[2:3] * o
        xo_ref[0, rows, :] = x
        h = _rms(x) * g_ref[...] * (1.0 + mod[4:5]) + mod[3:4]
        _store_token_tiles(hp_ref, (0,), rows.start, h)
        hs.append(_split_bf16(h, 2))
    for rows, (h_hi, h_lo) in zip(groups, hs):
        lg_ref[0, :, rows] = (_dot_nt(rwh_ref[...], h_hi) + _dot_nt(rwh_ref[...], h_lo)
                              + _dot_nt(rwl_ref[...], h_hi))


def _outproj_call(acts, ws, x_src, x_off, modl, norm_g, rwh, rwl, n_tiles, ctx_src=None):
    B = x_src.shape[0]
    n_act = len(acts)
    T = n_tiles * TILE
    const2 = lambda b, j: (0, 0)
    in_specs = []
    for a, a_off in acts:
        in_specs.append(pl.BlockSpec((1, TILE, a.shape[2]), functools.partial(lambda b, j, o: (b, j + o, 0), o=a_off)))
    for w in ws:
        in_specs.append(pl.BlockSpec(w.shape, const2))
    streams = [x_src]
    if ctx_src is None:
        in_specs.append(pl.BlockSpec((1, TILE, D_MODEL), lambda b, j: (b, j + x_off, 0)))
    else:
        assert x_off == 0
        in_specs += _combined_stream_specs()
        streams.append(ctx_src)
    in_specs += [
        pl.BlockSpec((1, 6, D_MODEL), lambda b, j: _mod_index(b, j + x_off)),
        pl.BlockSpec((1, D_MODEL), const2),
        pl.BlockSpec(rwh.shape, const2),
        pl.BlockSpec(rwl.shape, const2),
    ]
    tok = lambda b, j: (b, j, 0)
    out_shape = [
        jax.ShapeDtypeStruct((B, T, D_MODEL), F32),
        jax.ShapeDtypeStruct((B, T * SUBLANES, 128), F32),
        jax.ShapeDtypeStruct((B, N_EXPERTS, T), F32),
    ]
    out_specs = [
        pl.BlockSpec((1, TILE, D_MODEL), tok),
        pl.BlockSpec((1, TILE * SUBLANES, 128), tok),
        pl.BlockSpec((1, N_EXPERTS, TILE), lambda b, j: (b, 0, j)),
    ]
    return pl.pallas_call(
        functools.partial(_outproj_kernel, n_act, ctx_src is not None),
        grid=(B, n_tiles),
        in_specs=in_specs,
        out_specs=out_specs,
        out_shape=out_shape,
        compiler_params=_cparams(("arbitrary", "arbitrary")),
        name="outproj",
    )(*[a for a, _ in acts], *ws, *streams, modl, norm_g, rwh, rwl)


def _route_kernel(lg_ref, rb_ref, su_ref, ri_ref, rw_ref, cnt_ref):
    logits_t = lg_ref[0]
    T = logits_t.shape[1]
    e1, e2, w1, w2 = _route(logits_t, rb_ref[...])
    eidx = lax.broadcasted_iota(I32, logits_t.shape, 0)
    oh1 = eidx == e1
    oh2 = eidx == e2
    onehot = jnp.where(oh1, 1.0, jnp.where(oh2, 1.0, 0.0))
    count = jnp.zeros((N_EXPERTS, 1), F32)
    ranks = []
    for jj in range(T // TILE):
        oh = onehot[:, jj * TILE:(jj + 1) * TILE]
        ranks.append(_dot(oh.astype(BF16), su_ref[...]) + count)
        count = count + jnp.sum(oh, axis=1, keepdims=True)
    cpad = jnp.floor((count + (SUBLANES - 1.0)) * (1.0 / SUBLANES)) * SUBLANES
    ecol = lax.broadcasted_iota(I32, (N_EXPERTS, 1), 0)
    start = jnp.zeros((N_EXPERTS, 1), F32)
    for e in range(N_EXPERTS - 1):
        start = start + jnp.where(ecol > e, cpad[e:e + 1, :], 0.0)
    row = jnp.concatenate(ranks, axis=1) + start
    r1 = jnp.sum(jnp.where(oh1, row, 0.0), axis=0, keepdims=True)
    r2 = jnp.sum(jnp.where(oh2, row, 0.0), axis=0, keepdims=True)
    zi = jnp.zeros((SUBLANES - 4, T), I32)
    ri_ref[0] = jnp.concatenate([e1, e2, r1.astype(I32), r2.astype(I32), zi], axis=0)
    zf = jnp.zeros((SUBLANES - 2, T), F32)
    rw_ref[0] = jnp.concatenate([w1, w2, zf], axis=0)
    cnt_ref[0] = jnp.broadcast_to(count, (N_EXPERTS, 128)).astype(I32)


def _route_call(lg, rb, su):
    B, _, T = lg.shape
    per_sample = lambda b: (b, 0, 0)
    return pl.pallas_call(
        _route_kernel,
        grid=(B,),
        in_specs=[pl.BlockSpec((1, N_EXPERTS, T), per_sample),
                  pl.BlockSpec(rb.shape, lambda b: (0, 0)),
                  pl.BlockSpec(su.shape, lambda b: (0, 0))],
        out_specs=[pl.BlockSpec((1, SUBLANES, T), per_sample),
                   pl.BlockSpec((1, SUBLANES, T), per_sample),
                   pl.BlockSpec((1, N_EXPERTS, 128), per_sample)],
        out_shape=[jax.ShapeDtypeStruct((B, SUBLANES, T), I32),
                   jax.ShapeDtypeStruct((B, SUBLANES, T), F32),
                   jax.ShapeDtypeStruct((B, N_EXPERTS, 128), I32)],
        compiler_params=_cparams(("arbitrary",)),
        name="route",
    )(lg, rb, su)


def _moe_kernel(T, n_rows, cnt_ref, ri_ref, rw_ref, h_ref, w1_ref, w3_ref, w2_ref, y_ref,
                xb_ref, ob_ref, tokl_ref, off_ref):
    b = pl.program_id(0)
    e = pl.program_id(1)

    def tile_rows(row):
        return pl.ds(pl.multiple_of(row * SUBLANES, SUBLANES), SUBLANES)

    @pl.when(e == 0)
    def _():
        off_ref[0] = 0
        for i in range(N_EXPERTS):
            c = cnt_ref[b, i]
            off_ref[i + 1] = off_ref[i] + ((c + SUBLANES - 1) // SUBLANES) * SUBLANES

        for i in range(N_EXPERTS):
            for k in range(SUBLANES):
                tokl_ref[jnp.maximum(off_ref[i + 1] - SUBLANES + k, 0)] = 0

        def clear(i, carry):
            tokl_ref[off_ref[N_EXPERTS] + i] = 0
            return carry

        lax.fori_loop(0, MOE_RB, clear, 0, unroll=8)

        def place(t, carry):
            tokl_ref[ri_ref[2 * T + t]] = t
            tokl_ref[ri_ref[3 * T + t]] = t
            return carry

        lax.fori_loop(0, T, place, 0, unroll=8)

        def gather(i, carry):
            xb_ref[tile_rows(i), :] = h_ref[0, tile_rows(tokl_ref[i]), :]
            return carry

        lax.fori_loop(0, MOE_RB, gather, 0, unroll=8)

    c = cnt_ref[b, e]
    base = off_ref[e]
    next_base = off_ref[e + 1]

    def block(p0, n_blk_rows, p_next):
        xb = _load_token_tiles(xb_ref, (), 0, n_blk_rows).astype(BF16)
        for i in range(MOE_RB):
            xb_ref[i * SUBLANES:(i + 1) * SUBLANES, :] = h_ref[0, tile_rows(tokl_ref[p_next + i]), :]
        h1 = _dot(xb, w1_ref[0, 0])
        h3 = _dot(xb, w3_ref[0, 0])
        act = (h1 * _sigmoid(h1)) * h3
        _store_token_tiles(ob_ref, (), p0, _dot(act.astype(BF16), w2_ref[0, 0]))

    n_big = (c + MOE_RB - MOE_RB_TAIL - 1) // MOE_RB
    n_big = jnp.maximum(n_big, 0)
    has_tail = c > n_big * MOE_RB

    def big_block(rb, carry):
        p0 = pl.multiple_of(base + rb * MOE_RB, SUBLANES)
        last = jnp.logical_and(rb == n_big - 1, jnp.logical_not(has_tail))
        block(p0, MOE_RB, jnp.where(last, next_base, p0 + MOE_RB))
        return carry

    lax.fori_loop(0, n_big, big_block, 0)

    @pl.when(has_tail)
    def _():
        block(pl.multiple_of(base + n_big * MOE_RB, SUBLANES), MOE_RB_TAIL, next_base)

    @pl.when(e == N_EXPERTS - 1)
    def _():
        def combine(t, carry):
            y_ref[0, tile_rows(t), :] = (rw_ref[t] * ob_ref[tile_rows(ri_ref[2 * T + t]), :]
                                         + rw_ref[T + t] * ob_ref[tile_rows(ri_ref[3 * T + t]), :])
            return carry

        lax.fori_loop(0, T, combine, 0, unroll=8)


def _moe_call(cnt, ri, rw, h, w1, w3, w2, layer):
    B = h.shape[0]
    T = h.shape[1] // SUBLANES
    n_rows = 2 * T + N_EXPERTS * SUBLANES + MOE_RB
    smem = functools.partial(pl.BlockSpec, memory_space=pltpu.SMEM)
    once = pl.Buffered(1)
    return pl.pallas_call(
        functools.partial(_moe_kernel, T, n_rows),
        grid=(B, N_EXPERTS),
        in_specs=[
            smem(cnt.shape, lambda b, e: (0, 0)),
            smem((4 * T,), lambda b, e: (b,)),
            smem((4 * T,), lambda b, e: (b,)),
            pl.BlockSpec((1, T * SUBLANES, 128), lambda b, e: (b, 0, 0), pipeline_mode=once),
            pl.BlockSpec((1, 1, D_MODEL, D_FF), lambda b, e: (layer, e, 0, 0)),
            pl.BlockSpec((1, 1, D_MODEL, D_FF), lambda b, e: (layer, e, 0, 0)),
            pl.BlockSpec((1, 1, D_FF, D_MODEL), lambda b, e: (layer, e, 0, 0)),
        ],
        out_specs=pl.BlockSpec((1, T * SUBLANES, 128), lambda b, e: (b, 0, 0), pipeline_mode=once),
        out_shape=jax.ShapeDtypeStruct((B, T * SUBLANES, 128), F32),
        scratch_shapes=[
            pltpu.VMEM((MOE_RB * SUBLANES, 128), F32),
            pltpu.VMEM((n_rows * SUBLANES, 128), F32),
            pltpu.SMEM((n_rows,), I32),
            pltpu.SMEM((N_EXPERTS + 1,), I32),
        ],
        compiler_params=_cparams(("arbitrary", "arbitrary")),
        name="moe",
    )(cnt, ri, rw, h, w1, w3, w2)


def _route_tables(ri, rw, cnt):
    return cnt[:, :, 0], ri[:, :4].reshape(-1), rw[:, :4].reshape(-1)


def _inproj1_kernel(x_ref, y_ref, mod0_ref, mod_ref, g_ref, w_ref, qg_ref, kg_ref, cos_ref, sin_ref,
                    xo_ref, q_ref, k_ref, v_ref):
    x = x_ref[0] + mod0_ref[0][5:6] * _load_token_tiles(y_ref, (0,), 0, TILE)
    xo_ref[0] = x
    mod = mod_ref[0]
    hb = (_rms(x) * g_ref[...] * (1.0 + mod[1:2]) + mod[0:1]).astype(BF16)
    cos = cos_ref[...]
    sin = sin_ref[...]

    def rope_head(a, gain):
        n = _rms(a) * gain
        return n * cos + pltpu.roll(n, HD_C // 2, 1) * sin

    qg = qg_ref[...]
    kg = kg_ref[...]
    ko = H_C * HD_C
    vo = (H_C + KV_C) * HD_C
    kv = _dot(hb, w_ref[:, ko:vo + KV_C * HD_C])
    for h in range(KV_C):
        sl = slice(h * HD_C, (h + 1) * HD_C)
        k_ref[0, :, sl] = rope_head(kv[:, sl], kg).astype(BF16)
    v_ref[0] = kv[:, KV_C * HD_C:].astype(BF16)

    @pl.when(pl.program_id(1) >= NT_CTX)
    def _():
        qs = _dot(hb, w_ref[:, :ko])
        for h in range(H_C):
            sl = slice(h * HD_C, (h + 1) * HD_C)
            q_ref[0, :, sl] = (rope_head(qs[:, sl], qg) * HD_C ** -0.5).astype(BF16)


def _inproj1_call(x1, y0, mod0, mod1, norm_g, w, qg, kg, cos, sin):
    B = x1.shape[0]
    tok = lambda b, j: (b, j, 0)
    const2 = lambda b, j: (0, 0)
    lat_tok = lambda b, j: (b, jnp.maximum(j - NT_CTX, 0), 0)
    widths = [(T_ALL, D_MODEL, F32), (SEQ, H_C * HD_C, BF16),
              (T_ALL, KV_C * HD_C, BF16), (T_ALL, KV_C * HD_C, BF16)]
    return pl.pallas_call(
        _inproj1_kernel,
        grid=(B, NT_ALL),
        in_specs=[
            pl.BlockSpec((1, TILE, D_MODEL), tok),
            pl.BlockSpec((1, TILE * SUBLANES, 128), tok),
            pl.BlockSpec((1, 6, D_MODEL), _mod_index),
            pl.BlockSpec((1, 6, D_MODEL), _mod_index),
            pl.BlockSpec((1, D_MODEL), const2),
            pl.BlockSpec(w.shape, const2),
            pl.BlockSpec(qg.shape, const2),
            pl.BlockSpec(kg.shape, const2),
            pl.BlockSpec((TILE, HD_C), lambda b, j: (j, 0)),
            pl.BlockSpec((TILE, HD_C), lambda b, j: (j, 0)),
        ],
        out_specs=[pl.BlockSpec((1, TILE, w_), lat_tok if t == SEQ else tok) for t, w_, _ in widths],
        out_shape=[jax.ShapeDtypeStruct((B, t, w_), dt) for t, w_, dt in widths],
        compiler_params=_cparams(("arbitrary", "arbitrary")),
        name="inproj1",
    )(x1, y0, mod0, mod1, norm_g, w, qg, kg, cos, sin)


def _gqa_kernel(q_ref, k_ref, v_ref, o_ref):
    k = k_ref[0]
    v = v_ref[0]
    for h0 in range(0, H_C // KV_C, GQA_STAGE_HEADS):
        sls = [slice(h * HD_C, (h + 1) * HD_C) for h in range(h0, h0 + GQA_STAGE_HEADS)]
        ss = [_dot_nt(q_ref[0, :, sl], k) for sl in sls]
        ps, ls = [], []
        for s in ss:
            p = jnp.exp(s - s.max(axis=-1, keepdims=True))
            ls.append(p.sum(axis=-1, keepdims=True))
            ps.append(p.astype(BF16))
        for sl, p, l in zip(sls, ps, ls):
            o_ref[0, :, sl] = (_dot(p, v) / l).astype(BF16)


def _gqa_call(q, k, v):
    B = q.shape[0]
    gw = (H_C // KV_C) * HD_C
    return pl.pallas_call(
        _gqa_kernel,
        grid=(B, KV_C, SEQ // GQA_TQ),
        in_specs=[
            pl.BlockSpec((1, GQA_TQ, gw), lambda b, g, j: (b, j, g)),
            pl.BlockSpec((1, T_ALL, HD_C), lambda b, g, j: (b, 0, g)),
            pl.BlockSpec((1, T_ALL, HD_C), lambda b, g, j: (b, 0, g)),
        ],
        out_specs=pl.BlockSpec((1, GQA_TQ, gw), lambda b, g, j: (b, j, g)),
        out_shape=jax.ShapeDtypeStruct((B, SEQ, H_C * HD_C), BF16),
        compiler_params=_cparams(("arbitrary", "arbitrary", "arbitrary")),
        name="gqa",
    )(q, k, v)


def _final_kernel(x_ref, y_ref, mod_ref, o_ref):
    o_ref[0] = x_ref[0] + mod_ref[0][5:6] * _load_token_tiles(y_ref, (0,), 0, FINAL_TILE)


def _final_call(x, y, modl):
    B, T, D = x.shape
    tok = lambda b, j: (b, j, 0)
    return pl.pallas_call(
        _final_kernel,
        grid=(B, T // FINAL_TILE),
        in_specs=[pl.BlockSpec((1, FINAL_TILE, D), tok), pl.BlockSpec((1, FINAL_TILE * SUBLANES, 128), tok),
                  pl.BlockSpec((1, 6, D), lambda b, j: (b, 0, 0))],
        out_specs=pl.BlockSpec((1, FINAL_TILE, D), tok),
        out_shape=jax.ShapeDtypeStruct((B, T, D), F32),
        compiler_params=_cparams(("arbitrary", "arbitrary")),
        name="final_residual",
    )(x, y, modl)


def _chunk_tri(lower):
    i = np.arange(TILE)
    same = (i[:, None] // MLSTM_CHUNK) == (i[None, :] // MLSTM_CHUNK)
    tri = (i[None, :] <= i[:, None]) if lower else (i[None, :] >= i[:, None])
    return jnp.asarray((same & tri).astype(np.float32), BF16)


def _rope_tables():
    n_freq = HD_C // 4
    inv_freq = ROPE_THETA ** (-jnp.arange(n_freq, dtype=F32) / n_freq)
    t = jnp.arange(SEQ)
    rows = (t // GRID_W).astype(F32)
    cols = (t % GRID_W).astype(F32)
    ang = jnp.concatenate([rows[:, None] * inv_freq, cols[:, None] * inv_freq], axis=-1)
    cos, sin = jnp.cos(ang), jnp.sin(ang)
    cos_l = jnp.concatenate([cos, cos], axis=-1)
    sin_l = jnp.concatenate([-sin, sin], axis=-1)
    cos_all = jnp.concatenate([jnp.ones((CTX_LEN, HD_C), F32), cos_l], axis=0)
    sin_all = jnp.concatenate([jnp.zeros((CTX_LEN, HD_C), F32), sin_l], axis=0)
    return cos_all, sin_all


_HEAD_PERM = np.concatenate([np.arange(0, HD_C, 2), np.arange(1, HD_C, 2)])


def kernel(x, c, ctx, c_ctx, ada_w, ada_b, norm_mix_g, norm_ffn_g, even_w_in, even_w_out,
           na_q_norm_g, na_k_norm_g, na_rpb, mlstm_gate_b, mlstm_norm_g, odd_w_in, odd_w_out,
           gqa_q_norm_g, gqa_k_norm_g, router_w, router_b, exp_w1, exp_w3, exp_w2):
    B = x.shape[0]
    assert B <= N_MOD_CTX_ROW and x.shape[1:] == (SEQ, D_MODEL) and ctx.shape[1:] == (CTX_LEN, D_MODEL)
    n_g = 4 * H_B

    cvec = jnp.zeros((N_MOD_ROWS, D_MODEL), F32).at[:B].set(c).at[N_MOD_CTX_ROW].set(c_ctx)
    mod = _ada_call(cvec, ada_w, ada_b).reshape(2, N_MOD_ROWS, 6, D_MODEL)
    mod0, mod1 = mod[0], mod[1]

    w_in = even_w_in[0]
    n_main = w_in.shape[1] - n_g
    wm = w_in[:, :n_main].astype(BF16)
    wg_f = w_in[:, n_main:]
    wg = jnp.pad(wg_f, ((0, 0), (0, 128 - n_g))).astype(BF16)
    wgt = wg_f.T.astype(BF16)
    gb = jnp.pad(mlstm_gate_b[0].reshape(1, n_g), ((0, 0), (0, 128 - n_g)))
    gbt = mlstm_gate_b[0].reshape(n_g, 1)
    qg = jnp.tile(na_q_norm_g[0], H_A).reshape(1, D_A)
    kg = jnp.tile(na_k_norm_g[0], H_A).reshape(1, D_A)
    hid = np.arange(128) // HD_A
    bd = jnp.asarray((hid[:, None] == hid[None, :]).astype(np.float32), BF16)
    k_lo = 3 * D_A + H_B * DK_B
    wkt = w_in[:, k_lo:k_lo + H_B * DK_B].T.astype(BF16)
    aq, ak, av, mq, mv, og, mkt, gc, gr = _inproj0_call(
        x, ctx, mod0, norm_mix_g[0].reshape(1, D_MODEL), wm, wkt, wg, wgt, gb, gbt, qg, kg, bd,
        _chunk_tri(True), _chunk_tri(False))
    hm = _mlstm_call(mq, mkt, mv, gc, gr, og, mlstm_norm_g[0].reshape(1, H_B * DV_B))
    oa = _na_call(aq, ak, av, _na_bias_tables(na_rpb[0]))

    rw_t = router_w.T
    rwh = rw_t.astype(BF16)
    rwl = (rw_t - rwh.astype(F32)).astype(BF16)
    rb = router_b.reshape(N_EXPERTS, 1).astype(F32)
    i = np.arange(TILE)
    su = jnp.asarray((i[:, None] < i[None, :]).astype(np.float32), BF16)
    w_out = even_w_out[0].astype(BF16)
    ew1, ew3, ew2 = exp_w1.astype(BF16), exp_w3.astype(BF16), exp_w2.astype(BF16)
    x1, hp0, lg0 = _outproj_call(
        [(oa, 0), (hm, 0)], [w_out[:D_A], w_out[D_A:]], x, 0, mod0,
        norm_ffn_g[0].reshape(1, D_MODEL), rwh, rwl, NT_ALL, ctx_src=ctx)
    y0 = _moe_call(*_route_tables(*_route_call(lg0, rb, su)), hp0, ew1, ew3, ew2, 0)

    w1_in = odd_w_in[0]
    qk_cols = np.concatenate([h * HD_C + _HEAD_PERM for h in range(H_C + KV_C)])
    cols = np.concatenate([qk_cols, np.arange((H_C + KV_C) * HD_C, w1_in.shape[1])])
    w1_in = w1_in[:, cols].astype(BF16)
    cos_all, sin_all = _rope_tables()
    x2, q, k, v = _inproj1_call(
        x1, y0, mod0, mod1, norm_mix_g[1].reshape(1, D_MODEL), w1_in,
        gqa_q_norm_g[0][_HEAD_PERM].reshape(1, HD_C), gqa_k_norm_g[0][_HEAD_PERM].reshape(1, HD_C),
        cos_all, sin_all)
    o = _gqa_call(q, k, v)
    x3, hp1, lg1 = _outproj_call(
        [(o, 0)], [odd_w_out[0].astype(BF16)], x2, NT_CTX, mod1,
        norm_ffn_g[1].reshape(1, D_MODEL), rwh, rwl, SEQ // TILE)
    y1 = _moe_call(*_route_tables(*_route_call(lg1, rb, su)), hp1, ew1, ew3, ew2, 1)
    return _final_call(x3, y1, mod1)
```

```python
import functools
import math

import numpy as np
import jax
import jax.numpy as jnp
from jax import lax
from jax.experimental import pallas as pl
from jax.experimental.pallas import tpu as pltpu

F32 = jnp.float32
BF16 = jnp.bfloat16
I32 = jnp.int32
U32 = jnp.uint32

D_MODEL = 1024
SEQ = 2048
GRID_W = 64
GRID_ROWS = SEQ // GRID_W
CTX_LEN = 256
T_ALL = CTX_LEN + SEQ
WIN_H = 8
WIN_W = 16
HD_A = 64
H_A = 8
D_A = H_A * HD_A
H_B = 4
DV_B = 128
DK_B = 64
MLSTM_CHUNK = 64
N_CHUNKS = T_ALL // MLSTM_CHUNK
N_CTX_CHUNKS = CTX_LEN // MLSTM_CHUNK
HD_C = 128
H_C = 8
KV_C = 2
ROPE_THETA = 10000.0
N_EXPERTS = 16
N_GROUPS = 4
EXP_PER_GROUP = 4
D_FF = 512
EPS = 1e-6
NEG_BIG = -1e30

TILE = 256
NT_ALL = T_ALL // TILE
NT_CTX = CTX_LEN // TILE
NA_QROWS = 4
NA_KROWS = WIN_H + NA_QROWS - 1
NA_TQ = NA_QROWS * GRID_W
NA_TK = NA_KROWS * GRID_W
NA_STAGE_PAIRS = 2


def _na_geometry(r):
    u0 = min(max(r - WIN_H // 2, 0), GRID_ROWS - NA_KROWS)
    r0s = tuple(min(max(r + qi - WIN_H // 2, 0), GRID_ROWS - WIN_H) - r for qi in range(NA_QROWS))
    return (u0 - r, r0s)


def _na_cases():
    reps, step_case = [], []
    for r in range(0, GRID_ROWS, NA_QROWS):
        geo = _na_geometry(r)
        known = [_na_geometry(q) for q in reps]
        if geo not in known:
            reps.append(r)
            known.append(geo)
        step_case.append(known.index(geo))
    return tuple(reps), tuple(step_case)


NA_CASES, NA_STEP_CASE = _na_cases()
PROJ_SAMPLES = 4
INPROJ0_SAMPLES = 2
INPROJ1_SAMPLES = 1
GQA_TQ = 512
GQA_STAGE_HEADS = 2
FINAL_TILE = 1024
MOE_RB = 256
MOE_RB_TAIL = 128
SUBLANES = 8
VMEM_LIMIT = 56 * 1024 * 1024


def _cparams(sem):
    return pltpu.CompilerParams(dimension_semantics=sem, vmem_limit_bytes=VMEM_LIMIT)


def _sigmoid(x):
    return 1.0 / (1.0 + jnp.exp(-x))


def _rms(x):
    return x * lax.rsqrt(jnp.mean(x * x, axis=-1, keepdims=True) + EPS)


def _dot(a, b):
    return jnp.dot(a, b, preferred_element_type=F32)


def _dot_nt(a, b):
    return lax.dot_general(a, b, (((1,), (1,)), ((), ())), preferred_element_type=F32)


def _dot_tn(a, b):
    return lax.dot_general(a, b, (((0,), (0,)), ((), ())), preferred_element_type=F32)


def _split_bf16(x, n):
    parts = []
    r = x
    for _ in range(n):
        p = r.astype(BF16)
        parts.append(p)
        r = r - p.astype(F32)
    return parts


LANE_TILES = D_MODEL // 128


def _load_token_tiles(ref, lead, tok0, n_tok):
    parts = [ref[(*lead, pl.ds(tok0 * SUBLANES + c, n_tok, stride=SUBLANES), slice(None))]
             for c in range(LANE_TILES)]
    return jnp.concatenate(parts, axis=1)


def _store_token_tiles(ref, lead, tok0, val):
    for c in range(LANE_TILES):
        ref[(*lead, pl.ds(tok0 * SUBLANES + c, val.shape[0], stride=SUBLANES), slice(None))] = (
            val[:, c * 128:(c + 1) * 128])


def _stream_specs(ns, x_src, x_off, ctx_src, ctx_tile, mod_tables):
    if ctx_src is None:
        specs = [pl.BlockSpec((ns, TILE, D_MODEL), lambda b, j: (b, j + x_off, 0))]
        arrays = [x_src]
    else:
        assert x_off == 0 and ctx_tile
        specs = [pl.BlockSpec((ns, TILE, D_MODEL), lambda b, j: (b, jnp.maximum(j - NT_CTX, 0), 0)),
                 pl.BlockSpec((ns, TILE, D_MODEL), lambda b, j: (b, jnp.minimum(j, NT_CTX - 1), 0))]
        arrays = [x_src, ctx_src]
    for table in mod_tables:
        specs.append(pl.BlockSpec((ns, 6, D_MODEL), lambda b, j: (b, 0, 0)))
        arrays.append(table)
        if ctx_tile:
            specs.append(pl.BlockSpec((1, 6, D_MODEL), lambda b, j: (N_MOD_CTX_ROW, 0, 0)))
            arrays.append(table)
    return specs, arrays


def _stream_tiles(two_streams, ctx_tile, refs, ns):
    is_ctx = pl.program_id(1) < NT_CTX
    if two_streams:
        x_ins = [jnp.where(is_ctx, refs[1][s], refs[0][s]) for s in range(ns)]
        refs = refs[2:]
    else:
        x_ins = [refs[0][s] for s in range(ns)]
        refs = refs[1:]
    tables = []
    if ctx_tile:
        for t in range(0, len(refs), 2):
            tables.append([jnp.where(is_ctx, refs[t + 1][0], refs[t][s]) for s in range(ns)])
    else:
        for ref in refs:
            tables.append([ref[s] for s in range(ns)])
    return x_ins, tables


N_MOD_ROWS = 16
N_MOD_CTX_ROW = 8


ADA_TN = 1536


def _ada_kernel(c_ref, w_ref, b_ref, o_ref):
    c = c_ref[...]
    s = (c * _sigmoid(c)).astype(BF16)
    o_ref[0] = _dot(s, w_ref[0].astype(BF16)) + b_ref[0]


def _ada_call(cvec, ada_w, ada_b):
    depth, d, n = ada_w.shape
    return pl.pallas_call(
        _ada_kernel,
        grid=(depth, n // ADA_TN),
        in_specs=[
            pl.BlockSpec((N_MOD_ROWS, d), lambda l, j: (0, 0)),
            pl.BlockSpec((1, d, ADA_TN), lambda l, j: (l, 0, j)),
            pl.BlockSpec((1, 1, ADA_TN), lambda l, j: (l, 0, j)),
        ],
        out_specs=pl.BlockSpec((1, N_MOD_ROWS, ADA_TN), lambda l, j: (l, 0, j)),
        out_shape=jax.ShapeDtypeStruct((depth, N_MOD_ROWS, n), F32),
        compiler_params=_cparams(("arbitrary", "arbitrary")),
        name="ada_mod",
    )(cvec, ada_w, ada_b.reshape(depth, 1, n))


def _log_sigmoid(x):
    return jnp.minimum(x, 0.0) - jnp.log1p(jnp.exp(-jnp.abs(x)))


def _inproj0_kernel(*refs):
    (g_ref, wm_ref, wkt_ref, wg_ref, wgt_ref, gb_ref, gbt_ref, qg_ref, kg_ref, bd_ref, tril_ref, triu_ref,
     aq_ref, ak_ref, av_ref, mq_ref, mv_ref, og_ref, mkt_ref, gc_ref, gr_ref) = refs[-21:]
    ns = aq_ref.shape[0]
    x_ins, (mods,) = _stream_tiles(True, True, refs[:-21], ns)
    hb = jnp.concatenate([(_rms(x_ins[s]) * g_ref[...] * (1.0 + mods[s][1:2]) + mods[s][0:1]).astype(BF16)
                          for s in range(ns)], axis=0)

    def per_sample(ref, val):
        for s in range(ns):
            ref[s] = val[s * TILE:(s + 1) * TILE]

    def proj(lo, hi):
        return _dot(hb, wm_ref[:, lo:hi])

    def head_norm(a, gain):
        a2 = a * a
        sums = []
        for s0 in range(0, D_A, 128):
            hi_, lo_ = _split_bf16(a2[:, s0:s0 + 128], 2)
            sums.append(_dot(hi_, bd_ref[...]) + _dot(lo_, bd_ref[...]))
        ss = jnp.concatenate(sums, axis=1)
        return a * lax.rsqrt(ss * (1.0 / HD_A) + EPS) * gain

    n_g = 4 * H_B
    p_aq, p_ak, p_av = proj(0, 512), proj(512, 1024), proj(1024, 1536)
    p_mq, p_mv, p_og = proj(1536, 1792), proj(2048, 2560), proj(2560, 3072)
    kt = _dot_nt(wkt_ref[...], hb).astype(BF16)
    gcol = _dot(hb, wg_ref[...]) + gb_ref[...]
    grow = _dot_nt(wgt_ref[...], hb) + gbt_ref[...]

    per_sample(aq_ref, (head_norm(p_aq, qg_ref[...]) * HD_A ** -0.5).astype(BF16))
    per_sample(ak_ref, head_norm(p_ak, kg_ref[...]).astype(BF16))
    per_sample(av_ref, p_av.astype(BF16))
    per_sample(mq_ref, (p_mq * DK_B ** -0.5).astype(BF16))
    per_sample(mv_ref, p_mv.astype(BF16))
    per_sample(og_ref, _sigmoid(p_og).astype(BF16))

    tril = tril_ref[...]
    triu = triu_ref[...]
    ls_c = _split_bf16(_log_sigmoid(gcol), 3)
    ls_r = _split_bf16(_log_sigmoid(grow), 3)
    cidx = lax.broadcasted_iota(I32, (TILE, 128), 1)
    ridx = lax.broadcasted_iota(I32, (n_g, TILE), 0)

    def pick(idx, raw, pre, suf):
        is_ff = (idx >= H_B) & (idx < 2 * H_B)
        is_fb = idx >= 3 * H_B
        return jnp.where(is_ff, pre, jnp.where(is_fb, suf, raw))

    for s in range(ns):
        tok = slice(s * TILE, (s + 1) * TILE)
        pre_c = sum(_dot(tril, p[tok, :]) for p in ls_c)
        suf_c = sum(_dot(triu, p[tok, :]) for p in ls_c)
        pre_r = sum(_dot(p[:, tok], triu) for p in ls_r)
        suf_r = sum(_dot(p[:, tok], tril) for p in ls_r)
        gc_ref[s] = pick(cidx, gcol[tok, :], pre_c, suf_c)[:, :n_g]
        grow_s = pick(ridx, grow[:, tok], pre_r, suf_r)
        for c in range(TILE // MLSTM_CHUNK):
            chunk = slice(c * MLSTM_CHUNK, (c + 1) * MLSTM_CHUNK)
            gr_ref[s, c] = grow_s[:, chunk]
            mkt_ref[s, c] = kt[:, s * TILE + c * MLSTM_CHUNK:s * TILE + (c + 1) * MLSTM_CHUNK]


def _inproj0_call(x, ctx, modl, norm_g, wm, wkt, wg, wgt, gb, gbt, qg, kg, bd, tril, triu):
    B = x.shape[0]
    ns = math.gcd(B, INPROJ0_SAMPLES)
    n_g = 4 * H_B
    tok = lambda b, j: (b, j, 0)
    const2 = lambda b, j: (0, 0)
    chunked = lambda b, j: (b, j, 0, 0)
    tile_chunks = TILE // MLSTM_CHUNK
    outs = [
        (D_A, BF16), (D_A, BF16), (D_A, BF16),
        (H_B * DK_B, BF16), (H_B * DV_B, BF16),
        (H_B * DV_B, BF16),
    ]
    out_shape = [jax.ShapeDtypeStruct((B, T_ALL, w), dt) for w, dt in outs]
    out_specs = [pl.BlockSpec((ns, TILE, w), tok) for w, _ in outs]
    out_shape += [jax.ShapeDtypeStruct((B, N_CHUNKS, H_B * DK_B, MLSTM_CHUNK), BF16),
                  jax.ShapeDtypeStruct((B, T_ALL, n_g), F32),
                  jax.ShapeDtypeStruct((B, N_CHUNKS, n_g, MLSTM_CHUNK), F32)]
    out_specs += [pl.BlockSpec((ns, tile_chunks, H_B * DK_B, MLSTM_CHUNK), chunked),
                  pl.BlockSpec((ns, TILE, n_g), tok),
                  pl.BlockSpec((ns, tile_chunks, n_g, MLSTM_CHUNK), chunked)]
    stream_specs, streams = _stream_specs(ns, x, 0, ctx, True, [modl])
    return pl.pallas_call(
        _inproj0_kernel,
        grid=(B // ns, NT_ALL),
        in_specs=stream_specs + [
            pl.BlockSpec((1, D_MODEL), const2),
            pl.BlockSpec(wm.shape, const2),
            pl.BlockSpec(wkt.shape, const2),
            pl.BlockSpec(wg.shape, const2),
            pl.BlockSpec(wgt.shape, const2),
            pl.BlockSpec(gb.shape, const2),
            pl.BlockSpec(gbt.shape, const2),
            pl.BlockSpec(qg.shape, const2),
            pl.BlockSpec(kg.shape, const2),
            pl.BlockSpec(bd.shape, const2),
            pl.BlockSpec(tril.shape, const2),
            pl.BlockSpec(triu.shape, const2),
        ],
        out_specs=out_specs,
        out_shape=out_shape,
        compiler_params=_cparams(("arbitrary", "arbitrary")),
        name="inproj0",
    )(*streams, norm_g, wm, wkt, wg, wgt, gb, gbt, qg, kg, bd, tril, triu)


def _mlstm_kernel(mq_ref, mkt_ref, mv_ref, gc_ref, gr_ref, og_ref, ng_ref, out_ref,
                  hf_ref, hb_ref, s_ref, *local_refs):
    L = MLSTM_CHUNK
    n_chain = 2 * H_B
    slots = (local_refs[:4], local_refs[4:])
    s_ref[...] = jnp.zeros(s_ref.shape, F32)
    ri = lax.broadcasted_iota(I32, (L, L), 0)
    ci = lax.broadcasted_iota(I32, (L, L), 1)
    masks = (ci <= ri, ci >= ri)
    ones_aug = jnp.ones((L, DV_B), BF16)

    def tile_up(x, n_rows, n_cols):
        x = jnp.concatenate([x] * n_rows, axis=0)
        return x if n_cols == 1 else jnp.concatenate([x] * n_cols, axis=1)

    def chunk_rows(step, d):
        if d == 0:
            chunk = step
        else:
            chunk = jnp.where(step < N_CTX_CHUNKS, N_CTX_CHUNKS - 1 - step, N_CHUNKS + N_CTX_CHUNKS - 1 - step)
        return chunk, pl.ds(pl.multiple_of(chunk * L, L), L)

    def local_part(step, slot, rec=None):
        intra_ref, u_ref, col_ref, rep_ref = slot
        rec = rec or (lambda: None,) * 3
        chains = []
        for d in range(2):
            chunk, rows = chunk_rows(step, d)
            gcc = gc_ref[0, rows, :]
            grr = gr_ref[0, chunk]
            for h in range(H_B):
                gi = 2 * H_B * d + h
                b_col = gcc[:, gi + H_B:gi + H_B + 1]
                chains.append(dict(
                    d=d,
                    q=mq_ref[0, rows, h * DK_B:(h + 1) * DK_B],
                    kt=mkt_ref[0, chunk, h * DK_B:(h + 1) * DK_B, :],
                    v=mv_ref[0, rows, h * DV_B:(h + 1) * DV_B],
                    ig_col=gcc[:, gi:gi + 1], b_col=b_col,
                    ig_row=grr[gi:gi + 1, :], b_row=grr[gi + H_B:gi + H_B + 1, :],
                    b_last=b_col[L - 1:L, :] if d == 0 else b_col[0:1, :]))
        for ch in chains:
            ch['qk'] = _dot(ch['q'], ch['kt'])
            ch['v_aug'] = jnp.concatenate([ch['v'], ones_aug], axis=1)
            g = ch['b_last'] - ch['b_col'] + ch['ig_col']
            ch['g_max'] = jnp.max(g, axis=0, keepdims=True)
            ch['wv'] = (jnp.exp(g - ch['g_max']) * ch['v_aug'].astype(F32)).astype(BF16)
        us = [_dot(ch['kt'], ch['wv']) for ch in chains]
        for ch in chains:
            dm = jnp.where(masks[ch['d']], ch['b_col'] - ch['b_row'] + ch['ig_row'], NEG_BIG)
            ch['m_loc'] = jnp.max(dm, axis=-1, keepdims=True)
            ch['p'] = (ch['qk'] * jnp.exp(dm - ch['m_loc'])).astype(BF16)
        intras = [_dot(ch['p'], ch['v_aug']) for ch in chains]
        rec[0]()
        cols, reps = [], []
        for ch in chains:
            cols += [jnp.broadcast_to(ch['b_col'], (L, 128)), jnp.broadcast_to(ch['m_loc'], (L, 128))]
            reps += [jnp.broadcast_to(ch['b_last'], (SUBLANES, 128)),
                     jnp.broadcast_to(ch['g_max'], (SUBLANES, 128))]
        intra_ref[...] = jnp.concatenate(intras, axis=0)
        u_ref[...] = jnp.concatenate(us, axis=0)
        col_ref[...] = jnp.concatenate(cols, axis=0)
        rep_ref[...] = jnp.concatenate(reps, axis=0)
        rec[1]()
        rec[2]()

    def recurrence_stages(step, ms, slot):
        intra_ref, u_ref, col_ref, rep_ref = slot
        n_all = range(n_chain)
        rows = [chunk_rows(step, d)[1] for d in range(2)]
        live = {}
        new_ms = []

        def products():
            live['states'] = [s_ref[c * DK_B:(c + 1) * DK_B, :] for c in n_all]
            live['inters'] = [_dot(mq_ref[0, rows[c // H_B], (c % H_B) * DK_B:(c % H_B + 1) * DK_B],
                                   live['states'][c].astype(BF16)) for c in n_all]

        def update():
            new_states = []
            for c in n_all:
                m = ms[c]
                b_last = rep_ref[2 * c * SUBLANES:(2 * c + 1) * SUBLANES, :]
                g_max = rep_ref[(2 * c + 1) * SUBLANES:(2 * c + 2) * SUBLANES, :]
                m_new = jnp.maximum(b_last + m, g_max)
                w_old = tile_up(jnp.exp(b_last + m - m_new), DK_B // SUBLANES, 2)
                w_new = tile_up(jnp.exp(g_max - m_new), DK_B // SUBLANES, 2)
                new_states.append(w_old * live['states'][c] + w_new * u_ref[c * DK_B:(c + 1) * DK_B, :])
                new_ms.append(m_new)
            s_ref[...] = jnp.concatenate(new_states, axis=0)

        def outputs():
            houts = []
            for c in n_all:
                a = col_ref[2 * c * L:(2 * c + 1) * L, :] + tile_up(ms[c], L // SUBLANES, 1)
                m_loc = col_ref[(2 * c + 1) * L:(2 * c + 2) * L, :]
                m_row = jnp.maximum(a, m_loc)
                w_inter = jnp.exp(a - m_row)
                w_loc = jnp.exp(m_loc - m_row)
                inter = live['inters'][c]
                intra = intra_ref[c * L:(c + 1) * L, :]
                num = w_inter * inter[:, :DV_B] + w_loc * intra[:, :DV_B]
                den = w_inter * inter[:, DV_B:] + w_loc * intra[:, DV_B:]
                houts.append(num / jnp.maximum(jnp.abs(den), jnp.exp(-m_row)))
            hf_ref[rows[0], :] = jnp.concatenate(houts[:H_B], axis=1)
            hb_ref[rows[1], :] = jnp.concatenate(houts[H_B:], axis=1)

        return (products, update, outputs), new_ms

    local_part(0, slots[0])

    def step_pair(i, ms):
        s0 = 2 * i
        stages, ms1 = recurrence_stages(s0, ms, slots[0])
        local_part(s0 + 1, slots[1], stages)
        stages, ms2 = recurrence_stages(s0 + 1, tuple(ms1), slots[1])
        local_part(jnp.minimum(s0 + 2, N_CHUNKS - 1), slots[0], stages)
        return tuple(ms2)

    m0 = tuple(jnp.full((SUBLANES, 128), NEG_BIG, F32) for _ in range(n_chain))
    lax.fori_loop(0, N_CHUNKS // 2, step_pair, m0)

    def finish(i, carry):
        rows = pl.ds(pl.multiple_of(i * TILE, TILE), TILE)
        hs = hf_ref[rows, :] + hb_ref[rows, :]
        ng = ng_ref[...]
        og = og_ref[0, rows, :].astype(F32)
        for h in range(H_B):
            sl = slice(h * DV_B, (h + 1) * DV_B)
            out_ref[0, rows, sl] = (_rms(hs[:, sl]) * ng[:, sl] * og[:, sl]).astype(BF16)
        return carry

    lax.fori_loop(0, T_ALL // TILE, finish, 0)


def _mlstm_call(mq, mk, mv, gc, gr, og, ng):
    B = mq.shape[0]
    full = lambda a: pl.BlockSpec((1,) + a.shape[1:], lambda b: (b,) + (0,) * (a.ndim - 1))
    return pl.pallas_call(
        _mlstm_kernel,
        grid=(B,),
        in_specs=[full(mq), full(mk), full(mv), full(gc), full(gr), full(og),
                  pl.BlockSpec(ng.shape, lambda b: (0, 0))],
        out_specs=pl.BlockSpec((1, T_ALL, H_B * DV_B), lambda b: (b, 0, 0)),
        out_shape=jax.ShapeDtypeStruct((B, T_ALL, H_B * DV_B), BF16),
        scratch_shapes=[
            pltpu.VMEM((T_ALL, H_B * DV_B), F32),
            pltpu.VMEM((T_ALL, H_B * DV_B), F32),
            pltpu.VMEM((2 * H_B * DK_B, 2 * DV_B), F32),
        ] + 2 * [
            pltpu.VMEM((2 * H_B * MLSTM_CHUNK, 2 * DV_B), F32),
            pltpu.VMEM((2 * H_B * DK_B, 2 * DV_B), F32),
            pltpu.VMEM((2 * H_B * 2 * MLSTM_CHUNK, 128), F32),
            pltpu.VMEM((2 * H_B * 2 * SUBLANES, 128), F32),
        ],
        compiler_params=_cparams(("arbitrary",)),
        name="mlstm",
    )(mq, mk, mv, gc, gr, og, ng)


def _na_bias_tables(rpb):
    kh = WIN_H
    n_drow = 2 * WIN_H - 1
    qcol = np.arange(GRID_W)
    col_start = np.clip(qcol - WIN_W // 2, 0, GRID_W - WIN_W)
    col_ok = (qcol[None, :] >= col_start[:, None]) & (qcol[None, :] < col_start[:, None] + WIN_W)
    dcol = qcol[None, :] - qcol[:, None] + (WIN_W - 1)
    onehot = (dcol[None] == np.arange(2 * WIN_W - 1)[:, None, None]) & col_ok[None]
    blocks = jnp.einsum('hdx,xck->hdck', rpb.astype(F32), jnp.asarray(onehot, F32),
                        precision=lax.Precision.HIGHEST)
    blocks = jnp.where(col_ok[None, None], blocks, NEG_BIG)
    outside = jnp.full((H_A, 1, GRID_W, GRID_W), NEG_BIG, F32)
    blocks = jnp.concatenate([blocks, outside], axis=1)
    idx = np.full((len(NA_CASES), NA_QROWS, NA_KROWS), n_drow, np.int32)
    for ci, r in enumerate(NA_CASES):
        u0 = int(np.clip(r - kh // 2, 0, GRID_ROWS - NA_KROWS))
        for qi in range(NA_QROWS):
            rq = r + qi
            r0 = int(np.clip(rq - kh // 2, 0, GRID_ROWS - kh))
            for ui in range(NA_KROWS):
                kr = u0 + ui
                if r0 <= kr < r0 + kh:
                    idx[ci, qi, ui] = kr - rq + (WIN_H - 1)

    def assemble(blk_ref, tab_ref):
        for ci in range(len(NA_CASES)):
            for qi in range(NA_QROWS):
                for ui in range(NA_KROWS):
                    tab_ref[ci, 0, qi * GRID_W:(qi + 1) * GRID_W, ui * GRID_W:(ui + 1) * GRID_W] = (
                        blk_ref[0, int(idx[ci, qi, ui])])

    return pl.pallas_call(
        assemble,
        grid=(H_A,),
        in_specs=[pl.BlockSpec((1, n_drow + 1, GRID_W, GRID_W), lambda h: (h, 0, 0, 0))],
        out_specs=pl.BlockSpec((len(NA_CASES), 1, NA_TQ, NA_TK), lambda h: (0, h, 0, 0)),
        out_shape=jax.ShapeDtypeStruct((len(NA_CASES), H_A, NA_TQ, NA_TK), F32),
        compiler_params=_cparams(("arbitrary",)),
        name="nbr_bias_table",
    )(blocks)


def _na_case(j):
    case = jnp.int32(NA_STEP_CASE[0])
    for step, c in enumerate(NA_STEP_CASE):
        if c != NA_STEP_CASE[0]:
            case = jnp.where(j - 1 == step, c, case)
    return case


def _na_kernel(q_ref, k_ref, v_ref, bias_ref, o_ref):
    j = pl.program_id(1)
    lane = lax.broadcasted_iota(I32, (1, 2 * HD_A), 1)
    lo_half = lane < HD_A

    def attend(q_rows, n_q, key_sets, bias_for_head):
        for pp0 in range(0, H_A // 2, NA_STAGE_PAIRS):
            heads = []
            for pp in range(pp0, pp0 + NA_STAGE_PAIRS):
                lanes = slice(pp * 2 * HD_A, (pp + 1) * 2 * HD_A)
                qp = q_ref[0, q_rows, lanes]
                ks = [k_ref[0, rs, lanes] for rs in key_sets]
                vs = [v_ref[0, rs, lanes] for rs in key_sets]
                for hh in range(2):
                    qm = jnp.where(lo_half if hh == 0 else ~lo_half, qp, jnp.zeros_like(qp))
                    heads.append(dict(head=2 * pp + hh, vs=vs, ss=[_dot_nt(qm, kk) for kk in ks]))
            for hd in heads:
                ss = hd['ss']
                if bias_for_head is not None:
                    ss[0] = ss[0] + bias_for_head(hd['head'])
                m = ss[0].max(axis=-1, keepdims=True)
                for s in ss[1:]:
                    m = jnp.maximum(m, s.max(axis=-1, keepdims=True))
                ps = [jnp.exp(s - m) for s in ss]
                l = ps[0].sum(axis=-1, keepdims=True)
                for p in ps[1:]:
                    l = l + p.sum(axis=-1, keepdims=True)
                hd['ps'] = [p.astype(BF16) for p in ps]
                hd['l'] = l
            for hd in heads:
                acc = _dot(hd['ps'][0], hd['vs'][0])
                for p, vv in zip(hd['ps'][1:], hd['vs'][1:]):
                    acc = acc + _dot(p, vv)
                hd['o'] = acc / hd['l']
            for i, pp in enumerate(range(pp0, pp0 + NA_STAGE_PAIRS)):
                lanes = slice(pp * 2 * HD_A, (pp + 1) * 2 * HD_A)
                o_ref[0, q_rows, lanes] = jnp.where(lo_half, heads[2 * i]['o'], heads[2 * i + 1]['o']).astype(BF16)

    ctx_rows = pl.ds(0, CTX_LEN)

    @pl.when(j == 0)
    def _():
        attend(ctx_rows, CTX_LEN, [ctx_rows], None)

    @pl.when(j > 0)
    def _():
        r = (j - 1) * NA_QROWS
        u0 = jnp.clip(r - WIN_H // 2, 0, GRID_ROWS - NA_KROWS)
        q_rows = pl.ds(pl.multiple_of(CTX_LEN + r * GRID_W, NA_TQ), NA_TQ)
        k_rows = pl.ds(pl.multiple_of(CTX_LEN + u0 * GRID_W, GRID_W), NA_TK)
        attend(q_rows, NA_TQ, [k_rows, ctx_rows], lambda head: bias_ref[0, head])


def _na_call(aq, ak, av, bias):
    B = aq.shape[0]
    full = pl.BlockSpec((1, T_ALL, D_A), lambda b, j: (b, 0, 0))
    return pl.pallas_call(
        _na_kernel,
        grid=(B, 1 + GRID_ROWS // NA_QROWS),
        in_specs=[full, full, full,
                  pl.BlockSpec((1, H_A, NA_TQ, NA_TK), lambda b, j: (_na_case(j), 0, 0, 0))],
        out_specs=full,
        out_shape=jax.ShapeDtypeStruct((B, T_ALL, D_A), BF16),
        compiler_params=_cparams(("arbitrary", "arbitrary")),
        name="nbr_attn",
    )(aq, ak, av, bias)


def _route(logits_t, rb_col):
    sc = _sigmoid(logits_t)
    sel = sc + rb_col
    selr = [sel[e:e + 1, :] for e in range(N_EXPERTS)]
    scr = [sc[e:e + 1, :] for e in range(N_EXPERTS)]
    gscore = []
    for g in range(N_GROUPS):
        a, b, c, d = selr[EXP_PER_GROUP * g:EXP_PER_GROUP * (g + 1)]
        s1, t1 = jnp.maximum(a, b), jnp.minimum(a, b)
        s2, t2 = jnp.maximum(c, d), jnp.minimum(c, d)
        gscore.append(jnp.maximum(s1, s2) + jnp.maximum(jnp.minimum(s1, s2), jnp.maximum(t1, t2)))
    best = gscore[0]
    gi = jnp.zeros(best.shape, I32)
    for g in range(1, N_GROUPS):
        better = gscore[g] > best
        gi = jnp.where(better, g, gi)
        best = jnp.where(better, gscore[g], best)
    vs, ws = [], []
    for k in range(EXP_PER_GROUP):
        v = selr[k]
        w = scr[k]
        for g in range(1, N_GROUPS):
            v = jnp.where(gi == g, selr[EXP_PER_GROUP * g + k], v)
            w = jnp.where(gi == g, scr[EXP_PER_GROUP * g + k], w)
        vs.append(v)
        ws.append(w)
    b1, i1 = vs[0], jnp.zeros(best.shape, I32)
    for k in range(1, EXP_PER_GROUP):
        better = vs[k] > b1
        i1 = jnp.where(better, k, i1)
        b1 = jnp.where(better, vs[k], b1)
    b2 = jnp.full(best.shape, -jnp.inf, F32)
    i2 = jnp.zeros(best.shape, I32)
    for k in range(EXP_PER_GROUP):
        vk = jnp.where(i1 == k, -jnp.inf, vs[k])
        better = vk > b2
        i2 = jnp.where(better, k, i2)
        b2 = jnp.where(better, vk, b2)
    w1 = ws[0]
    w2 = ws[0]
    for k in range(1, EXP_PER_GROUP):
        w1 = jnp.where(i1 == k, ws[k], w1)
        w2 = jnp.where(i2 == k, ws[k], w2)
    tot = w1 + w2
    return gi * EXP_PER_GROUP + i1, gi * EXP_PER_GROUP + i2, w1 / tot, w2 / tot


def _outproj_kernel(n_act, has_ctx, *refs):
    acts = refs[:n_act]
    ws = refs[n_act:2 * n_act]
    refs = refs[2 * n_act:]
    g_ref, rwh_ref, rwl_ref, xo_ref, hp_ref, lg_ref = refs[-6:]
    ns = xo_ref.shape[0]
    x_ins, (mods,) = _stream_tiles(has_ctx, has_ctx, refs[:-6], ns)
    o = _dot(acts[0][...].reshape(ns * TILE, -1), ws[0][...])
    for a, w in zip(acts[1:], ws[1:]):
        o = o + _dot(a[...].reshape(ns * TILE, -1), w[...])
    hs = []
    for s in range(ns):
        x = x_ins[s] + mods[s][2:3] * o[s * TILE:(s + 1) * TILE]
        xo_ref[s] = x
        h = _rms(x) * g_ref[...] * (1.0 + mods[s][4:5]) + mods[s][3:4]
        _store_token_tiles(hp_ref, (s,), 0, h)
        hs.append(_split_bf16(h, 2))
    for s, (h_hi, h_lo) in enumerate(hs):
        lg_ref[s] = (_dot_nt(rwh_ref[...], h_hi) + _dot_nt(rwh_ref[...], h_lo)
                     + _dot_nt(rwl_ref[...], h_hi))


def _outproj_call(acts, ws, x_src, x_off, modl, norm_g, rwh, rwl, n_tiles, ctx_src=None):
    B = x_src.shape[0]
    ns = math.gcd(B, PROJ_SAMPLES)
    n_act = len(acts)
    T = n_tiles * TILE
    const2 = lambda b, j: (0, 0)
    tok = lambda b, j: (b, j, 0)
    in_specs = [pl.BlockSpec((ns, TILE, a.shape[2]), tok) for a in acts]
    in_specs += [pl.BlockSpec(w.shape, const2) for w in ws]
    stream_specs, streams = _stream_specs(ns, x_src, x_off, ctx_src, ctx_src is not None, [modl])
    in_specs += stream_specs + [
        pl.BlockSpec((1, D_MODEL), const2),
        pl.BlockSpec(rwh.shape, const2),
        pl.BlockSpec(rwl.shape, const2),
    ]
    out_shape = [
        jax.ShapeDtypeStruct((B, T, D_MODEL), F32),
        jax.ShapeDtypeStruct((B, T * SUBLANES, 128), F32),
        jax.ShapeDtypeStruct((B, N_EXPERTS, T), F32),
    ]
    out_specs = [
        pl.BlockSpec((ns, TILE, D_MODEL), tok),
        pl.BlockSpec((ns, TILE * SUBLANES, 128), tok),
        pl.BlockSpec((ns, N_EXPERTS, TILE), lambda b, j: (b, 0, j)),
    ]
    return pl.pallas_call(
        functools.partial(_outproj_kernel, n_act, ctx_src is not None),
        grid=(B // ns, n_tiles),
        in_specs=in_specs,
        out_specs=out_specs,
        out_shape=out_shape,
        compiler_params=_cparams(("arbitrary", "arbitrary")),
        name="outproj",
    )(*acts, *ws, *streams, norm_g, rwh, rwl)


def _route_kernel(lg_ref, rb_ref, su_ref, ri_ref, rw_ref, cnt_ref):
    logits_t = lg_ref[0]
    T = logits_t.shape[1]
    e1, e2, w1, w2 = _route(logits_t, rb_ref[...])
    eidx = lax.broadcasted_iota(I32, logits_t.shape, 0)
    oh1 = eidx == e1
    oh2 = eidx == e2
    onehot = jnp.where(oh1, 1.0, jnp.where(oh2, 1.0, 0.0))
    count = jnp.zeros((N_EXPERTS, 1), F32)
    ranks = []
    for jj in range(T // TILE):
        oh = onehot[:, jj * TILE:(jj + 1) * TILE]
        ranks.append(_dot(oh.astype(BF16), su_ref[...]) + count)
        count = count + jnp.sum(oh, axis=1, keepdims=True)
    cpad = jnp.floor((count + (SUBLANES - 1.0)) * (1.0 / SUBLANES)) * SUBLANES
    ecol = lax.broadcasted_iota(I32, (N_EXPERTS, 1), 0)
    start = jnp.zeros((N_EXPERTS, 1), F32)
    for e in range(N_EXPERTS - 1):
        start = start + jnp.where(ecol > e, cpad[e:e + 1, :], 0.0)
    row = jnp.concatenate(ranks, axis=1) + start
    r1 = jnp.sum(jnp.where(oh1, row, 0.0), axis=0, keepdims=True)
    r2 = jnp.sum(jnp.where(oh2, row, 0.0), axis=0, keepdims=True)
    zi = jnp.zeros((SUBLANES - 4, T), I32)
    ri_ref[0] = jnp.concatenate([e1, e2, r1.astype(I32), r2.astype(I32), zi], axis=0)
    zf = jnp.zeros((SUBLANES - 2, T), F32)
    rw_ref[0] = jnp.concatenate([w1, w2, zf], axis=0)
    cnt_ref[0] = jnp.broadcast_to(count, (N_EXPERTS, 128)).astype(I32)


def _route_call(lg, rb, su):
    B, _, T = lg.shape
    per_sample = lambda b: (b, 0, 0)
    return pl.pallas_call(
        _route_kernel,
        grid=(B,),
        in_specs=[pl.BlockSpec((1, N_EXPERTS, T), per_sample),
                  pl.BlockSpec(rb.shape, lambda b: (0, 0)),
                  pl.BlockSpec(su.shape, lambda b: (0, 0))],
        out_specs=[pl.BlockSpec((1, SUBLANES, T), per_sample),
                   pl.BlockSpec((1, SUBLANES, T), per_sample),
                   pl.BlockSpec((1, N_EXPERTS, 128), per_sample)],
        out_shape=[jax.ShapeDtypeStruct((B, SUBLANES, T), I32),
                   jax.ShapeDtypeStruct((B, SUBLANES, T), F32),
                   jax.ShapeDtypeStruct((B, N_EXPERTS, 128), I32)],
        compiler_params=_cparams(("arbitrary",)),
        name="route",
    )(lg, rb, su)


def _moe_kernel(T, n_rows, cnt_ref, ri_ref, rw_ref, h_ref, w1_ref, w3_ref, w2_ref, y_ref,
                xb_ref, ob_ref, tokl_ref, off_ref):
    b = pl.program_id(0)
    e = pl.program_id(1)

    def tile_rows(row):
        return pl.ds(pl.multiple_of(row * SUBLANES, SUBLANES), SUBLANES)

    @pl.when(e == 0)
    def _():
        off_ref[0] = 0
        for i in range(N_EXPERTS):
            c = cnt_ref[b, i]
            off_ref[i + 1] = off_ref[i] + ((c + SUBLANES - 1) // SUBLANES) * SUBLANES

        for i in range(N_EXPERTS):
            for k in range(SUBLANES):
                tokl_ref[jnp.maximum(off_ref[i + 1] - SUBLANES + k, 0)] = 0

        def clear(i, carry):
            tokl_ref[off_ref[N_EXPERTS] + i] = 0
            return carry

        lax.fori_loop(0, MOE_RB, clear, 0, unroll=8)

        def place(t, carry):
            tokl_ref[ri_ref[2 * T + t]] = t
            tokl_ref[ri_ref[3 * T + t]] = t
            return carry

        lax.fori_loop(0, T, place, 0, unroll=8)

        def gather(i, carry):
            xb_ref[tile_rows(i), :] = h_ref[0, tile_rows(tokl_ref[i]), :]
            return carry

        lax.fori_loop(0, MOE_RB, gather, 0, unroll=8)

    c = cnt_ref[b, e]
    base = off_ref[e]
    next_base = off_ref[e + 1]

    def block(p0, n_blk_rows, p_next):
        xb = _load_token_tiles(xb_ref, (), 0, n_blk_rows).astype(BF16)
        for i in range(MOE_RB):
            xb_ref[i * SUBLANES:(i + 1) * SUBLANES, :] = h_ref[0, tile_rows(tokl_ref[p_next + i]), :]
        h1 = _dot(xb, w1_ref[0, 0])
        h3 = _dot(xb, w3_ref[0, 0])
        act = (h1 * _sigmoid(h1)) * h3
        _store_token_tiles(ob_ref, (), p0, _dot(act.astype(BF16), w2_ref[0, 0]))

    n_big = (c + MOE_RB - MOE_RB_TAIL - 1) // MOE_RB
    n_big = jnp.maximum(n_big, 0)
    has_tail = c > n_big * MOE_RB

    def big_block(rb, carry):
        p0 = pl.multiple_of(base + rb * MOE_RB, SUBLANES)
        last = jnp.logical_and(rb == n_big - 1, jnp.logical_not(has_tail))
        block(p0, MOE_RB, jnp.where(last, next_base, p0 + MOE_RB))
        return carry

    lax.fori_loop(0, n_big, big_block, 0)

    @pl.when(has_tail)
    def _():
        block(pl.multiple_of(base + n_big * MOE_RB, SUBLANES), MOE_RB_TAIL, next_base)

    @pl.when(e == N_EXPERTS - 1)
    def _():
        def combine(t, carry):
            y_ref[0, tile_rows(t), :] = (rw_ref[t] * ob_ref[tile_rows(ri_ref[2 * T + t]), :]
                                         + rw_ref[T + t] * ob_ref[tile_rows(ri_ref[3 * T + t]), :])
            return carry

        lax.fori_loop(0, T, combine, 0, unroll=8)


def _moe_call(cnt, ri, rw, h, w1, w3, w2, layer):
    B = h.shape[0]
    T = h.shape[1] // SUBLANES
    n_rows = 2 * T + N_EXPERTS * SUBLANES + MOE_RB
    smem = functools.partial(pl.BlockSpec, memory_space=pltpu.SMEM)
    once = pl.Buffered(1)
    return pl.pallas_call(
        functools.partial(_moe_kernel, T, n_rows),
        grid=(B, N_EXPERTS),
        in_specs=[
            smem(cnt.shape, lambda b, e: (0, 0)),
            smem((4 * T,), lambda b, e: (b,)),
            smem((4 * T,), lambda b, e: (b,)),
            pl.BlockSpec((1, T * SUBLANES, 128), lambda b, e: (b, 0, 0), pipeline_mode=once),
            pl.BlockSpec((1, 1, D_MODEL, D_FF), lambda b, e: (layer, e, 0, 0)),
            pl.BlockSpec((1, 1, D_MODEL, D_FF), lambda b, e: (layer, e, 0, 0)),
            pl.BlockSpec((1, 1, D_FF, D_MODEL), lambda b, e: (layer, e, 0, 0)),
        ],
        out_specs=pl.BlockSpec((1, T * SUBLANES, 128), lambda b, e: (b, 0, 0), pipeline_mode=once),
        out_shape=jax.ShapeDtypeStruct((B, T * SUBLANES, 128), F32),
        scratch_shapes=[
            pltpu.VMEM((MOE_RB * SUBLANES, 128), F32),
            pltpu.VMEM((n_rows * SUBLANES, 128), F32),
            pltpu.SMEM((n_rows,), I32),
            pltpu.SMEM((N_EXPERTS + 1,), I32),
        ],
        compiler_params=_cparams(("arbitrary", "arbitrary")),
        name="moe",
    )(cnt, ri, rw, h, w1, w3, w2)


def _route_tables(ri, rw, cnt):
    return cnt[:, :, 0], ri[:, :4].reshape(-1), rw[:, :4].reshape(-1)


def _inproj1_kernel(*refs):
    (y_ref, g_ref, w_ref, qg_ref, kg_ref, cos_ref, sin_ref, xo_ref, q_ref, k_ref, v_ref) = refs[-11:]
    ns = xo_ref.shape[0]
    x_ins, (mods0, mods) = _stream_tiles(False, True, refs[:-11], ns)
    hbs = []
    for s in range(ns):
        x = x_ins[s] + mods0[s][5:6] * _load_token_tiles(y_ref, (s,), 0, TILE)
        xo_ref[s] = x
        hbs.append((_rms(x) * g_ref[...] * (1.0 + mods[s][1:2]) + mods[s][0:1]).astype(BF16))
    hb = jnp.concatenate(hbs, axis=0)
    cos = cos_ref[...]
    sin = sin_ref[...]

    def rope_heads(ref, proj, n_heads, gain, scale):
        for s in range(ns):
            for h in range(n_heads):
                sl = slice(h * HD_C, (h + 1) * HD_C)
                n = _rms(proj[s * TILE:(s + 1) * TILE, sl]) * gain
                r = n * cos + pltpu.roll(n, HD_C // 2, 1) * sin
                ref[s, :, sl] = (r if scale is None else r * scale).astype(BF16)

    ko = H_C * HD_C
    vo = (H_C + KV_C) * HD_C
    kv = _dot(hb, w_ref[:, ko:vo + KV_C * HD_C])
    rope_heads(k_ref, kv, KV_C, kg_ref[...], None)
    for s in range(ns):
        v_ref[s] = kv[s * TILE:(s + 1) * TILE, KV_C * HD_C:].astype(BF16)

    @pl.when(pl.program_id(1) >= NT_CTX)
    def _():
        rope_heads(q_ref, _dot(hb, w_ref[:, :ko]), H_C, qg_ref[...], HD_C ** -0.5)


def _inproj1_call(x1, y0, mod0, mod1, norm_g, w, qg, kg, cos, sin):
    B = x1.shape[0]
    ns = math.gcd(B, INPROJ1_SAMPLES)
    tok = lambda b, j: (b, j, 0)
    const2 = lambda b, j: (0, 0)
    lat_tok = lambda b, j: (b, jnp.maximum(j - NT_CTX, 0), 0)
    widths = [(T_ALL, D_MODEL, F32), (SEQ, H_C * HD_C, BF16),
              (T_ALL, KV_C * HD_C, BF16), (T_ALL, KV_C * HD_C, BF16)]
    stream_specs, streams = _stream_specs(ns, x1, 0, None, True, [mod0, mod1])
    return pl.pallas_call(
        _inproj1_kernel,
        grid=(B // ns, NT_ALL),
        in_specs=stream_specs + [
            pl.BlockSpec((ns, TILE * SUBLANES, 128), tok),
            pl.BlockSpec((1, D_MODEL), const2),
            pl.BlockSpec(w.shape, const2),
            pl.BlockSpec(qg.shape, const2),
            pl.BlockSpec(kg.shape, const2),
            pl.BlockSpec((TILE, HD_C), lambda b, j: (j, 0)),
            pl.BlockSpec((TILE, HD_C), lambda b, j: (j, 0)),
        ],
        out_specs=[pl.BlockSpec((ns, TILE, w_), lat_tok if t == SEQ else tok) for t, w_, _ in widths],
        out_shape=[jax.ShapeDtypeStruct((B, t, w_), dt) for t, w_, dt in widths],
        compiler_params=_cparams(("arbitrary", "arbitrary")),
        name="inproj1",
    )(*streams, y0, norm_g, w, qg, kg, cos, sin)


def _gqa_kernel(q_ref, k_ref, v_ref, o_ref):
    k = k_ref[0]
    v = v_ref[0]
    for h0 in range(0, H_C // KV_C, GQA_STAGE_HEADS):
        sls = [slice(h * HD_C, (h + 1) * HD_C) for h in range(h0, h0 + GQA_STAGE_HEADS)]
        ss = [_dot_nt(q_ref[0, :, sl], k) for sl in sls]
        ps, ls = [], []
        for s in ss:
            p = jnp.exp(s - s.max(axis=-1, keepdims=True))
            ls.append(p.sum(axis=-1, keepdims=True))
            ps.append(p.astype(BF16))
        for sl, p, l in zip(sls, ps, ls):
            o_ref[0, :, sl] = (_dot(p, v) / l).astype(BF16)


def _gqa_call(q, k, v):
    B = q.shape[0]
    gw = (H_C // KV_C) * HD_C
    return pl.pallas_call(
        _gqa_kernel,
        grid=(B, KV_C, SEQ // GQA_TQ),
        in_specs=[
            pl.BlockSpec((1, GQA_TQ, gw), lambda b, g, j: (b, j, g)),
            pl.BlockSpec((1, T_ALL, HD_C), lambda b, g, j: (b, 0, g)),
            pl.BlockSpec((1, T_ALL, HD_C), lambda b, g, j: (b, 0, g)),
        ],
        out_specs=pl.BlockSpec((1, GQA_TQ, gw), lambda b, g, j: (b, j, g)),
        out_shape=jax.ShapeDtypeStruct((B, SEQ, H_C * HD_C), BF16),
        compiler_params=_cparams(("arbitrary", "arbitrary", "arbitrary")),
        name="gqa",
    )(q, k, v)


def _final_kernel(x_ref, y_ref, mod_ref, o_ref):
    o_ref[0] = x_ref[0] + mod_ref[0][5:6] * _load_token_tiles(y_ref, (0,), 0, FINAL_TILE)


def _final_call(x, y, modl):
    B, T, D = x.shape
    tok = lambda b, j: (b, j, 0)
    return pl.pallas_call(
        _final_kernel,
        grid=(B, T // FINAL_TILE),
        in_specs=[pl.BlockSpec((1, FINAL_TILE, D), tok), pl.BlockSpec((1, FINAL_TILE * SUBLANES, 128), tok),
                  pl.BlockSpec((1, 6, D), lambda b, j: (b, 0, 0))],
        out_specs=pl.BlockSpec((1, FINAL_TILE, D), tok),
        out_shape=jax.ShapeDtypeStruct((B, T, D), F32),
        compiler_params=_cparams(("arbitrary", "arbitrary")),
        name="final_residual",
    )(x, y, modl)


def _chunk_tri(lower):
    i = np.arange(TILE)
    same = (i[:, None] // MLSTM_CHUNK) == (i[None, :] // MLSTM_CHUNK)
    tri = (i[None, :] <= i[:, None]) if lower else (i[None, :] >= i[:, None])
    return jnp.asarray((same & tri).astype(np.float32), BF16)


def _rope_tables():
    n_freq = HD_C // 4
    inv_freq = ROPE_THETA ** (-jnp.arange(n_freq, dtype=F32) / n_freq)
    t = jnp.arange(SEQ)
    rows = (t // GRID_W).astype(F32)
    cols = (t % GRID_W).astype(F32)
    ang = jnp.concatenate([rows[:, None] * inv_freq, cols[:, None] * inv_freq], axis=-1)
    cos, sin = jnp.cos(ang), jnp.sin(ang)
    cos_l = jnp.concatenate([cos, cos], axis=-1)
    sin_l = jnp.concatenate([-sin, sin], axis=-1)
    cos_all = jnp.concatenate([jnp.ones((CTX_LEN, HD_C), F32), cos_l], axis=0)
    sin_all = jnp.concatenate([jnp.zeros((CTX_LEN, HD_C), F32), sin_l], axis=0)
    return cos_all, sin_all


_HEAD_PERM = np.concatenate([np.arange(0, HD_C, 2), np.arange(1, HD_C, 2)])


def kernel(x, c, ctx, c_ctx, ada_w, ada_b, norm_mix_g, norm_ffn_g, even_w_in, even_w_out,
           na_q_norm_g, na_k_norm_g, na_rpb, mlstm_gate_b, mlstm_norm_g, odd_w_in, odd_w_out,
           gqa_q_norm_g, gqa_k_norm_g, router_w, router_b, exp_w1, exp_w3, exp_w2):
    B = x.shape[0]
    assert B <= N_MOD_CTX_ROW and x.shape[1:] == (SEQ, D_MODEL) and ctx.shape[1:] == (CTX_LEN, D_MODEL)
    n_g = 4 * H_B

    cvec = jnp.zeros((N_MOD_ROWS, D_MODEL), F32).at[:B].set(c).at[N_MOD_CTX_ROW].set(c_ctx)
    mod = _ada_call(cvec, ada_w, ada_b).reshape(2, N_MOD_ROWS, 6, D_MODEL)
    mod0, mod1 = mod[0], mod[1]

    w_in = even_w_in[0]
    n_main = w_in.shape[1] - n_g
    wm = w_in[:, :n_main].astype(BF16)
    wg_f = w_in[:, n_main:]
    wg = jnp.pad(wg_f, ((0, 0), (0, 128 - n_g))).astype(BF16)
    wgt = wg_f.T.astype(BF16)
    gb = jnp.pad(mlstm_gate_b[0].reshape(1, n_g), ((0, 0), (0, 128 - n_g)))
    gbt = mlstm_gate_b[0].reshape(n_g, 1)
    qg = jnp.tile(na_q_norm_g[0], H_A).reshape(1, D_A)
    kg = jnp.tile(na_k_norm_g[0], H_A).reshape(1, D_A)
    hid = np.arange(128) // HD_A
    bd = jnp.asarray((hid[:, None] == hid[None, :]).astype(np.float32), BF16)
    k_lo = 3 * D_A + H_B * DK_B
    wkt = w_in[:, k_lo:k_lo + H_B * DK_B].T.astype(BF16)
    aq, ak, av, mq, mv, og, mkt, gc, gr = _inproj0_call(
        x, ctx, mod0, norm_mix_g[0].reshape(1, D_MODEL), wm, wkt, wg, wgt, gb, gbt, qg, kg, bd,
        _chunk_tri(True), _chunk_tri(False))
    hm = _mlstm_call(mq, mkt, mv, gc, gr, og, mlstm_norm_g[0].reshape(1, H_B * DV_B))
    oa = _na_call(aq, ak, av, _na_bias_tables(na_rpb[0]))

    rw_t = router_w.T
    rwh = rw_t.astype(BF16)
    rwl = (rw_t - rwh.astype(F32)).astype(BF16)
    rb = router_b.reshape(N_EXPERTS, 1).astype(F32)
    i = np.arange(TILE)
    su = jnp.asarray((i[:, None] < i[None, :]).astype(np.float32), BF16)
    w_out = even_w_out[0].astype(BF16)
    ew1, ew3, ew2 = exp_w1.astype(BF16), exp_w3.astype(BF16), exp_w2.astype(BF16)
    x1, hp0, lg0 = _outproj_call(
        [oa, hm], [w_out[:D_A], w_out[D_A:]], x, 0, mod0,
        norm_ffn_g[0].reshape(1, D_MODEL), rwh, rwl, NT_ALL, ctx_src=ctx)
    y0 = _moe_call(*_route_tables(*_route_call(lg0, rb, su)), hp0, ew1, ew3, ew2, 0)

    w1_in = odd_w_in[0]
    qk_cols = np.concatenate([h * HD_C + _HEAD_PERM for h in range(H_C + KV_C)])
    cols = np.concatenate([qk_cols, np.arange((H_C + KV_C) * HD_C, w1_in.shape[1])])
    w1_in = w1_in[:, cols].astype(BF16)
    cos_all, sin_all = _rope_tables()
    x2, q, k, v = _inproj1_call(
        x1, y0, mod0, mod1, norm_mix_g[1].reshape(1, D_MODEL), w1_in,
        gqa_q_norm_g[0][_HEAD_PERM].reshape(1, HD_C), gqa_k_norm_g[0][_HEAD_PERM].reshape(1, HD_C),
        cos_all, sin_all)
    o = _gqa_call(q, k, v)
    x3, hp1, lg1 = _outproj_call(
        [o], [odd_w_out[0].astype(BF16)], x2, NT_CTX, mod1,
        norm_ffn_g[1].reshape(1, D_MODEL), rwh, rwl, SEQ // TILE)
    y1 = _moe_call(*_route_tables(*_route_call(lg1, rb, su)), hp1, ew1, ew3, ew2, 1)
    return _final_call(x3, y1, mod1)
```

```python
import functools
import math

import numpy as np
import jax
import jax.numpy as jnp
from jax import lax
from jax.experimental import pallas as pl
from jax.experimental.pallas import tpu as pltpu

F32 = jnp.float32
BF16 = jnp.bfloat16
I32 = jnp.int32
U32 = jnp.uint32

D_MODEL = 1024
SEQ = 2048
GRID_W = 64
GRID_ROWS = SEQ // GRID_W
CTX_LEN = 256
T_ALL = CTX_LEN + SEQ
WIN_H = 8
WIN_W = 16
HD_A = 64
H_A = 8
D_A = H_A * HD_A
H_B = 4
DV_B = 128
DK_B = 64
MLSTM_CHUNK = 64
N_CHUNKS = T_ALL // MLSTM_CHUNK
N_CTX_CHUNKS = CTX_LEN // MLSTM_CHUNK
HD_C = 128
H_C = 8
KV_C = 2
ROPE_THETA = 10000.0
N_EXPERTS = 16
N_GROUPS = 4
EXP_PER_GROUP = 4
D_FF = 512
EPS = 1e-6
NEG_BIG = -1e30

TILE = 256
NT_ALL = T_ALL // TILE
NT_CTX = CTX_LEN // TILE
NA_QROWS = 4
NA_KROWS = WIN_H + NA_QROWS - 1
NA_TQ = NA_QROWS * GRID_W
NA_TK = NA_KROWS * GRID_W
NA_STAGE_PAIRS = 2


def _na_geometry(r):
    u0 = min(max(r - WIN_H // 2, 0), GRID_ROWS - NA_KROWS)
    r0s = tuple(min(max(r + qi - WIN_H // 2, 0), GRID_ROWS - WIN_H) - r for qi in range(NA_QROWS))
    return (u0 - r, r0s)


def _na_cases():
    reps, step_case = [], []
    for r in range(0, GRID_ROWS, NA_QROWS):
        geo = _na_geometry(r)
        known = [_na_geometry(q) for q in reps]
        if geo not in known:
            reps.append(r)
            known.append(geo)
        step_case.append(known.index(geo))
    return tuple(reps), tuple(step_case)


NA_CASES, NA_STEP_CASE = _na_cases()
PROJ_SAMPLES = 4
INPROJ0_SAMPLES = 2
INPROJ1_SAMPLES = 1
GQA_TQ = 512
GQA_STAGE_HEADS = 2
FINAL_TILE = 1024
MOE_RB = 256
MOE_RB_TAIL = 64
SUBLANES = 8
VMEM_LIMIT = 56 * 1024 * 1024


def _cparams(sem):
    return pltpu.CompilerParams(dimension_semantics=sem, vmem_limit_bytes=VMEM_LIMIT)


def _sigmoid(x):
    return 1.0 / (1.0 + jnp.exp(-x))


def _rms(x):
    return x * lax.rsqrt(jnp.mean(x * x, axis=-1, keepdims=True) + EPS)


def _dot(a, b):
    return jnp.dot(a, b, preferred_element_type=F32)


def _dot_nt(a, b):
    return lax.dot_general(a, b, (((1,), (1,)), ((), ())), preferred_element_type=F32)


def _dot_tn(a, b):
    return lax.dot_general(a, b, (((0,), (0,)), ((), ())), preferred_element_type=F32)


def _split_bf16(x, n):
    parts = []
    r = x
    for _ in range(n):
        p = r.astype(BF16)
        parts.append(p)
        r = r - p.astype(F32)
    return parts


LANE_TILES = D_MODEL // 128


def _load_token_tiles(ref, lead, tok0, n_tok):
    parts = [ref[(*lead, pl.ds(tok0 * SUBLANES + c, n_tok, stride=SUBLANES), slice(None))]
             for c in range(LANE_TILES)]
    return jnp.concatenate(parts, axis=1)


def _store_token_tiles(ref, lead, tok0, val):
    for c in range(LANE_TILES):
        ref[(*lead, pl.ds(tok0 * SUBLANES + c, val.shape[0], stride=SUBLANES), slice(None))] = (
            val[:, c * 128:(c + 1) * 128])


def _stream_specs(ns, x_src, x_off, ctx_src, ctx_tile, mod_tables):
    if ctx_src is None:
        specs = [pl.BlockSpec((ns, TILE, D_MODEL), lambda b, j: (b, j + x_off, 0))]
        arrays = [x_src]
    else:
        assert x_off == 0 and ctx_tile
        specs = [pl.BlockSpec((ns, TILE, D_MODEL), lambda b, j: (b, jnp.maximum(j - NT_CTX, 0), 0)),
                 pl.BlockSpec((ns, TILE, D_MODEL), lambda b, j: (b, jnp.minimum(j, NT_CTX - 1), 0))]
        arrays = [x_src, ctx_src]
    for table in mod_tables:
        specs.append(pl.BlockSpec((ns, 6, D_MODEL), lambda b, j: (b, 0, 0)))
        arrays.append(table)
        if ctx_tile:
            specs.append(pl.BlockSpec((1, 6, D_MODEL), lambda b, j: (N_MOD_CTX_ROW, 0, 0)))
            arrays.append(table)
    return specs, arrays


def _stream_tiles(two_streams, ctx_tile, refs, ns):
    is_ctx = pl.program_id(1) < NT_CTX
    if two_streams:
        x_ins = [jnp.where(is_ctx, refs[1][s], refs[0][s]) for s in range(ns)]
        refs = refs[2:]
    else:
        x_ins = [refs[0][s] for s in range(ns)]
        refs = refs[1:]
    tables = []
    if ctx_tile:
        for t in range(0, len(refs), 2):
            tables.append([jnp.where(is_ctx, refs[t + 1][0], refs[t][s]) for s in range(ns)])
    else:
        for ref in refs:
            tables.append([ref[s] for s in range(ns)])
    return x_ins, tables


N_MOD_ROWS = 16
N_MOD_CTX_ROW = 8


ADA_TN = 1536


def _ada_kernel(c_ref, w_ref, b_ref, o_ref):
    c = c_ref[...]
    s = (c * _sigmoid(c)).astype(BF16)
    o_ref[0] = _dot(s, w_ref[0].astype(BF16)) + b_ref[0]


def _ada_call(cvec, ada_w, ada_b):
    depth, d, n = ada_w.shape
    return pl.pallas_call(
        _ada_kernel,
        grid=(depth, n // ADA_TN),
        in_specs=[
            pl.BlockSpec((N_MOD_ROWS, d), lambda l, j: (0, 0)),
            pl.BlockSpec((1, d, ADA_TN), lambda l, j: (l, 0, j)),
            pl.BlockSpec((1, 1, ADA_TN), lambda l, j: (l, 0, j)),
        ],
        out_specs=pl.BlockSpec((1, N_MOD_ROWS, ADA_TN), lambda l, j: (l, 0, j)),
        out_shape=jax.ShapeDtypeStruct((depth, N_MOD_ROWS, n), F32),
        compiler_params=_cparams(("arbitrary", "arbitrary")),
        name="ada_mod",
    )(cvec, ada_w, ada_b.reshape(depth, 1, n))


def _log_sigmoid(x):
    return jnp.minimum(x, 0.0) - jnp.log1p(jnp.exp(-jnp.abs(x)))


def _inproj0_kernel(*refs):
    (g_ref, wm_ref, wkt_ref, wg_ref, wgt_ref, gb_ref, gbt_ref, qg_ref, kg_ref, bd_ref, tril_ref, triu_ref,
     aq_ref, ak_ref, av_ref, mq_ref, mv_ref, og_ref, mkt_ref, gc_ref, gr_ref) = refs[-21:]
    ns = aq_ref.shape[0]
    x_ins, (mods,) = _stream_tiles(True, True, refs[:-21], ns)
    hb = jnp.concatenate([(_rms(x_ins[s]) * g_ref[...] * (1.0 + mods[s][1:2]) + mods[s][0:1]).astype(BF16)
                          for s in range(ns)], axis=0)

    def per_sample(ref, val):
        for s in range(ns):
            ref[s] = val[s * TILE:(s + 1) * TILE]

    def proj(lo, hi):
        return _dot(hb, wm_ref[:, lo:hi])

    def head_norm(a, gain):
        a2 = a * a
        sums = []
        for s0 in range(0, D_A, 128):
            hi_, lo_ = _split_bf16(a2[:, s0:s0 + 128], 2)
            sums.append(_dot(hi_, bd_ref[...]) + _dot(lo_, bd_ref[...]))
        ss = jnp.concatenate(sums, axis=1)
        return a * lax.rsqrt(ss * (1.0 / HD_A) + EPS) * gain

    n_g = 4 * H_B
    p_aq, p_ak, p_av = proj(0, 512), proj(512, 1024), proj(1024, 1536)
    p_mq, p_mv, p_og = proj(1536, 1792), proj(2048, 2560), proj(2560, 3072)
    kt = _dot_nt(wkt_ref[...], hb).astype(BF16)
    gcol = _dot(hb, wg_ref[...]) + gb_ref[...]
    grow = _dot_nt(wgt_ref[...], hb) + gbt_ref[...]

    per_sample(aq_ref, (head_norm(p_aq, qg_ref[...]) * HD_A ** -0.5).astype(BF16))
    per_sample(ak_ref, head_norm(p_ak, kg_ref[...]).astype(BF16))
    per_sample(av_ref, p_av.astype(BF16))
    per_sample(mq_ref, (p_mq * DK_B ** -0.5).astype(BF16))
    per_sample(mv_ref, p_mv.astype(BF16))
    per_sample(og_ref, _sigmoid(p_og).astype(BF16))

    tril = tril_ref[...]
    triu = triu_ref[...]
    ls_c = _split_bf16(_log_sigmoid(gcol), 3)
    ls_r = _split_bf16(_log_sigmoid(grow), 3)
    cidx = lax.broadcasted_iota(I32, (TILE, 128), 1)
    ridx = lax.broadcasted_iota(I32, (n_g, TILE), 0)

    def pick(idx, raw, pre, suf):
        is_ff = (idx >= H_B) & (idx < 2 * H_B)
        is_fb = idx >= 3 * H_B
        return jnp.where(is_ff, pre, jnp.where(is_fb, suf, raw))

    for s in range(ns):
        tok = slice(s * TILE, (s + 1) * TILE)
        pre_c = sum(_dot(tril, p[tok, :]) for p in ls_c)
        suf_c = sum(_dot(triu, p[tok, :]) for p in ls_c)
        pre_r = sum(_dot(p[:, tok], triu) for p in ls_r)
        suf_r = sum(_dot(p[:, tok], tril) for p in ls_r)
        gc_ref[s] = pick(cidx, gcol[tok, :], pre_c, suf_c)[:, :n_g]
        grow_s = pick(ridx, grow[:, tok], pre_r, suf_r)
        for c in range(TILE // MLSTM_CHUNK):
            chunk = slice(c * MLSTM_CHUNK, (c + 1) * MLSTM_CHUNK)
            gr_ref[s, c] = grow_s[:, chunk]
            mkt_ref[s, c] = kt[:, s * TILE + c * MLSTM_CHUNK:s * TILE + (c + 1) * MLSTM_CHUNK]


def _inproj0_call(x, ctx, modl, norm_g, wm, wkt, wg, wgt, gb, gbt, qg, kg, bd, tril, triu):
    B = x.shape[0]
    ns = math.gcd(B, INPROJ0_SAMPLES)
    n_g = 4 * H_B
    tok = lambda b, j: (b, j, 0)
    const2 = lambda b, j: (0, 0)
    chunked = lambda b, j: (b, j, 0, 0)
    tile_chunks = TILE // MLSTM_CHUNK
    outs = [
        (D_A, BF16), (D_A, BF16), (D_A, BF16),
        (H_B * DK_B, BF16), (H_B * DV_B, BF16),
        (H_B * DV_B, BF16),
    ]
    out_shape = [jax.ShapeDtypeStruct((B, T_ALL, w), dt) for w, dt in outs]
    out_specs = [pl.BlockSpec((ns, TILE, w), tok) for w, _ in outs]
    out_shape += [jax.ShapeDtypeStruct((B, N_CHUNKS, H_B * DK_B, MLSTM_CHUNK), BF16),
                  jax.ShapeDtypeStruct((B, T_ALL, n_g), F32),
                  jax.ShapeDtypeStruct((B, N_CHUNKS, n_g, MLSTM_CHUNK), F32)]
    out_specs += [pl.BlockSpec((ns, tile_chunks, H_B * DK_B, MLSTM_CHUNK), chunked),
                  pl.BlockSpec((ns, TILE, n_g), tok),
                  pl.BlockSpec((ns, tile_chunks, n_g, MLSTM_CHUNK), chunked)]
    stream_specs, streams = _stream_specs(ns, x, 0, ctx, True, [modl])
    return pl.pallas_call(
        _inproj0_kernel,
        grid=(B // ns, NT_ALL),
        in_specs=stream_specs + [
            pl.BlockSpec((1, D_MODEL), const2),
            pl.BlockSpec(wm.shape, const2),
            pl.BlockSpec(wkt.shape, const2),
            pl.BlockSpec(wg.shape, const2),
            pl.BlockSpec(wgt.shape, const2),
            pl.BlockSpec(gb.shape, const2),
            pl.BlockSpec(gbt.shape, const2),
            pl.BlockSpec(qg.shape, const2),
            pl.BlockSpec(kg.shape, const2),
            pl.BlockSpec(bd.shape, const2),
            pl.BlockSpec(tril.shape, const2),
            pl.BlockSpec(triu.shape, const2),
        ],
        out_specs=out_specs,
        out_shape=out_shape,
        compiler_params=_cparams(("arbitrary", "arbitrary")),
        name="inproj0",
    )(*streams, norm_g, wm, wkt, wg, wgt, gb, gbt, qg, kg, bd, tril, triu)


def _mlstm_kernel(mq_ref, mkt_ref, mv_ref, gc_ref, gr_ref, og_ref, ng_ref, out_ref,
                  hf_ref, hb_ref, s_ref, *local_refs):
    L = MLSTM_CHUNK
    n_chain = 2 * H_B
    slots = (local_refs[:4], local_refs[4:])
    s_ref[...] = jnp.zeros(s_ref.shape, F32)
    ri = lax.broadcasted_iota(I32, (L, L), 0)
    ci = lax.broadcasted_iota(I32, (L, L), 1)
    masks = (ci <= ri, ci >= ri)
    ones_aug = jnp.ones((L, DV_B), BF16)

    def tile_up(x, n_rows, n_cols):
        x = jnp.concatenate([x] * n_rows, axis=0)
        return x if n_cols == 1 else jnp.concatenate([x] * n_cols, axis=1)

    def chunk_rows(step, d):
        if d == 0:
            chunk = step
        else:
            chunk = jnp.where(step < N_CTX_CHUNKS, N_CTX_CHUNKS - 1 - step, N_CHUNKS + N_CTX_CHUNKS - 1 - step)
        return chunk, pl.ds(pl.multiple_of(chunk * L, L), L)

    def local_part(step, slot, rec=None):
        intra_ref, u_ref, col_ref, rep_ref = slot
        rec = rec or (lambda: None,) * 3
        chains = []
        for d in range(2):
            chunk, rows = chunk_rows(step, d)
            gcc = gc_ref[0, rows, :]
            grr = gr_ref[0, chunk]
            for h in range(H_B):
                gi = 2 * H_B * d + h
                b_col = gcc[:, gi + H_B:gi + H_B + 1]
                chains.append(dict(
                    d=d,
                    q=mq_ref[0, rows, h * DK_B:(h + 1) * DK_B],
                    kt=mkt_ref[0, chunk, h * DK_B:(h + 1) * DK_B, :],
                    v=mv_ref[0, rows, h * DV_B:(h + 1) * DV_B],
                    ig_col=gcc[:, gi:gi + 1], b_col=b_col,
                    ig_row=grr[gi:gi + 1, :], b_row=grr[gi + H_B:gi + H_B + 1, :],
                    b_last=b_col[L - 1:L, :] if d == 0 else b_col[0:1, :]))
        for ch in chains:
            ch['qk'] = _dot(ch['q'], ch['kt'])
            ch['v_aug'] = jnp.concatenate([ch['v'], ones_aug], axis=1)
            g = ch['b_last'] - ch['b_col'] + ch['ig_col']
            ch['g_max'] = jnp.max(g, axis=0, keepdims=True)
            ch['wv'] = (jnp.exp(g - ch['g_max']) * ch['v_aug'].astype(F32)).astype(BF16)
        us = [_dot(ch['kt'], ch['wv']) for ch in chains]
        for ch in chains:
            dm = jnp.where(masks[ch['d']], ch['b_col'] - ch['b_row'] + ch['ig_row'], NEG_BIG)
            ch['m_loc'] = jnp.max(dm, axis=-1, keepdims=True)
            ch['p'] = (ch['qk'] * jnp.exp(dm - ch['m_loc'])).astype(BF16)
        intras = [_dot(ch['p'], ch['v_aug']) for ch in chains]
        rec[0]()
        cols, reps = [], []
        for ch in chains:
            cols += [jnp.broadcast_to(ch['b_col'], (L, 128)), jnp.broadcast_to(ch['m_loc'], (L, 128))]
            reps += [jnp.broadcast_to(ch['b_last'], (SUBLANES, 128)),
                     jnp.broadcast_to(ch['g_max'], (SUBLANES, 128))]
        intra_ref[...] = jnp.concatenate(intras, axis=0)
        u_ref[...] = jnp.concatenate(us, axis=0)
        col_ref[...] = jnp.concatenate(cols, axis=0)
        rep_ref[...] = jnp.concatenate(reps, axis=0)
        rec[1]()
        rec[2]()

    def recurrence_stages(step, ms, slot):
        intra_ref, u_ref, col_ref, rep_ref = slot
        n_all = range(n_chain)
        rows = [chunk_rows(step, d)[1] for d in range(2)]
        live = {}
        new_ms = []

        def products():
            live['states'] = [s_ref[c * DK_B:(c + 1) * DK_B, :] for c in n_all]
            live['inters'] = [_dot(mq_ref[0, rows[c // H_B], (c % H_B) * DK_B:(c % H_B + 1) * DK_B],
                                   live['states'][c].astype(BF16)) for c in n_all]

        def update():
            new_states = []
            for c in n_all:
                m = ms[c]
                b_last = rep_ref[2 * c * SUBLANES:(2 * c + 1) * SUBLANES, :]
                g_max = rep_ref[(2 * c + 1) * SUBLANES:(2 * c + 2) * SUBLANES, :]
                m_new = jnp.maximum(b_last + m, g_max)
                w_old = tile_up(jnp.exp(b_last + m - m_new), DK_B // SUBLANES, 2)
                w_new = tile_up(jnp.exp(g_max - m_new), DK_B // SUBLANES, 2)
                new_states.append(w_old * live['states'][c] + w_new * u_ref[c * DK_B:(c + 1) * DK_B, :])
                new_ms.append(m_new)
            s_ref[...] = jnp.concatenate(new_states, axis=0)

        def outputs():
            houts = []
            for c in n_all:
                a = col_ref[2 * c * L:(2 * c + 1) * L, :] + tile_up(ms[c], L // SUBLANES, 1)
                m_loc = col_ref[(2 * c + 1) * L:(2 * c + 2) * L, :]
                m_row = jnp.maximum(a, m_loc)
                w_inter = jnp.exp(a - m_row)
                w_loc = jnp.exp(m_loc - m_row)
                inter = live['inters'][c]
                intra = intra_ref[c * L:(c + 1) * L, :]
                num = w_inter * inter[:, :DV_B] + w_loc * intra[:, :DV_B]
                den = w_inter * inter[:, DV_B:] + w_loc * intra[:, DV_B:]
                houts.append(num / jnp.maximum(jnp.abs(den), jnp.exp(-m_row)))
            hf_ref[rows[0], :] = jnp.concatenate(houts[:H_B], axis=1)
            hb_ref[rows[1], :] = jnp.concatenate(houts[H_B:], axis=1)

        return (products, update, outputs), new_ms

    local_part(0, slots[0])

    def step_pair(i, ms):
        s0 = 2 * i
        stages, ms1 = recurrence_stages(s0, ms, slots[0])
        local_part(s0 + 1, slots[1], stages)
        stages, ms2 = recurrence_stages(s0 + 1, tuple(ms1), slots[1])
        local_part(jnp.minimum(s0 + 2, N_CHUNKS - 1), slots[0], stages)
        return tuple(ms2)

    m0 = tuple(jnp.full((SUBLANES, 128), NEG_BIG, F32) for _ in range(n_chain))
    lax.fori_loop(0, N_CHUNKS // 2, step_pair, m0)

    def finish(i, carry):
        rows = pl.ds(pl.multiple_of(i * TILE, TILE), TILE)
        hs = hf_ref[rows, :] + hb_ref[rows, :]
        ng = ng_ref[...]
        og = og_ref[0, rows, :].astype(F32)
        for h in range(H_B):
            sl = slice(h * DV_B, (h + 1) * DV_B)
            out_ref[0, rows, sl] = (_rms(hs[:, sl]) * ng[:, sl] * og[:, sl]).astype(BF16)
        return carry

    lax.fori_loop(0, T_ALL // TILE, finish, 0)


def _mlstm_call(mq, mk, mv, gc, gr, og, ng):
    B = mq.shape[0]
    full = lambda a: pl.BlockSpec((1,) + a.shape[1:], lambda b: (b,) + (0,) * (a.ndim - 1))
    return pl.pallas_call(
        _mlstm_kernel,
        grid=(B,),
        in_specs=[full(mq), full(mk), full(mv), full(gc), full(gr), full(og),
                  pl.BlockSpec(ng.shape, lambda b: (0, 0))],
        out_specs=pl.BlockSpec((1, T_ALL, H_B * DV_B), lambda b: (b, 0, 0)),
        out_shape=jax.ShapeDtypeStruct((B, T_ALL, H_B * DV_B), BF16),
        scratch_shapes=[
            pltpu.VMEM((T_ALL, H_B * DV_B), F32),
            pltpu.VMEM((T_ALL, H_B * DV_B), F32),
            pltpu.VMEM((2 * H_B * DK_B, 2 * DV_B), F32),
        ] + 2 * [
            pltpu.VMEM((2 * H_B * MLSTM_CHUNK, 2 * DV_B), F32),
            pltpu.VMEM((2 * H_B * DK_B, 2 * DV_B), F32),
            pltpu.VMEM((2 * H_B * 2 * MLSTM_CHUNK, 128), F32),
            pltpu.VMEM((2 * H_B * 2 * SUBLANES, 128), F32),
        ],
        compiler_params=_cparams(("arbitrary",)),
        name="mlstm",
    )(mq, mk, mv, gc, gr, og, ng)


def _na_bias_tables(rpb):
    kh = WIN_H
    n_drow = 2 * WIN_H - 1
    qcol = np.arange(GRID_W)
    col_start = np.clip(qcol - WIN_W // 2, 0, GRID_W - WIN_W)
    col_ok = (qcol[None, :] >= col_start[:, None]) & (qcol[None, :] < col_start[:, None] + WIN_W)
    dcol = qcol[None, :] - qcol[:, None] + (WIN_W - 1)
    onehot = (dcol[None] == np.arange(2 * WIN_W - 1)[:, None, None]) & col_ok[None]
    blocks = jnp.einsum('hdx,xck->hdck', rpb.astype(F32), jnp.asarray(onehot, F32),
                        precision=lax.Precision.HIGHEST)
    blocks = jnp.where(col_ok[None, None], blocks, NEG_BIG)
    outside = jnp.full((H_A, 1, GRID_W, GRID_W), NEG_BIG, F32)
    blocks = jnp.concatenate([blocks, outside], axis=1)
    idx = np.full((len(NA_CASES), NA_QROWS, NA_KROWS), n_drow, np.int32)
    for ci, r in enumerate(NA_CASES):
        u0 = int(np.clip(r - kh // 2, 0, GRID_ROWS - NA_KROWS))
        for qi in range(NA_QROWS):
            rq = r + qi
            r0 = int(np.clip(rq - kh // 2, 0, GRID_ROWS - kh))
            for ui in range(NA_KROWS):
                kr = u0 + ui
                if r0 <= kr < r0 + kh:
                    idx[ci, qi, ui] = kr - rq + (WIN_H - 1)

    def assemble(blk_ref, tab_ref):
        for ci in range(len(NA_CASES)):
            for qi in range(NA_QROWS):
                for ui in range(NA_KROWS):
                    tab_ref[ci, 0, qi * GRID_W:(qi + 1) * GRID_W, ui * GRID_W:(ui + 1) * GRID_W] = (
                        blk_ref[0, int(idx[ci, qi, ui])])

    return pl.pallas_call(
        assemble,
        grid=(H_A,),
        in_specs=[pl.BlockSpec((1, n_drow + 1, GRID_W, GRID_W), lambda h: (h, 0, 0, 0))],
        out_specs=pl.BlockSpec((len(NA_CASES), 1, NA_TQ, NA_TK), lambda h: (0, h, 0, 0)),
        out_shape=jax.ShapeDtypeStruct((len(NA_CASES), H_A, NA_TQ, NA_TK), F32),
        compiler_params=_cparams(("arbitrary",)),
        name="nbr_bias_table",
    )(blocks)


def _na_case(j):
    case = jnp.int32(NA_STEP_CASE[0])
    for step, c in enumerate(NA_STEP_CASE):
        if c != NA_STEP_CASE[0]:
            case = jnp.where(j - 1 == step, c, case)
    return case


def _na_kernel(q_ref, k_ref, v_ref, bias_ref, o_ref):
    j = pl.program_id(1)
    lane = lax.broadcasted_iota(I32, (1, 2 * HD_A), 1)
    lo_half = lane < HD_A

    def attend(q_rows, n_q, key_sets, bias_for_head):
        for pp0 in range(0, H_A // 2, NA_STAGE_PAIRS):
            heads = []
            for pp in range(pp0, pp0 + NA_STAGE_PAIRS):
                lanes = slice(pp * 2 * HD_A, (pp + 1) * 2 * HD_A)
                qp = q_ref[0, q_rows, lanes]
                ks = [k_ref[0, rs, lanes] for rs in key_sets]
                vs = [v_ref[0, rs, lanes] for rs in key_sets]
                for hh in range(2):
                    qm = jnp.where(lo_half if hh == 0 else ~lo_half, qp, jnp.zeros_like(qp))
                    heads.append(dict(head=2 * pp + hh, vs=vs, ss=[_dot_nt(qm, kk) for kk in ks]))
            for hd in heads:
                ss = hd['ss']
                if bias_for_head is not None:
                    ss[0] = ss[0] + bias_for_head(hd['head'])
                m = ss[0].max(axis=-1, keepdims=True)
                for s in ss[1:]:
                    m = jnp.maximum(m, s.max(axis=-1, keepdims=True))
                ps = [jnp.exp(s - m) for s in ss]
                l = ps[0].sum(axis=-1, keepdims=True)
                for p in ps[1:]:
                    l = l + p.sum(axis=-1, keepdims=True)
                hd['ps'] = [p.astype(BF16) for p in ps]
                hd['l'] = l
            for hd in heads:
                acc = _dot(hd['ps'][0], hd['vs'][0])
                for p, vv in zip(hd['ps'][1:], hd['vs'][1:]):
                    acc = acc + _dot(p, vv)
                hd['o'] = acc / hd['l']
            for i, pp in enumerate(range(pp0, pp0 + NA_STAGE_PAIRS)):
                lanes = slice(pp * 2 * HD_A, (pp + 1) * 2 * HD_A)
                o_ref[0, q_rows, lanes] = jnp.where(lo_half, heads[2 * i]['o'], heads[2 * i + 1]['o']).astype(BF16)

    ctx_rows = pl.ds(0, CTX_LEN)

    @pl.when(j == 0)
    def _():
        attend(ctx_rows, CTX_LEN, [ctx_rows], None)

    @pl.when(j > 0)
    def _():
        r = (j - 1) * NA_QROWS
        u0 = jnp.clip(r - WIN_H // 2, 0, GRID_ROWS - NA_KROWS)
        q_rows = pl.ds(pl.multiple_of(CTX_LEN + r * GRID_W, NA_TQ), NA_TQ)
        k_rows = pl.ds(pl.multiple_of(CTX_LEN + u0 * GRID_W, GRID_W), NA_TK)
        attend(q_rows, NA_TQ, [k_rows, ctx_rows], lambda head: bias_ref[0, head])


def _na_call(aq, ak, av, bias):
    B = aq.shape[0]
    full = pl.BlockSpec((1, T_ALL, D_A), lambda b, j: (b, 0, 0))
    return pl.pallas_call(
        _na_kernel,
        grid=(B, 1 + GRID_ROWS // NA_QROWS),
        in_specs=[full, full, full,
                  pl.BlockSpec((1, H_A, NA_TQ, NA_TK), lambda b, j: (_na_case(j), 0, 0, 0))],
        out_specs=full,
        out_shape=jax.ShapeDtypeStruct((B, T_ALL, D_A), BF16),
        compiler_params=_cparams(("arbitrary", "arbitrary")),
        name="nbr_attn",
    )(aq, ak, av, bias)


def _route(logits_t, rb_col):
    sc = _sigmoid(logits_t)
    sel = sc + rb_col
    selr = [sel[e:e + 1, :] for e in range(N_EXPERTS)]
    scr = [sc[e:e + 1, :] for e in range(N_EXPERTS)]
    gscore = []
    for g in range(N_GROUPS):
        a, b, c, d = selr[EXP_PER_GROUP * g:EXP_PER_GROUP * (g + 1)]
        s1, t1 = jnp.maximum(a, b), jnp.minimum(a, b)
        s2, t2 = jnp.maximum(c, d), jnp.minimum(c, d)
        gscore.append(jnp.maximum(s1, s2) + jnp.maximum(jnp.minimum(s1, s2), jnp.maximum(t1, t2)))
    best = gscore[0]
    gi = jnp.zeros(best.shape, I32)
    for g in range(1, N_GROUPS):
        better = gscore[g] > best
        gi = jnp.where(better, g, gi)
        best = jnp.where(better, gscore[g], best)
    vs, ws = [], []
    for k in range(EXP_PER_GROUP):
        v = selr[k]
        w = scr[k]
        for g in range(1, N_GROUPS):
            v = jnp.where(gi == g, selr[EXP_PER_GROUP * g + k], v)
            w = jnp.where(gi == g, scr[EXP_PER_GROUP * g + k], w)
        vs.append(v)
        ws.append(w)
    b1, i1 = vs[0], jnp.zeros(best.shape, I32)
    for k in range(1, EXP_PER_GROUP):
        better = vs[k] > b1
        i1 = jnp.where(better, k, i1)
        b1 = jnp.where(better, vs[k], b1)
    b2 = jnp.full(best.shape, -jnp.inf, F32)
    i2 = jnp.zeros(best.shape, I32)
    for k in range(EXP_PER_GROUP):
        vk = jnp.where(i1 == k, -jnp.inf, vs[k])
        better = vk > b2
        i2 = jnp.where(better, k, i2)
        b2 = jnp.where(better, vk, b2)
    w1 = ws[0]
    w2 = ws[0]
    for k in range(1, EXP_PER_GROUP):
        w1 = jnp.where(i1 == k, ws[k], w1)
        w2 = jnp.where(i2 == k, ws[k], w2)
    tot = w1 + w2
    return gi * EXP_PER_GROUP + i1, gi * EXP_PER_GROUP + i2, w1 / tot, w2 / tot


def _outproj_kernel(n_act, has_ctx, *refs):
    acts = refs[:n_act]
    ws = refs[n_act:2 * n_act]
    refs = refs[2 * n_act:]
    g_ref, rwh_ref, rwl_ref, xo_ref, hp_ref, lg_ref = refs[-6:]
    ns = xo_ref.shape[0]
    x_ins, (mods,) = _stream_tiles(has_ctx, has_ctx, refs[:-6], ns)
    o = _dot(acts[0][...].reshape(ns * TILE, -1), ws[0][...])
    for a, w in zip(acts[1:], ws[1:]):
        o = o + _dot(a[...].reshape(ns * TILE, -1), w[...])
    hs = []
    for s in range(ns):
        x = x_ins[s] + mods[s][2:3] * o[s * TILE:(s + 1) * TILE]
        xo_ref[s] = x
        h = _rms(x) * g_ref[...] * (1.0 + mods[s][4:5]) + mods[s][3:4]
        _store_token_tiles(hp_ref, (s,), 0, h)
        hs.append(_split_bf16(h, 2))
    for s, (h_hi, h_lo) in enumerate(hs):
        lg_ref[s] = (_dot_nt(rwh_ref[...], h_hi) + _dot_nt(rwh_ref[...], h_lo)
                     + _dot_nt(rwl_ref[...], h_hi))


def _outproj_call(acts, ws, x_src, x_off, modl, norm_g, rwh, rwl, n_tiles, ctx_src=None):
    B = x_src.shape[0]
    ns = math.gcd(B, PROJ_SAMPLES)
    n_act = len(acts)
    T = n_tiles * TILE
    const2 = lambda b, j: (0, 0)
    tok = lambda b, j: (b, j, 0)
    in_specs = [pl.BlockSpec((ns, TILE, a.shape[2]), tok) for a in acts]
    in_specs += [pl.BlockSpec(w.shape, const2) for w in ws]
    stream_specs, streams = _stream_specs(ns, x_src, x_off, ctx_src, ctx_src is not None, [modl])
    in_specs += stream_specs + [
        pl.BlockSpec((1, D_MODEL), const2),
        pl.BlockSpec(rwh.shape, const2),
        pl.BlockSpec(rwl.shape, const2),
    ]
    out_shape = [
        jax.ShapeDtypeStruct((B, T, D_MODEL), F32),
        jax.ShapeDtypeStruct((B, T * SUBLANES, 128), F32),
        jax.ShapeDtypeStruct((B, N_EXPERTS, T), F32),
    ]
    out_specs = [
        pl.BlockSpec((ns, TILE, D_MODEL), tok),
        pl.BlockSpec((ns, TILE * SUBLANES, 128), tok),
        pl.BlockSpec((ns, N_EXPERTS, TILE), lambda b, j: (b, 0, j)),
    ]
    return pl.pallas_call(
        functools.partial(_outproj_kernel, n_act, ctx_src is not None),
        grid=(B // ns, n_tiles),
        in_specs=in_specs,
        out_specs=out_specs,
        out_shape=out_shape,
        compiler_params=_cparams(("arbitrary", "arbitrary")),
        name="outproj",
    )(*acts, *ws, *streams, norm_g, rwh, rwl)


def _route_kernel(lg_ref, rb_ref, su_ref, ri_ref, rw_ref, cnt_ref):
    logits_t = lg_ref[0]
    T = logits_t.shape[1]
    e1, e2, w1, w2 = _route(logits_t, rb_ref[...])
    eidx = lax.broadcasted_iota(I32, logits_t.shape, 0)
    oh1 = eidx == e1
    oh2 = eidx == e2
    onehot = jnp.where(oh1, 1.0, jnp.where(oh2, 1.0, 0.0))
    count = jnp.zeros((N_EXPERTS, 1), F32)
    ranks = []
    for jj in range(T // TILE):
        oh = onehot[:, jj * TILE:(jj + 1) * TILE]
        ranks.append(_dot(oh.astype(BF16), su_ref[...]) + count)
        count = count + jnp.sum(oh, axis=1, keepdims=True)
    cpad = jnp.floor((count + (SUBLANES - 1.0)) * (1.0 / SUBLANES)) * SUBLANES
    ecol = lax.broadcasted_iota(I32, (N_EXPERTS, 1), 0)
    start = jnp.zeros((N_EXPERTS, 1), F32)
    for e in range(N_EXPERTS - 1):
        start = start + jnp.where(ecol > e, cpad[e:e + 1, :], 0.0)
    row = jnp.concatenate(ranks, axis=1) + start
    r1 = jnp.sum(jnp.where(oh1, row, 0.0), axis=0, keepdims=True)
    r2 = jnp.sum(jnp.where(oh2, row, 0.0), axis=0, keepdims=True)
    zi = jnp.zeros((SUBLANES - 4, T), I32)
    ri_ref[0] = jnp.concatenate([e1, e2, r1.astype(I32), r2.astype(I32), zi], axis=0)
    zf = jnp.zeros((SUBLANES - 2, T), F32)
    rw_ref[0] = jnp.concatenate([w1, w2, zf], axis=0)
    cnt_ref[0] = jnp.broadcast_to(count, (N_EXPERTS, 128)).astype(I32)


def _route_call(lg, rb, su):
    B, _, T = lg.shape
    per_sample = lambda b: (b, 0, 0)
    return pl.pallas_call(
        _route_kernel,
        grid=(B,),
        in_specs=[pl.BlockSpec((1, N_EXPERTS, T), per_sample),
                  pl.BlockSpec(rb.shape, lambda b: (0, 0)),
                  pl.BlockSpec(su.shape, lambda b: (0, 0))],
        out_specs=[pl.BlockSpec((1, SUBLANES, T), per_sample),
                   pl.BlockSpec((1, SUBLANES, T), per_sample),
                   pl.BlockSpec((1, N_EXPERTS, 128), per_sample)],
        out_shape=[jax.ShapeDtypeStruct((B, SUBLANES, T), I32),
                   jax.ShapeDtypeStruct((B, SUBLANES, T), F32),
                   jax.ShapeDtypeStruct((B, N_EXPERTS, 128), I32)],
        compiler_params=_cparams(("arbitrary",)),
        name="route",
    )(lg, rb, su)


def _moe_kernel(T, n_rows, cnt_ref, ri_ref, rw_ref, h_ref, w1_ref, w3_ref, w2_ref, y_ref,
                xb_ref, ob_ref, tokl_ref, off_ref):
    b = pl.program_id(0)
    e = pl.program_id(1)

    def tile_rows(row):
        return pl.ds(pl.multiple_of(row * SUBLANES, SUBLANES), SUBLANES)

    @pl.when(e == 0)
    def _():
        off_ref[0] = 0
        for i in range(N_EXPERTS):
            c = cnt_ref[b, i]
            off_ref[i + 1] = off_ref[i] + ((c + SUBLANES - 1) // SUBLANES) * SUBLANES

        for i in range(N_EXPERTS):
            for k in range(SUBLANES):
                tokl_ref[jnp.maximum(off_ref[i + 1] - SUBLANES + k, 0)] = 0

        def clear(i, carry):
            tokl_ref[off_ref[N_EXPERTS] + i] = 0
            return carry

        lax.fori_loop(0, MOE_RB, clear, 0, unroll=8)

        def place(t, carry):
            tokl_ref[ri_ref[2 * T + t]] = t
            tokl_ref[ri_ref[3 * T + t]] = t
            return carry

        lax.fori_loop(0, T, place, 0, unroll=8)

        def gather(i, carry):
            xb_ref[tile_rows(i), :] = h_ref[0, tile_rows(tokl_ref[i]), :]
            return carry

        lax.fori_loop(0, MOE_RB, gather, 0, unroll=8)

    c = cnt_ref[b, e]
    base = off_ref[e]
    next_base = off_ref[e + 1]

    def block(p0, n_blk_rows, p_next):
        xb = _load_token_tiles(xb_ref, (), 0, n_blk_rows).astype(BF16)
        for i in range(MOE_RB):
            xb_ref[i * SUBLANES:(i + 1) * SUBLANES, :] = h_ref[0, tile_rows(tokl_ref[p_next + i]), :]
        h1 = _dot(xb, w1_ref[0, 0])
        h3 = _dot(xb, w3_ref[0, 0])
        act = (h1 * _sigmoid(h1)) * h3
        _store_token_tiles(ob_ref, (), p0, _dot(act.astype(BF16), w2_ref[0, 0]))

    rows_up = ((c + MOE_RB_TAIL - 1) // MOE_RB_TAIL) * MOE_RB_TAIL
    n_big = rows_up // MOE_RB
    tail = rows_up - n_big * MOE_RB

    def big_block(rb, carry):
        p0 = pl.multiple_of(base + rb * MOE_RB, SUBLANES)
        last = jnp.logical_and(rb == n_big - 1, tail == 0)
        block(p0, MOE_RB, jnp.where(last, next_base, p0 + MOE_RB))
        return carry

    lax.fori_loop(0, n_big, big_block, 0)

    for tail_rows in range(MOE_RB_TAIL, MOE_RB, MOE_RB_TAIL):
        @pl.when(tail == tail_rows)
        def _():
            block(pl.multiple_of(base + n_big * MOE_RB, SUBLANES), tail_rows, next_base)

    @pl.when(e == N_EXPERTS - 1)
    def _():
        def combine(t, carry):
            y_ref[0, tile_rows(t), :] = (rw_ref[t] * ob_ref[tile_rows(ri_ref[2 * T + t]), :]
                                         + rw_ref[T + t] * ob_ref[tile_rows(ri_ref[3 * T + t]), :])
            return carry

        lax.fori_loop(0, T, combine, 0, unroll=8)


def _moe_call(cnt, ri, rw, h, w1, w3, w2, layer):
    B = h.shape[0]
    T = h.shape[1] // SUBLANES
    n_rows = 2 * T + N_EXPERTS * SUBLANES + MOE_RB
    smem = functools.partial(pl.BlockSpec, memory_space=pltpu.SMEM)
    once = pl.Buffered(1)
    return pl.pallas_call(
        functools.partial(_moe_kernel, T, n_rows),
        grid=(B, N_EXPERTS),
        in_specs=[
            smem(cnt.shape, lambda b, e: (0, 0)),
            smem((4 * T,), lambda b, e: (b,)),
            smem((4 * T,), lambda b, e: (b,)),
            pl.BlockSpec((1, T * SUBLANES, 128), lambda b, e: (b, 0, 0), pipeline_mode=once),
            pl.BlockSpec((1, 1, D_MODEL, D_FF), lambda b, e: (layer, e, 0, 0)),
            pl.BlockSpec((1, 1, D_MODEL, D_FF), lambda b, e: (layer, e, 0, 0)),
            pl.BlockSpec((1, 1, D_FF, D_MODEL), lambda b, e: (layer, e, 0, 0)),
        ],
        out_specs=pl.BlockSpec((1, T * SUBLANES, 128), lambda b, e: (b, 0, 0), pipeline_mode=once),
        out_shape=jax.ShapeDtypeStruct((B, T * SUBLANES, 128), F32),
        scratch_shapes=[
            pltpu.VMEM((MOE_RB * SUBLANES, 128), F32),
            pltpu.VMEM((n_rows * SUBLANES, 128), F32),
            pltpu.SMEM((n_rows,), I32),
            pltpu.SMEM((N_EXPERTS + 1,), I32),
        ],
        compiler_params=_cparams(("arbitrary", "arbitrary")),
        name="moe",
    )(cnt, ri, rw, h, w1, w3, w2)


def _route_tables(ri, rw, cnt):
    return cnt[:, :, 0], ri[:, :4].reshape(-1), rw[:, :4].reshape(-1)


def _inproj1_kernel(*refs):
    (y_ref, g_ref, w_ref, qg_ref, kg_ref, cos_ref, sin_ref, xo_ref, q_ref, k_ref, v_ref) = refs[-11:]
    ns = xo_ref.shape[0]
    x_ins, (mods0, mods) = _stream_tiles(False, True, refs[:-11], ns)
    hbs = []
    for s in range(ns):
        x = x_ins[s] + mods0[s][5:6] * _load_token_tiles(y_ref, (s,), 0, TILE)
        xo_ref[s] = x
        hbs.append((_rms(x) * g_ref[...] * (1.0 + mods[s][1:2]) + mods[s][0:1]).astype(BF16))
    hb = jnp.concatenate(hbs, axis=0)
    cos = cos_ref[...]
    sin = sin_ref[...]

    def rope_heads(ref, proj, n_heads, gain, scale):
        for s in range(ns):
            for h in range(n_heads):
                sl = slice(h * HD_C, (h + 1) * HD_C)
                n = _rms(proj[s * TILE:(s + 1) * TILE, sl]) * gain
                r = n * cos + pltpu.roll(n, HD_C // 2, 1) * sin
                ref[s, :, sl] = (r if scale is None else r * scale).astype(BF16)

    ko = H_C * HD_C
    vo = (H_C + KV_C) * HD_C
    qkv = _dot(hb, w_ref[...])
    qs, kv = qkv[:, :ko], qkv[:, ko:]
    rope_heads(k_ref, kv, KV_C, kg_ref[...], None)
    for s in range(ns):
        v_ref[s] = kv[s * TILE:(s + 1) * TILE, KV_C * HD_C:].astype(BF16)
    rope_heads(q_ref, qs, H_C, qg_ref[...], HD_C ** -0.5)


def _inproj1_call(x1, y0, mod0, mod1, norm_g, w, qg, kg, cos, sin):
    B = x1.shape[0]
    ns = math.gcd(B, INPROJ1_SAMPLES)
    tok = lambda b, j: (b, j, 0)
    const2 = lambda b, j: (0, 0)
    lat_tok = lambda b, j: (b, jnp.maximum(j - NT_CTX, 0), 0)
    widths = [(T_ALL, D_MODEL, F32), (SEQ, H_C * HD_C, BF16),
              (T_ALL, KV_C * HD_C, BF16), (T_ALL, KV_C * HD_C, BF16)]
    stream_specs, streams = _stream_specs(ns, x1, 0, None, True, [mod0, mod1])
    return pl.pallas_call(
        _inproj1_kernel,
        grid=(B // ns, NT_ALL),
        in_specs=stream_specs + [
            pl.BlockSpec((ns, TILE * SUBLANES, 128), tok),
            pl.BlockSpec((1, D_MODEL), const2),
            pl.BlockSpec(w.shape, const2),
            pl.BlockSpec(qg.shape, const2),
            pl.BlockSpec(kg.shape, const2),
            pl.BlockSpec((TILE, HD_C), lambda b, j: (j, 0)),
            pl.BlockSpec((TILE, HD_C), lambda b, j: (j, 0)),
        ],
        out_specs=[pl.BlockSpec((ns, TILE, w_), lat_tok if t == SEQ else tok) for t, w_, _ in widths],
        out_shape=[jax.ShapeDtypeStruct((B, t, w_), dt) for t, w_, dt in widths],
        compiler_params=_cparams(("arbitrary", "arbitrary")),
        name="inproj1",
    )(*streams, y0, norm_g, w, qg, kg, cos, sin)


def _gqa_kernel(q_ref, k_ref, v_ref, o_ref):
    k = k_ref[0]
    v = v_ref[0]
    for h0 in range(0, H_C // KV_C, GQA_STAGE_HEADS):
        sls = [slice(h * HD_C, (h + 1) * HD_C) for h in range(h0, h0 + GQA_STAGE_HEADS)]
        ss = [_dot_nt(q_ref[0, :, sl], k) for sl in sls]
        ps, ls = [], []
        for s in ss:
            p = jnp.exp(s - s.max(axis=-1, keepdims=True))
            ls.append(p.sum(axis=-1, keepdims=True))
            ps.append(p.astype(BF16))
        for sl, p, l in zip(sls, ps, ls):
            o_ref[0, :, sl] = (_dot(p, v) / l).astype(BF16)


def _gqa_call(q, k, v):
    B = q.shape[0]
    gw = (H_C // KV_C) * HD_C
    return pl.pallas_call(
        _gqa_kernel,
        grid=(B, KV_C, SEQ // GQA_TQ),
        in_specs=[
            pl.BlockSpec((1, GQA_TQ, gw), lambda b, g, j: (b, j, g)),
            pl.BlockSpec((1, T_ALL, HD_C), lambda b, g, j: (b, 0, g)),
            pl.BlockSpec((1, T_ALL, HD_C), lambda b, g, j: (b, 0, g)),
        ],
        out_specs=pl.BlockSpec((1, GQA_TQ, gw), lambda b, g, j: (b, j, g)),
        out_shape=jax.ShapeDtypeStruct((B, SEQ, H_C * HD_C), BF16),
        compiler_params=_cparams(("arbitrary", "arbitrary", "arbitrary")),
        name="gqa",
    )(q, k, v)


def _final_kernel(x_ref, y_ref, mod_ref, o_ref):
    o_ref[0] = x_ref[0] + mod_ref[0][5:6] * _load_token_tiles(y_ref, (0,), 0, FINAL_TILE)


def _final_call(x, y, modl):
    B, T, D = x.shape
    tok = lambda b, j: (b, j, 0)
    return pl.pallas_call(
        _final_kernel,
        grid=(B, T // FINAL_TILE),
        in_specs=[pl.BlockSpec((1, FINAL_TILE, D), tok), pl.BlockSpec((1, FINAL_TILE * SUBLANES, 128), tok),
                  pl.BlockSpec((1, 6, D), lambda b, j: (b, 0, 0))],
        out_specs=pl.BlockSpec((1, FINAL_TILE, D), tok),
        out_shape=jax.ShapeDtypeStruct((B, T, D), F32),
        compiler_params=_cparams(("arbitrary", "arbitrary")),
        name="final_residual",
    )(x, y, modl)


def _chunk_tri(lower):
    i = np.arange(TILE)
    same = (i[:, None] // MLSTM_CHUNK) == (i[None, :] // MLSTM_CHUNK)
    tri = (i[None, :] <= i[:, None]) if lower else (i[None, :] >= i[:, None])
    return jnp.asarray((same & tri).astype(np.float32), BF16)


def _rope_tables():
    n_freq = HD_C // 4
    inv_freq = ROPE_THETA ** (-jnp.arange(n_freq, dtype=F32) / n_freq)
    t = jnp.arange(SEQ)
    rows = (t // GRID_W).astype(F32)
    cols = (t % GRID_W).astype(F32)
    ang = jnp.concatenate([rows[:, None] * inv_freq, cols[:, None] * inv_freq], axis=-1)
    cos, sin = jnp.cos(ang), jnp.sin(ang)
    cos_l = jnp.concatenate([cos, cos], axis=-1)
    sin_l = jnp.concatenate([-sin, sin], axis=-1)
    cos_all = jnp.concatenate([jnp.ones((CTX_LEN, HD_C), F32), cos_l], axis=0)
    sin_all = jnp.concatenate([jnp.zeros((CTX_LEN, HD_C), F32), sin_l], axis=0)
    return cos_all, sin_all


_HEAD_PERM = np.concatenate([np.arange(0, HD_C, 2), np.arange(1, HD_C, 2)])


def kernel(x, c, ctx, c_ctx, ada_w, ada_b, norm_mix_g, norm_ffn_g, even_w_in, even_w_out,
           na_q_norm_g, na_k_norm_g, na_rpb, mlstm_gate_b, mlstm_norm_g, odd_w_in, odd_w_out,
           gqa_q_norm_g, gqa_k_norm_g, router_w, router_b, exp_w1, exp_w3, exp_w2):
    B = x.shape[0]
    assert B <= N_MOD_CTX_ROW and x.shape[1:] == (SEQ, D_MODEL) and ctx.shape[1:] == (CTX_LEN, D_MODEL)
    n_g = 4 * H_B

    cvec = jnp.zeros((N_MOD_ROWS, D_MODEL), F32).at[:B].set(c).at[N_MOD_CTX_ROW].set(c_ctx)
    mod = _ada_call(cvec, ada_w, ada_b).reshape(2, N_MOD_ROWS, 6, D_MODEL)
    mod0, mod1 = mod[0], mod[1]

    w_in = even_w_in[0]
    n_main = w_in.shape[1] - n_g
    wm = w_in[:, :n_main].astype(BF16)
    wg_f = w_in[:, n_main:]
    wg = jnp.pad(wg_f, ((0, 0), (0, 128 - n_g))).astype(BF16)
    wgt = wg_f.T.astype(BF16)
    gb = jnp.pad(mlstm_gate_b[0].reshape(1, n_g), ((0, 0), (0, 128 - n_g)))
    gbt = mlstm_gate_b[0].reshape(n_g, 1)
    qg = jnp.tile(na_q_norm_g[0], H_A).reshape(1, D_A)
    kg = jnp.tile(na_k_norm_g[0], H_A).reshape(1, D_A)
    hid = np.arange(128) // HD_A
    bd = jnp.asarray((hid[:, None] == hid[None, :]).astype(np.float32), BF16)
    k_lo = 3 * D_A + H_B * DK_B
    wkt = w_in[:, k_lo:k_lo + H_B * DK_B].T.astype(BF16)
    aq, ak, av, mq, mv, og, mkt, gc, gr = _inproj0_call(
        x, ctx, mod0, norm_mix_g[0].reshape(1, D_MODEL), wm, wkt, wg, wgt, gb, gbt, qg, kg, bd,
        _chunk_tri(True), _chunk_tri(False))
    hm = _mlstm_call(mq, mkt, mv, gc, gr, og, mlstm_norm_g[0].reshape(1, H_B * DV_B))
    oa = _na_call(aq, ak, av, _na_bias_tables(na_rpb[0]))

    rw_t = router_w.T
    rwh = rw_t.astype(BF16)
    rwl = (rw_t - rwh.astype(F32)).astype(BF16)
    rb = router_b.reshape(N_EXPERTS, 1).astype(F32)
    i = np.arange(TILE)
    su = jnp.asarray((i[:, None] < i[None, :]).astype(np.float32), BF16)
    w_out = even_w_out[0].astype(BF16)
    ew1, ew3, ew2 = exp_w1.astype(BF16), exp_w3.astype(BF16), exp_w2.astype(BF16)
    x1, hp0, lg0 = _outproj_call(
        [oa, hm], [w_out[:D_A], w_out[D_A:]], x, 0, mod0,
        norm_ffn_g[0].reshape(1, D_MODEL), rwh, rwl, NT_ALL, ctx_src=ctx)
    y0 = _moe_call(*_route_tables(*_route_call(lg0, rb, su)), hp0, ew1, ew3, ew2, 0)

    w1_in = odd_w_in[0]
    qk_cols = np.concatenate([h * HD_C + _HEAD_PERM for h in range(H_C + KV_C)])
    cols = np.concatenate([qk_cols, np.arange((H_C + KV_C) * HD_C, w1_in.shape[1])])
    w1_in = w1_in[:, cols].astype(BF16)
    cos_all, sin_all = _rope_tables()
    x2, q, k, v = _inproj1_call(
        x1, y0, mod0, mod1, norm_mix_g[1].reshape(1, D_MODEL), w1_in,
        gqa_q_norm_g[0][_HEAD_PERM].reshape(1, HD_C), gqa_k_norm_g[0][_HEAD_PERM].reshape(1, HD_C),
        cos_all, sin_all)
    o = _gqa_call(q, k, v)
    x3, hp1, lg1 = _outproj_call(
        [o], [odd_w_out[0].astype(BF16)], x2, NT_CTX, mod1,
        norm_ffn_g[1].reshape(1, D_MODEL), rwh, rwl, SEQ // TILE)
    y1 = _moe_call(*_route_tables(*_route_call(lg1, rb, su)), hp1, ew1, ew3, ew2, 1)
    return _final_call(x3, y1, mod1)
```

```python
import functools
import math

import numpy as np
import jax
import jax.numpy as jnp
from jax import lax
from jax.experimental import pallas as pl
from jax.experimental.pallas import tpu as pltpu

F32 = jnp.float32
BF16 = jnp.bfloat16
I32 = jnp.int32
U32 = jnp.uint32

D_MODEL = 1024
SEQ = 2048
GRID_W = 64
GRID_ROWS = SEQ // GRID_W
CTX_LEN = 256
T_ALL = CTX_LEN + SEQ
WIN_H = 8
WIN_W = 16
HD_A = 64
H_A = 8
D_A = H_A * HD_A
H_B = 4
DV_B = 128
DK_B = 64
MLSTM_CHUNK = 64
N_CHUNKS = T_ALL // MLSTM_CHUNK
N_CTX_CHUNKS = CTX_LEN // MLSTM_CHUNK
HD_C = 128
H_C = 8
KV_C = 2
ROPE_THETA = 10000.0
N_EXPERTS = 16
N_GROUPS = 4
EXP_PER_GROUP = 4
D_FF = 512
EPS = 1e-6
NEG_BIG = -1e30

TILE = 256
NT_ALL = T_ALL // TILE
NT_CTX = CTX_LEN // TILE
NA_QROWS = 4
NA_KROWS = WIN_H + NA_QROWS - 1
NA_TQ = NA_QROWS * GRID_W
NA_TK = NA_KROWS * GRID_W
NA_STAGE_PAIRS = 2


def _na_geometry(r):
    u0 = min(max(r - WIN_H // 2, 0), GRID_ROWS - NA_KROWS)
    r0s = tuple(min(max(r + qi - WIN_H // 2, 0), GRID_ROWS - WIN_H) - r for qi in range(NA_QROWS))
    return (u0 - r, r0s)


def _na_cases():
    reps, step_case = [], []
    for r in range(0, GRID_ROWS, NA_QROWS):
        geo = _na_geometry(r)
        known = [_na_geometry(q) for q in reps]
        if geo not in known:
            reps.append(r)
            known.append(geo)
        step_case.append(known.index(geo))
    return tuple(reps), tuple(step_case)


NA_CASES, NA_STEP_CASE = _na_cases()
PROJ_SAMPLES = 4
INPROJ0_SAMPLES = 4
INPROJ1_SAMPLES = 1
GQA_TQ = 512
GQA_STAGE_HEADS = 2
FINAL_TILE = 1024
MOE_RB = 256
MOE_RB_TAIL = 64
SUBLANES = 8
VMEM_LIMIT = 56 * 1024 * 1024


def _cparams(sem):
    return pltpu.CompilerParams(dimension_semantics=sem, vmem_limit_bytes=VMEM_LIMIT)


def _sigmoid(x):
    return 1.0 / (1.0 + jnp.exp(-x))


def _rms(x):
    return x * lax.rsqrt(jnp.mean(x * x, axis=-1, keepdims=True) + EPS)


def _dot(a, b):
    return jnp.dot(a, b, preferred_element_type=F32)


def _dot_nt(a, b):
    return lax.dot_general(a, b, (((1,), (1,)), ((), ())), preferred_element_type=F32)


def _dot_tn(a, b):
    return lax.dot_general(a, b, (((0,), (0,)), ((), ())), preferred_element_type=F32)


def _split_bf16(x, n):
    parts = []
    r = x
    for _ in range(n):
        p = r.astype(BF16)
        parts.append(p)
        r = r - p.astype(F32)
    return parts


LANE_TILES = D_MODEL // 128


def _load_token_tiles(ref, lead, tok0, n_tok):
    parts = [ref[(*lead, pl.ds(tok0 * SUBLANES + c, n_tok, stride=SUBLANES), slice(None))]
             for c in range(LANE_TILES)]
    return jnp.concatenate(parts, axis=1)


def _store_token_tiles(ref, lead, tok0, val):
    for c in range(LANE_TILES):
        ref[(*lead, pl.ds(tok0 * SUBLANES + c, val.shape[0], stride=SUBLANES), slice(None))] = (
            val[:, c * 128:(c + 1) * 128])


def _stream_specs(ns, x_src, x_off, ctx_src, ctx_tile, mod_tables):
    if ctx_src is None:
        specs = [pl.BlockSpec((ns, TILE, D_MODEL), lambda b, j: (b, j + x_off, 0))]
        arrays = [x_src]
    else:
        assert x_off == 0 and ctx_tile
        specs = [pl.BlockSpec((ns, TILE, D_MODEL), lambda b, j: (b, jnp.maximum(j - NT_CTX, 0), 0)),
                 pl.BlockSpec((ns, TILE, D_MODEL), lambda b, j: (b, jnp.minimum(j, NT_CTX - 1), 0))]
        arrays = [x_src, ctx_src]
    for table in mod_tables:
        specs.append(pl.BlockSpec((ns, 6, D_MODEL), lambda b, j: (b, 0, 0)))
        arrays.append(table)
        if ctx_tile:
            specs.append(pl.BlockSpec((1, 6, D_MODEL), lambda b, j: (N_MOD_CTX_ROW, 0, 0)))
            arrays.append(table)
    return specs, arrays


def _stream_tiles(two_streams, ctx_tile, refs, ns):
    is_ctx = pl.program_id(1) < NT_CTX
    if two_streams:
        x_ins = [jnp.where(is_ctx, refs[1][s], refs[0][s]) for s in range(ns)]
        refs = refs[2:]
    else:
        x_ins = [refs[0][s] for s in range(ns)]
        refs = refs[1:]
    tables = []
    if ctx_tile:
        for t in range(0, len(refs), 2):
            tables.append([jnp.where(is_ctx, refs[t + 1][0], refs[t][s]) for s in range(ns)])
    else:
        for ref in refs:
            tables.append([ref[s] for s in range(ns)])
    return x_ins, tables


N_MOD_ROWS = 16
N_MOD_CTX_ROW = 8


ADA_TN = 1536


def _ada_kernel(c_ref, w_ref, b_ref, o_ref):
    c = c_ref[...]
    s = (c * _sigmoid(c)).astype(BF16)
    o_ref[0] = _dot(s, w_ref[0].astype(BF16)) + b_ref[0]


def _ada_call(cvec, ada_w, ada_b):
    depth, d, n = ada_w.shape
    return pl.pallas_call(
        _ada_kernel,
        grid=(depth, n // ADA_TN),
        in_specs=[
            pl.BlockSpec((N_MOD_ROWS, d), lambda l, j: (0, 0)),
            pl.BlockSpec((1, d, ADA_TN), lambda l, j: (l, 0, j)),
            pl.BlockSpec((1, 1, ADA_TN), lambda l, j: (l, 0, j)),
        ],
        out_specs=pl.BlockSpec((1, N_MOD_ROWS, ADA_TN), lambda l, j: (l, 0, j)),
        out_shape=jax.ShapeDtypeStruct((depth, N_MOD_ROWS, n), F32),
        compiler_params=_cparams(("arbitrary", "arbitrary")),
        name="ada_mod",
    )(cvec, ada_w, ada_b.reshape(depth, 1, n))


def _log_sigmoid(x):
    return jnp.minimum(x, 0.0) - jnp.log1p(jnp.exp(-jnp.abs(x)))


def _inproj0_kernel(*refs):
    (g_ref, wm_ref, wkt_ref, wg_ref, wgt_ref, gb_ref, gbt_ref, qg_ref, kg_ref, bd_ref, tril_ref, triu_ref,
     aq_ref, ak_ref, av_ref, mq_ref, mv_ref, og_ref, mkt_ref, gc_ref, gr_ref) = refs[-21:]
    ns = aq_ref.shape[0]
    x_ins, (mods,) = _stream_tiles(True, True, refs[:-21], ns)
    hb = jnp.concatenate([(_rms(x_ins[s]) * g_ref[...] * (1.0 + mods[s][1:2]) + mods[s][0:1]).astype(BF16)
                          for s in range(ns)], axis=0)

    def per_sample(ref, val):
        for s in range(ns):
            ref[s] = val[s * TILE:(s + 1) * TILE]

    def proj(lo, hi):
        return _dot(hb, wm_ref[:, lo:hi])

    def head_norm(a, gain):
        a2 = a * a
        sums = []
        for s0 in range(0, D_A, 128):
            hi_, lo_ = _split_bf16(a2[:, s0:s0 + 128], 2)
            sums.append(_dot(hi_, bd_ref[...]) + _dot(lo_, bd_ref[...]))
        ss = jnp.concatenate(sums, axis=1)
        return a * lax.rsqrt(ss * (1.0 / HD_A) + EPS) * gain

    n_g = 4 * H_B
    p_aq, p_ak, p_av = proj(0, 512), proj(512, 1024), proj(1024, 1536)
    p_mq, p_mv, p_og = proj(1536, 1792), proj(2048, 2560), proj(2560, 3072)
    kt = _dot_nt(wkt_ref[...], hb).astype(BF16)
    gcol = _dot(hb, wg_ref[...]) + gb_ref[...]
    grow = _dot_nt(wgt_ref[...], hb) + gbt_ref[...]

    per_sample(aq_ref, (head_norm(p_aq, qg_ref[...]) * HD_A ** -0.5).astype(BF16))
    per_sample(ak_ref, head_norm(p_ak, kg_ref[...]).astype(BF16))
    per_sample(av_ref, p_av.astype(BF16))
    per_sample(mq_ref, (p_mq * DK_B ** -0.5).astype(BF16))
    per_sample(mv_ref, p_mv.astype(BF16))
    per_sample(og_ref, _sigmoid(p_og).astype(BF16))

    tril = tril_ref[...]
    triu = triu_ref[...]
    ls_c = _split_bf16(_log_sigmoid(gcol), 3)
    ls_r = _split_bf16(_log_sigmoid(grow), 3)
    cidx = lax.broadcasted_iota(I32, (TILE, 128), 1)
    ridx = lax.broadcasted_iota(I32, (n_g, TILE), 0)

    def pick(idx, raw, pre, suf):
        is_ff = (idx >= H_B) & (idx < 2 * H_B)
        is_fb = idx >= 3 * H_B
        return jnp.where(is_ff, pre, jnp.where(is_fb, suf, raw))

    for s in range(ns):
        tok = slice(s * TILE, (s + 1) * TILE)
        pre_c = sum(_dot(tril, p[tok, :]) for p in ls_c)
        suf_c = sum(_dot(triu, p[tok, :]) for p in ls_c)
        pre_r = sum(_dot(p[:, tok], triu) for p in ls_r)
        suf_r = sum(_dot(p[:, tok], tril) for p in ls_r)
        gc_ref[s] = pick(cidx, gcol[tok, :], pre_c, suf_c)[:, :n_g]
        grow_s = pick(ridx, grow[:, tok], pre_r, suf_r)
        for c in range(TILE // MLSTM_CHUNK):
            chunk = slice(c * MLSTM_CHUNK, (c + 1) * MLSTM_CHUNK)
            gr_ref[s, c] = grow_s[:, chunk]
            mkt_ref[s, c] = kt[:, s * TILE + c * MLSTM_CHUNK:s * TILE + (c + 1) * MLSTM_CHUNK]


def _inproj0_call(x, ctx, modl, norm_g, wm, wkt, wg, wgt, gb, gbt, qg, kg, bd, tril, triu):
    B = x.shape[0]
    ns = math.gcd(B, INPROJ0_SAMPLES)
    n_g = 4 * H_B
    tok = lambda b, j: (b, j, 0)
    const2 = lambda b, j: (0, 0)
    chunked = lambda b, j: (b, j, 0, 0)
    tile_chunks = TILE // MLSTM_CHUNK
    outs = [
        (D_A, BF16), (D_A, BF16), (D_A, BF16),
        (H_B * DK_B, BF16), (H_B * DV_B, BF16),
        (H_B * DV_B, BF16),
    ]
    out_shape = [jax.ShapeDtypeStruct((B, T_ALL, w), dt) for w, dt in outs]
    out_specs = [pl.BlockSpec((ns, TILE, w), tok) for w, _ in outs]
    out_shape += [jax.ShapeDtypeStruct((B, N_CHUNKS, H_B * DK_B, MLSTM_CHUNK), BF16),
                  jax.ShapeDtypeStruct((B, T_ALL, n_g), F32),
                  jax.ShapeDtypeStruct((B, N_CHUNKS, n_g, MLSTM_CHUNK), F32)]
    out_specs += [pl.BlockSpec((ns, tile_chunks, H_B * DK_B, MLSTM_CHUNK), chunked),
                  pl.BlockSpec((ns, TILE, n_g), tok),
                  pl.BlockSpec((ns, tile_chunks, n_g, MLSTM_CHUNK), chunked)]
    stream_specs, streams = _stream_specs(ns, x, 0, ctx, True, [modl])
    return pl.pallas_call(
        _inproj0_kernel,
        grid=(B // ns, NT_ALL),
        in_specs=stream_specs + [
            pl.BlockSpec((1, D_MODEL), const2),
            pl.BlockSpec(wm.shape, const2, pipeline_mode=pl.Buffered(1)),
            pl.BlockSpec(wkt.shape, const2),
            pl.BlockSpec(wg.shape, const2),
            pl.BlockSpec(wgt.shape, const2),
            pl.BlockSpec(gb.shape, const2),
            pl.BlockSpec(gbt.shape, const2),
            pl.BlockSpec(qg.shape, const2),
            pl.BlockSpec(kg.shape, const2),
            pl.BlockSpec(bd.shape, const2),
            pl.BlockSpec(tril.shape, const2),
            pl.BlockSpec(triu.shape, const2),
        ],
        out_specs=out_specs,
        out_shape=out_shape,
        compiler_params=_cparams(("arbitrary", "arbitrary")),
        name="inproj0",
    )(*streams, norm_g, wm, wkt, wg, wgt, gb, gbt, qg, kg, bd, tril, triu)


def _mlstm_kernel(mq_ref, mkt_ref, mv_ref, gc_ref, gr_ref, og_ref, ng_ref, out_ref,
                  hf_ref, hb_ref, s_ref, *local_refs):
    L = MLSTM_CHUNK
    n_chain = 2 * H_B
    slots = (local_refs[:4], local_refs[4:])
    s_ref[...] = jnp.zeros(s_ref.shape, F32)
    ri = lax.broadcasted_iota(I32, (L, L), 0)
    ci = lax.broadcasted_iota(I32, (L, L), 1)
    masks = (ci <= ri, ci >= ri)
    ones_aug = jnp.ones((L, DV_B), BF16)

    def tile_up(x, n_rows, n_cols):
        x = jnp.concatenate([x] * n_rows, axis=0)
        return x if n_cols == 1 else jnp.concatenate([x] * n_cols, axis=1)

    def chunk_rows(step, d):
        if d == 0:
            chunk = step
        else:
            chunk = jnp.where(step < N_CTX_CHUNKS, N_CTX_CHUNKS - 1 - step, N_CHUNKS + N_CTX_CHUNKS - 1 - step)
        return chunk, pl.ds(pl.multiple_of(chunk * L, L), L)

    def local_part(step, slot, rec=None):
        intra_ref, u_ref, col_ref, rep_ref = slot
        rec = rec or (lambda: None,) * 3
        chains = []
        for d in range(2):
            chunk, rows = chunk_rows(step, d)
            gcc = gc_ref[0, rows, :]
            grr = gr_ref[0, chunk]
            for h in range(H_B):
                gi = 2 * H_B * d + h
                b_col = gcc[:, gi + H_B:gi + H_B + 1]
                chains.append(dict(
                    d=d,
                    q=mq_ref[0, rows, h * DK_B:(h + 1) * DK_B],
                    kt=mkt_ref[0, chunk, h * DK_B:(h + 1) * DK_B, :],
                    v=mv_ref[0, rows, h * DV_B:(h + 1) * DV_B],
                    ig_col=gcc[:, gi:gi + 1], b_col=b_col,
                    ig_row=grr[gi:gi + 1, :], b_row=grr[gi + H_B:gi + H_B + 1, :],
                    b_last=b_col[L - 1:L, :] if d == 0 else b_col[0:1, :]))
        for ch in chains:
            ch['qk'] = _dot(ch['q'], ch['kt'])
            ch['v_aug'] = jnp.concatenate([ch['v'], ones_aug], axis=1)
            g = ch['b_last'] - ch['b_col'] + ch['ig_col']
            ch['g_max'] = jnp.max(g, axis=0, keepdims=True)
            ch['wv'] = (jnp.exp(g - ch['g_max']) * ch['v_aug'].astype(F32)).astype(BF16)
        us = [_dot(ch['kt'], ch['wv']) for ch in chains]
        for ch in chains:
            dm = jnp.where(masks[ch['d']], ch['b_col'] - ch['b_row'] + ch['ig_row'], NEG_BIG)
            ch['m_loc'] = jnp.max(dm, axis=-1, keepdims=True)
            ch['p'] = (ch['qk'] * jnp.exp(dm - ch['m_loc'])).astype(BF16)
        intras = [_dot(ch['p'], ch['v_aug']) for ch in chains]
        rec[0]()
        cols, reps = [], []
        for ch in chains:
            cols += [jnp.broadcast_to(ch['b_col'], (L, 128)), jnp.broadcast_to(ch['m_loc'], (L, 128))]
            reps += [jnp.broadcast_to(ch['b_last'], (SUBLANES, 128)),
                     jnp.broadcast_to(ch['g_max'], (SUBLANES, 128))]
        intra_ref[...] = jnp.concatenate(intras, axis=0)
        u_ref[...] = jnp.concatenate(us, axis=0)
        col_ref[...] = jnp.concatenate(cols, axis=0)
        rep_ref[...] = jnp.concatenate(reps, axis=0)
        rec[1]()
        rec[2]()

    def recurrence_stages(step, ms, slot):
        intra_ref, u_ref, col_ref, rep_ref = slot
        n_all = range(n_chain)
        rows = [chunk_rows(step, d)[1] for d in range(2)]
        live = {}
        new_ms = []

        def products():
            live['states'] = [s_ref[c * DK_B:(c + 1) * DK_B, :] for c in n_all]
            live['inters'] = [_dot(mq_ref[0, rows[c // H_B], (c % H_B) * DK_B:(c % H_B + 1) * DK_B],
                                   live['states'][c].astype(BF16)) for c in n_all]

        def update():
            new_states = []
            for c in n_all:
                m = ms[c]
                b_last = rep_ref[2 * c * SUBLANES:(2 * c + 1) * SUBLANES, :]
                g_max = rep_ref[(2 * c + 1) * SUBLANES:(2 * c + 2) * SUBLANES, :]
                m_new = jnp.maximum(b_last + m, g_max)
                w_old = tile_up(jnp.exp(b_last + m - m_new), DK_B // SUBLANES, 2)
                w_new = tile_up(jnp.exp(g_max - m_new), DK_B // SUBLANES, 2)
                new_states.append(w_old * live['states'][c] + w_new * u_ref[c * DK_B:(c + 1) * DK_B, :])
                new_ms.append(m_new)
            s_ref[...] = jnp.concatenate(new_states, axis=0)

        def outputs():
            houts = []
            for c in n_all:
                a = col_ref[2 * c * L:(2 * c + 1) * L, :] + tile_up(ms[c], L // SUBLANES, 1)
                m_loc = col_ref[(2 * c + 1) * L:(2 * c + 2) * L, :]
                m_row = jnp.maximum(a, m_loc)
                w_inter = jnp.exp(a - m_row)
                w_loc = jnp.exp(m_loc - m_row)
                inter = live['inters'][c]
                intra = intra_ref[c * L:(c + 1) * L, :]
                num = w_inter * inter[:, :DV_B] + w_loc * intra[:, :DV_B]
                den = w_inter * inter[:, DV_B:] + w_loc * intra[:, DV_B:]
                houts.append(num / jnp.maximum(jnp.abs(den), jnp.exp(-m_row)))
            hf_ref[rows[0], :] = jnp.concatenate(houts[:H_B], axis=1)
            hb_ref[rows[1], :] = jnp.concatenate(houts[H_B:], axis=1)

        return (products, update, outputs), new_ms

    local_part(0, slots[0])

    def step_pair(i, ms):
        s0 = 2 * i
        stages, ms1 = recurrence_stages(s0, ms, slots[0])
        local_part(s0 + 1, slots[1], stages)
        stages, ms2 = recurrence_stages(s0 + 1, tuple(ms1), slots[1])
        local_part(jnp.minimum(s0 + 2, N_CHUNKS - 1), slots[0], stages)
        return tuple(ms2)

    m0 = tuple(jnp.full((SUBLANES, 128), NEG_BIG, F32) for _ in range(n_chain))
    lax.fori_loop(0, N_CHUNKS // 2, step_pair, m0)

    def finish(i, carry):
        rows = pl.ds(pl.multiple_of(i * TILE, TILE), TILE)
        hs = hf_ref[rows, :] + hb_ref[rows, :]
        ng = ng_ref[...]
        og = og_ref[0, rows, :].astype(F32)
        for h in range(H_B):
            sl = slice(h * DV_B, (h + 1) * DV_B)
            out_ref[0, rows, sl] = (_rms(hs[:, sl]) * ng[:, sl] * og[:, sl]).astype(BF16)
        return carry

    lax.fori_loop(0, T_ALL // TILE, finish, 0)


def _mlstm_call(mq, mk, mv, gc, gr, og, ng):
    B = mq.shape[0]
    full = lambda a: pl.BlockSpec((1,) + a.shape[1:], lambda b: (b,) + (0,) * (a.ndim - 1))
    return pl.pallas_call(
        _mlstm_kernel,
        grid=(B,),
        in_specs=[full(mq), full(mk), full(mv), full(gc), full(gr), full(og),
                  pl.BlockSpec(ng.shape, lambda b: (0, 0))],
        out_specs=pl.BlockSpec((1, T_ALL, H_B * DV_B), lambda b: (b, 0, 0)),
        out_shape=jax.ShapeDtypeStruct((B, T_ALL, H_B * DV_B), BF16),
        scratch_shapes=[
            pltpu.VMEM((T_ALL, H_B * DV_B), F32),
            pltpu.VMEM((T_ALL, H_B * DV_B), F32),
            pltpu.VMEM((2 * H_B * DK_B, 2 * DV_B), F32),
        ] + 2 * [
            pltpu.VMEM((2 * H_B * MLSTM_CHUNK, 2 * DV_B), F32),
            pltpu.VMEM((2 * H_B * DK_B, 2 * DV_B), F32),
            pltpu.VMEM((2 * H_B * 2 * MLSTM_CHUNK, 128), F32),
            pltpu.VMEM((2 * H_B * 2 * SUBLANES, 128), F32),
        ],
        compiler_params=_cparams(("arbitrary",)),
        name="mlstm",
    )(mq, mk, mv, gc, gr, og, ng)


def _na_bias_tables(rpb):
    kh = WIN_H
    n_drow = 2 * WIN_H - 1
    qcol = np.arange(GRID_W)
    col_start = np.clip(qcol - WIN_W // 2, 0, GRID_W - WIN_W)
    col_ok = (qcol[None, :] >= col_start[:, None]) & (qcol[None, :] < col_start[:, None] + WIN_W)
    dcol = qcol[None, :] - qcol[:, None] + (WIN_W - 1)
    onehot = (dcol[None] == np.arange(2 * WIN_W - 1)[:, None, None]) & col_ok[None]
    blocks = jnp.einsum('hdx,xck->hdck', rpb.astype(F32), jnp.asarray(onehot, F32),
                        precision=lax.Precision.HIGHEST)
    blocks = jnp.where(col_ok[None, None], blocks, NEG_BIG)
    outside = jnp.full((H_A, 1, GRID_W, GRID_W), NEG_BIG, F32)
    blocks = jnp.concatenate([blocks, outside], axis=1)
    idx = np.full((len(NA_CASES), NA_QROWS, NA_KROWS), n_drow, np.int32)
    for ci, r in enumerate(NA_CASES):
        u0 = int(np.clip(r - kh // 2, 0, GRID_ROWS - NA_KROWS))
        for qi in range(NA_QROWS):
            rq = r + qi
            r0 = int(np.clip(rq - kh // 2, 0, GRID_ROWS - kh))
            for ui in range(NA_KROWS):
                kr = u0 + ui
                if r0 <= kr < r0 + kh:
                    idx[ci, qi, ui] = kr - rq + (WIN_H - 1)

    def assemble(blk_ref, tab_ref):
        for ci in range(len(NA_CASES)):
            for qi in range(NA_QROWS):
                for ui in range(NA_KROWS):
                    tab_ref[ci, 0, qi * GRID_W:(qi + 1) * GRID_W, ui * GRID_W:(ui + 1) * GRID_W] = (
                        blk_ref[0, int(idx[ci, qi, ui])])

    return pl.pallas_call(
        assemble,
        grid=(H_A,),
        in_specs=[pl.BlockSpec((1, n_drow + 1, GRID_W, GRID_W), lambda h: (h, 0, 0, 0))],
        out_specs=pl.BlockSpec((len(NA_CASES), 1, NA_TQ, NA_TK), lambda h: (0, h, 0, 0)),
        out_shape=jax.ShapeDtypeStruct((len(NA_CASES), H_A, NA_TQ, NA_TK), F32),
        compiler_params=_cparams(("arbitrary",)),
        name="nbr_bias_table",
    )(blocks)


def _na_case(j):
    case = jnp.int32(NA_STEP_CASE[0])
    for step, c in enumerate(NA_STEP_CASE):
        if c != NA_STEP_CASE[0]:
            case = jnp.where(j - 1 == step, c, case)
    return case


def _na_kernel(q_ref, k_ref, v_ref, bias_ref, o_ref):
    j = pl.program_id(1)
    lane = lax.broadcasted_iota(I32, (1, 2 * HD_A), 1)
    lo_half = lane < HD_A

    def attend(q_rows, n_q, key_sets, bias_for_head):
        for pp0 in range(0, H_A // 2, NA_STAGE_PAIRS):
            heads = []
            for pp in range(pp0, pp0 + NA_STAGE_PAIRS):
                lanes = slice(pp * 2 * HD_A, (pp + 1) * 2 * HD_A)
                qp = q_ref[0, q_rows, lanes]
                ks = [k_ref[0, rs, lanes] for rs in key_sets]
                vs = [v_ref[0, rs, lanes] for rs in key_sets]
                for hh in range(2):
                    qm = jnp.where(lo_half if hh == 0 else ~lo_half, qp, jnp.zeros_like(qp))
                    heads.append(dict(head=2 * pp + hh, vs=vs, ss=[_dot_nt(qm, kk) for kk in ks]))
            for hd in heads:
                ss = hd['ss']
                if bias_for_head is not None:
                    ss[0] = ss[0] + bias_for_head(hd['head'])
                m = ss[0].max(axis=-1, keepdims=True)
                for s in ss[1:]:
                    m = jnp.maximum(m, s.max(axis=-1, keepdims=True))
                ps = [jnp.exp(s - m) for s in ss]
                l = ps[0].sum(axis=-1, keepdims=True)
                for p in ps[1:]:
                    l = l + p.sum(axis=-1, keepdims=True)
                hd['ps'] = [p.astype(BF16) for p in ps]
                hd['l'] = l
            for hd in heads:
                acc = _dot(hd['ps'][0], hd['vs'][0])
                for p, vv in zip(hd['ps'][1:], hd['vs'][1:]):
                    acc = acc + _dot(p, vv)
                hd['o'] = acc / hd['l']
            for i, pp in enumerate(range(pp0, pp0 + NA_STAGE_PAIRS)):
                lanes = slice(pp * 2 * HD_A, (pp + 1) * 2 * HD_A)
                o_ref[0, q_rows, lanes] = jnp.where(lo_half, heads[2 * i]['o'], heads[2 * i + 1]['o']).astype(BF16)

    ctx_rows = pl.ds(0, CTX_LEN)

    @pl.when(j == 0)
    def _():
        attend(ctx_rows, CTX_LEN, [ctx_rows], None)

    @pl.when(j > 0)
    def _():
        r = (j - 1) * NA_QROWS
        u0 = jnp.clip(r - WIN_H // 2, 0, GRID_ROWS - NA_KROWS)
        q_rows = pl.ds(pl.multiple_of(CTX_LEN + r * GRID_W, NA_TQ), NA_TQ)
        k_rows = pl.ds(pl.multiple_of(CTX_LEN + u0 * GRID_W, GRID_W), NA_TK)
        attend(q_rows, NA_TQ, [k_rows, ctx_rows], lambda head: bias_ref[0, head])


def _na_call(aq, ak, av, bias):
    B = aq.shape[0]
    full = pl.BlockSpec((1, T_ALL, D_A), lambda b, j: (b, 0, 0))
    return pl.pallas_call(
        _na_kernel,
        grid=(B, 1 + GRID_ROWS // NA_QROWS),
        in_specs=[full, full, full,
                  pl.BlockSpec((1, H_A, NA_TQ, NA_TK), lambda b, j: (_na_case(j), 0, 0, 0))],
        out_specs=full,
        out_shape=jax.ShapeDtypeStruct((B, T_ALL, D_A), BF16),
        compiler_params=_cparams(("arbitrary", "arbitrary")),
        name="nbr_attn",
    )(aq, ak, av, bias)


def _route(logits_t, rb_col):
    sc = _sigmoid(logits_t)
    sel = sc + rb_col
    selr = [sel[e:e + 1, :] for e in range(N_EXPERTS)]
    scr = [sc[e:e + 1, :] for e in range(N_EXPERTS)]
    gscore = []
    for g in range(N_GROUPS):
        a, b, c, d = selr[EXP_PER_GROUP * g:EXP_PER_GROUP * (g + 1)]
        s1, t1 = jnp.maximum(a, b), jnp.minimum(a, b)
        s2, t2 = jnp.maximum(c, d), jnp.minimum(c, d)
        gscore.append(jnp.maximum(s1, s2) + jnp.maximum(jnp.minimum(s1, s2), jnp.maximum(t1, t2)))
    best = gscore[0]
    gi = jnp.zeros(best.shape, I32)
    for g in range(1, N_GROUPS):
        better = gscore[g] > best
        gi = jnp.where(better, g, gi)
        best = jnp.where(better, gscore[g], best)
    vs, ws = [], []
    for k in range(EXP_PER_GROUP):
        v = selr[k]
        w = scr[k]
        for g in range(1, N_GROUPS):
            v = jnp.where(gi == g, selr[EXP_PER_GROUP * g + k], v)
            w = jnp.where(gi == g, scr[EXP_PER_GROUP * g + k], w)
        vs.append(v)
        ws.append(w)
    b1, i1 = vs[0], jnp.zeros(best.shape, I32)
    for k in range(1, EXP_PER_GROUP):
        better = vs[k] > b1
        i1 = jnp.where(better, k, i1)
        b1 = jnp.where(better, vs[k], b1)
    b2 = jnp.full(best.shape, -jnp.inf, F32)
    i2 = jnp.zeros(best.shape, I32)
    for k in range(EXP_PER_GROUP):
        vk = jnp.where(i1 == k, -jnp.inf, vs[k])
        better = vk > b2
        i2 = jnp.where(better, k, i2)
        b2 = jnp.where(better, vk, b2)
    w1 = ws[0]
    w2 = ws[0]
    for k in range(1, EXP_PER_GROUP):
        w1 = jnp.where(i1 == k, ws[k], w1)
        w2 = jnp.where(i2 == k, ws[k], w2)
    tot = w1 + w2
    return gi * EXP_PER_GROUP + i1, gi * EXP_PER_GROUP + i2, w1 / tot, w2 / tot


def _outproj_kernel(n_act, has_ctx, *refs):
    acts = refs[:n_act]
    ws = refs[n_act:2 * n_act]
    refs = refs[2 * n_act:]
    g_ref, rwh_ref, rwl_ref, xo_ref, hp_ref, lg_ref = refs[-6:]
    ns = xo_ref.shape[0]
    x_ins, (mods,) = _stream_tiles(has_ctx, has_ctx, refs[:-6], ns)
    o = _dot(acts[0][...].reshape(ns * TILE, -1), ws[0][...])
    for a, w in zip(acts[1:], ws[1:]):
        o = o + _dot(a[...].reshape(ns * TILE, -1), w[...])
    hs = []
    for s in range(ns):
        x = x_ins[s] + mods[s][2:3] * o[s * TILE:(s + 1) * TILE]
        xo_ref[s] = x
        h = _rms(x) * g_ref[...] * (1.0 + mods[s][4:5]) + mods[s][3:4]
        _store_token_tiles(hp_ref, (s,), 0, h)
        hs.append(_split_bf16(h, 2))
    for s, (h_hi, h_lo) in enumerate(hs):
        lg_ref[s] = (_dot_nt(rwh_ref[...], h_hi) + _dot_nt(rwh_ref[...], h_lo)
                     + _dot_nt(rwl_ref[...], h_hi))


def _outproj_call(acts, ws, x_src, x_off, modl, norm_g, rwh, rwl, n_tiles, ctx_src=None):
    B = x_src.shape[0]
    ns = math.gcd(B, PROJ_SAMPLES)
    n_act = len(acts)
    T = n_tiles * TILE
    const2 = lambda b, j: (0, 0)
    tok = lambda b, j: (b, j, 0)
    in_specs = [pl.BlockSpec((ns, TILE, a.shape[2]), tok) for a in acts]
    in_specs += [pl.BlockSpec(w.shape, const2) for w in ws]
    stream_specs, streams = _stream_specs(ns, x_src, x_off, ctx_src, ctx_src is not None, [modl])
    in_specs += stream_specs + [
        pl.BlockSpec((1, D_MODEL), const2),
        pl.BlockSpec(rwh.shape, const2),
        pl.BlockSpec(rwl.shape, const2),
    ]
    out_shape = [
        jax.ShapeDtypeStruct((B, T, D_MODEL), F32),
        jax.ShapeDtypeStruct((B, T * SUBLANES, 128), F32),
        jax.ShapeDtypeStruct((B, N_EXPERTS, T), F32),
    ]
    out_specs = [
        pl.BlockSpec((ns, TILE, D_MODEL), tok),
        pl.BlockSpec((ns, TILE * SUBLANES, 128), tok),
        pl.BlockSpec((ns, N_EXPERTS, TILE), lambda b, j: (b, 0, j)),
    ]
    return pl.pallas_call(
        functools.partial(_outproj_kernel, n_act, ctx_src is not None),
        grid=(B // ns, n_tiles),
        in_specs=in_specs,
        out_specs=out_specs,
        out_shape=out_shape,
        compiler_params=_cparams(("arbitrary", "arbitrary")),
        name="outproj",
    )(*acts, *ws, *streams, norm_g, rwh, rwl)


def _route_kernel(lg_ref, rb_ref, su_ref, ri_ref, rw_ref, cnt_ref):
    logits_t = lg_ref[0]
    T = logits_t.shape[1]
    e1, e2, w1, w2 = _route(logits_t, rb_ref[...])
    eidx = lax.broadcasted_iota(I32, logits_t.shape, 0)
    oh1 = eidx == e1
    oh2 = eidx == e2
    onehot = jnp.where(oh1, 1.0, jnp.where(oh2, 1.0, 0.0))
    count = jnp.zeros((N_EXPERTS, 1), F32)
    ranks = []
    for jj in range(T // TILE):
        oh = onehot[:, jj * TILE:(jj + 1) * TILE]
        ranks.append(_dot(oh.astype(BF16), su_ref[...]) + count)
        count = count + jnp.sum(oh, axis=1, keepdims=True)
    cpad = jnp.floor((count + (SUBLANES - 1.0)) * (1.0 / SUBLANES)) * SUBLANES
    ecol = lax.broadcasted_iota(I32, (N_EXPERTS, 1), 0)
    start = jnp.zeros((N_EXPERTS, 1), F32)
    for e in range(N_EXPERTS - 1):
        start = start + jnp.where(ecol > e, cpad[e:e + 1, :], 0.0)
    row = jnp.concatenate(ranks, axis=1) + start
    r1 = jnp.sum(jnp.where(oh1, row, 0.0), axis=0, keepdims=True)
    r2 = jnp.sum(jnp.where(oh2, row, 0.0), axis=0, keepdims=True)
    zi = jnp.zeros((SUBLANES - 4, T), I32)
    ri_ref[0] = jnp.concatenate([e1, e2, r1.astype(I32), r2.astype(I32), zi], axis=0)
    zf = jnp.zeros((SUBLANES - 2, T), F32)
    rw_ref[0] = jnp.concatenate([w1, w2, zf], axis=0)
    cnt_ref[0] = jnp.broadcast_to(count, (N_EXPERTS, 128)).astype(I32)


def _route_call(lg, rb, su):
    B, _, T = lg.shape
    per_sample = lambda b: (b, 0, 0)
    return pl.pallas_call(
        _route_kernel,
        grid=(B,),
        in_specs=[pl.BlockSpec((1, N_EXPERTS, T), per_sample),
                  pl.BlockSpec(rb.shape, lambda b: (0, 0)),
                  pl.BlockSpec(su.shape, lambda b: (0, 0))],
        out_specs=[pl.BlockSpec((1, SUBLANES, T), per_sample),
                   pl.BlockSpec((1, SUBLANES, T), per_sample),
                   pl.BlockSpec((1, N_EXPERTS, 128), per_sample)],
        out_shape=[jax.ShapeDtypeStruct((B, SUBLANES, T), I32),
                   jax.ShapeDtypeStruct((B, SUBLANES, T), F32),
                   jax.ShapeDtypeStruct((B, N_EXPERTS, 128), I32)],
        compiler_params=_cparams(("arbitrary",)),
        name="route",
    )(lg, rb, su)


def _moe_kernel(T, n_rows, cnt_ref, ri_ref, rw_ref, h_ref, w1_ref, w3_ref, w2_ref, y_ref,
                xb_ref, ob_ref, tokl_ref, off_ref):
    b = pl.program_id(0)
    e = pl.program_id(1)

    def tile_rows(row):
        return pl.ds(pl.multiple_of(row * SUBLANES, SUBLANES), SUBLANES)

    @pl.when(e == 0)
    def _():
        off_ref[0] = 0
        for i in range(N_EXPERTS):
            c = cnt_ref[b, i]
            off_ref[i + 1] = off_ref[i] + ((c + SUBLANES - 1) // SUBLANES) * SUBLANES

        for i in range(N_EXPERTS):
            for k in range(SUBLANES):
                tokl_ref[jnp.maximum(off_ref[i + 1] - SUBLANES + k, 0)] = 0

        def clear(i, carry):
            tokl_ref[off_ref[N_EXPERTS] + i] = 0
            return carry

        lax.fori_loop(0, MOE_RB, clear, 0, unroll=8)

        def place(t, carry):
            tokl_ref[ri_ref[2 * T + t]] = t
            tokl_ref[ri_ref[3 * T + t]] = t
            return carry

        lax.fori_loop(0, T, place, 0, unroll=8)

        def gather(i, carry):
            xb_ref[tile_rows(i), :] = h_ref[0, tile_rows(tokl_ref[i]), :]
            return carry

        lax.fori_loop(0, MOE_RB, gather, 0, unroll=8)

    c = cnt_ref[b, e]
    base = off_ref[e]
    next_base = off_ref[e + 1]

    def block(p0, n_blk_rows, p_next):
        xb = _load_token_tiles(xb_ref, (), 0, n_blk_rows).astype(BF16)
        for i in range(MOE_RB):
            xb_ref[i * SUBLANES:(i + 1) * SUBLANES, :] = h_ref[0, tile_rows(tokl_ref[p_next + i]), :]
        h1 = _dot(xb, w1_ref[0, 0])
        h3 = _dot(xb, w3_ref[0, 0])
        act = (h1 * _sigmoid(h1)) * h3
        _store_token_tiles(ob_ref, (), p0, _dot(act.astype(BF16), w2_ref[0, 0]))

    rows_up = ((c + MOE_RB_TAIL - 1) // MOE_RB_TAIL) * MOE_RB_TAIL
    n_big = rows_up // MOE_RB
    tail = rows_up - n_big * MOE_RB

    def big_block(rb, carry):
        p0 = pl.multiple_of(base + rb * MOE_RB, SUBLANES)
        last = jnp.logical_and(rb == n_big - 1, tail == 0)
        block(p0, MOE_RB, jnp.where(last, next_base, p0 + MOE_RB))
        return carry

    lax.fori_loop(0, n_big, big_block, 0)

    for tail_rows in range(MOE_RB_TAIL, MOE_RB, MOE_RB_TAIL):
        @pl.when(tail == tail_rows)
        def _():
            block(pl.multiple_of(base + n_big * MOE_RB, SUBLANES), tail_rows, next_base)

    @pl.when(e == N_EXPERTS - 1)
    def _():
        def combine(t, carry):
            y_ref[0, tile_rows(t), :] = (rw_ref[t] * ob_ref[tile_rows(ri_ref[2 * T + t]), :]
                                         + rw_ref[T + t] * ob_ref[tile_rows(ri_ref[3 * T + t]), :])
            return carry

        lax.fori_loop(0, T, combine, 0, unroll=8)


def _moe_call(cnt, ri, rw, h, w1, w3, w2, layer):
    B = h.shape[0]
    T = h.shape[1] // SUBLANES
    n_rows = 2 * T + N_EXPERTS * SUBLANES + MOE_RB
    smem = functools.partial(pl.BlockSpec, memory_space=pltpu.SMEM)
    once = pl.Buffered(1)
    return pl.pallas_call(
        functools.partial(_moe_kernel, T, n_rows),
        grid=(B, N_EXPERTS),
        in_specs=[
            smem(cnt.shape, lambda b, e: (0, 0)),
            smem((4 * T,), lambda b, e: (b,)),
            smem((4 * T,), lambda b, e: (b,)),
            pl.BlockSpec((1, T * SUBLANES, 128), lambda b, e: (b, 0, 0)),
            pl.BlockSpec((1, 1, D_MODEL, D_FF), lambda b, e: (layer, e, 0, 0)),
            pl.BlockSpec((1, 1, D_MODEL, D_FF), lambda b, e: (layer, e, 0, 0)),
            pl.BlockSpec((1, 1, D_FF, D_MODEL), lambda b, e: (layer, e, 0, 0)),
        ],
        out_specs=pl.BlockSpec((1, T * SUBLANES, 128), lambda b, e: (b, 0, 0), pipeline_mode=once),
        out_shape=jax.ShapeDtypeStruct((B, T * SUBLANES, 128), F32),
        scratch_shapes=[
            pltpu.VMEM((MOE_RB * SUBLANES, 128), F32),
            pltpu.VMEM((n_rows * SUBLANES, 128), F32),
            pltpu.SMEM((n_rows,), I32),
            pltpu.SMEM((N_EXPERTS + 1,), I32),
        ],
        compiler_params=_cparams(("arbitrary", "arbitrary")),
        name="moe",
    )(cnt, ri, rw, h, w1, w3, w2)


def _route_tables(ri, rw, cnt):
    return cnt[:, :, 0], ri[:, :4].reshape(-1), rw[:, :4].reshape(-1)


def _inproj1_kernel(*refs):
    (y_ref, g_ref, w_ref, qg_ref, kg_ref, cos_ref, sin_ref, xo_ref, q_ref, k_ref, v_ref) = refs[-11:]
    ns = xo_ref.shape[0]
    x_ins, (mods0, mods) = _stream_tiles(False, True, refs[:-11], ns)
    hbs = []
    for s in range(ns):
        x = x_ins[s] + mods0[s][5:6] * _load_token_tiles(y_ref, (s,), 0, TILE)
        xo_ref[s] = x
        hbs.append((_rms(x) * g_ref[...] * (1.0 + mods[s][1:2]) + mods[s][0:1]).astype(BF16))
    hb = jnp.concatenate(hbs, axis=0)
    cos = cos_ref[...]
    sin = sin_ref[...]

    def rope_heads(ref, proj, n_heads, gain, scale):
        for s in range(ns):
            for h in range(n_heads):
                sl = slice(h * HD_C, (h + 1) * HD_C)
                n = _rms(proj[s * TILE:(s + 1) * TILE, sl]) * gain
                r = n * cos + pltpu.roll(n, HD_C // 2, 1) * sin
                ref[s, :, sl] = (r if scale is None else r * scale).astype(BF16)

    ko = H_C * HD_C
    vo = (H_C + KV_C) * HD_C
    qkv = _dot(hb, w_ref[...])
    qs, kv = qkv[:, :ko], qkv[:, ko:]
    rope_heads(k_ref, kv, KV_C, kg_ref[...], None)
    for s in range(ns):
        v_ref[s] = kv[s * TILE:(s + 1) * TILE, KV_C * HD_C:].astype(BF16)
    rope_heads(q_ref, qs, H_C, qg_ref[...], HD_C ** -0.5)


def _inproj1_call(x1, y0, mod0, mod1, norm_g, w, qg, kg, cos, sin):
    B = x1.shape[0]
    ns = math.gcd(B, INPROJ1_SAMPLES)
    tok = lambda b, j: (b, j, 0)
    const2 = lambda b, j: (0, 0)
    lat_tok = lambda b, j: (b, jnp.maximum(j - NT_CTX, 0), 0)
    widths = [(T_ALL, D_MODEL, F32), (SEQ, H_C * HD_C, BF16),
              (T_ALL, KV_C * HD_C, BF16), (T_ALL, KV_C * HD_C, BF16)]
    stream_specs, streams = _stream_specs(ns, x1, 0, None, True, [mod0, mod1])
    return pl.pallas_call(
        _inproj1_kernel,
        grid=(B // ns, NT_ALL),
        in_specs=stream_specs + [
            pl.BlockSpec((ns, TILE * SUBLANES, 128), tok),
            pl.BlockSpec((1, D_MODEL), const2),
            pl.BlockSpec(w.shape, const2),
            pl.BlockSpec(qg.shape, const2),
            pl.BlockSpec(kg.shape, const2),
            pl.BlockSpec((TILE, HD_C), lambda b, j: (j, 0)),
            pl.BlockSpec((TILE, HD_C), lambda b, j: (j, 0)),
        ],
        out_specs=[pl.BlockSpec((ns, TILE, w_), lat_tok if t == SEQ else tok) for t, w_, _ in widths],
        out_shape=[jax.ShapeDtypeStruct((B, t, w_), dt) for t, w_, dt in widths],
        compiler_params=_cparams(("arbitrary", "arbitrary")),
        name="inproj1",
    )(*streams, y0, norm_g, w, qg, kg, cos, sin)


def _gqa_kernel(q_ref, k_ref, v_ref, o_ref):
    group_heads = H_C // KV_C
    for g in range(KV_C):
        k = k_ref[0, :, g * HD_C:(g + 1) * HD_C]
        v = v_ref[0, :, g * HD_C:(g + 1) * HD_C]
        for h0 in range(g * group_heads, (g + 1) * group_heads, GQA_STAGE_HEADS):
            sls = [slice(h * HD_C, (h + 1) * HD_C) for h in range(h0, h0 + GQA_STAGE_HEADS)]
            ss = [_dot_nt(q_ref[0, :, sl], k) for sl in sls]
            ps, ls = [], []
            for s in ss:
                p = jnp.exp(s - s.max(axis=-1, keepdims=True))
                ls.append(p.sum(axis=-1, keepdims=True))
                ps.append(p.astype(BF16))
            for sl, p, l in zip(sls, ps, ls):
                o_ref[0, :, sl] = (_dot(p, v) / l).astype(BF16)


def _gqa_call(q, k, v):
    B = q.shape[0]
    whole = lambda b, j: (b, 0, 0)
    return pl.pallas_call(
        _gqa_kernel,
        grid=(B, SEQ // GQA_TQ),
        in_specs=[
            pl.BlockSpec((1, GQA_TQ, H_C * HD_C), lambda b, j: (b, j, 0)),
            pl.BlockSpec((1, T_ALL, KV_C * HD_C), whole),
            pl.BlockSpec((1, T_ALL, KV_C * HD_C), whole),
        ],
        out_specs=pl.BlockSpec((1, GQA_TQ, H_C * HD_C), lambda b, j: (b, j, 0)),
        out_shape=jax.ShapeDtypeStruct((B, SEQ, H_C * HD_C), BF16),
        compiler_params=_cparams(("arbitrary", "arbitrary")),
        name="gqa",
    )(q, k, v)


def _final_kernel(x_ref, y_ref, mod_ref, o_ref):
    o_ref[0] = x_ref[0] + mod_ref[0][5:6] * _load_token_tiles(y_ref, (0,), 0, FINAL_TILE)


def _final_call(x, y, modl):
    B, T, D = x.shape
    tok = lambda b, j: (b, j, 0)
    return pl.pallas_call(
        _final_kernel,
        grid=(B, T // FINAL_TILE),
        in_specs=[pl.BlockSpec((1, FINAL_TILE, D), tok), pl.BlockSpec((1, FINAL_TILE * SUBLANES, 128), tok),
                  pl.BlockSpec((1, 6, D), lambda b, j: (b, 0, 0))],
        out_specs=pl.BlockSpec((1, FINAL_TILE, D), tok),
        out_shape=jax.ShapeDtypeStruct((B, T, D), F32),
        compiler_params=_cparams(("arbitrary", "arbitrary")),
        name="final_residual",
    )(x, y, modl)


def _chunk_tri(lower):
    i = np.arange(TILE)
    same = (i[:, None] // MLSTM_CHUNK) == (i[None, :] // MLSTM_CHUNK)
    tri = (i[None, :] <= i[:, None]) if lower else (i[None, :] >= i[:, None])
    return jnp.asarray((same & tri).astype(np.float32), BF16)


def _rope_tables():
    n_freq = HD_C // 4
    inv_freq = ROPE_THETA ** (-jnp.arange(n_freq, dtype=F32) / n_freq)
    t = jnp.arange(SEQ)
    rows = (t // GRID_W).astype(F32)
    cols = (t % GRID_W).astype(F32)
    ang = jnp.concatenate([rows[:, None] * inv_freq, cols[:, None] * inv_freq], axis=-1)
    cos, sin = jnp.cos(ang), jnp.sin(ang)
    cos_l = jnp.concatenate([cos, cos], axis=-1)
    sin_l = jnp.concatenate([-sin, sin], axis=-1)
    cos_all = jnp.concatenate([jnp.ones((CTX_LEN, HD_C), F32), cos_l], axis=0)
    sin_all = jnp.concatenate([jnp.zeros((CTX_LEN, HD_C), F32), sin_l], axis=0)
    return cos_all, sin_all


_HEAD_PERM = np.concatenate([np.arange(0, HD_C, 2), np.arange(1, HD_C, 2)])


def kernel(x, c, ctx, c_ctx, ada_w, ada_b, norm_mix_g, norm_ffn_g, even_w_in, even_w_out,
           na_q_norm_g, na_k_norm_g, na_rpb, mlstm_gate_b, mlstm_norm_g, odd_w_in, odd_w_out,
           gqa_q_norm_g, gqa_k_norm_g, router_w, router_b, exp_w1, exp_w3, exp_w2):
    B = x.shape[0]
    assert B <= N_MOD_CTX_ROW and x.shape[1:] == (SEQ, D_MODEL) and ctx.shape[1:] == (CTX_LEN, D_MODEL)
    n_g = 4 * H_B

    cvec = jnp.zeros((N_MOD_ROWS, D_MODEL), F32).at[:B].set(c).at[N_MOD_CTX_ROW].set(c_ctx)
    mod = _ada_call(cvec, ada_w, ada_b).reshape(2, N_MOD_ROWS, 6, D_MODEL)
    mod0, mod1 = mod[0], mod[1]

    w_in = even_w_in[0]
    n_main = w_in.shape[1] - n_g
    wm = w_in[:, :n_main].astype(BF16)
    wg_f = w_in[:, n_main:]
    wg = jnp.pad(wg_f, ((0, 0), (0, 128 - n_g))).astype(BF16)
    wgt = wg_f.T.astype(BF16)
    gb = jnp.pad(mlstm_gate_b[0].reshape(1, n_g), ((0, 0), (0, 128 - n_g)))
    gbt = mlstm_gate_b[0].reshape(n_g, 1)
    qg = jnp.tile(na_q_norm_g[0], H_A).reshape(1, D_A)
    kg = jnp.tile(na_k_norm_g[0], H_A).reshape(1, D_A)
    hid = np.arange(128) // HD_A
    bd = jnp.asarray((hid[:, None] == hid[None, :]).astype(np.float32), BF16)
    k_lo = 3 * D_A + H_B * DK_B
    wkt = w_in[:, k_lo:k_lo + H_B * DK_B].T.astype(BF16)
    aq, ak, av, mq, mv, og, mkt, gc, gr = _inproj0_call(
        x, ctx, mod0, norm_mix_g[0].reshape(1, D_MODEL), wm, wkt, wg, wgt, gb, gbt, qg, kg, bd,
        _chunk_tri(True), _chunk_tri(False))
    hm = _mlstm_call(mq, mkt, mv, gc, gr, og, mlstm_norm_g[0].reshape(1, H_B * DV_B))
    oa = _na_call(aq, ak, av, _na_bias_tables(na_rpb[0]))

    rw_t = router_w.T
    rwh = rw_t.astype(BF16)
    rwl = (rw_t - rwh.astype(F32)).astype(BF16)
    rb = router_b.reshape(N_EXPERTS, 1).astype(F32)
    i = np.arange(TILE)
    su = jnp.asarray((i[:, None] < i[None, :]).astype(np.float32), BF16)
    w_out = even_w_out[0].astype(BF16)
    ew1, ew3, ew2 = exp_w1.astype(BF16), exp_w3.astype(BF16), exp_w2.astype(BF16)
    x1, hp0, lg0 = _outproj_call(
        [oa, hm], [w_out[:D_A], w_out[D_A:]], x, 0, mod0,
        norm_ffn_g[0].reshape(1, D_MODEL), rwh, rwl, NT_ALL, ctx_src=ctx)
    y0 = _moe_call(*_route_tables(*_route_call(lg0, rb, su)), hp0, ew1, ew3, ew2, 0)

    w1_in = odd_w_in[0]
    qk_cols = np.concatenate([h * HD_C + _HEAD_PERM for h in range(H_C + KV_C)])
    cols = np.concatenate([qk_cols, np.arange((H_C + KV_C) * HD_C, w1_in.shape[1])])
    w1_in = w1_in[:, cols].astype(BF16)
    cos_all, sin_all = _rope_tables()
    x2, q, k, v = _inproj1_call(
        x1, y0, mod0, mod1, norm_mix_g[1].reshape(1, D_MODEL), w1_in,
        gqa_q_norm_g[0][_HEAD_PERM].reshape(1, HD_C), gqa_k_norm_g[0][_HEAD_PERM].reshape(1, HD_C),
        cos_all, sin_all)
    o = _gqa_call(q, k, v)
    x3, hp1, lg1 = _outproj_call(
        [o], [odd_w_out[0].astype(BF16)], x2, NT_CTX, mod1,
        norm_ffn_g[1].reshape(1, D_MODEL), rwh, rwl, SEQ // TILE)
    y1 = _moe_call(*_route_tables(*_route_call(lg1, rb, su)), hp1, ew1, ew3, ew2, 1)
    return _final_call(x3, y1, mod1)
```

```python
import functools
import math

import numpy as np
import jax
import jax.numpy as jnp
from jax import lax
from jax.experimental import pallas as pl
from jax.experimental.pallas import tpu as pltpu

F32 = jnp.float32
BF16 = jnp.bfloat16
I32 = jnp.int32

D_MODEL = 1024
SEQ = 2048
GRID_W = 64
GRID_ROWS = SEQ // GRID_W
CTX_LEN = 256
T_ALL = CTX_LEN + SEQ
WIN_H = 8
WIN_W = 16
HD_A = 64
H_A = 8
D_A = H_A * HD_A
H_B = 4
DV_B = 128
DK_B = 64
MLSTM_CHUNK = 64
N_CHUNKS = T_ALL // MLSTM_CHUNK
N_CTX_CHUNKS = CTX_LEN // MLSTM_CHUNK
HD_C = 128
H_C = 8
KV_C = 2
ROPE_THETA = 10000.0
N_EXPERTS = 16
N_GROUPS = 4
EXP_PER_GROUP = 4
D_FF = 512
EPS = 1e-6
NEG_BIG = -1e30

TILE = 256
NT_ALL = T_ALL // TILE
NT_CTX = CTX_LEN // TILE
NA_QROWS = 4
NA_KROWS = WIN_H + NA_QROWS - 1
NA_TQ = NA_QROWS * GRID_W
NA_TK = NA_KROWS * GRID_W
NA_STAGE_PAIRS = 2
NA_STEP_BLOCKS = 2


def _na_geometry(r):
    u0 = min(max(r - WIN_H // 2, 0), GRID_ROWS - NA_KROWS)
    r0s = tuple(min(max(r + qi - WIN_H // 2, 0), GRID_ROWS - WIN_H) - r for qi in range(NA_QROWS))
    return (u0 - r, r0s)


def _na_cases():
    reps, step_case = [], []
    for r in range(0, GRID_ROWS, NA_QROWS):
        geo = _na_geometry(r)
        known = [_na_geometry(q) for q in reps]
        if geo not in known:
            reps.append(r)
            known.append(geo)
        step_case.append(known.index(geo))
    return tuple(reps), tuple(step_case)


NA_CASES, NA_STEP_CASE = _na_cases()
PROJ_SAMPLES = 4
INPROJ0_SAMPLES = 4
INPROJ1_SAMPLES = 1
GQA_TQ = 512
GQA_STAGE_HEADS = 2
FINAL_TILE = 1024
MOE_RB = 256
MOE_RB_TAIL = 64
SUBLANES = 8
VMEM_LIMIT = 56 * 1024 * 1024


def _cparams(sem):
    return pltpu.CompilerParams(dimension_semantics=sem, vmem_limit_bytes=VMEM_LIMIT)


def _sigmoid(x):
    return 1.0 / (1.0 + jnp.exp(-x))


def _rms(x):
    return x * lax.rsqrt(jnp.mean(x * x, axis=-1, keepdims=True) + EPS)


def _dot(a, b):
    return jnp.dot(a, b, preferred_element_type=F32)


def _dot_nt(a, b):
    return lax.dot_general(a, b, (((1,), (1,)), ((), ())), preferred_element_type=F32)


def _split_bf16(x, n):
    parts = []
    r = x
    for _ in range(n):
        p = r.astype(BF16)
        parts.append(p)
        r = r - p.astype(F32)
    return parts


LANE_TILES = D_MODEL // 128


def _load_token_tiles(ref, lead, tok0, n_tok):
    parts = [ref[(*lead, pl.ds(tok0 * SUBLANES + c, n_tok, stride=SUBLANES), slice(None))]
             for c in range(LANE_TILES)]
    return jnp.concatenate(parts, axis=1)


def _store_token_tiles(ref, lead, tok0, val):
    for c in range(LANE_TILES):
        ref[(*lead, pl.ds(tok0 * SUBLANES + c, val.shape[0], stride=SUBLANES), slice(None))] = (
            val[:, c * 128:(c + 1) * 128])


def _stream_specs(ns, x_src, x_off, ctx_src, ctx_tile, mod_tables):
    if ctx_src is None:
        specs = [pl.BlockSpec((ns, TILE, D_MODEL), lambda b, j: (b, j + x_off, 0))]
        arrays = [x_src]
    else:
        assert x_off == 0 and ctx_tile
        specs = [pl.BlockSpec((ns, TILE, D_MODEL), lambda b, j: (b, jnp.maximum(j - NT_CTX, 0), 0)),
                 pl.BlockSpec((ns, TILE, D_MODEL), lambda b, j: (b, jnp.minimum(j, NT_CTX - 1), 0))]
        arrays = [x_src, ctx_src]
    for table in mod_tables:
        specs.append(pl.BlockSpec((ns, 6, D_MODEL), lambda b, j: (b, 0, 0)))
        arrays.append(table)
        if ctx_tile:
            specs.append(pl.BlockSpec((1, 6, D_MODEL), lambda b, j: (N_MOD_CTX_ROW, 0, 0)))
            arrays.append(table)
    return specs, arrays


def _stream_tiles(two_streams, ctx_tile, refs, ns):
    is_ctx = pl.program_id(1) < NT_CTX
    if two_streams:
        x_ins = [jnp.where(is_ctx, refs[1][s], refs[0][s]) for s in range(ns)]
        refs = refs[2:]
    else:
        x_ins = [refs[0][s] for s in range(ns)]
        refs = refs[1:]
    tables = []
    if ctx_tile:
        for t in range(0, len(refs), 2):
            tables.append([jnp.where(is_ctx, refs[t + 1][0], refs[t][s]) for s in range(ns)])
    else:
        for ref in refs:
            tables.append([ref[s] for s in range(ns)])
    return x_ins, tables


N_MOD_ROWS = 16
N_MOD_CTX_ROW = 8


ADA_TN = 1536


def _ada_kernel(c_ref, w_ref, b_ref, o_ref):
    c = c_ref[...]
    s = (c * _sigmoid(c)).astype(BF16)
    o_ref[0] = _dot(s, w_ref[0].astype(BF16)) + b_ref[0]


def _ada_call(cvec, ada_w, ada_b):
    depth, d, n = ada_w.shape
    return pl.pallas_call(
        _ada_kernel,
        grid=(depth, n // ADA_TN),
        in_specs=[
            pl.BlockSpec((N_MOD_ROWS, d), lambda l, j: (0, 0)),
            pl.BlockSpec((1, d, ADA_TN), lambda l, j: (l, 0, j)),
            pl.BlockSpec((1, 1, ADA_TN), lambda l, j: (l, 0, j)),
        ],
        out_specs=pl.BlockSpec((1, N_MOD_ROWS, ADA_TN), lambda l, j: (l, 0, j)),
        out_shape=jax.ShapeDtypeStruct((depth, N_MOD_ROWS, n), F32),
        compiler_params=_cparams(("arbitrary", "arbitrary")),
        name="ada_mod",
    )(cvec, ada_w, ada_b.reshape(depth, 1, n))


def _log_sigmoid(x):
    return jnp.minimum(x, 0.0) - jnp.log1p(jnp.exp(-jnp.abs(x)))


def _inproj0_kernel(*refs):
    (g_ref, wm_ref, wkt_ref, wg_ref, wgt_ref, gb_ref, gbt_ref, qg_ref, kg_ref, bd_ref, tril_ref, triu_ref,
     aq_ref, ak_ref, av_ref, mq_ref, mv_ref, og_ref, mkt_ref, gc_ref, gr_ref) = refs[-21:]
    ns = aq_ref.shape[0]
    x_ins, (mods,) = _stream_tiles(True, True, refs[:-21], ns)
    hb = jnp.concatenate([(_rms(x_ins[s]) * g_ref[...] * (1.0 + mods[s][1:2]) + mods[s][0:1]).astype(BF16)
                          for s in range(ns)], axis=0)

    def per_sample(ref, val):
        for s in range(ns):
            ref[s] = val[s * TILE:(s + 1) * TILE]

    def proj(lo, hi):
        return _dot(hb, wm_ref[:, lo:hi])

    def head_norm(a, gain):
        a2 = a * a
        sums = []
        for s0 in range(0, D_A, 128):
            hi_, lo_ = _split_bf16(a2[:, s0:s0 + 128], 2)
            sums.append(_dot(hi_, bd_ref[...]) + _dot(lo_, bd_ref[...]))
        ss = jnp.concatenate(sums, axis=1)
        return a * lax.rsqrt(ss * (1.0 / HD_A) + EPS) * gain

    n_g = 4 * H_B
    p_aq, p_ak, p_av = proj(0, 512), proj(512, 1024), proj(1024, 1536)
    p_mq, p_mv, p_og = proj(1536, 1792), proj(2048, 2560), proj(2560, 3072)
    kt = _dot_nt(wkt_ref[...], hb).astype(BF16)
    gcol = _dot(hb, wg_ref[...]) + gb_ref[...]
    grow = _dot_nt(wgt_ref[...], hb) + gbt_ref[...]

    per_sample(aq_ref, (head_norm(p_aq, qg_ref[...]) * HD_A ** -0.5).astype(BF16))
    per_sample(ak_ref, head_norm(p_ak, kg_ref[...]).astype(BF16))
    per_sample(av_ref, p_av.astype(BF16))
    per_sample(mq_ref, (p_mq * DK_B ** -0.5).astype(BF16))
    per_sample(mv_ref, p_mv.astype(BF16))
    per_sample(og_ref, _sigmoid(p_og).astype(BF16))

    tril = tril_ref[...]
    triu = triu_ref[...]
    ls_c = _split_bf16(_log_sigmoid(gcol), 3)
    ls_r = _split_bf16(_log_sigmoid(grow), 3)
    cidx = lax.broadcasted_iota(I32, (TILE, 128), 1)
    ridx = lax.broadcasted_iota(I32, (n_g, TILE), 0)

    def pick(idx, raw, pre, suf):
        is_ff = (idx >= H_B) & (idx < 2 * H_B)
        is_fb = idx >= 3 * H_B
        return jnp.where(is_ff, pre, jnp.where(is_fb, suf, raw))

    for s in range(ns):
        tok = slice(s * TILE, (s + 1) * TILE)
        pre_c = sum(_dot(tril, p[tok, :]) for p in ls_c)
        suf_c = sum(_dot(triu, p[tok, :]) for p in ls_c)
        pre_r = sum(_dot(p[:, tok], triu) for p in ls_r)
        suf_r = sum(_dot(p[:, tok], tril) for p in ls_r)
        gc_ref[s] = pick(cidx, gcol[tok, :], pre_c, suf_c)[:, :n_g]
        grow_s = pick(ridx, grow[:, tok], pre_r, suf_r)
        for c in range(TILE // MLSTM_CHUNK):
            chunk = slice(c * MLSTM_CHUNK, (c + 1) * MLSTM_CHUNK)
            gr_ref[s, c] = grow_s[:, chunk]
            mkt_ref[s, c] = kt[:, s * TILE + c * MLSTM_CHUNK:s * TILE + (c + 1) * MLSTM_CHUNK]


def _inproj0_call(x, ctx, modl, norm_g, wm, wkt, wg, wgt, gb, gbt, qg, kg, bd, tril, triu):
    B = x.shape[0]
    ns = math.gcd(B, INPROJ0_SAMPLES)
    n_g = 4 * H_B
    tok = lambda b, j: (b, j, 0)
    const2 = lambda b, j: (0, 0)
    chunked = lambda b, j: (b, j, 0, 0)
    tile_chunks = TILE // MLSTM_CHUNK
    outs = [
        (D_A, BF16), (D_A, BF16), (D_A, BF16),
        (H_B * DK_B, BF16), (H_B * DV_B, BF16),
        (H_B * DV_B, BF16),
    ]
    out_shape = [jax.ShapeDtypeStruct((B, T_ALL, w), dt) for w, dt in outs]
    out_specs = [pl.BlockSpec((ns, TILE, w), tok) for w, _ in outs]
    out_shape += [jax.ShapeDtypeStruct((B, N_CHUNKS, H_B * DK_B, MLSTM_CHUNK), BF16),
                  jax.ShapeDtypeStruct((B, T_ALL, n_g), F32),
                  jax.ShapeDtypeStruct((B, N_CHUNKS, n_g, MLSTM_CHUNK), F32)]
    out_specs += [pl.BlockSpec((ns, tile_chunks, H_B * DK_B, MLSTM_CHUNK), chunked),
                  pl.BlockSpec((ns, TILE, n_g), tok),
                  pl.BlockSpec((ns, tile_chunks, n_g, MLSTM_CHUNK), chunked)]
    stream_specs, streams = _stream_specs(ns, x, 0, ctx, True, [modl])
    return pl.pallas_call(
        _inproj0_kernel,
        grid=(B // ns, NT_ALL),
        in_specs=stream_specs + [
            pl.BlockSpec((1, D_MODEL), const2),
            pl.BlockSpec(wm.shape, const2, pipeline_mode=pl.Buffered(1)),
            pl.BlockSpec(wkt.shape, const2),
            pl.BlockSpec(wg.shape, const2),
            pl.BlockSpec(wgt.shape, const2),
            pl.BlockSpec(gb.shape, const2),
            pl.BlockSpec(gbt.shape, const2),
            pl.BlockSpec(qg.shape, const2),
            pl.BlockSpec(kg.shape, const2),
            pl.BlockSpec(bd.shape, const2),
            pl.BlockSpec(tril.shape, const2),
            pl.BlockSpec(triu.shape, const2),
        ],
        out_specs=out_specs,
        out_shape=out_shape,
        compiler_params=_cparams(("arbitrary", "arbitrary")),
        name="inproj0",
    )(*streams, norm_g, wm, wkt, wg, wgt, gb, gbt, qg, kg, bd, tril, triu)


def _mlstm_kernel(mq_ref, mkt_ref, mv_ref, gc_ref, gr_ref, og_ref, ng_ref, out_ref,
                  hf_ref, hb_ref, s_ref, *local_refs):
    L = MLSTM_CHUNK
    n_chain = 2 * H_B
    slots = (local_refs[:4], local_refs[4:])
    s_ref[...] = jnp.zeros(s_ref.shape, F32)
    ri = lax.broadcasted_iota(I32, (L, L), 0)
    ci = lax.broadcasted_iota(I32, (L, L), 1)
    masks = (ci <= ri, ci >= ri)
    ones_aug = jnp.ones((L, DV_B), BF16)

    def tile_up(x, n_rows, n_cols):
        x = jnp.concatenate([x] * n_rows, axis=0)
        return x if n_cols == 1 else jnp.concatenate([x] * n_cols, axis=1)

    def chunk_rows(step, d):
        if d == 0:
            chunk = step
        else:
            chunk = jnp.where(step < N_CTX_CHUNKS, N_CTX_CHUNKS - 1 - step, N_CHUNKS + N_CTX_CHUNKS - 1 - step)
        return chunk, pl.ds(pl.multiple_of(chunk * L, L), L)

    def local_part(step, slot, rec=None):
        intra_ref, u_ref, col_ref, rep_ref = slot
        rec = rec or (lambda: None,) * 3
        chains = []
        for d in range(2):
            chunk, rows = chunk_rows(step, d)
            gcc = gc_ref[0, rows, :]
            grr = gr_ref[0, chunk]
            for h in range(H_B):
                gi = 2 * H_B * d + h
                b_col = gcc[:, gi + H_B:gi + H_B + 1]
                chains.append(dict(
                    d=d,
                    q=mq_ref[0, rows, h * DK_B:(h + 1) * DK_B],
                    kt=mkt_ref[0, chunk, h * DK_B:(h + 1) * DK_B, :],
                    v=mv_ref[0, rows, h * DV_B:(h + 1) * DV_B],
                    ig_col=gcc[:, gi:gi + 1], b_col=b_col,
                    ig_row=grr[gi:gi + 1, :], b_row=grr[gi + H_B:gi + H_B + 1, :],
                    b_last=b_col[L - 1:L, :] if d == 0 else b_col[0:1, :]))
        for ch in chains:
            ch['qk'] = _dot(ch['q'], ch['kt'])
            ch['v_aug'] = jnp.concatenate([ch['v'], ones_aug], axis=1)
            g = ch['b_last'] - ch['b_col'] + ch['ig_col']
            ch['g_max'] = jnp.max(g, axis=0, keepdims=True)
            ch['wv'] = (jnp.exp(g - ch['g_max']) * ch['v_aug'].astype(F32)).astype(BF16)
        us = [_dot(ch['kt'], ch['wv']) for ch in chains]
        for ch in chains:
            dm = jnp.where(masks[ch['d']], ch['b_col'] - ch['b_row'] + ch['ig_row'], NEG_BIG)
            ch['m_loc'] = jnp.max(dm, axis=-1, keepdims=True)
            ch['p'] = (ch['qk'] * jnp.exp(dm - ch['m_loc'])).astype(BF16)
        intras = [_dot(ch['p'], ch['v_aug']) for ch in chains]
        rec[0]()
        cols, reps = [], []
        for ch in chains:
            cols += [jnp.broadcast_to(ch['b_col'], (L, 128)), jnp.broadcast_to(ch['m_loc'], (L, 128))]
            reps += [jnp.broadcast_to(ch['b_last'], (SUBLANES, 128)),
                     jnp.broadcast_to(ch['g_max'], (SUBLANES, 128))]
        intra_ref[...] = jnp.concatenate(intras, axis=0)
        u_ref[...] = jnp.concatenate(us, axis=0)
        col_ref[...] = jnp.concatenate(cols, axis=0)
        rep_ref[...] = jnp.concatenate(reps, axis=0)
        rec[1]()
        rec[2]()

    def recurrence_stages(step, ms, slot):
        intra_ref, u_ref, col_ref, rep_ref = slot
        n_all = range(n_chain)
        rows = [chunk_rows(step, d)[1] for d in range(2)]
        live = {}
        new_ms = []

        def products():
            live['states'] = [s_ref[c * DK_B:(c + 1) * DK_B, :] for c in n_all]
            live['inters'] = [_dot(mq_ref[0, rows[c // H_B], (c % H_B) * DK_B:(c % H_B + 1) * DK_B],
                                   live['states'][c].astype(BF16)) for c in n_all]

        def update():
            new_states = []
            for c in n_all:
                m = ms[c]
                b_last = rep_ref[2 * c * SUBLANES:(2 * c + 1) * SUBLANES, :]
                g_max = rep_ref[(2 * c + 1) * SUBLANES:(2 * c + 2) * SUBLANES, :]
                m_new = jnp.maximum(b_last + m, g_max)
                w_old = tile_up(jnp.exp(b_last + m - m_new), DK_B // SUBLANES, 2)
                w_new = tile_up(jnp.exp(g_max - m_new), DK_B // SUBLANES, 2)
                new_states.append(w_old * live['states'][c] + w_new * u_ref[c * DK_B:(c + 1) * DK_B, :])
                new_ms.append(m_new)
            s_ref[...] = jnp.concatenate(new_states, axis=0)

        def outputs():
            houts = []
            for c in n_all:
                a = col_ref[2 * c * L:(2 * c + 1) * L, :] + tile_up(ms[c], L // SUBLANES, 1)
                m_loc = col_ref[(2 * c + 1) * L:(2 * c + 2) * L, :]
                m_row = jnp.maximum(a, m_loc)
                w_inter = jnp.exp(a - m_row)
                w_loc = jnp.exp(m_loc - m_row)
                inter = live['inters'][c]
                intra = intra_ref[c * L:(c + 1) * L, :]
                num = w_inter * inter[:, :DV_B] + w_loc * intra[:, :DV_B]
                den = w_inter * inter[:, DV_B:] + w_loc * intra[:, DV_B:]
                houts.append(num / jnp.maximum(jnp.abs(den), jnp.exp(-m_row)))
            hf_ref[rows[0], :] = jnp.concatenate(houts[:H_B], axis=1)
            hb_ref[rows[1], :] = jnp.concatenate(houts[H_B:], axis=1)

        return (products, update, outputs), new_ms

    local_part(0, slots[0])

    def step_pair(i, ms):
        s0 = 2 * i
        stages, ms1 = recurrence_stages(s0, ms, slots[0])
        local_part(s0 + 1, slots[1], stages)
        stages, ms2 = recurrence_stages(s0 + 1, tuple(ms1), slots[1])
        local_part(jnp.minimum(s0 + 2, N_CHUNKS - 1), slots[0], stages)
        return tuple(ms2)

    m0 = tuple(jnp.full((SUBLANES, 128), NEG_BIG, F32) for _ in range(n_chain))
    lax.fori_loop(0, N_CHUNKS // 2, step_pair, m0)

    def finish(i, carry):
        rows = pl.ds(pl.multiple_of(i * TILE, TILE), TILE)
        hs = hf_ref[rows, :] + hb_ref[rows, :]
        ng = ng_ref[...]
        og = og_ref[0, rows, :].astype(F32)
        for h in range(H_B):
            sl = slice(h * DV_B, (h + 1) * DV_B)
            out_ref[0, rows, sl] = (_rms(hs[:, sl]) * ng[:, sl] * og[:, sl]).astype(BF16)
        return carry

    lax.fori_loop(0, T_ALL // TILE, finish, 0)


def _mlstm_call(mq, mk, mv, gc, gr, og, ng):
    B = mq.shape[0]
    full = lambda a: pl.BlockSpec((1,) + a.shape[1:], lambda b: (b,) + (0,) * (a.ndim - 1))
    return pl.pallas_call(
        _mlstm_kernel,
        grid=(B,),
        in_specs=[full(mq), full(mk), full(mv), full(gc), full(gr), full(og),
                  pl.BlockSpec(ng.shape, lambda b: (0, 0))],
        out_specs=pl.BlockSpec((1, T_ALL, H_B * DV_B), lambda b: (b, 0, 0)),
        out_shape=jax.ShapeDtypeStruct((B, T_ALL, H_B * DV_B), BF16),
        scratch_shapes=[
            pltpu.VMEM((T_ALL, H_B * DV_B), F32),
            pltpu.VMEM((T_ALL, H_B * DV_B), F32),
            pltpu.VMEM((2 * H_B * DK_B, 2 * DV_B), F32),
        ] + 2 * [
            pltpu.VMEM((2 * H_B * MLSTM_CHUNK, 2 * DV_B), F32),
            pltpu.VMEM((2 * H_B * DK_B, 2 * DV_B), F32),
            pltpu.VMEM((2 * H_B * 2 * MLSTM_CHUNK, 128), F32),
            pltpu.VMEM((2 * H_B * 2 * SUBLANES, 128), F32),
        ],
        compiler_params=_cparams(("arbitrary",)),
        name="mlstm",
    )(mq, mk, mv, gc, gr, og, ng)


def _na_bias_tables(rpb):
    kh = WIN_H
    n_drow = 2 * WIN_H - 1
    qcol = np.arange(GRID_W)
    col_start = np.clip(qcol - WIN_W // 2, 0, GRID_W - WIN_W)
    col_ok = (qcol[None, :] >= col_start[:, None]) & (qcol[None, :] < col_start[:, None] + WIN_W)
    dcol = qcol[None, :] - qcol[:, None] + (WIN_W - 1)
    onehot = (dcol[None] == np.arange(2 * WIN_W - 1)[:, None, None]) & col_ok[None]
    blocks = jnp.einsum('hdx,xck->hdck', rpb.astype(F32), jnp.asarray(onehot, F32),
                        precision=lax.Precision.HIGHEST)
    blocks = jnp.where(col_ok[None, None], blocks, NEG_BIG)
    outside = jnp.full((H_A, 1, GRID_W, GRID_W), NEG_BIG, F32)
    blocks = jnp.concatenate([blocks, outside], axis=1)
    idx = np.full((len(NA_CASES), NA_QROWS, NA_KROWS), n_drow, np.int32)
    for ci, r in enumerate(NA_CASES):
        u0 = int(np.clip(r - kh // 2, 0, GRID_ROWS - NA_KROWS))
        for qi in range(NA_QROWS):
            rq = r + qi
            r0 = int(np.clip(rq - kh // 2, 0, GRID_ROWS - kh))
            for ui in range(NA_KROWS):
                kr = u0 + ui
                if r0 <= kr < r0 + kh:
                    idx[ci, qi, ui] = kr - rq + (WIN_H - 1)

    def assemble(blk_ref, tab_ref):
        for ci in range(len(NA_CASES)):
            for qi in range(NA_QROWS):
                for ui in range(NA_KROWS):
                    tab_ref[ci, 0, qi * GRID_W:(qi + 1) * GRID_W, ui * GRID_W:(ui + 1) * GRID_W] = (
                        blk_ref[0, int(idx[ci, qi, ui])])

    return pl.pallas_call(
        assemble,
        grid=(H_A,),
        in_specs=[pl.BlockSpec((1, n_drow + 1, GRID_W, GRID_W), lambda h: (h, 0, 0, 0))],
        out_specs=pl.BlockSpec((len(NA_CASES), 1, NA_TQ, NA_TK), lambda h: (0, h, 0, 0)),
        out_shape=jax.ShapeDtypeStruct((len(NA_CASES), H_A, NA_TQ, NA_TK), F32),
        compiler_params=_cparams(("arbitrary",)),
        name="nbr_bias_table",
    )(blocks)


def _na_case(qb):
    case = jnp.int32(NA_STEP_CASE[0])
    for blk, c in enumerate(NA_STEP_CASE):
        if c != NA_STEP_CASE[0]:
            case = jnp.where(qb == blk, c, case)
    return case


def _na_kernel(q_ref, k_ref, v_ref, *refs):
    bias_refs, o_ref = refs[:-1], refs[-1]
    j = pl.program_id(1)
    lane = lax.broadcasted_iota(I32, (1, 2 * HD_A), 1)
    lo_half = lane < HD_A

    def attend(q_rows, key_sets, bias_for_head):
        for pp0 in range(0, H_A // 2, NA_STAGE_PAIRS):
            heads = []
            for pp in range(pp0, pp0 + NA_STAGE_PAIRS):
                lanes = slice(pp * 2 * HD_A, (pp + 1) * 2 * HD_A)
                qp = q_ref[0, q_rows, lanes]
                ks = [k_ref[0, rs, lanes] for rs in key_sets]
                vs = [v_ref[0, rs, lanes] for rs in key_sets]
                for hh in range(2):
                    qm = jnp.where(lo_half if hh == 0 else ~lo_half, qp, jnp.zeros_like(qp))
                    heads.append(dict(head=2 * pp + hh, vs=vs, ss=[_dot_nt(qm, kk) for kk in ks]))
            for hd in heads:
                ss = hd['ss']
                if bias_for_head is not None:
                    ss[0] = ss[0] + bias_for_head(hd['head'])
                m = ss[0].max(axis=-1, keepdims=True)
                for s in ss[1:]:
                    m = jnp.maximum(m, s.max(axis=-1, keepdims=True))
                ps = [jnp.exp(s - m) for s in ss]
                l = ps[0].sum(axis=-1, keepdims=True)
                for p in ps[1:]:
                    l = l + p.sum(axis=-1, keepdims=True)
                hd['ps'] = [p.astype(BF16) for p in ps]
                hd['l'] = l
            for hd in heads:
                acc = _dot(hd['ps'][0], hd['vs'][0])
                for p, vv in zip(hd['ps'][1:], hd['vs'][1:]):
                    acc = acc + _dot(p, vv)
                hd['o'] = acc / hd['l']
            for i, pp in enumerate(range(pp0, pp0 + NA_STAGE_PAIRS)):
                lanes = slice(pp * 2 * HD_A, (pp + 1) * 2 * HD_A)
                o_ref[0, q_rows, lanes] = jnp.where(lo_half, heads[2 * i]['o'], heads[2 * i + 1]['o']).astype(BF16)

    ctx_rows = pl.ds(0, CTX_LEN)

    @pl.when(j == 0)
    def _():
        attend(ctx_rows, [ctx_rows], None)

    @pl.when(j > 0)
    def _():
        for i, bias_ref in enumerate(bias_refs):
            r = ((j - 1) * NA_STEP_BLOCKS + i) * NA_QROWS
            u0 = jnp.clip(r - WIN_H // 2, 0, GRID_ROWS - NA_KROWS)
            q_rows = pl.ds(pl.multiple_of(CTX_LEN + r * GRID_W, NA_TQ), NA_TQ)
            k_rows = pl.ds(pl.multiple_of(CTX_LEN + u0 * GRID_W, GRID_W), NA_TK)
            attend(q_rows, [k_rows, ctx_rows], lambda head, ref=bias_ref: ref[0, head])


def _na_call(aq, ak, av, bias):
    B = aq.shape[0]
    full = pl.BlockSpec((1, T_ALL, D_A), lambda b, j: (b, 0, 0))
    n_blocks = GRID_ROWS // NA_QROWS
    bias_specs = [pl.BlockSpec((1, H_A, NA_TQ, NA_TK),
                               functools.partial(lambda b, j, i: (_na_case((j - 1) * NA_STEP_BLOCKS + i), 0, 0, 0), i=i))
                  for i in range(NA_STEP_BLOCKS)]
    return pl.pallas_call(
        _na_kernel,
        grid=(B, 1 + n_blocks // NA_STEP_BLOCKS),
        in_specs=[full, full, full] + bias_specs,
        out_specs=full,
        out_shape=jax.ShapeDtypeStruct((B, T_ALL, D_A), BF16),
        compiler_params=_cparams(("arbitrary", "arbitrary")),
        name="nbr_attn",
    )(aq, ak, av, *([bias] * NA_STEP_BLOCKS))


def _route(logits_t, rb_col):
    sc = _sigmoid(logits_t)
    sel = sc + rb_col
    selr = [sel[e:e + 1, :] for e in range(N_EXPERTS)]
    scr = [sc[e:e + 1, :] for e in range(N_EXPERTS)]
    gscore = []
    for g in range(N_GROUPS):
        a, b, c, d = selr[EXP_PER_GROUP * g:EXP_PER_GROUP * (g + 1)]
        s1, t1 = jnp.maximum(a, b), jnp.minimum(a, b)
        s2, t2 = jnp.maximum(c, d), jnp.minimum(c, d)
        gscore.append(jnp.maximum(s1, s2) + jnp.maximum(jnp.minimum(s1, s2), jnp.maximum(t1, t2)))
    best = gscore[0]
    gi = jnp.zeros(best.shape, I32)
    for g in range(1, N_GROUPS):
        better = gscore[g] > best
        gi = jnp.where(better, g, gi)
        best = jnp.where(better, gscore[g], best)
    vs, ws = [], []
    for k in range(EXP_PER_GROUP):
        v = selr[k]
        w = scr[k]
        for g in range(1, N_GROUPS):
            v = jnp.where(gi == g, selr[EXP_PER_GROUP * g + k], v)
            w = jnp.where(gi == g, scr[EXP_PER_GROUP * g + k], w)
        vs.append(v)
        ws.append(w)
    b1, i1 = vs[0], jnp.zeros(best.shape, I32)
    for k in range(1, EXP_PER_GROUP):
        better = vs[k] > b1
        i1 = jnp.where(better, k, i1)
        b1 = jnp.where(better, vs[k], b1)
    b2 = jnp.full(best.shape, -jnp.inf, F32)
    i2 = jnp.zeros(best.shape, I32)
    for k in range(EXP_PER_GROUP):
        vk = jnp.where(i1 == k, -jnp.inf, vs[k])
        better = vk > b2
        i2 = jnp.where(better, k, i2)
        b2 = jnp.where(better, vk, b2)
    w1 = ws[0]
    w2 = ws[0]
    for k in range(1, EXP_PER_GROUP):
        w1 = jnp.where(i1 == k, ws[k], w1)
        w2 = jnp.where(i2 == k, ws[k], w2)
    tot = w1 + w2
    return gi * EXP_PER_GROUP + i1, gi * EXP_PER_GROUP + i2, w1 / tot, w2 / tot


def _outproj_kernel(n_act, has_ctx, *refs):
    acts = refs[:n_act]
    ws = refs[n_act:2 * n_act]
    refs = refs[2 * n_act:]
    g_ref, rwh_ref, rwl_ref, xo_ref, hp_ref, lg_ref = refs[-6:]
    ns = xo_ref.shape[0]
    x_ins, (mods,) = _stream_tiles(has_ctx, has_ctx, refs[:-6], ns)
    o = _dot(acts[0][...].reshape(ns * TILE, -1), ws[0][...])
    for a, w in zip(acts[1:], ws[1:]):
        o = o + _dot(a[...].reshape(ns * TILE, -1), w[...])
    hs = []
    for s in range(ns):
        x = x_ins[s] + mods[s][2:3] * o[s * TILE:(s + 1) * TILE]
        xo_ref[s] = x
        h = _rms(x) * g_ref[...] * (1.0 + mods[s][4:5]) + mods[s][3:4]
        _store_token_tiles(hp_ref, (s,), 0, h)
        hs.append(_split_bf16(h, 2))
    for s, (h_hi, h_lo) in enumerate(hs):
        lg_ref[s] = (_dot_nt(rwh_ref[...], h_hi) + _dot_nt(rwh_ref[...], h_lo)
                     + _dot_nt(rwl_ref[...], h_hi))


def _outproj_call(acts, ws, x_src, x_off, modl, norm_g, rwh, rwl, n_tiles, ctx_src=None):
    B = x_src.shape[0]
    ns = math.gcd(B, PROJ_SAMPLES)
    n_act = len(acts)
    T = n_tiles * TILE
    const2 = lambda b, j: (0, 0)
    tok = lambda b, j: (b, j, 0)
    in_specs = [pl.BlockSpec((ns, TILE, a.shape[2]), tok) for a in acts]
    in_specs += [pl.BlockSpec(w.shape, const2) for w in ws]
    stream_specs, streams = _stream_specs(ns, x_src, x_off, ctx_src, ctx_src is not None, [modl])
    in_specs += stream_specs + [
        pl.BlockSpec((1, D_MODEL), const2),
        pl.BlockSpec(rwh.shape, const2),
        pl.BlockSpec(rwl.shape, const2),
    ]
    out_shape = [
        jax.ShapeDtypeStruct((B, T, D_MODEL), F32),
        jax.ShapeDtypeStruct((B, T * SUBLANES, 128), F32),
        jax.ShapeDtypeStruct((B, N_EXPERTS, T), F32),
    ]
    out_specs = [
        pl.BlockSpec((ns, TILE, D_MODEL), tok),
        pl.BlockSpec((ns, TILE * SUBLANES, 128), tok),
        pl.BlockSpec((ns, N_EXPERTS, TILE), lambda b, j: (b, 0, j)),
    ]
    return pl.pallas_call(
        functools.partial(_outproj_kernel, n_act, ctx_src is not None),
        grid=(B // ns, n_tiles),
        in_specs=in_specs,
        out_specs=out_specs,
        out_shape=out_shape,
        compiler_params=_cparams(("arbitrary", "arbitrary")),
        name="outproj",
    )(*acts, *ws, *streams, norm_g, rwh, rwl)


def _route_kernel(lg_ref, rb_ref, su_ref, ri_ref, rw_ref, cnt_ref):
    logits_t = lg_ref[0]
    T = logits_t.shape[1]
    e1, e2, w1, w2 = _route(logits_t, rb_ref[...])
    eidx = lax.broadcasted_iota(I32, logits_t.shape, 0)
    oh1 = eidx == e1
    oh2 = eidx == e2
    onehot = jnp.where(oh1, 1.0, jnp.where(oh2, 1.0, 0.0))
    count = jnp.zeros((N_EXPERTS, 1), F32)
    ranks = []
    for jj in range(T // TILE):
        oh = onehot[:, jj * TILE:(jj + 1) * TILE]
        ranks.append(_dot(oh.astype(BF16), su_ref[...]) + count)
        count = count + jnp.sum(oh, axis=1, keepdims=True)
    cpad = jnp.floor((count + (SUBLANES - 1.0)) * (1.0 / SUBLANES)) * SUBLANES
    ecol = lax.broadcasted_iota(I32, (N_EXPERTS, 1), 0)
    start = jnp.zeros((N_EXPERTS, 1), F32)
    for e in range(N_EXPERTS - 1):
        start = start + jnp.where(ecol > e, cpad[e:e + 1, :], 0.0)
    row = jnp.concatenate(ranks, axis=1) + start
    r1 = jnp.sum(jnp.where(oh1, row, 0.0), axis=0, keepdims=True)
    r2 = jnp.sum(jnp.where(oh2, row, 0.0), axis=0, keepdims=True)
    zi = jnp.zeros((SUBLANES - 4, T), I32)
    ri_ref[0] = jnp.concatenate([e1, e2, r1.astype(I32), r2.astype(I32), zi], axis=0)
    zf = jnp.zeros((SUBLANES - 2, T), F32)
    rw_ref[0] = jnp.concatenate([w1, w2, zf], axis=0)
    cnt_ref[0] = jnp.broadcast_to(count, (N_EXPERTS, 128)).astype(I32)


def _route_call(lg, rb, su):
    B, _, T = lg.shape
    per_sample = lambda b: (b, 0, 0)
    return pl.pallas_call(
        _route_kernel,
        grid=(B,),
        in_specs=[pl.BlockSpec((1, N_EXPERTS, T), per_sample),
                  pl.BlockSpec(rb.shape, lambda b: (0, 0)),
                  pl.BlockSpec(su.shape, lambda b: (0, 0))],
        out_specs=[pl.BlockSpec((1, SUBLANES, T), per_sample),
                   pl.BlockSpec((1, SUBLANES, T), per_sample),
                   pl.BlockSpec((1, N_EXPERTS, 128), per_sample)],
        out_shape=[jax.ShapeDtypeStruct((B, SUBLANES, T), I32),
                   jax.ShapeDtypeStruct((B, SUBLANES, T), F32),
                   jax.ShapeDtypeStruct((B, N_EXPERTS, 128), I32)],
        compiler_params=_cparams(("arbitrary",)),
        name="route",
    )(lg, rb, su)


def _moe_kernel(T, n_rows, cnt_ref, ri_ref, rw_ref, h_ref, w1_ref, w3_ref, w2_ref, y_ref,
                xb_ref, ob_ref, tokl_ref, off_ref):
    b = pl.program_id(0)
    e = pl.program_id(1)

    def tile_rows(row):
        return pl.ds(pl.multiple_of(row * SUBLANES, SUBLANES), SUBLANES)

    @pl.when(e == 0)
    def _():
        off_ref[0] = 0
        for i in range(N_EXPERTS):
            c = cnt_ref[b, i]
            off_ref[i + 1] = off_ref[i] + ((c + SUBLANES - 1) // SUBLANES) * SUBLANES

        for i in range(N_EXPERTS):
            for k in range(SUBLANES):
                tokl_ref[jnp.maximum(off_ref[i + 1] - SUBLANES + k, 0)] = 0

        def clear(i, carry):
            tokl_ref[off_ref[N_EXPERTS] + i] = 0
            return carry

        lax.fori_loop(0, MOE_RB, clear, 0, unroll=8)

        def place(t, carry):
            tokl_ref[ri_ref[2 * T + t]] = t
            tokl_ref[ri_ref[3 * T + t]] = t
            return carry

        lax.fori_loop(0, T, place, 0, unroll=8)

        def gather(i, carry):
            xb_ref[tile_rows(i), :] = h_ref[0, tile_rows(tokl_ref[i]), :]
            return carry

        lax.fori_loop(0, MOE_RB, gather, 0, unroll=8)

    c = cnt_ref[b, e]
    base = off_ref[e]
    next_base = off_ref[e + 1]

    def block(p0, n_blk_rows, p_next):
        xb = _load_token_tiles(xb_ref, (), 0, n_blk_rows).astype(BF16)
        for i in range(MOE_RB):
            xb_ref[i * SUBLANES:(i + 1) * SUBLANES, :] = h_ref[0, tile_rows(tokl_ref[p_next + i]), :]
        h1 = _dot(xb, w1_ref[0, 0])
        h3 = _dot(xb, w3_ref[0, 0])
        act = (h1 * _sigmoid(h1)) * h3
        _store_token_tiles(ob_ref, (), p0, _dot(act.astype(BF16), w2_ref[0, 0]))

    rows_up = ((c + MOE_RB_TAIL - 1) // MOE_RB_TAIL) * MOE_RB_TAIL
    n_big = rows_up // MOE_RB
    tail = rows_up - n_big * MOE_RB

    def big_block(rb, carry):
        p0 = pl.multiple_of(base + rb * MOE_RB, SUBLANES)
        last = jnp.logical_and(rb == n_big - 1, tail == 0)
        block(p0, MOE_RB, jnp.where(last, next_base, p0 + MOE_RB))
        return carry

    lax.fori_loop(0, n_big, big_block, 0)

    for tail_rows in range(MOE_RB_TAIL, MOE_RB, MOE_RB_TAIL):
        @pl.when(tail == tail_rows)
        def _():
            block(pl.multiple_of(base + n_big * MOE_RB, SUBLANES), tail_rows, next_base)

    @pl.when(e == N_EXPERTS - 1)
    def _():
        def combine(t, carry):
            y_ref[0, tile_rows(t), :] = (rw_ref[t] * ob_ref[tile_rows(ri_ref[2 * T + t]), :]
                                         + rw_ref[T + t] * ob_ref[tile_rows(ri_ref[3 * T + t]), :])
            return carry

        lax.fori_loop(0, T, combine, 0, unroll=8)


def _moe_call(cnt, ri, rw, h, w1, w3, w2, layer):
    B = h.shape[0]
    T = h.shape[1] // SUBLANES
    n_rows = 2 * T + N_EXPERTS * SUBLANES + MOE_RB
    smem = functools.partial(pl.BlockSpec, memory_space=pltpu.SMEM)
    once = pl.Buffered(1)
    return pl.pallas_call(
        functools.partial(_moe_kernel, T, n_rows),
        grid=(B, N_EXPERTS),
        in_specs=[
            smem(cnt.shape, lambda b, e: (0, 0)),
            smem((4 * T,), lambda b, e: (b,)),
            smem((4 * T,), lambda b, e: (b,)),
            pl.BlockSpec((1, T * SUBLANES, 128), lambda b, e: (b, 0, 0)),
            pl.BlockSpec((1, 1, D_MODEL, D_FF), lambda b, e: (layer, e, 0, 0)),
            pl.BlockSpec((1, 1, D_MODEL, D_FF), lambda b, e: (layer, e, 0, 0)),
            pl.BlockSpec((1, 1, D_FF, D_MODEL), lambda b, e: (layer, e, 0, 0)),
        ],
        out_specs=pl.BlockSpec((1, T * SUBLANES, 128), lambda b, e: (b, 0, 0), pipeline_mode=once),
        out_shape=jax.ShapeDtypeStruct((B, T * SUBLANES, 128), F32),
        scratch_shapes=[
            pltpu.VMEM((MOE_RB * SUBLANES, 128), F32),
            pltpu.VMEM((n_rows * SUBLANES, 128), F32),
            pltpu.SMEM((n_rows,), I32),
            pltpu.SMEM((N_EXPERTS + 1,), I32),
        ],
        compiler_params=_cparams(("arbitrary", "arbitrary")),
        name="moe",
    )(cnt, ri, rw, h, w1, w3, w2)


def _route_tables(ri, rw, cnt):
    return cnt[:, :, 0], ri[:, :4].reshape(-1), rw[:, :4].reshape(-1)


def _inproj1_kernel(*refs):
    (y_ref, g_ref, w_ref, qg_ref, kg_ref, cos_ref, sin_ref, xo_ref, q_ref, k_ref, v_ref) = refs[-11:]
    ns = xo_ref.shape[0]
    x_ins, (mods0, mods) = _stream_tiles(False, True, refs[:-11], ns)
    hbs = []
    for s in range(ns):
        x = x_ins[s] + mods0[s][5:6] * _load_token_tiles(y_ref, (s,), 0, TILE)
        xo_ref[s] = x
        hbs.append((_rms(x) * g_ref[...] * (1.0 + mods[s][1:2]) + mods[s][0:1]).astype(BF16))
    hb = jnp.concatenate(hbs, axis=0)
    cos = cos_ref[...]
    sin = sin_ref[...]

    def rope_heads(ref, proj, n_heads, gain, scale):
        for s in range(ns):
            for h in range(n_heads):
                sl = slice(h * HD_C, (h + 1) * HD_C)
                n = _rms(proj[s * TILE:(s + 1) * TILE, sl]) * gain
                r = n * cos + pltpu.roll(n, HD_C // 2, 1) * sin
                ref[s, :, sl] = (r if scale is None else r * scale).astype(BF16)

    ko = H_C * HD_C
    qkv = _dot(hb, w_ref[...])
    qs, kv = qkv[:, :ko], qkv[:, ko:]
    rope_heads(k_ref, kv, KV_C, kg_ref[...], None)
    for s in range(ns):
        v_ref[s] = kv[s * TILE:(s + 1) * TILE, KV_C * HD_C:].astype(BF16)
    rope_heads(q_ref, qs, H_C, qg_ref[...], HD_C ** -0.5)


def _inproj1_call(x1, y0, mod0, mod1, norm_g, w, qg, kg, cos, sin):
    B = x1.shape[0]
    ns = math.gcd(B, INPROJ1_SAMPLES)
    tok = lambda b, j: (b, j, 0)
    const2 = lambda b, j: (0, 0)
    lat_tok = lambda b, j: (b, jnp.maximum(j - NT_CTX, 0), 0)
    widths = [(T_ALL, D_MODEL, F32), (SEQ, H_C * HD_C, BF16),
              (T_ALL, KV_C * HD_C, BF16), (T_ALL, KV_C * HD_C, BF16)]
    stream_specs, streams = _stream_specs(ns, x1, 0, None, True, [mod0, mod1])
    return pl.pallas_call(
        _inproj1_kernel,
        grid=(B // ns, NT_ALL),
        in_specs=stream_specs + [
            pl.BlockSpec((ns, TILE * SUBLANES, 128), tok),
            pl.BlockSpec((1, D_MODEL), const2),
            pl.BlockSpec(w.shape, const2),
            pl.BlockSpec(qg.shape, const2),
            pl.BlockSpec(kg.shape, const2),
            pl.BlockSpec((TILE, HD_C), lambda b, j: (j, 0)),
            pl.BlockSpec((TILE, HD_C), lambda b, j: (j, 0)),
        ],
        out_specs=[pl.BlockSpec((ns, TILE, w_), lat_tok if t == SEQ else tok) for t, w_, _ in widths],
        out_shape=[jax.ShapeDtypeStruct((B, t, w_), dt) for t, w_, dt in widths],
        compiler_params=_cparams(("arbitrary", "arbitrary")),
        name="inproj1",
    )(*streams, y0, norm_g, w, qg, kg, cos, sin)


def _gqa_kernel(q_ref, k_ref, v_ref, o_ref):
    group_heads = H_C // KV_C
    for g in range(KV_C):
        k = k_ref[0, :, g * HD_C:(g + 1) * HD_C]
        v = v_ref[0, :, g * HD_C:(g + 1) * HD_C]
        for h0 in range(g * group_heads, (g + 1) * group_heads, GQA_STAGE_HEADS):
            sls = [slice(h * HD_C, (h + 1) * HD_C) for h in range(h0, h0 + GQA_STAGE_HEADS)]
            ss = [_dot_nt(q_ref[0, :, sl], k) for sl in sls]
            ps, ls = [], []
            for s in ss:
                p = jnp.exp(s - s.max(axis=-1, keepdims=True))
                ls.append(p.sum(axis=-1, keepdims=True))
                ps.append(p.astype(BF16))
            for sl, p, l in zip(sls, ps, ls):
                o_ref[0, :, sl] = (_dot(p, v) / l).astype(BF16)


def _gqa_call(q, k, v):
    B = q.shape[0]
    whole = lambda b, j: (b, 0, 0)
    return pl.pallas_call(
        _gqa_kernel,
        grid=(B, SEQ // GQA_TQ),
        in_specs=[
            pl.BlockSpec((1, GQA_TQ, H_C * HD_C), lambda b, j: (b, j, 0)),
            pl.BlockSpec((1, T_ALL, KV_C * HD_C), whole),
            pl.BlockSpec((1, T_ALL, KV_C * HD_C), whole),
        ],
        out_specs=pl.BlockSpec((1, GQA_TQ, H_C * HD_C), lambda b, j: (b, j, 0)),
        out_shape=jax.ShapeDtypeStruct((B, SEQ, H_C * HD_C), BF16),
        compiler_params=_cparams(("arbitrary", "arbitrary")),
        name="gqa",
    )(q, k, v)


def _final_kernel(x_ref, y_ref, mod_ref, o_ref):
    o_ref[0] = x_ref[0] + mod_ref[0][5:6] * _load_token_tiles(y_ref, (0,), 0, FINAL_TILE)


def _final_call(x, y, modl):
    B, T, D = x.shape
    tok = lambda b, j: (b, j, 0)
    return pl.pallas_call(
        _final_kernel,
        grid=(B, T // FINAL_TILE),
        in_specs=[pl.BlockSpec((1, FINAL_TILE, D), tok), pl.BlockSpec((1, FINAL_TILE * SUBLANES, 128), tok),
                  pl.BlockSpec((1, 6, D), lambda b, j: (b, 0, 0))],
        out_specs=pl.BlockSpec((1, FINAL_TILE, D), tok),
        out_shape=jax.ShapeDtypeStruct((B, T, D), F32),
        compiler_params=_cparams(("arbitrary", "arbitrary")),
        name="final_residual",
    )(x, y, modl)


def _chunk_tri(lower):
    i = np.arange(TILE)
    same = (i[:, None] // MLSTM_CHUNK) == (i[None, :] // MLSTM_CHUNK)
    tri = (i[None, :] <= i[:, None]) if lower else (i[None, :] >= i[:, None])
    return jnp.asarray((same & tri).astype(np.float32), BF16)


def _rope_tables():
    n_freq = HD_C // 4
    inv_freq = ROPE_THETA ** (-jnp.arange(n_freq, dtype=F32) / n_freq)
    t = jnp.arange(SEQ)
    rows = (t // GRID_W).astype(F32)
    cols = (t % GRID_W).astype(F32)
    ang = jnp.concatenate([rows[:, None] * inv_freq, cols[:, None] * inv_freq], axis=-1)
    cos, sin = jnp.cos(ang), jnp.sin(ang)
    cos_l = jnp.concatenate([cos, cos], axis=-1)
    sin_l = jnp.concatenate([-sin, sin], axis=-1)
    cos_all = jnp.concatenate([jnp.ones((CTX_LEN, HD_C), F32), cos_l], axis=0)
    sin_all = jnp.concatenate([jnp.zeros((CTX_LEN, HD_C), F32), sin_l], axis=0)
    return cos_all, sin_all


_HEAD_PERM = np.concatenate([np.arange(0, HD_C, 2), np.arange(1, HD_C, 2)])


def kernel(x, c, ctx, c_ctx, ada_w, ada_b, norm_mix_g, norm_ffn_g, even_w_in, even_w_out,
           na_q_norm_g, na_k_norm_g, na_rpb, mlstm_gate_b, mlstm_norm_g, odd_w_in, odd_w_out,
           gqa_q_norm_g, gqa_k_norm_g, router_w, router_b, exp_w1, exp_w3, exp_w2):
    B = x.shape[0]
    assert B <= N_MOD_CTX_ROW and x.shape[1:] == (SEQ, D_MODEL) and ctx.shape[1:] == (CTX_LEN, D_MODEL)
    n_g = 4 * H_B

    cvec = jnp.zeros((N_MOD_ROWS, D_MODEL), F32).at[:B].set(c).at[N_MOD_CTX_ROW].set(c_ctx)
    mod = _ada_call(cvec, ada_w, ada_b).reshape(2, N_MOD_ROWS, 6, D_MODEL)
    mod0, mod1 = mod[0], mod[1]

    w_in = even_w_in[0]
    n_main = w_in.shape[1] - n_g
    wm = w_in[:, :n_main].astype(BF16)
    wg_f = w_in[:, n_main:]
    wg = jnp.pad(wg_f, ((0, 0), (0, 128 - n_g))).astype(BF16)
    wgt = wg_f.T.astype(BF16)
    gb = jnp.pad(mlstm_gate_b[0].reshape(1, n_g), ((0, 0), (0, 128 - n_g)))
    gbt = mlstm_gate_b[0].reshape(n_g, 1)
    qg = jnp.tile(na_q_norm_g[0], H_A).reshape(1, D_A)
    kg = jnp.tile(na_k_norm_g[0], H_A).reshape(1, D_A)
    hid = np.arange(128) // HD_A
    bd = jnp.asarray((hid[:, None] == hid[None, :]).astype(np.float32), BF16)
    k_lo = 3 * D_A + H_B * DK_B
    wkt = w_in[:, k_lo:k_lo + H_B * DK_B].T.astype(BF16)
    aq, ak, av, mq, mv, og, mkt, gc, gr = _inproj0_call(
        x, ctx, mod0, norm_mix_g[0].reshape(1, D_MODEL), wm, wkt, wg, wgt, gb, gbt, qg, kg, bd,
        _chunk_tri(True), _chunk_tri(False))
    hm = _mlstm_call(mq, mkt, mv, gc, gr, og, mlstm_norm_g[0].reshape(1, H_B * DV_B))
    oa = _na_call(aq, ak, av, _na_bias_tables(na_rpb[0]))

    rw_t = router_w.T
    rwh = rw_t.astype(BF16)
    rwl = (rw_t - rwh.astype(F32)).astype(BF16)
    rb = router_b.reshape(N_EXPERTS, 1).astype(F32)
    i = np.arange(TILE)
    su = jnp.asarray((i[:, None] < i[None, :]).astype(np.float32), BF16)
    w_out = even_w_out[0].astype(BF16)
    ew1, ew3, ew2 = exp_w1.astype(BF16), exp_w3.astype(BF16), exp_w2.astype(BF16)
    x1, hp0, lg0 = _outproj_call(
        [oa, hm], [w_out[:D_A], w_out[D_A:]], x, 0, mod0,
        norm_ffn_g[0].reshape(1, D_MODEL), rwh, rwl, NT_ALL, ctx_src=ctx)
    y0 = _moe_call(*_route_tables(*_route_call(lg0, rb, su)), hp0, ew1, ew3, ew2, 0)

    w1_in = odd_w_in[0]
    qk_cols = np.concatenate([h * HD_C + _HEAD_PERM for h in range(H_C + KV_C)])
    cols = np.concatenate([qk_cols, np.arange((H_C + KV_C) * HD_C, w1_in.shape[1])])
    w1_in = w1_in[:, cols].astype(BF16)
    cos_all, sin_all = _rope_tables()
    x2, q, k, v = _inproj1_call(
        x1, y0, mod0, mod1, norm_mix_g[1].reshape(1, D_MODEL), w1_in,
        gqa_q_norm_g[0][_HEAD_PERM].reshape(1, HD_C), gqa_k_norm_g[0][_HEAD_PERM].reshape(1, HD_C),
        cos_all, sin_all)
    o = _gqa_call(q, k, v)
    x3, hp1, lg1 = _outproj_call(
        [o], [odd_w_out[0].astype(BF16)], x2, NT_CTX, mod1,
        norm_ffn_g[1].reshape(1, D_MODEL), rwh, rwl, SEQ // TILE)
    y1 = _moe_call(*_route_tables(*_route_call(lg1, rb, su)), hp1, ew1, ew3, ew2, 1)
    return _final_call(x3, y1, mod1)
```

```python
import functools
import math

import numpy as np
import jax
import jax.numpy as jnp
from jax import lax
from jax.experimental import pallas as pl
from jax.experimental.pallas import tpu as pltpu

F32 = jnp.float32
BF16 = jnp.bfloat16
I32 = jnp.int32

D_MODEL = 1024
SEQ = 2048
GRID_W = 64
GRID_ROWS = SEQ // GRID_W
CTX_LEN = 256
T_ALL = CTX_LEN + SEQ
WIN_H = 8
WIN_W = 16
HD_A = 64
H_A = 8
D_A = H_A * HD_A
H_B = 4
DV_B = 128
DK_B = 64
MLSTM_CHUNK = 64
N_CHUNKS = T_ALL // MLSTM_CHUNK
N_CTX_CHUNKS = CTX_LEN // MLSTM_CHUNK
HD_C = 128
H_C = 8
KV_C = 2
ROPE_THETA = 10000.0
N_EXPERTS = 16
N_GROUPS = 4
EXP_PER_GROUP = 4
D_FF = 512
EPS = 1e-6
NEG_BIG = -1e30

TILE = 256
NT_ALL = T_ALL // TILE
NT_CTX = CTX_LEN // TILE
NA_QROWS = 4
NA_KROWS = WIN_H + NA_QROWS - 1
NA_TQ = NA_QROWS * GRID_W
NA_TK = NA_KROWS * GRID_W
NA_STAGE_PAIRS = 2
NA_STEP_BLOCKS = 2


def _na_geometry(r):
    u0 = min(max(r - WIN_H // 2, 0), GRID_ROWS - NA_KROWS)
    r0s = tuple(min(max(r + qi - WIN_H // 2, 0), GRID_ROWS - WIN_H) - r for qi in range(NA_QROWS))
    return (u0 - r, r0s)


def _na_cases():
    reps, step_case = [], []
    for r in range(0, GRID_ROWS, NA_QROWS):
        geo = _na_geometry(r)
        known = [_na_geometry(q) for q in reps]
        if geo not in known:
            reps.append(r)
            known.append(geo)
        step_case.append(known.index(geo))
    return tuple(reps), tuple(step_case)


NA_CASES, NA_STEP_CASE = _na_cases()
PROJ_SAMPLES = 4
INPROJ0_SAMPLES = 4
INPROJ1_SAMPLES = 1
GQA_TQ = 512
GQA_STAGE_HEADS = 2
FINAL_TILE = 1024
MOE_RB = 256
MOE_RB_TAIL = 64
SUBLANES = 8
VMEM_LIMIT = 56 * 1024 * 1024


def _cparams(sem, fuse_inputs=None):
    return pltpu.CompilerParams(dimension_semantics=sem, vmem_limit_bytes=VMEM_LIMIT,
                                allow_input_fusion=fuse_inputs)


def _sigmoid(x):
    return 1.0 / (1.0 + jnp.exp(-x))


def _rms(x):
    return x * lax.rsqrt(jnp.mean(x * x, axis=-1, keepdims=True) + EPS)


def _dot(a, b):
    return jnp.dot(a, b, preferred_element_type=F32)


def _dot_nt(a, b):
    return lax.dot_general(a, b, (((1,), (1,)), ((), ())), preferred_element_type=F32)


def _split_bf16(x, n):
    parts = []
    r = x
    for _ in range(n):
        p = r.astype(BF16)
        parts.append(p)
        r = r - p.astype(F32)
    return parts


LANE_TILES = D_MODEL // 128


def _load_token_tiles(ref, lead, tok0, n_tok):
    parts = [ref[(*lead, pl.ds(tok0 * SUBLANES + c, n_tok, stride=SUBLANES), slice(None))]
             for c in range(LANE_TILES)]
    return jnp.concatenate(parts, axis=1)


def _store_token_tiles(ref, lead, tok0, val):
    for c in range(LANE_TILES):
        ref[(*lead, pl.ds(tok0 * SUBLANES + c, val.shape[0], stride=SUBLANES), slice(None))] = (
            val[:, c * 128:(c + 1) * 128])


def _stream_specs(ns, x_src, x_off, ctx_src, ctx_tile, mod_tables):
    if ctx_src is None:
        specs = [pl.BlockSpec((ns, TILE, D_MODEL), lambda b, j: (b, j + x_off, 0))]
        arrays = [x_src]
    else:
        assert x_off == 0 and ctx_tile
        specs = [pl.BlockSpec((ns, TILE, D_MODEL), lambda b, j: (b, jnp.maximum(j - NT_CTX, 0), 0)),
                 pl.BlockSpec((ns, TILE, D_MODEL), lambda b, j: (b, jnp.minimum(j, NT_CTX - 1), 0))]
        arrays = [x_src, ctx_src]
    for table in mod_tables:
        specs.append(pl.BlockSpec((ns, 6, D_MODEL), lambda b, j: (b, 0, 0)))
        arrays.append(table)
        if ctx_tile:
            specs.append(pl.BlockSpec((1, 6, D_MODEL), lambda b, j: (N_MOD_CTX_ROW, 0, 0)))
            arrays.append(table)
    return specs, arrays


def _stream_tiles(two_streams, ctx_tile, refs, ns):
    is_ctx = pl.program_id(1) < NT_CTX
    if two_streams:
        x_ins = [jnp.where(is_ctx, refs[1][s], refs[0][s]) for s in range(ns)]
        refs = refs[2:]
    else:
        x_ins = [refs[0][s] for s in range(ns)]
        refs = refs[1:]
    tables = []
    if ctx_tile:
        for t in range(0, len(refs), 2):
            tables.append([jnp.where(is_ctx, refs[t + 1][0], refs[t][s]) for s in range(ns)])
    else:
        for ref in refs:
            tables.append([ref[s] for s in range(ns)])
    return x_ins, tables


N_MOD_ROWS = 16
N_MOD_CTX_ROW = 8


ADA_TN = 1536


def _ada_kernel(c_ref, w_ref, b_ref, o_ref):
    c = c_ref[...]
    s = (c * _sigmoid(c)).astype(BF16)
    o_ref[0] = _dot(s, w_ref[0].astype(BF16)) + b_ref[0]


def _ada_call(cvec, ada_w, ada_b):
    depth, d, n = ada_w.shape
    return pl.pallas_call(
        _ada_kernel,
        grid=(depth, n // ADA_TN),
        in_specs=[
            pl.BlockSpec((N_MOD_ROWS, d), lambda l, j: (0, 0)),
            pl.BlockSpec((1, d, ADA_TN), lambda l, j: (l, 0, j)),
            pl.BlockSpec((1, 1, ADA_TN), lambda l, j: (l, 0, j)),
        ],
        out_specs=pl.BlockSpec((1, N_MOD_ROWS, ADA_TN), lambda l, j: (l, 0, j)),
        out_shape=jax.ShapeDtypeStruct((depth, N_MOD_ROWS, n), F32),
        compiler_params=_cparams(("arbitrary", "arbitrary")),
        name="ada_mod",
    )(cvec, ada_w, ada_b.reshape(depth, 1, n))


def _log_sigmoid(x):
    return jnp.minimum(x, 0.0) - jnp.log1p(jnp.exp(-jnp.abs(x)))


def _inproj0_kernel(*refs):
    (g_ref, wm_ref, wkt_ref, wg_ref, wgt_ref, gb_ref, gbt_ref, qg_ref, kg_ref, bd_ref, tril_ref, triu_ref,
     aq_ref, ak_ref, av_ref, mq_ref, mv_ref, og_ref, mkt_ref, gc_ref, gr_ref) = refs[-21:]
    ns = aq_ref.shape[0]
    x_ins, (mods,) = _stream_tiles(True, True, refs[:-21], ns)
    hb = jnp.concatenate([(_rms(x_ins[s]) * g_ref[...] * (1.0 + mods[s][1:2]) + mods[s][0:1]).astype(BF16)
                          for s in range(ns)], axis=0)

    def per_sample(ref, val):
        for s in range(ns):
            ref[s] = val[s * TILE:(s + 1) * TILE]

    def proj(lo, hi):
        return _dot(hb, wm_ref[:, lo:hi])

    def head_norm(a, gain):
        a2 = a * a
        sums = []
        for s0 in range(0, D_A, 128):
            hi_, lo_ = _split_bf16(a2[:, s0:s0 + 128], 2)
            sums.append(_dot(hi_, bd_ref[...]) + _dot(lo_, bd_ref[...]))
        ss = jnp.concatenate(sums, axis=1)
        return a * lax.rsqrt(ss * (1.0 / HD_A) + EPS) * gain

    n_g = 4 * H_B
    p_aq, p_ak, p_av = proj(0, 512), proj(512, 1024), proj(1024, 1536)
    p_mq, p_mv, p_og = proj(1536, 1792), proj(2048, 2560), proj(2560, 3072)
    kt = _dot_nt(wkt_ref[...], hb).astype(BF16)
    gcol = _dot(hb, wg_ref[...]) + gb_ref[...]
    grow = _dot_nt(wgt_ref[...], hb) + gbt_ref[...]

    per_sample(aq_ref, (head_norm(p_aq, qg_ref[...]) * HD_A ** -0.5).astype(BF16))
    per_sample(ak_ref, head_norm(p_ak, kg_ref[...]).astype(BF16))
    per_sample(av_ref, p_av.astype(BF16))
    per_sample(mq_ref, (p_mq * DK_B ** -0.5).astype(BF16))
    per_sample(mv_ref, p_mv.astype(BF16))
    per_sample(og_ref, _sigmoid(p_og).astype(BF16))

    tril = tril_ref[...]
    triu = triu_ref[...]
    ls_c = _split_bf16(_log_sigmoid(gcol), 3)
    ls_r = _split_bf16(_log_sigmoid(grow), 3)
    cidx = lax.broadcasted_iota(I32, (TILE, 128), 1)
    ridx = lax.broadcasted_iota(I32, (n_g, TILE), 0)

    def pick(idx, raw, pre, suf):
        is_ff = (idx >= H_B) & (idx < 2 * H_B)
        is_fb = idx >= 3 * H_B
        return jnp.where(is_ff, pre, jnp.where(is_fb, suf, raw))

    for s in range(ns):
        tok = slice(s * TILE, (s + 1) * TILE)
        pre_c = sum(_dot(tril, p[tok, :]) for p in ls_c)
        suf_c = sum(_dot(triu, p[tok, :]) for p in ls_c)
        pre_r = sum(_dot(p[:, tok], triu) for p in ls_r)
        suf_r = sum(_dot(p[:, tok], tril) for p in ls_r)
        gc_ref[s] = pick(cidx, gcol[tok, :], pre_c, suf_c)[:, :n_g]
        grow_s = pick(ridx, grow[:, tok], pre_r, suf_r)
        for c in range(TILE // MLSTM_CHUNK):
            chunk = slice(c * MLSTM_CHUNK, (c + 1) * MLSTM_CHUNK)
            gr_ref[s, c] = grow_s[:, chunk]
            mkt_ref[s, c] = kt[:, s * TILE + c * MLSTM_CHUNK:s * TILE + (c + 1) * MLSTM_CHUNK]


def _inproj0_call(x, ctx, modl, norm_g, wm, wkt, wg, wgt, gb, gbt, qg, kg, bd, tril, triu):
    B = x.shape[0]
    ns = math.gcd(B, INPROJ0_SAMPLES)
    n_g = 4 * H_B
    tok = lambda b, j: (b, j, 0)
    const2 = lambda b, j: (0, 0)
    chunked = lambda b, j: (b, j, 0, 0)
    tile_chunks = TILE // MLSTM_CHUNK
    outs = [
        (D_A, BF16), (D_A, BF16), (D_A, BF16),
        (H_B * DK_B, BF16), (H_B * DV_B, BF16),
        (H_B * DV_B, BF16),
    ]
    out_shape = [jax.ShapeDtypeStruct((B, T_ALL, w), dt) for w, dt in outs]
    out_specs = [pl.BlockSpec((ns, TILE, w), tok) for w, _ in outs]
    out_shape += [jax.ShapeDtypeStruct((B, N_CHUNKS, H_B * DK_B, MLSTM_CHUNK), BF16),
                  jax.ShapeDtypeStruct((B, T_ALL, n_g), F32),
                  jax.ShapeDtypeStruct((B, N_CHUNKS, n_g, MLSTM_CHUNK), F32)]
    out_specs += [pl.BlockSpec((ns, tile_chunks, H_B * DK_B, MLSTM_CHUNK), chunked),
                  pl.BlockSpec((ns, TILE, n_g), tok),
                  pl.BlockSpec((ns, tile_chunks, n_g, MLSTM_CHUNK), chunked)]
    stream_specs, streams = _stream_specs(ns, x, 0, ctx, True, [modl])
    return pl.pallas_call(
        _inproj0_kernel,
        grid=(B // ns, NT_ALL),
        in_specs=stream_specs + [
            pl.BlockSpec((1, D_MODEL), const2),
            pl.BlockSpec(wm.shape, const2, pipeline_mode=pl.Buffered(1)),
            pl.BlockSpec(wkt.shape, const2),
            pl.BlockSpec(wg.shape, const2),
            pl.BlockSpec(wgt.shape, const2),
            pl.BlockSpec(gb.shape, const2),
            pl.BlockSpec(gbt.shape, const2),
            pl.BlockSpec(qg.shape, const2),
            pl.BlockSpec(kg.shape, const2),
            pl.BlockSpec(bd.shape, const2),
            pl.BlockSpec(tril.shape, const2),
            pl.BlockSpec(triu.shape, const2),
        ],
        out_specs=out_specs,
        out_shape=out_shape,
        compiler_params=_cparams(("arbitrary", "arbitrary")),
        name="inproj0",
    )(*streams, norm_g, wm, wkt, wg, wgt, gb, gbt, qg, kg, bd, tril, triu)


def _mlstm_kernel(mq_ref, mkt_ref, mv_ref, gc_ref, gr_ref, og_ref, ng_ref, out_ref,
                  hf_ref, hb_ref, s_ref, *local_refs):
    L = MLSTM_CHUNK
    n_chain = 2 * H_B
    slots = (local_refs[:4], local_refs[4:])
    s_ref[...] = jnp.zeros(s_ref.shape, F32)
    ri = lax.broadcasted_iota(I32, (L, L), 0)
    ci = lax.broadcasted_iota(I32, (L, L), 1)
    masks = (ci <= ri, ci >= ri)
    ones_aug = jnp.ones((L, DV_B), BF16)

    def tile_up(x, n_rows, n_cols):
        x = jnp.concatenate([x] * n_rows, axis=0)
        return x if n_cols == 1 else jnp.concatenate([x] * n_cols, axis=1)

    def chunk_rows(step, d):
        if d == 0:
            chunk = step
        else:
            chunk = jnp.where(step < N_CTX_CHUNKS, N_CTX_CHUNKS - 1 - step, N_CHUNKS + N_CTX_CHUNKS - 1 - step)
        return chunk, pl.ds(pl.multiple_of(chunk * L, L), L)

    def local_part(step, slot, rec=None):
        intra_ref, u_ref, col_ref, rep_ref = slot
        rec = rec or (lambda: None,) * 3
        chains = []
        for d in range(2):
            chunk, rows = chunk_rows(step, d)
            gcc = gc_ref[0, rows, :]
            grr = gr_ref[0, chunk]
            for h in range(H_B):
                gi = 2 * H_B * d + h
                b_col = gcc[:, gi + H_B:gi + H_B + 1]
                chains.append(dict(
                    d=d,
                    q=mq_ref[0, rows, h * DK_B:(h + 1) * DK_B],
                    kt=mkt_ref[0, chunk, h * DK_B:(h + 1) * DK_B, :],
                    v=mv_ref[0, rows, h * DV_B:(h + 1) * DV_B],
                    ig_col=gcc[:, gi:gi + 1], b_col=b_col,
                    ig_row=grr[gi:gi + 1, :], b_row=grr[gi + H_B:gi + H_B + 1, :],
                    b_last=b_col[L - 1:L, :] if d == 0 else b_col[0:1, :]))
        for ch in chains:
            ch['qk'] = _dot(ch['q'], ch['kt'])
            ch['v_aug'] = jnp.concatenate([ch['v'], ones_aug], axis=1)
            g = ch['b_last'] - ch['b_col'] + ch['ig_col']
            ch['g_max'] = jnp.max(g, axis=0, keepdims=True)
            ch['wv'] = (jnp.exp(g - ch['g_max']) * ch['v_aug'].astype(F32)).astype(BF16)
        us = [_dot(ch['kt'], ch['wv']) for ch in chains]
        for ch in chains:
            dm = jnp.where(masks[ch['d']], ch['b_col'] - ch['b_row'] + ch['ig_row'], NEG_BIG)
            ch['m_loc'] = jnp.max(dm, axis=-1, keepdims=True)
            ch['p'] = (ch['qk'] * jnp.exp(dm - ch['m_loc'])).astype(BF16)
        intras = [_dot(ch['p'], ch['v_aug']) for ch in chains]
        rec[0]()
        cols, reps = [], []
        for ch in chains:
            cols += [jnp.broadcast_to(ch['b_col'], (L, 128)), jnp.broadcast_to(ch['m_loc'], (L, 128))]
            reps += [jnp.broadcast_to(ch['b_last'], (SUBLANES, 128)),
                     jnp.broadcast_to(ch['g_max'], (SUBLANES, 128))]
        intra_ref[...] = jnp.concatenate(intras, axis=0)
        u_ref[...] = jnp.concatenate(us, axis=0)
        col_ref[...] = jnp.concatenate(cols, axis=0)
        rep_ref[...] = jnp.concatenate(reps, axis=0)
        rec[1]()
        rec[2]()

    def recurrence_stages(step, ms, slot):
        intra_ref, u_ref, col_ref, rep_ref = slot
        n_all = range(n_chain)
        rows = [chunk_rows(step, d)[1] for d in range(2)]
        live = {}
        new_ms = []

        def products():
            live['states'] = [s_ref[c * DK_B:(c + 1) * DK_B, :] for c in n_all]
            live['inters'] = [_dot(mq_ref[0, rows[c // H_B], (c % H_B) * DK_B:(c % H_B + 1) * DK_B],
                                   live['states'][c].astype(BF16)) for c in n_all]

        def update():
            new_states = []
            for c in n_all:
                m = ms[c]
                b_last = rep_ref[2 * c * SUBLANES:(2 * c + 1) * SUBLANES, :]
                g_max = rep_ref[(2 * c + 1) * SUBLANES:(2 * c + 2) * SUBLANES, :]
                m_new = jnp.maximum(b_last + m, g_max)
                w_old = tile_up(jnp.exp(b_last + m - m_new), DK_B // SUBLANES, 2)
                w_new = tile_up(jnp.exp(g_max - m_new), DK_B // SUBLANES, 2)
                new_states.append(w_old * live['states'][c] + w_new * u_ref[c * DK_B:(c + 1) * DK_B, :])
                new_ms.append(m_new)
            s_ref[...] = jnp.concatenate(new_states, axis=0)

        def outputs():
            houts = []
            for c in n_all:
                a = col_ref[2 * c * L:(2 * c + 1) * L, :] + tile_up(ms[c], L // SUBLANES, 1)
                m_loc = col_ref[(2 * c + 1) * L:(2 * c + 2) * L, :]
                m_row = jnp.maximum(a, m_loc)
                w_inter = jnp.exp(a - m_row)
                w_loc = jnp.exp(m_loc - m_row)
                inter = live['inters'][c]
                intra = intra_ref[c * L:(c + 1) * L, :]
                num = w_inter * inter[:, :DV_B] + w_loc * intra[:, :DV_B]
                den = w_inter * inter[:, DV_B:] + w_loc * intra[:, DV_B:]
                houts.append(num / jnp.maximum(jnp.abs(den), jnp.exp(-m_row)))
            hf_ref[rows[0], :] = jnp.concatenate(houts[:H_B], axis=1)
            hb_ref[rows[1], :] = jnp.concatenate(houts[H_B:], axis=1)

        return (products, update, outputs), new_ms

    local_part(0, slots[0])

    def step_pair(i, ms):
        s0 = 2 * i
        stages, ms1 = recurrence_stages(s0, ms, slots[0])
        local_part(s0 + 1, slots[1], stages)
        stages, ms2 = recurrence_stages(s0 + 1, tuple(ms1), slots[1])
        local_part(jnp.minimum(s0 + 2, N_CHUNKS - 1), slots[0], stages)
        return tuple(ms2)

    m0 = tuple(jnp.full((SUBLANES, 128), NEG_BIG, F32) for _ in range(n_chain))
    lax.fori_loop(0, N_CHUNKS // 2, step_pair, m0)

    def finish(i, carry):
        rows = pl.ds(pl.multiple_of(i * TILE, TILE), TILE)
        hs = hf_ref[rows, :] + hb_ref[rows, :]
        ng = ng_ref[...]
        og = og_ref[0, rows, :].astype(F32)
        for h in range(H_B):
            sl = slice(h * DV_B, (h + 1) * DV_B)
            out_ref[0, rows, sl] = (_rms(hs[:, sl]) * ng[:, sl] * og[:, sl]).astype(BF16)
        return carry

    lax.fori_loop(0, T_ALL // TILE, finish, 0)


def _mlstm_call(mq, mk, mv, gc, gr, og, ng):
    B = mq.shape[0]
    full = lambda a: pl.BlockSpec((1,) + a.shape[1:], lambda b: (b,) + (0,) * (a.ndim - 1))
    return pl.pallas_call(
        _mlstm_kernel,
        grid=(B,),
        in_specs=[full(mq), full(mk), full(mv), full(gc), full(gr), full(og),
                  pl.BlockSpec(ng.shape, lambda b: (0, 0))],
        out_specs=pl.BlockSpec((1, T_ALL, H_B * DV_B), lambda b: (b, 0, 0)),
        out_shape=jax.ShapeDtypeStruct((B, T_ALL, H_B * DV_B), BF16),
        scratch_shapes=[
            pltpu.VMEM((T_ALL, H_B * DV_B), F32),
            pltpu.VMEM((T_ALL, H_B * DV_B), F32),
            pltpu.VMEM((2 * H_B * DK_B, 2 * DV_B), F32),
        ] + 2 * [
            pltpu.VMEM((2 * H_B * MLSTM_CHUNK, 2 * DV_B), F32),
            pltpu.VMEM((2 * H_B * DK_B, 2 * DV_B), F32),
            pltpu.VMEM((2 * H_B * 2 * MLSTM_CHUNK, 128), F32),
            pltpu.VMEM((2 * H_B * 2 * SUBLANES, 128), F32),
        ],
        compiler_params=_cparams(("arbitrary",)),
        name="mlstm",
    )(mq, mk, mv, gc, gr, og, ng)


def _na_bias_tables(rpb):
    kh = WIN_H
    n_drow = 2 * WIN_H - 1
    qcol = np.arange(GRID_W)
    col_start = np.clip(qcol - WIN_W // 2, 0, GRID_W - WIN_W)
    col_ok = (qcol[None, :] >= col_start[:, None]) & (qcol[None, :] < col_start[:, None] + WIN_W)
    dcol = qcol[None, :] - qcol[:, None] + (WIN_W - 1)
    onehot = (dcol[None] == np.arange(2 * WIN_W - 1)[:, None, None]) & col_ok[None]
    blocks = jnp.einsum('hdx,xck->hdck', rpb.astype(F32), jnp.asarray(onehot, F32),
                        precision=lax.Precision.HIGHEST)
    blocks = jnp.where(col_ok[None, None], blocks, NEG_BIG)
    outside = jnp.full((H_A, 1, GRID_W, GRID_W), NEG_BIG, F32)
    blocks = jnp.concatenate([blocks, outside], axis=1)
    idx = np.full((len(NA_CASES), NA_QROWS, NA_KROWS), n_drow, np.int32)
    for ci, r in enumerate(NA_CASES):
        u0 = int(np.clip(r - kh // 2, 0, GRID_ROWS - NA_KROWS))
        for qi in range(NA_QROWS):
            rq = r + qi
            r0 = int(np.clip(rq - kh // 2, 0, GRID_ROWS - kh))
            for ui in range(NA_KROWS):
                kr = u0 + ui
                if r0 <= kr < r0 + kh:
                    idx[ci, qi, ui] = kr - rq + (WIN_H - 1)

    def assemble(blk_ref, tab_ref):
        for ci in range(len(NA_CASES)):
            for qi in range(NA_QROWS):
                for ui in range(NA_KROWS):
                    tab_ref[ci, 0, qi * GRID_W:(qi + 1) * GRID_W, ui * GRID_W:(ui + 1) * GRID_W] = (
                        blk_ref[0, int(idx[ci, qi, ui])])

    return pl.pallas_call(
        assemble,
        grid=(H_A,),
        in_specs=[pl.BlockSpec((1, n_drow + 1, GRID_W, GRID_W), lambda h: (h, 0, 0, 0))],
        out_specs=pl.BlockSpec((len(NA_CASES), 1, NA_TQ, NA_TK), lambda h: (0, h, 0, 0)),
        out_shape=jax.ShapeDtypeStruct((len(NA_CASES), H_A, NA_TQ, NA_TK), F32),
        compiler_params=_cparams(("arbitrary",)),
        name="nbr_bias_table",
    )(blocks)


def _na_case(qb):
    case = jnp.int32(NA_STEP_CASE[0])
    for blk, c in enumerate(NA_STEP_CASE):
        if c != NA_STEP_CASE[0]:
            case = jnp.where(qb == blk, c, case)
    return case


def _na_kernel(q_ref, k_ref, v_ref, *refs):
    bias_refs, o_ref = refs[:-1], refs[-1]
    j = pl.program_id(1)
    lane = lax.broadcasted_iota(I32, (1, 2 * HD_A), 1)
    lo_half = lane < HD_A

    def attend(q_rows, key_sets, bias_for_head):
        for pp0 in range(0, H_A // 2, NA_STAGE_PAIRS):
            heads = []
            for pp in range(pp0, pp0 + NA_STAGE_PAIRS):
                lanes = slice(pp * 2 * HD_A, (pp + 1) * 2 * HD_A)
                qp = q_ref[0, q_rows, lanes]
                ks = [k_ref[0, rs, lanes] for rs in key_sets]
                vs = [v_ref[0, rs, lanes] for rs in key_sets]
                for hh in range(2):
                    qm = jnp.where(lo_half if hh == 0 else ~lo_half, qp, jnp.zeros_like(qp))
                    heads.append(dict(head=2 * pp + hh, vs=vs, ss=[_dot_nt(qm, kk) for kk in ks]))
            for hd in heads:
                ss = hd['ss']
                if bias_for_head is not None:
                    ss[0] = ss[0] + bias_for_head(hd['head'])
                m = ss[0].max(axis=-1, keepdims=True)
                for s in ss[1:]:
                    m = jnp.maximum(m, s.max(axis=-1, keepdims=True))
                ps = [jnp.exp(s - m) for s in ss]
                l = ps[0].sum(axis=-1, keepdims=True)
                for p in ps[1:]:
                    l = l + p.sum(axis=-1, keepdims=True)
                hd['ps'] = [p.astype(BF16) for p in ps]
                hd['l'] = l
            for hd in heads:
                acc = _dot(hd['ps'][0], hd['vs'][0])
                for p, vv in zip(hd['ps'][1:], hd['vs'][1:]):
                    acc = acc + _dot(p, vv)
                hd['o'] = acc / hd['l']
            for i, pp in enumerate(range(pp0, pp0 + NA_STAGE_PAIRS)):
                lanes = slice(pp * 2 * HD_A, (pp + 1) * 2 * HD_A)
                o_ref[0, q_rows, lanes] = jnp.where(lo_half, heads[2 * i]['o'], heads[2 * i + 1]['o']).astype(BF16)

    ctx_rows = pl.ds(0, CTX_LEN)

    @pl.when(j == 0)
    def _():
        attend(ctx_rows, [ctx_rows], None)

    @pl.when(j > 0)
    def _():
        for i, bias_ref in enumerate(bias_refs):
            r = ((j - 1) * NA_STEP_BLOCKS + i) * NA_QROWS
            u0 = jnp.clip(r - WIN_H // 2, 0, GRID_ROWS - NA_KROWS)
            q_rows = pl.ds(pl.multiple_of(CTX_LEN + r * GRID_W, NA_TQ), NA_TQ)
            k_rows = pl.ds(pl.multiple_of(CTX_LEN + u0 * GRID_W, GRID_W), NA_TK)
            attend(q_rows, [k_rows, ctx_rows], lambda head, ref=bias_ref: ref[0, head])


def _na_call(aq, ak, av, bias):
    B = aq.shape[0]
    full = pl.BlockSpec((1, T_ALL, D_A), lambda b, j: (b, 0, 0))
    n_blocks = GRID_ROWS // NA_QROWS
    bias_specs = [pl.BlockSpec((1, H_A, NA_TQ, NA_TK),
                               functools.partial(lambda b, j, i: (_na_case((j - 1) * NA_STEP_BLOCKS + i), 0, 0, 0), i=i))
                  for i in range(NA_STEP_BLOCKS)]
    return pl.pallas_call(
        _na_kernel,
        grid=(B, 1 + n_blocks // NA_STEP_BLOCKS),
        in_specs=[full, full, full] + bias_specs,
        out_specs=full,
        out_shape=jax.ShapeDtypeStruct((B, T_ALL, D_A), BF16),
        compiler_params=_cparams(("arbitrary", "arbitrary")),
        name="nbr_attn",
    )(aq, ak, av, *([bias] * NA_STEP_BLOCKS))


def _route(logits_t, rb_col):
    sc = _sigmoid(logits_t)
    sel = sc + rb_col
    selr = [sel[e:e + 1, :] for e in range(N_EXPERTS)]
    scr = [sc[e:e + 1, :] for e in range(N_EXPERTS)]
    gscore = []
    for g in range(N_GROUPS):
        a, b, c, d = selr[EXP_PER_GROUP * g:EXP_PER_GROUP * (g + 1)]
        s1, t1 = jnp.maximum(a, b), jnp.minimum(a, b)
        s2, t2 = jnp.maximum(c, d), jnp.minimum(c, d)
        gscore.append(jnp.maximum(s1, s2) + jnp.maximum(jnp.minimum(s1, s2), jnp.maximum(t1, t2)))
    best = gscore[0]
    gi = jnp.zeros(best.shape, I32)
    for g in range(1, N_GROUPS):
        better = gscore[g] > best
        gi = jnp.where(better, g, gi)
        best = jnp.where(better, gscore[g], best)
    vs, ws = [], []
    for k in range(EXP_PER_GROUP):
        v = selr[k]
        w = scr[k]
        for g in range(1, N_GROUPS):
            v = jnp.where(gi == g, selr[EXP_PER_GROUP * g + k], v)
            w = jnp.where(gi == g, scr[EXP_PER_GROUP * g + k], w)
        vs.append(v)
        ws.append(w)
    b1, i1 = vs[0], jnp.zeros(best.shape, I32)
    for k in range(1, EXP_PER_GROUP):
        better = vs[k] > b1
        i1 = jnp.where(better, k, i1)
        b1 = jnp.where(better, vs[k], b1)
    b2 = jnp.full(best.shape, -jnp.inf, F32)
    i2 = jnp.zeros(best.shape, I32)
    for k in range(EXP_PER_GROUP):
        vk = jnp.where(i1 == k, -jnp.inf, vs[k])
        better = vk > b2
        i2 = jnp.where(better, k, i2)
        b2 = jnp.where(better, vk, b2)
    w1 = ws[0]
    w2 = ws[0]
    for k in range(1, EXP_PER_GROUP):
        w1 = jnp.where(i1 == k, ws[k], w1)
        w2 = jnp.where(i2 == k, ws[k], w2)
    tot = w1 + w2
    return gi * EXP_PER_GROUP + i1, gi * EXP_PER_GROUP + i2, w1 / tot, w2 / tot


def _outproj_kernel(n_act, has_ctx, *refs):
    acts = refs[:n_act]
    ws = refs[n_act:2 * n_act]
    refs = refs[2 * n_act:]
    g_ref, rwh_ref, rwl_ref, xo_ref, hp_ref, lg_ref = refs[-6:]
    ns = xo_ref.shape[0]
    x_ins, (mods,) = _stream_tiles(has_ctx, has_ctx, refs[:-6], ns)
    o = _dot(acts[0][...].reshape(ns * TILE, -1), ws[0][...])
    for a, w in zip(acts[1:], ws[1:]):
        o = o + _dot(a[...].reshape(ns * TILE, -1), w[...])
    hs = []
    for s in range(ns):
        x = x_ins[s] + mods[s][2:3] * o[s * TILE:(s + 1) * TILE]
        xo_ref[s] = x
        h = _rms(x) * g_ref[...] * (1.0 + mods[s][4:5]) + mods[s][3:4]
        _store_token_tiles(hp_ref, (s,), 0, h)
        hs.append(_split_bf16(h, 2))
    for s, (h_hi, h_lo) in enumerate(hs):
        lg_ref[s] = (_dot_nt(rwh_ref[...], h_hi) + _dot_nt(rwh_ref[...], h_lo)
                     + _dot_nt(rwl_ref[...], h_hi))


def _outproj_call(acts, ws, x_src, x_off, modl, norm_g, rwh, rwl, n_tiles, ctx_src=None):
    B = x_src.shape[0]
    ns = math.gcd(B, PROJ_SAMPLES)
    n_act = len(acts)
    T = n_tiles * TILE
    const2 = lambda b, j: (0, 0)
    tok = lambda b, j: (b, j, 0)
    in_specs = [pl.BlockSpec((ns, TILE, a.shape[2]), tok) for a in acts]
    in_specs += [pl.BlockSpec(w.shape, const2) for w in ws]
    stream_specs, streams = _stream_specs(ns, x_src, x_off, ctx_src, ctx_src is not None, [modl])
    in_specs += stream_specs + [
        pl.BlockSpec((1, D_MODEL), const2),
        pl.BlockSpec(rwh.shape, const2),
        pl.BlockSpec(rwl.shape, const2),
    ]
    out_shape = [
        jax.ShapeDtypeStruct((B, T, D_MODEL), F32),
        jax.ShapeDtypeStruct((B, T * SUBLANES, 128), F32),
        jax.ShapeDtypeStruct((B, N_EXPERTS, T), F32),
    ]
    out_specs = [
        pl.BlockSpec((ns, TILE, D_MODEL), tok),
        pl.BlockSpec((ns, TILE * SUBLANES, 128), tok),
        pl.BlockSpec((ns, N_EXPERTS, TILE), lambda b, j: (b, 0, j)),
    ]
    return pl.pallas_call(
        functools.partial(_outproj_kernel, n_act, ctx_src is not None),
        grid=(B // ns, n_tiles),
        in_specs=in_specs,
        out_specs=out_specs,
        out_shape=out_shape,
        compiler_params=_cparams(("arbitrary", "arbitrary")),
        name="outproj",
    )(*acts, *ws, *streams, norm_g, rwh, rwl)


def _route_kernel(lg_ref, rb_ref, su_ref, ri_ref, rw_ref, cnt_ref):
    logits_t = lg_ref[0]
    T = logits_t.shape[1]
    e1, e2, w1, w2 = _route(logits_t, rb_ref[...])
    eidx = lax.broadcasted_iota(I32, logits_t.shape, 0)
    oh1 = eidx == e1
    oh2 = eidx == e2
    onehot = jnp.where(oh1, 1.0, jnp.where(oh2, 1.0, 0.0))
    count = jnp.zeros((N_EXPERTS, 1), F32)
    ranks = []
    for jj in range(T // TILE):
        oh = onehot[:, jj * TILE:(jj + 1) * TILE]
        ranks.append(_dot(oh.astype(BF16), su_ref[...]) + count)
        count = count + jnp.sum(oh, axis=1, keepdims=True)
    cpad = jnp.floor((count + (SUBLANES - 1.0)) * (1.0 / SUBLANES)) * SUBLANES
    ecol = lax.broadcasted_iota(I32, (N_EXPERTS, 1), 0)
    start = jnp.zeros((N_EXPERTS, 1), F32)
    for e in range(N_EXPERTS - 1):
        start = start + jnp.where(ecol > e, cpad[e:e + 1, :], 0.0)
    row = jnp.concatenate(ranks, axis=1) + start
    r1 = jnp.sum(jnp.where(oh1, row, 0.0), axis=0, keepdims=True)
    r2 = jnp.sum(jnp.where(oh2, row, 0.0), axis=0, keepdims=True)
    zi = jnp.zeros((SUBLANES - 4, T), I32)
    ri_ref[0] = jnp.concatenate([e1, e2, r1.astype(I32), r2.astype(I32), zi], axis=0)
    zf = jnp.zeros((SUBLANES - 2, T), F32)
    rw_ref[0] = jnp.concatenate([w1, w2, zf], axis=0)
    cnt_ref[0] = jnp.broadcast_to(count, (N_EXPERTS, 128)).astype(I32)


def _route_call(lg, rb, su):
    B, _, T = lg.shape
    per_sample = lambda b: (b, 0, 0)
    return pl.pallas_call(
        _route_kernel,
        grid=(B,),
        in_specs=[pl.BlockSpec((1, N_EXPERTS, T), per_sample),
                  pl.BlockSpec(rb.shape, lambda b: (0, 0)),
                  pl.BlockSpec(su.shape, lambda b: (0, 0))],
        out_specs=[pl.BlockSpec((1, SUBLANES, T), per_sample),
                   pl.BlockSpec((1, SUBLANES, T), per_sample),
                   pl.BlockSpec((1, N_EXPERTS, 128), per_sample)],
        out_shape=[jax.ShapeDtypeStruct((B, SUBLANES, T), I32),
                   jax.ShapeDtypeStruct((B, SUBLANES, T), F32),
                   jax.ShapeDtypeStruct((B, N_EXPERTS, 128), I32)],
        compiler_params=_cparams(("arbitrary",)),
        name="route",
    )(lg, rb, su)


def _moe_kernel(T, n_rows, cnt_ref, ri_ref, rw_ref, h_ref, w1_ref, w3_ref, w2_ref, y_ref,
                xb_ref, ob_ref, tokl_ref, off_ref):
    b = pl.program_id(0)
    e = pl.program_id(1)

    def tile_rows(row):
        return pl.ds(pl.multiple_of(row * SUBLANES, SUBLANES), SUBLANES)

    @pl.when(e == 0)
    def _():
        off_ref[0] = 0
        for i in range(N_EXPERTS):
            c = cnt_ref[b, i]
            off_ref[i + 1] = off_ref[i] + ((c + SUBLANES - 1) // SUBLANES) * SUBLANES

        for i in range(N_EXPERTS):
            for k in range(SUBLANES):
                tokl_ref[jnp.maximum(off_ref[i + 1] - SUBLANES + k, 0)] = 0

        def clear(i, carry):
            tokl_ref[off_ref[N_EXPERTS] + i] = 0
            return carry

        lax.fori_loop(0, MOE_RB, clear, 0, unroll=8)

        def place(t, carry):
            tokl_ref[ri_ref[2 * T + t]] = t
            tokl_ref[ri_ref[3 * T + t]] = t
            return carry

        lax.fori_loop(0, T, place, 0, unroll=8)

        def gather(i, carry):
            xb_ref[tile_rows(i), :] = h_ref[0, tile_rows(tokl_ref[i]), :]
            return carry

        lax.fori_loop(0, MOE_RB, gather, 0, unroll=8)

    c = cnt_ref[b, e]
    base = off_ref[e]
    next_base = off_ref[e + 1]

    def block(p0, n_blk_rows, p_next):
        xb = _load_token_tiles(xb_ref, (), 0, n_blk_rows).astype(BF16)
        for i in range(MOE_RB):
            xb_ref[i * SUBLANES:(i + 1) * SUBLANES, :] = h_ref[0, tile_rows(tokl_ref[p_next + i]), :]
        h1 = _dot(xb, w1_ref[0, 0])
        h3 = _dot(xb, w3_ref[0, 0])
        act = (h1 * _sigmoid(h1)) * h3
        _store_token_tiles(ob_ref, (), p0, _dot(act.astype(BF16), w2_ref[0, 0]))

    rows_up = ((c + MOE_RB_TAIL - 1) // MOE_RB_TAIL) * MOE_RB_TAIL
    n_big = rows_up // MOE_RB
    tail = rows_up - n_big * MOE_RB

    def big_block(rb, carry):
        p0 = pl.multiple_of(base + rb * MOE_RB, SUBLANES)
        last = jnp.logical_and(rb == n_big - 1, tail == 0)
        block(p0, MOE_RB, jnp.where(last, next_base, p0 + MOE_RB))
        return carry

    lax.fori_loop(0, n_big, big_block, 0)

    for tail_rows in range(MOE_RB_TAIL, MOE_RB, MOE_RB_TAIL):
        @pl.when(tail == tail_rows)
        def _():
            block(pl.multiple_of(base + n_big * MOE_RB, SUBLANES), tail_rows, next_base)

    @pl.when(e == N_EXPERTS - 1)
    def _():
        def combine(t, carry):
            y_ref[0, tile_rows(t), :] = (rw_ref[t] * ob_ref[tile_rows(ri_ref[2 * T + t]), :]
                                         + rw_ref[T + t] * ob_ref[tile_rows(ri_ref[3 * T + t]), :])
            return carry

        lax.fori_loop(0, T, combine, 0, unroll=8)


def _moe_call(cnt, ri, rw, h, w1, w3, w2, layer):
    B = h.shape[0]
    T = h.shape[1] // SUBLANES
    n_rows = 2 * T + N_EXPERTS * SUBLANES + MOE_RB
    smem = functools.partial(pl.BlockSpec, memory_space=pltpu.SMEM)
    once = pl.Buffered(1)
    return pl.pallas_call(
        functools.partial(_moe_kernel, T, n_rows),
        grid=(B, N_EXPERTS),
        in_specs=[
            smem(cnt.shape, lambda b, e: (0, 0)),
            smem((4 * T,), lambda b, e: (b,)),
            smem((4 * T,), lambda b, e: (b,)),
            pl.BlockSpec((1, T * SUBLANES, 128), lambda b, e: (b, 0, 0)),
            pl.BlockSpec((1, 1, D_MODEL, D_FF), lambda b, e: (layer, e, 0, 0)),
            pl.BlockSpec((1, 1, D_MODEL, D_FF), lambda b, e: (layer, e, 0, 0)),
            pl.BlockSpec((1, 1, D_FF, D_MODEL), lambda b, e: (layer, e, 0, 0)),
        ],
        out_specs=pl.BlockSpec((1, T * SUBLANES, 128), lambda b, e: (b, 0, 0), pipeline_mode=once),
        out_shape=jax.ShapeDtypeStruct((B, T * SUBLANES, 128), F32),
        scratch_shapes=[
            pltpu.VMEM((MOE_RB * SUBLANES, 128), F32),
            pltpu.VMEM((n_rows * SUBLANES, 128), F32),
            pltpu.SMEM((n_rows,), I32),
            pltpu.SMEM((N_EXPERTS + 1,), I32),
        ],
        compiler_params=_cparams(("arbitrary", "arbitrary"), fuse_inputs=[False] * 4 + [True] * 3),
        name="moe",
    )(cnt, ri, rw, h, w1, w3, w2)


def _route_tables(ri, rw, cnt):
    return cnt[:, :, 0], ri[:, :4].reshape(-1), rw[:, :4].reshape(-1)


def _inproj1_kernel(*refs):
    (y_ref, g_ref, w_ref, qg_ref, kg_ref, cos_ref, sin_ref, xo_ref, q_ref, k_ref, v_ref) = refs[-11:]
    ns = xo_ref.shape[0]
    x_ins, (mods0, mods) = _stream_tiles(False, True, refs[:-11], ns)
    hbs = []
    for s in range(ns):
        x = x_ins[s] + mods0[s][5:6] * _load_token_tiles(y_ref, (s,), 0, TILE)
        xo_ref[s] = x
        hbs.append((_rms(x) * g_ref[...] * (1.0 + mods[s][1:2]) + mods[s][0:1]).astype(BF16))
    hb = jnp.concatenate(hbs, axis=0)
    cos = cos_ref[...]
    sin = sin_ref[...]

    def rope_heads(ref, proj, n_heads, gain, scale):
        for s in range(ns):
            for h in range(n_heads):
                sl = slice(h * HD_C, (h + 1) * HD_C)
                n = _rms(proj[s * TILE:(s + 1) * TILE, sl]) * gain
                r = n * cos + pltpu.roll(n, HD_C // 2, 1) * sin
                ref[s, :, sl] = (r if scale is None else r * scale).astype(BF16)

    ko = H_C * HD_C
    qkv = _dot(hb, w_ref[...])
    qs, kv = qkv[:, :ko], qkv[:, ko:]
    rope_heads(k_ref, kv, KV_C, kg_ref[...], None)
    for s in range(ns):
        v_ref[s] = kv[s * TILE:(s + 1) * TILE, KV_C * HD_C:].astype(BF16)
    rope_heads(q_ref, qs, H_C, qg_ref[...], HD_C ** -0.5)


def _inproj1_call(x1, y0, mod0, mod1, norm_g, w, qg, kg, cos, sin):
    B = x1.shape[0]
    ns = math.gcd(B, INPROJ1_SAMPLES)
    tok = lambda b, j: (b, j, 0)
    const2 = lambda b, j: (0, 0)
    lat_tok = lambda b, j: (b, jnp.maximum(j - NT_CTX, 0), 0)
    widths = [(T_ALL, D_MODEL, F32), (SEQ, H_C * HD_C, BF16),
              (T_ALL, KV_C * HD_C, BF16), (T_ALL, KV_C * HD_C, BF16)]
    stream_specs, streams = _stream_specs(ns, x1, 0, None, True, [mod0, mod1])
    return pl.pallas_call(
        _inproj1_kernel,
        grid=(B // ns, NT_ALL),
        in_specs=stream_specs + [
            pl.BlockSpec((ns, TILE * SUBLANES, 128), tok),
            pl.BlockSpec((1, D_MODEL), const2),
            pl.BlockSpec(w.shape, const2),
            pl.BlockSpec(qg.shape, const2),
            pl.BlockSpec(kg.shape, const2),
            pl.BlockSpec((TILE, HD_C), lambda b, j: (j, 0)),
            pl.BlockSpec((TILE, HD_C), lambda b, j: (j, 0)),
        ],
        out_specs=[pl.BlockSpec((ns, TILE, w_), lat_tok if t == SEQ else tok) for t, w_, _ in widths],
        out_shape=[jax.ShapeDtypeStruct((B, t, w_), dt) for t, w_, dt in widths],
        compiler_params=_cparams(("arbitrary", "arbitrary")),
        name="inproj1",
    )(*streams, y0, norm_g, w, qg, kg, cos, sin)


def _gqa_kernel(q_ref, k_ref, v_ref, o_ref):
    group_heads = H_C // KV_C
    for g in range(KV_C):
        k = k_ref[0, :, g * HD_C:(g + 1) * HD_C]
        v = v_ref[0, :, g * HD_C:(g + 1) * HD_C]
        for h0 in range(g * group_heads, (g + 1) * group_heads, GQA_STAGE_HEADS):
            sls = [slice(h * HD_C, (h + 1) * HD_C) for h in range(h0, h0 + GQA_STAGE_HEADS)]
            ss = [_dot_nt(q_ref[0, :, sl], k) for sl in sls]
            ps, ls = [], []
            for s in ss:
                p = jnp.exp(s - s.max(axis=-1, keepdims=True))
                ls.append(p.sum(axis=-1, keepdims=True))
                ps.append(p.astype(BF16))
            for sl, p, l in zip(sls, ps, ls):
                o_ref[0, :, sl] = (_dot(p, v) / l).astype(BF16)


def _gqa_call(q, k, v):
    B = q.shape[0]
    whole = lambda b, j: (b, 0, 0)
    return pl.pallas_call(
        _gqa_kernel,
        grid=(B, SEQ // GQA_TQ),
        in_specs=[
            pl.BlockSpec((1, GQA_TQ, H_C * HD_C), lambda b, j: (b, j, 0)),
            pl.BlockSpec((1, T_ALL, KV_C * HD_C), whole),
            pl.BlockSpec((1, T_ALL, KV_C * HD_C), whole),
        ],
        out_specs=pl.BlockSpec((1, GQA_TQ, H_C * HD_C), lambda b, j: (b, j, 0)),
        out_shape=jax.ShapeDtypeStruct((B, SEQ, H_C * HD_C), BF16),
        compiler_params=_cparams(("arbitrary", "arbitrary")),
        name="gqa",
    )(q, k, v)


def _final_kernel(x_ref, y_ref, mod_ref, o_ref):
    o_ref[0] = x_ref[0] + mod_ref[0][5:6] * _load_token_tiles(y_ref, (0,), 0, FINAL_TILE)


def _final_call(x, y, modl):
    B, T, D = x.shape
    tok = lambda b, j: (b, j, 0)
    return pl.pallas_call(
        _final_kernel,
        grid=(B, T // FINAL_TILE),
        in_specs=[pl.BlockSpec((1, FINAL_TILE, D), tok), pl.BlockSpec((1, FINAL_TILE * SUBLANES, 128), tok),
                  pl.BlockSpec((1, 6, D), lambda b, j: (b, 0, 0))],
        out_specs=pl.BlockSpec((1, FINAL_TILE, D), tok),
        out_shape=jax.ShapeDtypeStruct((B, T, D), F32),
        compiler_params=_cparams(("arbitrary", "arbitrary")),
        name="final_residual",
    )(x, y, modl)


def _chunk_tri(lower):
    i = np.arange(TILE)
    same = (i[:, None] // MLSTM_CHUNK) == (i[None, :] // MLSTM_CHUNK)
    tri = (i[None, :] <= i[:, None]) if lower else (i[None, :] >= i[:, None])
    return jnp.asarray((same & tri).astype(np.float32), BF16)


def _rope_tables():
    n_freq = HD_C // 4
    inv_freq = ROPE_THETA ** (-jnp.arange(n_freq, dtype=F32) / n_freq)
    t = jnp.arange(SEQ)
    rows = (t // GRID_W).astype(F32)
    cols = (t % GRID_W).astype(F32)
    ang = jnp.concatenate([rows[:, None] * inv_freq, cols[:, None] * inv_freq], axis=-1)
    cos, sin = jnp.cos(ang), jnp.sin(ang)
    cos_l = jnp.concatenate([cos, cos], axis=-1)
    sin_l = jnp.concatenate([-sin, sin], axis=-1)
    cos_all = jnp.concatenate([jnp.ones((CTX_LEN, HD_C), F32), cos_l], axis=0)
    sin_all = jnp.concatenate([jnp.zeros((CTX_LEN, HD_C), F32), sin_l], axis=0)
    return cos_all, sin_all


_HEAD_PERM = np.concatenate([np.arange(0, HD_C, 2), np.arange(1, HD_C, 2)])


def kernel(x, c, ctx, c_ctx, ada_w, ada_b, norm_mix_g, norm_ffn_g, even_w_in, even_w_out,
           na_q_norm_g, na_k_norm_g, na_rpb, mlstm_gate_b, mlstm_norm_g, odd_w_in, odd_w_out,
           gqa_q_norm_g, gqa_k_norm_g, router_w, router_b, exp_w1, exp_w3, exp_w2):
    B = x.shape[0]
    assert B <= N_MOD_CTX_ROW and x.shape[1:] == (SEQ, D_MODEL) and ctx.shape[1:] == (CTX_LEN, D_MODEL)
    n_g = 4 * H_B

    cvec = jnp.zeros((N_MOD_ROWS, D_MODEL), F32).at[:B].set(c).at[N_MOD_CTX_ROW].set(c_ctx)
    mod = _ada_call(cvec, ada_w, ada_b).reshape(2, N_MOD_ROWS, 6, D_MODEL)
    mod0, mod1 = mod[0], mod[1]

    w_in = even_w_in[0]
    n_main = w_in.shape[1] - n_g
    wm = w_in[:, :n_main].astype(BF16)
    wg_f = w_in[:, n_main:]
    wg = jnp.pad(wg_f, ((0, 0), (0, 128 - n_g))).astype(BF16)
    wgt = wg_f.T.astype(BF16)
    gb = jnp.pad(mlstm_gate_b[0].reshape(1, n_g), ((0, 0), (0, 128 - n_g)))
    gbt = mlstm_gate_b[0].reshape(n_g, 1)
    qg = jnp.tile(na_q_norm_g[0], H_A).reshape(1, D_A)
    kg = jnp.tile(na_k_norm_g[0], H_A).reshape(1, D_A)
    hid = np.arange(128) // HD_A
    bd = jnp.asarray((hid[:, None] == hid[None, :]).astype(np.float32), BF16)
    k_lo = 3 * D_A + H_B * DK_B
    wkt = w_in[:, k_lo:k_lo + H_B * DK_B].T.astype(BF16)
    aq, ak, av, mq, mv, og, mkt, gc, gr = _inproj0_call(
        x, ctx, mod0, norm_mix_g[0].reshape(1, D_MODEL), wm, wkt, wg, wgt, gb, gbt, qg, kg, bd,
        _chunk_tri(True), _chunk_tri(False))
    hm = _mlstm_call(mq, mkt, mv, gc, gr, og, mlstm_norm_g[0].reshape(1, H_B * DV_B))
    oa = _na_call(aq, ak, av, _na_bias_tables(na_rpb[0]))

    rw_t = router_w.T
    rwh = rw_t.astype(BF16)
    rwl = (rw_t - rwh.astype(F32)).astype(BF16)
    rb = router_b.reshape(N_EXPERTS, 1).astype(F32)
    i = np.arange(TILE)
    su = jnp.asarray((i[:, None] < i[None, :]).astype(np.float32), BF16)
    w_out = even_w_out[0].astype(BF16)
    experts = [tuple(w[l:l + 1].astype(BF16) for w in (exp_w1, exp_w3, exp_w2)) for l in range(2)]
    x1, hp0, lg0 = _outproj_call(
        [oa, hm], [w_out[:D_A], w_out[D_A:]], x, 0, mod0,
        norm_ffn_g[0].reshape(1, D_MODEL), rwh, rwl, NT_ALL, ctx_src=ctx)
    y0 = _moe_call(*_route_tables(*_route_call(lg0, rb, su)), hp0, *experts[0], 0)

    w1_in = odd_w_in[0]
    qk_cols = np.concatenate([h * HD_C + _HEAD_PERM for h in range(H_C + KV_C)])
    cols = np.concatenate([qk_cols, np.arange((H_C + KV_C) * HD_C, w1_in.shape[1])])
    w1_in = w1_in[:, cols].astype(BF16)
    cos_all, sin_all = _rope_tables()
    x2, q, k, v = _inproj1_call(
        x1, y0, mod0, mod1, norm_mix_g[1].reshape(1, D_MODEL), w1_in,
        gqa_q_norm_g[0][_HEAD_PERM].reshape(1, HD_C), gqa_k_norm_g[0][_HEAD_PERM].reshape(1, HD_C),
        cos_all, sin_all)
    o = _gqa_call(q, k, v)
    x3, hp1, lg1 = _outproj_call(
        [o], [odd_w_out[0].astype(BF16)], x2, NT_CTX, mod1,
        norm_ffn_g[1].reshape(1, D_MODEL), rwh, rwl, SEQ // TILE)
    y1 = _moe_call(*_route_tables(*_route_call(lg1, rb, su)), hp1, *experts[1], 0)
    return _final_call(x3, y1, mod1)
```
